```python
import math
import jax, jax.numpy as jnp
from jax import lax
import numpy as np

D_MODEL = 1024
BATCH = 8
SEQ = 4096
DEPTH = 4

CHUNK = 64
Q_BLOCK = 128
N_A_LAYERS = DEPTH // 2
N_B_LAYERS = DEPTH - N_A_LAYERS
SSM_GROUP = 16
SSM_GROUPS = D_MODEL // SSM_GROUP
SSM_STATE = 64
DT_MIN = 0.001
DT_MAX = 0.1
N_HEADS = 16
QK_NOPE_DIM = 64
QK_ROPE_DIM = 32
V_HEAD_DIM = 64
Q_LORA_RANK = 256
KV_LORA_RANK = 256
ROPE_THETA = 10000.0
ATTN_SCALE = 1.0 / math.sqrt(QK_NOPE_DIM + QK_ROPE_DIM)
D_FF = ((8 * D_MODEL + 3 * 256 - 1) // (3 * 256)) * 256
EPS = 1e-6

kernel_name = "yoco_s5_mla_adaln_encoder"


def rms_norm(x, g):
    xf = x.astype(jnp.float32)
    y = xf * lax.rsqrt(jnp.mean(xf * xf, axis=-1, keepdims=True) + EPS)
    return (y * g.astype(jnp.float32)).astype(x.dtype)


def modulate(h, shift, scale):
    return h * (1.0 + scale[:, None, :]) + shift[:, None, :]


def rope_cos_sin(positions):
    inv = 1.0 / (ROPE_THETA ** (jnp.arange(0, QK_ROPE_DIM, 2, dtype=jnp.float32) / QK_ROPE_DIM))
    ang = positions.astype(jnp.float32)[..., None] * inv
    return jnp.cos(ang), jnp.sin(ang)


def apply_rope(x, cos, sin):
    shape = cos.shape[:2] + (1,) * (x.ndim - 3) + cos.shape[-1:]
    cos = cos.reshape(shape)
    sin = sin.reshape(shape)
    x1, x2 = jnp.split(x.astype(jnp.float32), 2, axis=-1)
    return jnp.concatenate([x1 * cos - x2 * sin, x1 * sin + x2 * cos], axis=-1).astype(x.dtype)


def _complex_linear_combine(e1, e2):
    a1r, a1i, b1r, b1i = e1
    a2r, a2i, b2r, b2i = e2
    ar = a1r * a2r - a1i * a2i
    ai = a1r * a2i + a1i * a2r
    br = a2r * b1r - a2i * b1i + b2r
    bi = a2r * b1i + a2i * b1r + b2i
    return (ar, ai, br, bi)


def s5_mixer(h, lam_re, lam_im, log_dt, b_re, b_im, c_re, c_im, d_skip, w_glu, b_glu):
    bsz, s_len, d = h.shape
    f32 = jnp.float32
    lr = lam_re.astype(f32)
    li = lam_im.astype(f32)
    dt = jnp.exp(log_dt.astype(f32))[:, None]
    mag = jnp.exp(lr * dt)
    ab_re = mag * jnp.cos(li * dt)
    ab_im = mag * jnp.sin(li * dt)
    den = lr * lr + li * li
    nr = ab_re - 1.0
    ni = ab_im
    f_re = (nr * lr + ni * li) / den
    f_im = (ni * lr - nr * li) / den
    br = b_re.astype(f32)
    bi = b_im.astype(f32)
    bb_re = f_re[..., None] * br - f_im[..., None] * bi
    bb_im = f_re[..., None] * bi + f_im[..., None] * br
    cr = c_re.astype(f32)
    ci = c_im.astype(f32)

    n_chunks = s_len // CHUNK
    u = h.astype(f32).reshape(bsz, n_chunks, CHUNK, SSM_GROUPS, SSM_GROUP).transpose(1, 0, 2, 3, 4)

    def chunk_step(carry, u_c):
        s_re, s_im = carry
        bu_re = jnp.einsum('blgp,gnp->blgn', u_c, bb_re)
        bu_im = jnp.einsum('blgp,gnp->blgn', u_c, bb_im)
        a_re = jnp.broadcast_to(ab_re, bu_re.shape)
        a_im = jnp.broadcast_to(ab_im, bu_im.shape)
        pa_re, pa_im, loc_re, loc_im = lax.associative_scan(
            _complex_linear_combine, (a_re, a_im, bu_re, bu_im), axis=1)
        st_re = loc_re + pa_re * s_re[:, None] - pa_im * s_im[:, None]
        st_im = loc_im + pa_re * s_im[:, None] + pa_im * s_re[:, None]
        y = jnp.einsum('blgn,gpn->blgp', st_re, cr) - jnp.einsum('blgn,gpn->blgp', st_im, ci)
        return (st_re[:, -1], st_im[:, -1]), y

    init = (jnp.zeros((bsz, SSM_GROUPS, SSM_STATE), f32), jnp.zeros((bsz, SSM_GROUPS, SSM_STATE), f32))
    _, y = lax.scan(chunk_step, init, u)
    y = y.transpose(1, 0, 2, 3, 4).reshape(bsz, s_len, d)
    y = (y + d_skip.astype(f32) * h.astype(f32)).astype(h.dtype)
    g = jax.nn.gelu(y)
    return g * jax.nn.sigmoid(g @ w_glu + b_glu)


def shared_mla_kv(hk, cos, sin, w_kv_a, kv_a_norm_g, w_kv_b, k_nope_norm_g, k_rope_norm_g):
    bsz, s_len, _ = hk.shape
    kv_a = hk @ w_kv_a
    c_kv, k_rope = jnp.split(kv_a, [KV_LORA_RANK], axis=-1)
    c_kv = rms_norm(c_kv, kv_a_norm_g)
    kv = (c_kv @ w_kv_b).reshape(bsz, s_len, N_HEADS, QK_NOPE_DIM + V_HEAD_DIM)
    k_nope, v = jnp.split(kv, [QK_NOPE_DIM], axis=-1)
    k_nope = rms_norm(k_nope, k_nope_norm_g)
    k_rope = apply_rope(rms_norm(k_rope, k_rope_norm_g), cos, sin)
    return k_nope, k_rope, v


def mla_attention(h, cos, sin, k_nope, k_rope, v, w_dq, q_norm_g, w_uq, q_nope_norm_g, q_rope_norm_g, w_o):
    bsz, s_len, _ = h.shape
    q = rms_norm(h @ w_dq, q_norm_g) @ w_uq
    q = q.reshape(bsz, s_len, N_HEADS, QK_NOPE_DIM + QK_ROPE_DIM)
    q_nope, q_rope = jnp.split(q, [QK_NOPE_DIM], axis=-1)
    q_nope = rms_norm(q_nope, q_nope_norm_g)
    q_rope = apply_rope(rms_norm(q_rope, q_rope_norm_g), cos, sin)
    n_blocks = s_len // Q_BLOCK
    qn_b = q_nope.reshape(bsz, n_blocks, Q_BLOCK, N_HEADS, QK_NOPE_DIM).transpose(1, 0, 2, 3, 4)
    qr_b = q_rope.reshape(bsz, n_blocks, Q_BLOCK, N_HEADS, QK_ROPE_DIM).transpose(1, 0, 2, 3, 4)
    key_chunk = jnp.arange(s_len) // CHUNK

    def block_attn(args):
        qn, qr, blk = args
        s = (jnp.einsum('bqhd,bkhd->bhqk', qn, k_nope, preferred_element_type=jnp.float32)
             + jnp.einsum('bqhr,bkr->bhqk', qr, k_rope, preferred_element_type=jnp.float32)) * ATTN_SCALE
        q_chunk = (blk * Q_BLOCK + jnp.arange(Q_BLOCK)) // CHUNK
        mask = q_chunk[:, None] >= key_chunk[None, :]
        s = jnp.where(mask[None, None], s, -1e30)
        p = jax.nn.softmax(s, axis=-1)
        return jnp.einsum('bhqk,bkhd->bqhd', p.astype(v.dtype), v)

    o = lax.map(block_attn, (qn_b, qr_b, jnp.arange(n_blocks)))
    o = o.transpose(1, 0, 2, 3, 4).reshape(bsz, s_len, N_HEADS * V_HEAD_DIM)
    return o @ w_o


def swiglu(h, w_gate, w_up, w_down):
    return (jax.nn.silu(h @ w_gate) * (h @ w_up)) @ w_down


def _fwd_setup_inputs(seed: int = 0) -> dict:
    key = jax.random.key(seed)
    ks = iter(jax.random.split(key, 48))
    f32 = jnp.float32
    D, F, G, N, P = D_MODEL, D_FF, SSM_GROUPS, SSM_STATE, SSM_GROUP
    NA, NB = N_A_LAYERS, N_B_LAYERS

    def nrm(shape, scale):
        return jax.random.normal(next(ks), shape, f32) * scale

    def gain(shape):
        return 1.0 + 0.02 * jax.random.normal(next(ks), shape, f32)

    x = jax.random.normal(next(ks), (BATCH, SEQ, D), f32)
    c = jax.random.normal(next(ks), (BATCH, D), f32)
    offsets = jax.random.randint(next(ks), (BATCH, 1), 0, 4096, dtype=jnp.int32)
    positions = offsets + jnp.arange(SEQ, dtype=jnp.int32)[None, :]

    n_idx = jnp.arange(N, dtype=f32)
    inputs = {
        "x": x, "c": c, "positions": positions,
        "ada_w": nrm((DEPTH, D, 6 * D), 0.5 * D ** -0.5),
        "ada_b": nrm((DEPTH, 6 * D), 0.02),
        "norm1_g": gain((DEPTH, D)),
        "norm2_g": gain((DEPTH, D)),
        "ffn_w_gate": nrm((DEPTH, D, F), D ** -0.5),
        "ffn_w_up": nrm((DEPTH, D, F), D ** -0.5),
        "ffn_w_down": nrm((DEPTH, F, D), F ** -0.5),
        "s5_lam_re": -0.5 + 0.01 * jax.random.normal(next(ks), (NA, G, N), f32),
        "s5_lam_im": math.pi * n_idx[None, None, :] + 0.01 * jax.random.normal(next(ks), (NA, G, N), f32),
        "s5_log_dt": jax.random.uniform(next(ks), (NA, G), f32, math.log(DT_MIN), math.log(DT_MAX)),
        "s5_b_re": nrm((NA, G, N, P), P ** -0.5),
        "s5_b_im": nrm((NA, G, N, P), P ** -0.5),
        "s5_c_re": nrm((NA, G, P, N), N ** -0.5),
        "s5_c_im": nrm((NA, G, P, N), N ** -0.5),
        "s5_d": nrm((NA, D), 1.0),
        "s5_w_glu": nrm((NA, D, D), D ** -0.5),
        "s5_b_glu": nrm((NA, D), 0.02),
        "kv_ada_w": nrm((D, 2 * D), 0.5 * D ** -0.5),
        "kv_ada_b": nrm((2 * D,), 0.02),
        "kv_norm_g": gain((D,)),
        "w_kv_a": nrm((D, KV_LORA_RANK + QK_ROPE_DIM), D ** -0.5),
        "kv_a_norm_g": gain((KV_LORA_RANK,)),
        "w_kv_b": nrm((KV_LORA_RANK, N_HEADS * (QK_NOPE_DIM + V_HEAD_DIM)), KV_LORA_RANK ** -0.5),
        "k_nope_norm_g": gain((QK_NOPE_DIM,)),
        "k_rope_norm_g": gain((QK_ROPE_DIM,)),
        "mla_w_dq": nrm((NB, D, Q_LORA_RANK), D ** -0.5),
        "mla_q_norm_g": gain((NB, Q_LORA_RANK)),
        "mla_w_uq": nrm((NB, Q_LORA_RANK, N_HEADS * (QK_NOPE_DIM + QK_ROPE_DIM)), Q_LORA_RANK ** -0.5),
        "mla_q_nope_norm_g": gain((NB, QK_NOPE_DIM)),
        "mla_q_rope_norm_g": gain((NB, QK_ROPE_DIM)),
        "mla_w_o": nrm((NB, N_HEADS * V_HEAD_DIM, D), (N_HEADS * V_HEAD_DIM) ** -0.5),
    }
    return inputs


def _fwd_reference(x, c, positions, ada_w, ada_b, norm1_g, norm2_g, ffn_w_gate, ffn_w_up, ffn_w_down,
              s5_lam_re, s5_lam_im, s5_log_dt, s5_b_re, s5_b_im, s5_c_re, s5_c_im, s5_d, s5_w_glu, s5_b_glu,
              kv_ada_w, kv_ada_b, kv_norm_g, w_kv_a, kv_a_norm_g, w_kv_b, k_nope_norm_g, k_rope_norm_g,
              mla_w_dq, mla_q_norm_g, mla_w_uq, mla_q_nope_norm_g, mla_q_rope_norm_g, mla_w_o):
    cos, sin = rope_cos_sin(positions)
    c_act = jax.nn.silu(c)
    k_nope = k_rope = v = None
    for l in range(DEPTH):
        shift1, scale1, gate1, shift2, scale2, gate2 = jnp.split(c_act @ ada_w[l] + ada_b[l], 6, axis=-1)
        if l == N_A_LAYERS:
            k_shift, k_scale = jnp.split(c_act @ kv_ada_w + kv_ada_b, 2, axis=-1)
            hk = modulate(rms_norm(x, kv_norm_g), k_shift, k_scale)
            k_nope, k_rope, v = shared_mla_kv(hk, cos, sin, w_kv_a, kv_a_norm_g, w_kv_b,
                                              k_nope_norm_g, k_rope_norm_g)
        h = modulate(rms_norm(x, norm1_g[l]), shift1, scale1)
        if l < N_A_LAYERS:
            mix = s5_mixer(h, s5_lam_re[l], s5_lam_im[l], s5_log_dt[l], s5_b_re[l], s5_b_im[l],
                           s5_c_re[l], s5_c_im[l], s5_d[l], s5_w_glu[l], s5_b_glu[l])
        else:
            j = l - N_A_LAYERS
            mix = mla_attention(h, cos, sin, k_nope, k_rope, v, mla_w_dq[j], mla_q_norm_g[j], mla_w_uq[j],
                                mla_q_nope_norm_g[j], mla_q_rope_norm_g[j], mla_w_o[j])
        x = x + gate1[:, None, :] * mix
        h = modulate(rms_norm(x, norm2_g[l]), shift2, scale2)
        x = x + gate2[:, None, :] * swiglu(h, ffn_w_gate[l], ffn_w_up[l], ffn_w_down[l])
    return x


import jax as _jax
import jax.numpy as _jnp

TWIN_FORMAT = 'train_step'
FWD_PARAMS = ['x', 'c', 'positions', 'ada_w', 'ada_b', 'norm1_g', 'norm2_g', 'ffn_w_gate', 'ffn_w_up', 'ffn_w_down', 's5_lam_re', 's5_lam_im', 's5_log_dt', 's5_b_re', 's5_b_im', 's5_c_re', 's5_c_im', 's5_d', 's5_w_glu', 's5_b_glu', 'kv_ada_w', 'kv_ada_b', 'kv_norm_g', 'w_kv_a', 'kv_a_norm_g', 'w_kv_b', 'k_nope_norm_g', 'k_rope_norm_g', 'mla_w_dq', 'mla_q_norm_g', 'mla_w_uq', 'mla_q_nope_norm_g', 'mla_q_rope_norm_g', 'mla_w_o']
TWIN_WEIGHTS = ['ada_w', 'ada_b', 'norm1_g', 'norm2_g', 'ffn_w_gate', 'ffn_w_up', 'ffn_w_down', 's5_lam_re', 's5_lam_im', 's5_log_dt', 's5_b_re', 's5_b_im', 's5_c_re', 's5_c_im', 's5_d', 's5_w_glu', 's5_b_glu', 'kv_ada_w', 'kv_ada_b', 'kv_norm_g', 'w_kv_a', 'kv_a_norm_g', 'w_kv_b', 'k_nope_norm_g', 'k_rope_norm_g', 'mla_w_dq', 'mla_q_norm_g', 'mla_w_uq', 'mla_q_nope_norm_g', 'mla_q_rope_norm_g', 'mla_w_o']
TWIN_DIFF_INPUT = 'x'
TWIN_INPUTS = ['x', 'c', 'positions', 'ada_w', 'ada_b', 'norm1_g', 'norm2_g', 'ffn_w_gate', 'ffn_w_up', 'ffn_w_down', 's5_lam_re', 's5_lam_im', 's5_log_dt', 's5_b_re', 's5_b_im', 's5_c_re', 's5_c_im', 's5_d', 's5_w_glu', 's5_b_glu', 'kv_ada_w', 'kv_ada_b', 'kv_norm_g', 'w_kv_a', 'kv_a_norm_g', 'w_kv_b', 'k_nope_norm_g', 'k_rope_norm_g', 'mla_w_dq', 'mla_q_norm_g', 'mla_w_uq', 'mla_q_nope_norm_g', 'mla_q_rope_norm_g', 'mla_w_o', 'loss_target', 'm_ada_w', 'm_ada_b', 'm_norm1_g', 'm_norm2_g', 'm_ffn_w_gate', 'm_ffn_w_up', 'm_ffn_w_down', 'm_s5_lam_re', 'm_s5_lam_im', 'm_s5_log_dt', 'm_s5_b_re', 'm_s5_b_im', 'm_s5_c_re', 'm_s5_c_im', 'm_s5_d', 'm_s5_w_glu', 'm_s5_b_glu', 'm_kv_ada_w', 'm_kv_ada_b', 'm_kv_norm_g', 'm_w_kv_a', 'm_kv_a_norm_g', 'm_w_kv_b', 'm_k_nope_norm_g', 'm_k_rope_norm_g', 'm_mla_w_dq', 'm_mla_q_norm_g', 'm_mla_w_uq', 'm_mla_q_nope_norm_g', 'm_mla_q_rope_norm_g', 'm_mla_w_o', 'v_ada_w', 'v_ada_b', 'v_norm1_g', 'v_norm2_g', 'v_ffn_w_gate', 'v_ffn_w_up', 'v_ffn_w_down', 'v_s5_lam_re', 'v_s5_lam_im', 'v_s5_log_dt', 'v_s5_b_re', 'v_s5_b_im', 'v_s5_c_re', 'v_s5_c_im', 'v_s5_d', 'v_s5_w_glu', 'v_s5_b_glu', 'v_kv_ada_w', 'v_kv_ada_b', 'v_kv_norm_g', 'v_w_kv_a', 'v_kv_a_norm_g', 'v_w_kv_b', 'v_k_nope_norm_g', 'v_k_rope_norm_g', 'v_mla_w_dq', 'v_mla_q_norm_g', 'v_mla_w_uq', 'v_mla_q_nope_norm_g', 'v_mla_q_rope_norm_g', 'v_mla_w_o']
TWIN_OUTPUTS = ['loss', 'grad_x', 'grad_ada_w', 'grad_ada_b', 'grad_norm1_g', 'grad_norm2_g', 'grad_ffn_w_gate', 'grad_ffn_w_up', 'grad_ffn_w_down', 'grad_s5_lam_re', 'grad_s5_lam_im', 'grad_s5_log_dt', 'grad_s5_b_re', 'grad_s5_b_im', 'grad_s5_c_re', 'grad_s5_c_im', 'grad_s5_d', 'grad_s5_w_glu', 'grad_s5_b_glu', 'grad_kv_ada_w', 'grad_kv_ada_b', 'grad_kv_norm_g', 'grad_w_kv_a', 'grad_kv_a_norm_g', 'grad_w_kv_b', 'grad_k_nope_norm_g', 'grad_k_rope_norm_g', 'grad_mla_w_dq', 'grad_mla_q_norm_g', 'grad_mla_w_uq', 'grad_mla_q_nope_norm_g', 'grad_mla_q_rope_norm_g', 'grad_mla_w_o', 'delta_ada_w', 'delta_ada_b', 'delta_norm1_g', 'delta_norm2_g', 'delta_ffn_w_gate', 'delta_ffn_w_up', 'delta_ffn_w_down', 'delta_s5_lam_re', 'delta_s5_lam_im', 'delta_s5_log_dt', 'delta_s5_b_re', 'delta_s5_b_im', 'delta_s5_c_re', 'delta_s5_c_im', 'delta_s5_d', 'delta_s5_w_glu', 'delta_s5_b_glu', 'delta_kv_ada_w', 'delta_kv_ada_b', 'delta_kv_norm_g', 'delta_w_kv_a', 'delta_kv_a_norm_g', 'delta_w_kv_b', 'delta_k_nope_norm_g', 'delta_k_rope_norm_g', 'delta_mla_w_dq', 'delta_mla_q_norm_g', 'delta_mla_w_uq', 'delta_mla_q_nope_norm_g', 'delta_mla_q_rope_norm_g', 'delta_mla_w_o', 'new_m_ada_w', 'new_m_ada_b', 'new_m_norm1_g', 'new_m_norm2_g', 'new_m_ffn_w_gate', 'new_m_ffn_w_up', 'new_m_ffn_w_down', 'new_m_s5_lam_re', 'new_m_s5_lam_im', 'new_m_s5_log_dt', 'new_m_s5_b_re', 'new_m_s5_b_im', 'new_m_s5_c_re', 'new_m_s5_c_im', 'new_m_s5_d', 'new_m_s5_w_glu', 'new_m_s5_b_glu', 'new_m_kv_ada_w', 'new_m_kv_ada_b', 'new_m_kv_norm_g', 'new_m_w_kv_a', 'new_m_kv_a_norm_g', 'new_m_w_kv_b', 'new_m_k_nope_norm_g', 'new_m_k_rope_norm_g', 'new_m_mla_w_dq', 'new_m_mla_q_norm_g', 'new_m_mla_w_uq', 'new_m_mla_q_nope_norm_g', 'new_m_mla_q_rope_norm_g', 'new_m_mla_w_o', 'new_v_ada_w', 'new_v_ada_b', 'new_v_norm1_g', 'new_v_norm2_g', 'new_v_ffn_w_gate', 'new_v_ffn_w_up', 'new_v_ffn_w_down', 'new_v_s5_lam_re', 'new_v_s5_lam_im', 'new_v_s5_log_dt', 'new_v_s5_b_re', 'new_v_s5_b_im', 'new_v_s5_c_re', 'new_v_s5_c_im', 'new_v_s5_d', 'new_v_s5_w_glu', 'new_v_s5_b_glu', 'new_v_kv_ada_w', 'new_v_kv_ada_b', 'new_v_kv_norm_g', 'new_v_w_kv_a', 'new_v_kv_a_norm_g', 'new_v_w_kv_b', 'new_v_k_nope_norm_g', 'new_v_k_rope_norm_g', 'new_v_mla_w_dq', 'new_v_mla_q_norm_g', 'new_v_mla_w_uq', 'new_v_mla_q_nope_norm_g', 'new_v_mla_q_rope_norm_g', 'new_v_mla_w_o']
TWIN_LEAF_KINDS = {'loss': 'loss', 'grad_x': 'grad_x', 'grad_ada_w': 'grad_w', 'grad_ada_b': 'grad_w', 'grad_norm1_g': 'grad_w', 'grad_norm2_g': 'grad_w', 'grad_ffn_w_gate': 'grad_w', 'grad_ffn_w_up': 'grad_w', 'grad_ffn_w_down': 'grad_w', 'grad_s5_lam_re': 'grad_w', 'grad_s5_lam_im': 'grad_w', 'grad_s5_log_dt': 'grad_w', 'grad_s5_b_re': 'grad_w', 'grad_s5_b_im': 'grad_w', 'grad_s5_c_re': 'grad_w', 'grad_s5_c_im': 'grad_w', 'grad_s5_d': 'grad_w', 'grad_s5_w_glu': 'grad_w', 'grad_s5_b_glu': 'grad_w', 'grad_kv_ada_w': 'grad_w', 'grad_kv_ada_b': 'grad_w', 'grad_kv_norm_g': 'grad_w', 'grad_w_kv_a': 'grad_w', 'grad_kv_a_norm_g': 'grad_w', 'grad_w_kv_b': 'grad_w', 'grad_k_nope_norm_g': 'grad_w', 'grad_k_rope_norm_g': 'grad_w', 'grad_mla_w_dq': 'grad_w', 'grad_mla_q_norm_g': 'grad_w', 'grad_mla_w_uq': 'grad_w', 'grad_mla_q_nope_norm_g': 'grad_w', 'grad_mla_q_rope_norm_g': 'grad_w', 'grad_mla_w_o': 'grad_w', 'delta_ada_w': 'delta_w', 'delta_ada_b': 'delta_w', 'delta_norm1_g': 'delta_w', 'delta_norm2_g': 'delta_w', 'delta_ffn_w_gate': 'delta_w', 'delta_ffn_w_up': 'delta_w', 'delta_ffn_w_down': 'delta_w', 'delta_s5_lam_re': 'delta_w', 'delta_s5_lam_im': 'delta_w', 'delta_s5_log_dt': 'delta_w', 'delta_s5_b_re': 'delta_w', 'delta_s5_b_im': 'delta_w', 'delta_s5_c_re': 'delta_w', 'delta_s5_c_im': 'delta_w', 'delta_s5_d': 'delta_w', 'delta_s5_w_glu': 'delta_w', 'delta_s5_b_glu': 'delta_w', 'delta_kv_ada_w': 'delta_w', 'delta_kv_ada_b': 'delta_w', 'delta_kv_norm_g': 'delta_w', 'delta_w_kv_a': 'delta_w', 'delta_kv_a_norm_g': 'delta_w', 'delta_w_kv_b': 'delta_w', 'delta_k_nope_norm_g': 'delta_w', 'delta_k_rope_norm_g': 'delta_w', 'delta_mla_w_dq': 'delta_w', 'delta_mla_q_norm_g': 'delta_w', 'delta_mla_w_uq': 'delta_w', 'delta_mla_q_nope_norm_g': 'delta_w', 'delta_mla_q_rope_norm_g': 'delta_w', 'delta_mla_w_o': 'delta_w', 'new_m_ada_w': 'new_m', 'new_m_ada_b': 'new_m', 'new_m_norm1_g': 'new_m', 'new_m_norm2_g': 'new_m', 'new_m_ffn_w_gate': 'new_m', 'new_m_ffn_w_up': 'new_m', 'new_m_ffn_w_down': 'new_m', 'new_m_s5_lam_re': 'new_m', 'new_m_s5_lam_im': 'new_m', 'new_m_s5_log_dt': 'new_m', 'new_m_s5_b_re': 'new_m', 'new_m_s5_b_im': 'new_m', 'new_m_s5_c_re': 'new_m', 'new_m_s5_c_im': 'new_m', 'new_m_s5_d': 'new_m', 'new_m_s5_w_glu': 'new_m', 'new_m_s5_b_glu': 'new_m', 'new_m_kv_ada_w': 'new_m', 'new_m_kv_ada_b': 'new_m', 'new_m_kv_norm_g': 'new_m', 'new_m_w_kv_a': 'new_m', 'new_m_kv_a_norm_g': 'new_m', 'new_m_w_kv_b': 'new_m', 'new_m_k_nope_norm_g': 'new_m', 'new_m_k_rope_norm_g': 'new_m', 'new_m_mla_w_dq': 'new_m', 'new_m_mla_q_norm_g': 'new_m', 'new_m_mla_w_uq': 'new_m', 'new_m_mla_q_nope_norm_g': 'new_m', 'new_m_mla_q_rope_norm_g': 'new_m', 'new_m_mla_w_o': 'new_m', 'new_v_ada_w': 'new_v', 'new_v_ada_b': 'new_v', 'new_v_norm1_g': 'new_v', 'new_v_norm2_g': 'new_v', 'new_v_ffn_w_gate': 'new_v', 'new_v_ffn_w_up': 'new_v', 'new_v_ffn_w_down': 'new_v', 'new_v_s5_lam_re': 'new_v', 'new_v_s5_lam_im': 'new_v', 'new_v_s5_log_dt': 'new_v', 'new_v_s5_b_re': 'new_v', 'new_v_s5_b_im': 'new_v', 'new_v_s5_c_re': 'new_v', 'new_v_s5_c_im': 'new_v', 'new_v_s5_d': 'new_v', 'new_v_s5_w_glu': 'new_v', 'new_v_s5_b_glu': 'new_v', 'new_v_kv_ada_w': 'new_v', 'new_v_kv_ada_b': 'new_v', 'new_v_kv_norm_g': 'new_v', 'new_v_w_kv_a': 'new_v', 'new_v_kv_a_norm_g': 'new_v', 'new_v_w_kv_b': 'new_v', 'new_v_k_nope_norm_g': 'new_v', 'new_v_k_rope_norm_g': 'new_v', 'new_v_mla_w_dq': 'new_v', 'new_v_mla_q_norm_g': 'new_v', 'new_v_mla_w_uq': 'new_v', 'new_v_mla_q_nope_norm_g': 'new_v', 'new_v_mla_q_rope_norm_g': 'new_v', 'new_v_mla_w_o': 'new_v'}


def _forward(args):
    return _fwd_reference(*[args[k] for k in FWD_PARAMS])


def _output_shape():
    out = _jax.eval_shape(lambda: _forward(_fwd_setup_inputs(0)))
    return out.shape, out.dtype

N_MICROBATCH = 1
ADAM_LR = 0.001
ADAM_B1 = 0.9
ADAM_B2 = 0.999
ADAM_EPS = 1e-08
ADAM_WD = 0.01
ADAM_STEP = 10
PER_EXAMPLE_BATCH_AXIS = {'x': 0, 'c': 0, 'positions': 0, 'loss_target': 0}
SHARED_INPUTS = []
_WEIGHT_DTYPES = {'ada_w': _jnp.float32, 'ada_b': _jnp.float32, 'norm1_g': _jnp.float32, 'norm2_g': _jnp.float32, 'ffn_w_gate': _jnp.float32, 'ffn_w_up': _jnp.float32, 'ffn_w_down': _jnp.float32, 's5_lam_re': _jnp.float32, 's5_lam_im': _jnp.float32, 's5_log_dt': _jnp.float32, 's5_b_re': _jnp.float32, 's5_b_im': _jnp.float32, 's5_c_re': _jnp.float32, 's5_c_im': _jnp.float32, 's5_d': _jnp.float32, 's5_w_glu': _jnp.float32, 's5_b_glu': _jnp.float32, 'kv_ada_w': _jnp.float32, 'kv_ada_b': _jnp.float32, 'kv_norm_g': _jnp.float32, 'w_kv_a': _jnp.float32, 'kv_a_norm_g': _jnp.float32, 'w_kv_b': _jnp.float32, 'k_nope_norm_g': _jnp.float32, 'k_rope_norm_g': _jnp.float32, 'mla_w_dq': _jnp.float32, 'mla_q_norm_g': _jnp.float32, 'mla_w_uq': _jnp.float32, 'mla_q_nope_norm_g': _jnp.float32, 'mla_q_rope_norm_g': _jnp.float32, 'mla_w_o': _jnp.float32}
MOMENT_SCALE = {'ada_w': 1.379591e+00, 'ada_b': 3.270729e+00, 'norm1_g': 1.472465e+00, 'norm2_g': 3.469978e+00, 'ffn_w_gate': 7.569298e-02, 'ffn_w_up': 6.471723e-02, 'ffn_w_down': 1.020358e-01, 's5_lam_re': 5.172014e-02, 's5_lam_im': 4.443197e-02, 's5_log_dt': 3.277676e+00, 's5_b_re': 2.643541e-02, 's5_b_im': 2.544768e-02, 's5_c_re': 5.223592e-02, 's5_c_im': 5.218685e-02, 's5_d': 1.796980e+00, 's5_w_glu': 3.594792e-01, 's5_b_glu': 8.665701e-01, 'kv_ada_w': 9.861164e-01, 'kv_ada_b': 1.856876e+00, 'kv_norm_g': 4.810481e-01, 'w_kv_a': 1.250653e+00, 'kv_a_norm_g': 2.890517e+00, 'w_kv_b': 3.374582e-01, 'k_nope_norm_g': 2.565305e-01, 'k_rope_norm_g': 1.837384e-01, 'mla_w_dq': 2.664536e-02, 'mla_q_norm_g': 2.802568e-02, 'mla_w_uq': 1.095091e-02, 'mla_q_nope_norm_g': 1.312520e-01, 'mla_q_rope_norm_g': 9.469945e-02, 'mla_w_o': 2.885566e-01}


def _to_microbatches(a, axis):
    t = _jnp.moveaxis(a, axis, 0)
    t = t.reshape((N_MICROBATCH, t.shape[0] // N_MICROBATCH) + t.shape[1:])
    return _jnp.moveaxis(t, 1, axis + 1)


def setup_inputs(seed: int = 0) -> dict:
    inp = _fwd_setup_inputs(seed)
    key = _jax.random.fold_in(_jax.random.key(seed), 7919)
    shape, _ = _output_shape()
    out = dict(inp)
    out["loss_target"] = _jax.random.normal(_jax.random.fold_in(key, 0), shape, _jnp.float32)
    for i, name in enumerate(TWIN_WEIGHTS):
        w = inp[name].astype(_jnp.float32)
        if MOMENT_SCALE is None:
            s = _jnp.sqrt(_jnp.mean(_jnp.square(w)) + 1e-30)
        else:
            s = MOMENT_SCALE[name]
        km, kv = _jax.random.split(_jax.random.fold_in(key, i + 1))
        out[name] = w
        out["m_" + name] = s * _jax.random.normal(km, w.shape, _jnp.float32)
        out["v_" + name] = (s * s) * _jax.random.uniform(kv, w.shape, _jnp.float32, 0.5, 1.5)
    if N_MICROBATCH > 1:
        for name, axis in PER_EXAMPLE_BATCH_AXIS.items():
            out[name] = _to_microbatches(out[name], axis)
    return {'x': out['x'], 'c': out['c'], 'positions': out['positions'], 'ada_w': out['ada_w'], 'ada_b': out['ada_b'], 'norm1_g': out['norm1_g'], 'norm2_g': out['norm2_g'], 'ffn_w_gate': out['ffn_w_gate'], 'ffn_w_up': out['ffn_w_up'], 'ffn_w_down': out['ffn_w_down'], 's5_lam_re': out['s5_lam_re'], 's5_lam_im': out['s5_lam_im'], 's5_log_dt': out['s5_log_dt'], 's5_b_re': out['s5_b_re'], 's5_b_im': out['s5_b_im'], 's5_c_re': out['s5_c_re'], 's5_c_im': out['s5_c_im'], 's5_d': out['s5_d'], 's5_w_glu': out['s5_w_glu'], 's5_b_glu': out['s5_b_glu'], 'kv_ada_w': out['kv_ada_w'], 'kv_ada_b': out['kv_ada_b'], 'kv_norm_g': out['kv_norm_g'], 'w_kv_a': out['w_kv_a'], 'kv_a_norm_g': out['kv_a_norm_g'], 'w_kv_b': out['w_kv_b'], 'k_nope_norm_g': out['k_nope_norm_g'], 'k_rope_norm_g': out['k_rope_norm_g'], 'mla_w_dq': out['mla_w_dq'], 'mla_q_norm_g': out['mla_q_norm_g'], 'mla_w_uq': out['mla_w_uq'], 'mla_q_nope_norm_g': out['mla_q_nope_norm_g'], 'mla_q_rope_norm_g': out['mla_q_rope_norm_g'], 'mla_w_o': out['mla_w_o'], 'loss_target': out['loss_target'], 'm_ada_w': out['m_ada_w'], 'm_ada_b': out['m_ada_b'], 'm_norm1_g': out['m_norm1_g'], 'm_norm2_g': out['m_norm2_g'], 'm_ffn_w_gate': out['m_ffn_w_gate'], 'm_ffn_w_up': out['m_ffn_w_up'], 'm_ffn_w_down': out['m_ffn_w_down'], 'm_s5_lam_re': out['m_s5_lam_re'], 'm_s5_lam_im': out['m_s5_lam_im'], 'm_s5_log_dt': out['m_s5_log_dt'], 'm_s5_b_re': out['m_s5_b_re'], 'm_s5_b_im': out['m_s5_b_im'], 'm_s5_c_re': out['m_s5_c_re'], 'm_s5_c_im': out['m_s5_c_im'], 'm_s5_d': out['m_s5_d'], 'm_s5_w_glu': out['m_s5_w_glu'], 'm_s5_b_glu': out['m_s5_b_glu'], 'm_kv_ada_w': out['m_kv_ada_w'], 'm_kv_ada_b': out['m_kv_ada_b'], 'm_kv_norm_g': out['m_kv_norm_g'], 'm_w_kv_a': out['m_w_kv_a'], 'm_kv_a_norm_g': out['m_kv_a_norm_g'], 'm_w_kv_b': out['m_w_kv_b'], 'm_k_nope_norm_g': out['m_k_nope_norm_g'], 'm_k_rope_norm_g': out['m_k_rope_norm_g'], 'm_mla_w_dq': out['m_mla_w_dq'], 'm_mla_q_norm_g': out['m_mla_q_norm_g'], 'm_mla_w_uq': out['m_mla_w_uq'], 'm_mla_q_nope_norm_g': out['m_mla_q_nope_norm_g'], 'm_mla_q_rope_norm_g': out['m_mla_q_rope_norm_g'], 'm_mla_w_o': out['m_mla_w_o'], 'v_ada_w': out['v_ada_w'], 'v_ada_b': out['v_ada_b'], 'v_norm1_g': out['v_norm1_g'], 'v_norm2_g': out['v_norm2_g'], 'v_ffn_w_gate': out['v_ffn_w_gate'], 'v_ffn_w_up': out['v_ffn_w_up'], 'v_ffn_w_down': out['v_ffn_w_down'], 'v_s5_lam_re': out['v_s5_lam_re'], 'v_s5_lam_im': out['v_s5_lam_im'], 'v_s5_log_dt': out['v_s5_log_dt'], 'v_s5_b_re': out['v_s5_b_re'], 'v_s5_b_im': out['v_s5_b_im'], 'v_s5_c_re': out['v_s5_c_re'], 'v_s5_c_im': out['v_s5_c_im'], 'v_s5_d': out['v_s5_d'], 'v_s5_w_glu': out['v_s5_w_glu'], 'v_s5_b_glu': out['v_s5_b_glu'], 'v_kv_ada_w': out['v_kv_ada_w'], 'v_kv_ada_b': out['v_kv_ada_b'], 'v_kv_norm_g': out['v_kv_norm_g'], 'v_w_kv_a': out['v_w_kv_a'], 'v_kv_a_norm_g': out['v_kv_a_norm_g'], 'v_w_kv_b': out['v_w_kv_b'], 'v_k_nope_norm_g': out['v_k_nope_norm_g'], 'v_k_rope_norm_g': out['v_k_rope_norm_g'], 'v_mla_w_dq': out['v_mla_w_dq'], 'v_mla_q_norm_g': out['v_mla_q_norm_g'], 'v_mla_w_uq': out['v_mla_w_uq'], 'v_mla_q_nope_norm_g': out['v_mla_q_nope_norm_g'], 'v_mla_q_rope_norm_g': out['v_mla_q_rope_norm_g'], 'v_mla_w_o': out['v_mla_w_o']}


def _loss(weights, diff, rest, loss_target):
    with _jax.named_scope("forward"):
        args = {**rest, TWIN_DIFF_INPUT: diff, **{k: w.astype(_WEIGHT_DTYPES[k]) for k, w in weights.items()}}
        y = _forward(args)
    with _jax.named_scope("loss_head"):
        err = _jnp.square(y.astype(_jnp.float32) - loss_target)
        return 0.5 * _jnp.sum(_jnp.mean(err, axis=-1)) if err.ndim else 0.5 * err


def _adamw(w, g, m, v):
    m = ADAM_B1 * m + (1.0 - ADAM_B1) * g
    v = ADAM_B2 * v + (1.0 - ADAM_B2) * _jnp.square(g)
    m_hat = m / (1.0 - ADAM_B1 ** ADAM_STEP)
    v_hat = v / (1.0 - ADAM_B2 ** ADAM_STEP)
    delta = -ADAM_LR * (m_hat / (_jnp.sqrt(v_hat) + ADAM_EPS) + ADAM_WD * w)
    return delta, m, v


def reference(x, c, positions, ada_w, ada_b, norm1_g, norm2_g, ffn_w_gate, ffn_w_up, ffn_w_down, s5_lam_re, s5_lam_im, s5_log_dt, s5_b_re, s5_b_im, s5_c_re, s5_c_im, s5_d, s5_w_glu, s5_b_glu, kv_ada_w, kv_ada_b, kv_norm_g, w_kv_a, kv_a_norm_g, w_kv_b, k_nope_norm_g, k_rope_norm_g, mla_w_dq, mla_q_norm_g, mla_w_uq, mla_q_nope_norm_g, mla_q_rope_norm_g, mla_w_o, loss_target, m_ada_w, m_ada_b, m_norm1_g, m_norm2_g, m_ffn_w_gate, m_ffn_w_up, m_ffn_w_down, m_s5_lam_re, m_s5_lam_im, m_s5_log_dt, m_s5_b_re, m_s5_b_im, m_s5_c_re, m_s5_c_im, m_s5_d, m_s5_w_glu, m_s5_b_glu, m_kv_ada_w, m_kv_ada_b, m_kv_norm_g, m_w_kv_a, m_kv_a_norm_g, m_w_kv_b, m_k_nope_norm_g, m_k_rope_norm_g, m_mla_w_dq, m_mla_q_norm_g, m_mla_w_uq, m_mla_q_nope_norm_g, m_mla_q_rope_norm_g, m_mla_w_o, v_ada_w, v_ada_b, v_norm1_g, v_norm2_g, v_ffn_w_gate, v_ffn_w_up, v_ffn_w_down, v_s5_lam_re, v_s5_lam_im, v_s5_log_dt, v_s5_b_re, v_s5_b_im, v_s5_c_re, v_s5_c_im, v_s5_d, v_s5_w_glu, v_s5_b_glu, v_kv_ada_w, v_kv_ada_b, v_kv_norm_g, v_w_kv_a, v_kv_a_norm_g, v_w_kv_b, v_k_nope_norm_g, v_k_rope_norm_g, v_mla_w_dq, v_mla_q_norm_g, v_mla_w_uq, v_mla_q_nope_norm_g, v_mla_q_rope_norm_g, v_mla_w_o):
    given = dict(x=x, c=c, positions=positions, ada_w=ada_w, ada_b=ada_b, norm1_g=norm1_g, norm2_g=norm2_g, ffn_w_gate=ffn_w_gate, ffn_w_up=ffn_w_up, ffn_w_down=ffn_w_down, s5_lam_re=s5_lam_re, s5_lam_im=s5_lam_im, s5_log_dt=s5_log_dt, s5_b_re=s5_b_re, s5_b_im=s5_b_im, s5_c_re=s5_c_re, s5_c_im=s5_c_im, s5_d=s5_d, s5_w_glu=s5_w_glu, s5_b_glu=s5_b_glu, kv_ada_w=kv_ada_w, kv_ada_b=kv_ada_b, kv_norm_g=kv_norm_g, w_kv_a=w_kv_a, kv_a_norm_g=kv_a_norm_g, w_kv_b=w_kv_b, k_nope_norm_g=k_nope_norm_g, k_rope_norm_g=k_rope_norm_g, mla_w_dq=mla_w_dq, mla_q_norm_g=mla_q_norm_g, mla_w_uq=mla_w_uq, mla_q_nope_norm_g=mla_q_nope_norm_g, mla_q_rope_norm_g=mla_q_rope_norm_g, mla_w_o=mla_w_o, loss_target=loss_target, m_ada_w=m_ada_w, m_ada_b=m_ada_b, m_norm1_g=m_norm1_g, m_norm2_g=m_norm2_g, m_ffn_w_gate=m_ffn_w_gate, m_ffn_w_up=m_ffn_w_up, m_ffn_w_down=m_ffn_w_down, m_s5_lam_re=m_s5_lam_re, m_s5_lam_im=m_s5_lam_im, m_s5_log_dt=m_s5_log_dt, m_s5_b_re=m_s5_b_re, m_s5_b_im=m_s5_b_im, m_s5_c_re=m_s5_c_re, m_s5_c_im=m_s5_c_im, m_s5_d=m_s5_d, m_s5_w_glu=m_s5_w_glu, m_s5_b_glu=m_s5_b_glu, m_kv_ada_w=m_kv_ada_w, m_kv_ada_b=m_kv_ada_b, m_kv_norm_g=m_kv_norm_g, m_w_kv_a=m_w_kv_a, m_kv_a_norm_g=m_kv_a_norm_g, m_w_kv_b=m_w_kv_b, m_k_nope_norm_g=m_k_nope_norm_g, m_k_rope_norm_g=m_k_rope_norm_g, m_mla_w_dq=m_mla_w_dq, m_mla_q_norm_g=m_mla_q_norm_g, m_mla_w_uq=m_mla_w_uq, m_mla_q_nope_norm_g=m_mla_q_nope_norm_g, m_mla_q_rope_norm_g=m_mla_q_rope_norm_g, m_mla_w_o=m_mla_w_o, v_ada_w=v_ada_w, v_ada_b=v_ada_b, v_norm1_g=v_norm1_g, v_norm2_g=v_norm2_g, v_ffn_w_gate=v_ffn_w_gate, v_ffn_w_up=v_ffn_w_up, v_ffn_w_down=v_ffn_w_down, v_s5_lam_re=v_s5_lam_re, v_s5_lam_im=v_s5_lam_im, v_s5_log_dt=v_s5_log_dt, v_s5_b_re=v_s5_b_re, v_s5_b_im=v_s5_b_im, v_s5_c_re=v_s5_c_re, v_s5_c_im=v_s5_c_im, v_s5_d=v_s5_d, v_s5_w_glu=v_s5_w_glu, v_s5_b_glu=v_s5_b_glu, v_kv_ada_w=v_kv_ada_w, v_kv_ada_b=v_kv_ada_b, v_kv_norm_g=v_kv_norm_g, v_w_kv_a=v_w_kv_a, v_kv_a_norm_g=v_kv_a_norm_g, v_w_kv_b=v_w_kv_b, v_k_nope_norm_g=v_k_nope_norm_g, v_k_rope_norm_g=v_k_rope_norm_g, v_mla_w_dq=v_mla_w_dq, v_mla_q_norm_g=v_mla_q_norm_g, v_mla_w_uq=v_mla_w_uq, v_mla_q_nope_norm_g=v_mla_q_nope_norm_g, v_mla_q_rope_norm_g=v_mla_q_rope_norm_g, v_mla_w_o=v_mla_w_o)
    weights = {n: given[n] for n in TWIN_WEIGHTS}
    shared = {n: given[n] for n in SHARED_INPUTS}
    per_example = {n: given[n] for n in ['x', 'c', 'positions']}
    grad_fn = _jax.value_and_grad(_loss, argnums=(0, 1))

    def one_microbatch(ex, loss_target):
        ex = dict(ex)
        diff = ex.pop(TWIN_DIFF_INPUT)
        return grad_fn(weights, diff, {**shared, **ex}, loss_target)

    if N_MICROBATCH == 1:
        loss, (grad_w, grad_x) = one_microbatch(per_example, given["loss_target"])
    else:
        def body(carry, xs):
            loss_sum, grad_sum = carry
            l_k, (gw_k, gx_k) = one_microbatch(xs[0], xs[1])
            with _jax.named_scope("update"):
                return (loss_sum + l_k, _jax.tree.map(_jnp.add, grad_sum, gw_k)), gx_k

        init = (_jnp.zeros((), _jnp.float32), _jax.tree.map(_jnp.zeros_like, weights))
        (loss, grad_w), grad_x = _jax.lax.scan(body, init, (per_example, given["loss_target"]))
    with _jax.named_scope("update"):
        delta_w, new_m, new_v = {}, {}, {}
        for n in TWIN_WEIGHTS:
            delta_w[n], new_m[n], new_v[n] = _adamw(weights[n], grad_w[n], given["m_" + n], given["v_" + n])
    return (loss, grad_x, *[grad_w[n] for n in TWIN_WEIGHTS], *[delta_w[n] for n in TWIN_WEIGHTS],
            *[new_m[n] for n in TWIN_WEIGHTS], *[new_v[n] for n in TWIN_WEIGHTS])
```

```python
import functools
import math

import jax
import jax.numpy as jnp
from jax import lax
from jax.experimental import pallas as pl
from jax.experimental.pallas import tpu as pltpu

f32 = jnp.float32
bf16 = jnp.bfloat16
SDS = jax.ShapeDtypeStruct

EPS = 1e-6
CHUNK = 64
N_HEADS = 16
NOPE = 64
ROPE = 32
VDIM = 64
HEAD_PAD = 128
KV_RANK = 256
ROPE_THETA = 10000.0
ATTN_SCALE = 1.0 / math.sqrt(NOPE + ROPE)
SSM_GROUP = 16
SSM_STATE = 64
N_A = 2
DEPTH = 4
LANES = 128
SUBLANES = 8
SEGMENTS = 8
SLAB_CH = 128
SLAB_ST = 512
ADAM_LR, ADAM_B1, ADAM_B2, ADAM_EPS, ADAM_WD, ADAM_STEP = 0.001, 0.9, 0.999, 1e-08, 0.01, 10


def _pick(n, prefs=(512, 256, 128)):
    for p in prefs:
        if n % p == 0:
            return p
    return n


def _mm(a, b, *, ta=False, tb=False, out_dtype=f32, cast=True, name="mm"):
    if ta:
        K, M = a.shape
    else:
        M, K = a.shape
    if tb:
        N, K2 = b.shape
    else:
        K2, N = b.shape
    assert K == K2, (a.shape, b.shape, ta, tb)
    tm, tn, tk = _pick(M), _pick(N), _pick(K)
    nk = K // tk
    dims = (((0 if ta else 1,), (1 if tb else 0,)), ((), ()))

    def body(a_ref, b_ref, o_ref, acc):
        k = pl.program_id(2)

        @pl.when(k == 0)
        def _():
            acc[...] = jnp.zeros_like(acc)

        av, bv = a_ref[...], b_ref[...]
        if cast:
            av, bv = av.astype(bf16), bv.astype(bf16)
            acc[...] += lax.dot_general(av, bv, dims, preferred_element_type=f32)
        else:
            acc[...] += lax.dot_general(av, bv, dims, preferred_element_type=f32, precision=lax.Precision.HIGHEST)

        @pl.when(k == nk - 1)
        def _():
            o_ref[...] = acc[...].astype(o_ref.dtype)

    a_spec = pl.BlockSpec((tk, tm), lambda i, j, k: (k, i)) if ta else pl.BlockSpec((tm, tk), lambda i, j, k: (i, k))
    b_spec = pl.BlockSpec((tn, tk), lambda i, j, k: (j, k)) if tb else pl.BlockSpec((tk, tn), lambda i, j, k: (k, j))
    return pl.pallas_call(
        body, name=name, grid=(M // tm, N // tn, nk),
        in_specs=[a_spec, b_spec], out_specs=pl.BlockSpec((tm, tn), lambda i, j, k: (i, j)),
        out_shape=SDS((M, N), out_dtype), scratch_shapes=[pltpu.VMEM((tm, tn), f32)],
        compiler_params=pltpu.CompilerParams(dimension_semantics=("parallel", "parallel", "arbitrary")),
    )(a, b)


def _rowwise(fn, rows, consts, row_outs, acc_outs=(), *, tm=256, name="rowwise"):
    S = rows[0].shape[0]
    tm = min(tm, S)
    assert S % tm == 0, (S, tm)
    n_r, n_c, n_ro, n_ao = len(rows), len(consts), len(row_outs), len(acc_outs)

    def body(*refs):
        vals = [r[...] for r in refs[:n_r + n_c]]
        outs = fn(*vals)
        if not isinstance(outs, (tuple, list)):
            outs = (outs,)
        assert len(outs) == n_ro + n_ao, (name, len(outs), n_ro, n_ao)
        o_refs = refs[n_r + n_c:]
        for ref, val in zip(o_refs[:n_ro], outs[:n_ro]):
            ref[...] = val.astype(ref.dtype)
        if n_ao:
            @pl.when(pl.program_id(0) == 0)
            def _():
                for ref in o_refs[n_ro:]:
                    ref[...] = jnp.zeros_like(ref)
            for ref, val in zip(o_refs[n_ro:], outs[n_ro:]):
                ref[...] += jnp.broadcast_to(val, ref.shape).astype(f32)

    def full(shape):
        nd = len(shape)
        return pl.BlockSpec(tuple(shape), lambda i: (0,) * nd)

    in_specs = [pl.BlockSpec((tm, r.shape[1]), lambda i: (i, 0)) for r in rows] + [full(c.shape) for c in consts]
    out_specs = [pl.BlockSpec((tm, w), lambda i: (i, 0)) for (w, _) in row_outs] + [full(s) for s in acc_outs]
    out_shape = [SDS((S, w), dt) for (w, dt) in row_outs] + [SDS(tuple(s), f32) for s in acc_outs]
    res = pl.pallas_call(
        body, name=name, grid=(S // tm,), in_specs=in_specs, out_specs=out_specs, out_shape=out_shape,
        compiler_params=pltpu.CompilerParams(dimension_semantics=("arbitrary",)),
    )(*rows, *consts)
    return res


def _rowwise_bwd(f, rows, consts, cts, want, row_dtypes, *, adds=None, tm=256, name="rowwise_bwd"):
    n_r, n_c, n_ct = len(rows), len(consts), len(cts)
    adds = adds or {}
    add_keys = sorted(adds)
    add_rows = [adds[k] for k in add_keys]

    def fn(*args):
        r = [a.astype(f32) for a in args[:n_r]]
        ct = [a.astype(f32) for a in args[n_r:n_r + n_ct]]
        ad = args[n_r + n_ct:n_r + n_ct + len(add_rows)]
        c = list(args[n_r + n_ct + len(add_rows):])
        _, vjp = jax.vjp(f, *r, *c)
        g = vjp(ct[0] if n_ct == 1 else tuple(ct))
        d_rows = []
        for pos, i in enumerate(want):
            d = g[i]
            if pos in adds:
                d = d + ad[add_keys.index(pos)].astype(f32)
            d_rows.append(d)
        return (*d_rows, *g[n_r:])

    return _rowwise(fn, list(rows) + list(cts) + add_rows, list(consts),
                    [(rows[i].shape[1], dt) for i, dt in zip(want, row_dtypes)],
                    [c.shape for c in consts], tm=tm, name=name)


def _sum_lead(arr, out_dtype, name):
    n, R, C = arr.shape
    tm = _pick(R, (256, 128, 64, 32, 16, 8))

    def body(a_ref, o_ref):
        acc = a_ref[0].astype(f32)
        for q in range(1, n):
            acc = acc + a_ref[q].astype(f32)
        o_ref[...] = acc.astype(o_ref.dtype)

    return pl.pallas_call(
        body, name=name, grid=(R // tm,), in_specs=[pl.BlockSpec((n, tm, C), lambda i: (0, i, 0))],
        out_specs=pl.BlockSpec((tm, C), lambda i: (i, 0)), out_shape=SDS((R, C), out_dtype),
    )(arr)


_GROUPS = {
    "all": [(kx, ky, kc) for kx in (0, 1) for ky in (0, 1) for kc in (0, 1)][1:],
    "xy": [(0, 1, 0), (1, 0, 0), (1, 1, 0)],
    "c": [(0, 0, 1)],
}


def _group_pos(group, x, y, c):
    return {"all": 4 * x + 2 * y + c, "xy": 2 * x + y, "c": c}[group]


def _xchg(send, group, bcast, name):
    flips = _GROUPS[group]
    n = len(flips) + 1
    piece = send.shape if bcast else send.shape[1:]
    if not bcast:
        assert send.shape[0] == n

    def body(s_ref, r_ref, send_sems, recv_sems, own_sem):
        x, y, c = lax.axis_index("x"), lax.axis_index("y"), lax.axis_index("c")
        me = _group_pos(group, x, y, c)

        def src(p):
            return s_ref if bcast else s_ref.at[p]

        own = pltpu.make_async_copy(src(me), r_ref.at[me], own_sem)
        own.start()
        copies = []
        for k, (kx, ky, kc) in enumerate(flips):
            tx, ty, tc = x ^ kx, y ^ ky, c ^ kc
            cp = pltpu.make_async_remote_copy(
                src_ref=src(_group_pos(group, tx, ty, tc)), dst_ref=r_ref.at[me],
                send_sem=send_sems.at[k], recv_sem=recv_sems.at[k],
                device_id=(tx, ty, tc), device_id_type=pl.DeviceIdType.MESH)
            cp.start()
            copies.append(cp)
        for cp in copies:
            cp.wait()
        own.wait()

    return pl.pallas_call(
        body, name=name, out_shape=SDS((n,) + tuple(piece), send.dtype),
        in_specs=[pl.BlockSpec(memory_space=pl.ANY)], out_specs=pl.BlockSpec(memory_space=pl.ANY),
        scratch_shapes=[pltpu.SemaphoreType.DMA((n - 1,)), pltpu.SemaphoreType.DMA((n - 1,)), pltpu.SemaphoreType.DMA],
        compiler_params=pltpu.CompilerParams(has_side_effects=True),
    )(send)


def _rms(x, g):
    return x * lax.rsqrt(jnp.mean(x * x, axis=-1, keepdims=True) + EPS) * g


def _normmod(x, g, shift, scale):
    return _rms(x, g) * (1.0 + scale) + shift


def _lane(shape):
    return lax.broadcasted_iota(jnp.int32, shape, 1)


def _partner(x):
    lane = _lane(x.shape)
    lo = (lane >= NOPE) & (lane < NOPE + ROPE // 2)
    hi = (lane >= NOPE + ROPE // 2) & (lane < NOPE + ROPE)
    return jnp.where(lo, pltpu.roll(x, HEAD_PAD - ROPE // 2, 1), jnp.where(hi, pltpu.roll(x, ROPE // 2, 1), 0.0))


@jax.custom_vjp
def _rope(x, cf, ss):
    return x * cf + _partner(x) * ss


def _rope_fwd(x, cf, ss):
    return _rope(x, cf, ss), (cf, ss)


def _rope_bwd(res, dy):
    cf, ss = res
    return dy * cf + _partner(dy * ss), jnp.zeros_like(cf), jnp.zeros_like(ss)


_rope.defvjp(_rope_fwd, _rope_bwd)


def _head_norm(xh, gn, gr):
    lane = _lane(xh.shape)
    x2 = xh * xh
    ms_n = jnp.sum(jnp.where(lane < NOPE, x2, 0.0), axis=-1, keepdims=True) * (1.0 / NOPE)
    ms_r = jnp.sum(jnp.where((lane >= NOPE) & (lane < NOPE + ROPE), x2, 0.0), axis=-1, keepdims=True) * (1.0 / ROPE)
    return xh * (lax.rsqrt(ms_n + EPS) * gn + lax.rsqrt(ms_r + EPS) * gr)


def _q_heads(qraw, cf, ss, gn, gr):
    outs = []
    for h in range(N_HEADS):
        xh = qraw[:, h * HEAD_PAD:(h + 1) * HEAD_PAD]
        outs.append(_rope(_head_norm(xh, gn, gr), cf, ss))
    return jnp.concatenate(outs, axis=1)


def _k_heads(kraw, kr, gn):
    outs = []
    zero = jnp.zeros_like(gn)
    for h in range(N_HEADS):
        xh = kraw[:, h * HEAD_PAD:(h + 1) * HEAD_PAD]
        outs.append(_head_norm(xh, gn, zero) + kr)
    return jnp.concatenate(outs, axis=1)


def _kv_a_post(kva, cf, ss, ga, gr):
    ckv = _rms(kva[:, :KV_RANK], ga)
    kr = _rope(_head_norm(kva[:, KV_RANK:], jnp.zeros_like(gr), gr), cf, ss)
    return ckv, kr


def _cmul(ar, ai, br, bi):
    return ar * br - ai * bi, ar * bi + ai * br


def _s5_scan_fwd(u, wbr, wbi, ar, ai, init_r, init_i, wcr, wci, *, want_y, tb, name):
    S, D = u.shape
    ns = D // SLAB_CH
    seg = S // SEGMENTS
    tb = min(tb, seg)
    nb = seg // tb
    R = tb * SEGMENTS
    nt = (((1,), (0,)), ((), ()))

    def body(u_ref, wbr_ref, wbi_ref, ar_ref, ai_ref, ir_ref, ii_ref, wcr_ref, wci_ref, *rest):
        if want_y:
            y_ref, er_ref, ei_ref, bsr_ref, bsi_ref, cr, ci, bur, bui, sr, si = rest
        else:
            er_ref, ei_ref, bsr_ref, bsi_ref, cr, ci, bur, bui, sr, si = rest
        t = pl.program_id(1)

        @pl.when(t == 0)
        def _():
            cr[...] = ir_ref[0]
            ci[...] = ii_ref[0]

        bsr_ref[0, 0] = cr[...]
        bsi_ref[0, 0] = ci[...]
        ub = u_ref[...].astype(bf16)
        bur[...] = lax.dot_general(ub, wbr_ref[0].astype(bf16), nt, preferred_element_type=f32)
        bui[...] = lax.dot_general(ub, wbi_ref[0].astype(bf16), nt, preferred_element_type=f32)
        a_r = jnp.broadcast_to(ar_ref[0], (SEGMENTS, SLAB_ST))
        a_i = jnp.broadcast_to(ai_ref[0], (SEGMENTS, SLAB_ST))

        def step(tau, carry):
            c_r, c_i = carry
            r = pl.multiple_of(tau * SEGMENTS, SEGMENTS)
            p_r, p_i = _cmul(a_r, a_i, c_r, c_i)
            n_r = p_r + bur[pl.ds(r, SEGMENTS), :]
            n_i = p_i + bui[pl.ds(r, SEGMENTS), :]
            sr[pl.ds(r, SEGMENTS), :] = n_r
            si[pl.ds(r, SEGMENTS), :] = n_i
            return n_r, n_i

        c_r, c_i = lax.fori_loop(0, tb, step, (cr[...], ci[...]))
        cr[...] = c_r
        ci[...] = c_i
        if want_y:
            y_ref[...] = (lax.dot_general(sr[...].astype(bf16), wcr_ref[0].astype(bf16), nt, preferred_element_type=f32)
                          - lax.dot_general(si[...].astype(bf16), wci_ref[0].astype(bf16), nt, preferred_element_type=f32))

        @pl.when(t == nb - 1)
        def _():
            er_ref[0] = c_r
            ei_ref[0] = c_i

    slab3 = lambda s: pl.BlockSpec((1,) + s, lambda k, t: (k, 0, 0))
    in_specs = [pl.BlockSpec((R, SLAB_CH), lambda k, t: (t, k)),
                slab3((SLAB_CH, SLAB_ST)), slab3((SLAB_CH, SLAB_ST)), slab3((1, SLAB_ST)), slab3((1, SLAB_ST)),
                slab3((SEGMENTS, SLAB_ST)), slab3((SEGMENTS, SLAB_ST)), slab3((SLAB_ST, SLAB_CH)), slab3((SLAB_ST, SLAB_CH))]
    out_specs = [slab3((SEGMENTS, SLAB_ST)), slab3((SEGMENTS, SLAB_ST)),
                 pl.BlockSpec((1, 1, SEGMENTS, SLAB_ST), lambda k, t: (t, k, 0, 0)),
                 pl.BlockSpec((1, 1, SEGMENTS, SLAB_ST), lambda k, t: (t, k, 0, 0))]
    out_shape = [SDS((ns, SEGMENTS, SLAB_ST), f32)] * 2 + [SDS((nb, ns, SEGMENTS, SLAB_ST), f32)] * 2
    if want_y:
        out_specs = [pl.BlockSpec((R, SLAB_CH), lambda k, t: (t, k))] + out_specs
        out_shape = [SDS((S, D), f32)] + out_shape
    return pl.pallas_call(
        body, name=name, grid=(ns, nb), in_specs=in_specs, out_specs=out_specs, out_shape=out_shape,
        scratch_shapes=[pltpu.VMEM((SEGMENTS, SLAB_ST), f32)] * 2 + [pltpu.VMEM((R, SLAB_ST), f32)] * 4,
        compiler_params=pltpu.CompilerParams(dimension_semantics=("parallel", "arbitrary")),
    )(u, wbr, wbi, ar, ai, init_r, init_i, wcr, wci)


def _s5_scan_bwd(u, dy, wbr, wbi, ar, ai, bs_r, bs_i, ginit_r, ginit_i, wcr, wci, *, full, tb, name):
    S, D = dy.shape
    ns = D // SLAB_CH
    seg = S // SEGMENTS
    tb = min(tb, seg)
    nb = seg // tb
    R = tb * SEGMENTS
    nn = (((1,), (0,)), ((), ()))
    nt = (((1,), (1,)), ((), ()))
    tn = (((0,), (0,)), ((), ()))

    def body(*refs):
        if full:
            (u_ref, dy_ref, wbr_ref, wbi_ref, ar_ref, ai_ref, bsr_ref, bsi_ref, gir_ref, gii_ref, wcr_ref, wci_ref,
             du_ref, dwbr_ref, dwbi_ref, dwcr_ref, dwci_ref, dar_ref, dai_ref, gfr_ref, gfi_ref,
             gr_c, gi_c, bur, bui, sr, si, gsr, gsi) = refs
        else:
            (dy_ref, ar_ref, ai_ref, gir_ref, gii_ref, wcr_ref, wci_ref, gfr_ref, gfi_ref, gr_c, gi_c, gsr, gsi) = refs
        t = pl.program_id(1)

        @pl.when(t == 0)
        def _():
            gr_c[...] = gir_ref[0]
            gi_c[...] = gii_ref[0]
            if full:
                for ref in (dwbr_ref, dwbi_ref, dwcr_ref, dwci_ref, dar_ref, dai_ref):
                    ref[...] = jnp.zeros_like(ref)

        a_r = jnp.broadcast_to(ar_ref[0], (SEGMENTS, SLAB_ST))
        a_i = jnp.broadcast_to(ai_ref[0], (SEGMENTS, SLAB_ST))
        dyb = dy_ref[...].astype(bf16)
        gsr[...] = lax.dot_general(dyb, wcr_ref[0].astype(bf16), nt, preferred_element_type=f32)
        gsi[...] = -lax.dot_general(dyb, wci_ref[0].astype(bf16), nt, preferred_element_type=f32)

        if full:
            ub = u_ref[...].astype(bf16)
            bur[...] = lax.dot_general(ub, wbr_ref[0].astype(bf16), nn, preferred_element_type=f32)
            bui[...] = lax.dot_general(ub, wbi_ref[0].astype(bf16), nn, preferred_element_type=f32)

            def fstep(tau, carry):
                c_r, c_i = carry
                r = pl.multiple_of(tau * SEGMENTS, SEGMENTS)
                p_r, p_i = _cmul(a_r, a_i, c_r, c_i)
                n_r = p_r + bur[pl.ds(r, SEGMENTS), :]
                n_i = p_i + bui[pl.ds(r, SEGMENTS), :]
                sr[pl.ds(r, SEGMENTS), :] = n_r
                si[pl.ds(r, SEGMENTS), :] = n_i
                return n_r, n_i

            lax.fori_loop(0, tb, fstep, (bsr_ref[0, 0], bsi_ref[0, 0]))

        def adj(g_r, g_i):
            return a_r * g_r + a_i * g_i, a_r * g_i - a_i * g_r

        def rstep(k, carry):
            tau = tb - 1 - k
            r = pl.multiple_of(tau * SEGMENTS, SEGMENTS)
            if full:
                g_r, g_i, acc_r, acc_i = carry
            else:
                g_r, g_i = carry
            b_r, b_i = adj(g_r, g_i)
            n_r = b_r + gsr[pl.ds(r, SEGMENTS), :]
            n_i = b_i + gsi[pl.ds(r, SEGMENTS), :]
            gsr[pl.ds(r, SEGMENTS), :] = n_r
            gsi[pl.ds(r, SEGMENTS), :] = n_i
            if not full:
                return n_r, n_i
            rp = pl.multiple_of(jnp.maximum(tau - 1, 0) * SEGMENTS, SEGMENTS)
            first = tau == 0
            p_r = jnp.where(first, bsr_ref[0, 0], sr[pl.ds(rp, SEGMENTS), :])
            p_i = jnp.where(first, bsi_ref[0, 0], si[pl.ds(rp, SEGMENTS), :])
            return n_r, n_i, acc_r + n_r * p_r + n_i * p_i, acc_i + n_i * p_r - n_r * p_i

        zero = jnp.zeros((SEGMENTS, SLAB_ST), f32)
        if full:
            g_r, g_i, acc_r, acc_i = lax.fori_loop(0, tb, rstep, (gr_c[...], gi_c[...], zero, zero))
            dar_ref[0] += acc_r
            dai_ref[0] += acc_i
        else:
            g_r, g_i = lax.fori_loop(0, tb, rstep, (gr_c[...], gi_c[...]))
        gr_c[...] = g_r
        gi_c[...] = g_i

        if full:
            gbr = gsr[...].astype(bf16)
            gbi = gsi[...].astype(bf16)
            du_ref[...] = (lax.dot_general(gbr, wbr_ref[0].astype(bf16), nt, preferred_element_type=f32)
                           + lax.dot_general(gbi, wbi_ref[0].astype(bf16), nt, preferred_element_type=f32))
            dwbr_ref[0] += lax.dot_general(ub, gbr, tn, preferred_element_type=f32)
            dwbi_ref[0] += lax.dot_general(ub, gbi, tn, preferred_element_type=f32)
            dwcr_ref[0] += lax.dot_general(sr[...].astype(bf16), dyb, tn, preferred_element_type=f32)
            dwci_ref[0] -= lax.dot_general(si[...].astype(bf16), dyb, tn, preferred_element_type=f32)

        @pl.when(t == nb - 1)
        def _():
            gfr_ref[0] = g_r
            gfi_ref[0] = g_i

    slab3 = lambda s: pl.BlockSpec((1,) + s, lambda k, t: (k, 0, 0))
    rev_rows = pl.BlockSpec((R, SLAB_CH), lambda k, t: (nb - 1 - t, k))
    rev_bs = pl.BlockSpec((1, 1, SEGMENTS, SLAB_ST), lambda k, t: (nb - 1 - t, k, 0, 0))
    st = slab3((SEGMENTS, SLAB_ST))
    state_sds = SDS((ns, SEGMENTS, SLAB_ST), f32)
    if full:
        in_specs = [rev_rows, rev_rows, slab3((SLAB_CH, SLAB_ST)), slab3((SLAB_CH, SLAB_ST)), slab3((1, SLAB_ST)), slab3((1, SLAB_ST)),
                    rev_bs, rev_bs, st, st, slab3((SLAB_ST, SLAB_CH)), slab3((SLAB_ST, SLAB_CH))]
        args = (u, dy, wbr, wbi, ar, ai, bs_r, bs_i, ginit_r, ginit_i, wcr, wci)
        out_specs = [rev_rows, slab3((SLAB_CH, SLAB_ST)), slab3((SLAB_CH, SLAB_ST)), slab3((SLAB_ST, SLAB_CH)), slab3((SLAB_ST, SLAB_CH)),
                     st, st, st, st]
        out_shape = [SDS((S, D), f32), SDS((ns, SLAB_CH, SLAB_ST), f32), SDS((ns, SLAB_CH, SLAB_ST), f32),
                     SDS((ns, SLAB_ST, SLAB_CH), f32), SDS((ns, SLAB_ST, SLAB_CH), f32)] + [state_sds] * 4
        scratch = [pltpu.VMEM((SEGMENTS, SLAB_ST), f32)] * 2 + [pltpu.VMEM((R, SLAB_ST), f32)] * 6
    else:
        in_specs = [rev_rows, slab3((1, SLAB_ST)), slab3((1, SLAB_ST)), st, st, slab3((SLAB_ST, SLAB_CH)), slab3((SLAB_ST, SLAB_CH))]
        args = (dy, ar, ai, ginit_r, ginit_i, wcr, wci)
        out_specs = [st, st]
        out_shape = [state_sds] * 2
        scratch = [pltpu.VMEM((SEGMENTS, SLAB_ST), f32)] * 2 + [pltpu.VMEM((R, SLAB_ST), f32)] * 2
    return pl.pallas_call(
        body, name=name, grid=(ns, nb), in_specs=in_specs, out_specs=out_specs, out_shape=out_shape, scratch_shapes=scratch,
        compiler_params=pltpu.CompilerParams(dimension_semantics=("parallel", "arbitrary")),
    )(*args)


def _s5_chain(e_r, e_i, ar, ai, *, seg, reverse, name):
    ns = e_r.shape[0]
    assert seg & (seg - 1) == 0

    def body(er_ref, ei_ref, ar_ref, ai_ref, or_ref, oi_ref):
        p_r, p_i = ar_ref[0], ai_ref[0]
        if reverse:
            p_i = -p_i
        for _ in range(seg.bit_length() - 1):
            p_r, p_i = _cmul(p_r, p_i, p_r, p_i)
        c_r = jnp.zeros((1, SLAB_ST), f32)
        c_i = jnp.zeros((1, SLAB_ST), f32)
        order = range(SEGMENTS - 1, -1, -1) if reverse else range(SEGMENTS)
        for j in order:
            or_ref[0, pl.ds(j, 1), :] = c_r
            oi_ref[0, pl.ds(j, 1), :] = c_i
            m_r, m_i = _cmul(p_r, p_i, c_r, c_i)
            c_r = er_ref[0, pl.ds(j, 1), :] + m_r
            c_i = ei_ref[0, pl.ds(j, 1), :] + m_i

    st = pl.BlockSpec((1, SEGMENTS, SLAB_ST), lambda k: (k, 0, 0))
    av = pl.BlockSpec((1, 1, SLAB_ST), lambda k: (k, 0, 0))
    return pl.pallas_call(
        body, name=name, grid=(ns,), in_specs=[st, st, av, av], out_specs=[st, st],
        out_shape=[SDS(e_r.shape, f32)] * 2,
    )(e_r, e_i, ar, ai)


def _s5_disc(lr, li, log_dt):
    dt = jnp.exp(log_dt)
    mag = jnp.exp(lr * dt)
    ab_re = mag * jnp.cos(li * dt)
    ab_im = mag * jnp.sin(li * dt)
    den = lr * lr + li * li
    nr = ab_re - 1.0
    ni = ab_im
    return ab_re, ab_im, (nr * lr + ni * li) / den, (ni * lr - nr * li) / den


def _s5_bbar(b_r, b_i, fr, fi):
    return fr * b_r - fi * b_i, fr * b_i + fi * b_r


def _blockdiag(w, rows, cols):
    g = w.shape[0]
    w = w.reshape(g // 8, 8, rows, cols)
    eye = jnp.eye(8, dtype=w.dtype)
    full = w[:, :, :, None, :] * eye[None, :, None, :, None]
    return full.reshape(g // 8, 8 * rows, 8 * cols)


def _blockdiag_take(w, rows, cols):
    ns = w.shape[0]
    w = w.reshape(ns, 8, rows, 8, cols)
    idx = jnp.arange(8)
    d = w[:, idx, :, idx, :]
    return jnp.moveaxis(d, 0, 1).reshape(ns * 8, rows, cols)


def _mask(i, j, tq, tk):
    row = i * tq + lax.broadcasted_iota(jnp.int32, (tq, tk), 0)
    col = j * tk + lax.broadcasted_iota(jnp.int32, (tq, tk), 1)
    return (row // CHUNK) >= (col // CHUNK)


_NT = (((1,), (1,)), ((), ()))
_NN = (((1,), (0,)), ((), ()))
_TN = (((0,), (0,)), ((), ()))
_NEG = -1e30


def _attn_fwd(q, k, v, *, tq, name):
    S = q.shape[0]
    tq = min(tq, S)
    tk = tq
    nq = S // tq

    def body(q_ref, k_ref, v_ref, o_ref, lse_ref, m_s, l_s, acc):
        i, j = pl.program_id(1), pl.program_id(2)

        @pl.when(j == 0)
        def _():
            m_s[...] = jnp.full_like(m_s, _NEG)
            l_s[...] = jnp.zeros_like(l_s)
            acc[...] = jnp.zeros_like(acc)

        @pl.when(j <= i)
        def _():
            s = lax.dot_general(q_ref[...], k_ref[...], _NT, preferred_element_type=f32) * ATTN_SCALE
            s = jnp.where(_mask(i, j, tq, tk), s, _NEG)
            m_old = m_s[...]
            m_new = jnp.maximum(m_old, jnp.max(s, axis=-1, keepdims=True))
            alpha = jnp.exp(m_old - m_new)
            p = jnp.exp(s - m_new)
            l_s[...] = alpha * l_s[...] + jnp.sum(p, axis=-1, keepdims=True)
            acc[...] = alpha * acc[...] + lax.dot_general(p.astype(bf16), v_ref[...], _NN, preferred_element_type=f32)
            m_s[...] = m_new

        @pl.when(j == nq - 1)
        def _():
            o_ref[...] = acc[...] / l_s[...]
            lse_ref[0] = m_s[...] + jnp.log(l_s[...])

    qs = pl.BlockSpec((tq, HEAD_PAD), lambda h, i, j: (i, h))
    ks = pl.BlockSpec((tk, HEAD_PAD), lambda h, i, j: (jnp.minimum(j, i), h))
    return pl.pallas_call(
        body, name=name, grid=(N_HEADS, nq, nq), in_specs=[qs, ks, ks],
        out_specs=[qs, pl.BlockSpec((1, tq, 1), lambda h, i, j: (h, i, 0))],
        out_shape=[SDS((S, N_HEADS * HEAD_PAD), f32), SDS((N_HEADS, S, 1), f32)],
        scratch_shapes=[pltpu.VMEM((tq, 1), f32), pltpu.VMEM((tq, 1), f32), pltpu.VMEM((tq, HEAD_PAD), f32)],
        compiler_params=pltpu.CompilerParams(dimension_semantics=("parallel", "parallel", "arbitrary")),
    )(q, k, v)


def _attn_probs(q_ref, k_ref, v_ref, do_ref, o_ref, lse_ref, i, j, tq, tk):
    s = lax.dot_general(q_ref[...], k_ref[...], _NT, preferred_element_type=f32) * ATTN_SCALE
    s = jnp.where(_mask(i, j, tq, tk), s, _NEG)
    p = jnp.exp(s - lse_ref[0])
    do = do_ref[...]
    dsum = jnp.sum(do * o_ref[...], axis=-1, keepdims=True)
    dob = do.astype(bf16)
    dp = lax.dot_general(dob, v_ref[...], _NT, preferred_element_type=f32)
    ds = p * (dp - dsum) * ATTN_SCALE
    return p, ds, dob


def _attn_bwd_q(q, k, v, do, o, lse, *, tq, name):
    S = q.shape[0]
    tq = min(tq, S)
    tk = tq
    nq = S // tq

    def body(q_ref, k_ref, v_ref, do_ref, o_ref, lse_ref, dq_ref, acc):
        i, j = pl.program_id(1), pl.program_id(2)

        @pl.when(j == 0)
        def _():
            acc[...] = jnp.zeros_like(acc)

        @pl.when(j <= i)
        def _():
            _, ds, _ = _attn_probs(q_ref, k_ref, v_ref, do_ref, o_ref, lse_ref, i, j, tq, tk)
            acc[...] += lax.dot_general(ds.astype(bf16), k_ref[...], _NN, preferred_element_type=f32)

        @pl.when(j == nq - 1)
        def _():
            dq_ref[...] = acc[...]

    qs = pl.BlockSpec((tq, HEAD_PAD), lambda h, i, j: (i, h))
    ks = pl.BlockSpec((tk, HEAD_PAD), lambda h, i, j: (jnp.minimum(j, i), h))
    ls = pl.BlockSpec((1, tq, 1), lambda h, i, j: (h, i, 0))
    return pl.pallas_call(
        body, name=name, grid=(N_HEADS, nq, nq), in_specs=[qs, ks, ks, qs, qs, ls], out_specs=qs,
        out_shape=SDS((S, N_HEADS * HEAD_PAD), f32), scratch_shapes=[pltpu.VMEM((tq, HEAD_PAD), f32)],
        compiler_params=pltpu.CompilerParams(dimension_semantics=("parallel", "parallel", "arbitrary")),
    )(q, k, v, do, o, lse)


def _attn_bwd_kv(q, k, v, do, o, lse, *, tq, name):
    S = q.shape[0]
    tq = min(tq, S)
    tk = tq
    nq = S // tq

    def body(q_ref, k_ref, v_ref, do_ref, o_ref, lse_ref, dk_ref, dv_ref, dk_acc, dv_acc):
        j, i = pl.program_id(1), pl.program_id(2)

        @pl.when(i == 0)
        def _():
            dk_acc[...] = jnp.zeros_like(dk_acc)
            dv_acc[...] = jnp.zeros_like(dv_acc)

        @pl.when(i >= j)
        def _():
            p, ds, dob = _attn_probs(q_ref, k_ref, v_ref, do_ref, o_ref, lse_ref, i, j, tq, tk)
            dv_acc[...] += lax.dot_general(p.astype(bf16), dob, _TN, preferred_element_type=f32)
            dk_acc[...] += lax.dot_general(ds.astype(bf16), q_ref[...], _TN, preferred_element_type=f32)

        @pl.when(i == nq - 1)
        def _():
            dk_ref[...] = dk_acc[...]
            dv_ref[...] = dv_acc[...]

    qs = pl.BlockSpec((tq, HEAD_PAD), lambda h, j, i: (jnp.maximum(i, j), h))
    ks = pl.BlockSpec((tk, HEAD_PAD), lambda h, j, i: (j, h))
    ls = pl.BlockSpec((1, tq, 1), lambda h, j, i: (h, jnp.maximum(i, j), 0))
    return pl.pallas_call(
        body, name=name, grid=(N_HEADS, nq, nq), in_specs=[qs, ks, ks, qs, qs, ls], out_specs=[ks, ks],
        out_shape=[SDS((S, N_HEADS * HEAD_PAD), f32)] * 2,
        scratch_shapes=[pltpu.VMEM((tk, HEAD_PAD), f32)] * 2,
        compiler_params=pltpu.CompilerParams(dimension_semantics=("parallel", "parallel", "arbitrary")),
    )(q, k, v, do, o, lse)


def _pad_heads(w, per_head, axis):
    w = jnp.moveaxis(w, axis, -1)
    lead = w.shape[:-1]
    w = w.reshape(lead + (N_HEADS, per_head))
    w = jnp.pad(w, [(0, 0)] * len(lead) + [(0, 0), (0, HEAD_PAD - per_head)])
    return jnp.moveaxis(w.reshape(lead + (N_HEADS * HEAD_PAD,)), -1, axis)


def _unpad_heads(w, per_head, axis):
    w = jnp.moveaxis(w, axis, -1)
    lead = w.shape[:-1]
    w = w.reshape(lead + (N_HEADS, HEAD_PAD))[..., :per_head]
    return jnp.moveaxis(w.reshape(lead + (N_HEADS * per_head,)), -1, axis)


def _lanes128(vec, offset):
    return jnp.zeros((1, HEAD_PAD), f32).at[0, offset:offset + vec.shape[0]].set(vec)


def _to_segments(a):
    S, D = a.shape
    return a.reshape(SEGMENTS, S // SEGMENTS, D).transpose(1, 0, 2).reshape(S, D)


def _from_segments(a):
    S, D = a.shape
    return a.reshape(S // SEGMENTS, SEGMENTS, D).transpose(1, 0, 2).reshape(S, D)


def _s5_params_fwd(p):
    G, N, P = p["b_re"].shape
    def disc_body(lr, li, ld, o1, o2, o3, o4):
        o1[...], o2[...], o3[...], o4[...] = _s5_disc(lr[...], li[...], ld[...])

    ab_r, ab_i, f_r, f_i = pl.pallas_call(disc_body, name="s5_disc", out_shape=[SDS((G, N), f32)] * 4)(
        p["lam_re"], p["lam_im"], p["log_dt"].reshape(G, 1))
    bb_r, bb_i = _rowwise(_s5_bbar, [p["b_re"].reshape(G * N, P), p["b_im"].reshape(G * N, P), f_r.reshape(G * N, 1), f_i.reshape(G * N, 1)],
                          [], [(P, f32), (P, f32)], tm=512, name="s5_bbar")
    ns = G // 8
    out = dict(
        f_r=f_r, f_i=f_i,
        a_r=ab_r.reshape(ns, 1, SLAB_ST), a_i=ab_i.reshape(ns, 1, SLAB_ST),
        wb_r=_blockdiag(bb_r.reshape(G, N, P).transpose(0, 2, 1), P, N), wb_i=_blockdiag(bb_i.reshape(G, N, P).transpose(0, 2, 1), P, N),
        wc_r=_blockdiag(p["c_re"].transpose(0, 2, 1), N, P), wc_i=_blockdiag(p["c_im"].transpose(0, 2, 1), N, P),
    )
    return out


def _s5_params_bwd(p, sp, d_ar, d_ai, d_wbr, d_wbi, d_wcr, d_wci):
    G, N, P = p["b_re"].shape
    dbb_r = _blockdiag_take(d_wbr, P, N).transpose(0, 2, 1).reshape(G * N, P)
    dbb_i = _blockdiag_take(d_wbi, P, N).transpose(0, 2, 1).reshape(G * N, P)
    d_cre = _blockdiag_take(d_wcr, N, P).transpose(0, 2, 1)
    d_cim = _blockdiag_take(d_wci, N, P).transpose(0, 2, 1)
    rows = [p["b_re"].reshape(G * N, P), p["b_im"].reshape(G * N, P), sp["f_r"].reshape(G * N, 1), sp["f_i"].reshape(G * N, 1)]
    d_br, d_bi, d_fr, d_fi = _rowwise_bwd(_s5_bbar, rows, [], [dbb_r, dbb_i], [0, 1, 2, 3], [f32] * 4, tm=512, name="s5_bbar_bwd")

    def seg_sum(d, name):
        return _sum_lead(d.transpose(1, 0, 2).reshape(SEGMENTS, G, N), f32, name)

    def body(lr, li, ld, c1, c2, c3, c4, o1, o2, o3):
        _, vjp = jax.vjp(_s5_disc, lr[...], li[...], ld[...])
        o1[...], o2[...], o3[...] = vjp((c1[...], c2[...], c3[...], c4[...]))

    d_lr, d_li, d_ld = pl.pallas_call(
        body, name="s5_disc_bwd", out_shape=[SDS((G, N), f32), SDS((G, N), f32), SDS((G, 1), f32)],
    )(p["lam_re"], p["lam_im"], p["log_dt"].reshape(G, 1), seg_sum(d_ar, "s5_da_re_sum"), seg_sum(d_ai, "s5_da_im_sum"),
      d_fr.reshape(G, N), d_fi.reshape(G, N))
    return dict(lam_re=d_lr, lam_im=d_li, log_dt=d_ld.reshape(G), b_re=d_br.reshape(G, N, P), b_im=d_bi.reshape(G, N, P),
                c_re=d_cre, c_im=d_cim)


_SCAN_TB = 32


def _s5_mix_fwd(h, sp, name):
    S = h.shape[0]
    seg = S // SEGMENTS
    zeros = jnp.zeros((h.shape[1] // SLAB_CH, SEGMENTS, SLAB_ST), f32)
    common = (sp["wb_r"], sp["wb_i"], sp["a_r"], sp["a_i"])
    e_r, e_i, _, _ = _s5_scan_fwd(h, *common, zeros, zeros, sp["wc_r"], sp["wc_i"], want_y=False, tb=_SCAN_TB, name=name + "_local")
    i_r, i_i = _s5_chain(e_r, e_i, sp["a_r"], sp["a_i"], seg=seg, reverse=False, name=name + "_chain")
    y, _, _, bs_r, bs_i = _s5_scan_fwd(h, *common, i_r, i_i, sp["wc_r"], sp["wc_i"], want_y=True, tb=_SCAN_TB, name=name)
    return y, (bs_r, bs_i)


def _s5_mix_bwd(h, dy, sp, saved, name):
    S = h.shape[0]
    seg = S // SEGMENTS
    bs_r, bs_i = saved
    zeros = jnp.zeros((h.shape[1] // SLAB_CH, SEGMENTS, SLAB_ST), f32)
    gf_r, gf_i = _s5_scan_bwd(None, dy, None, None, sp["a_r"], sp["a_i"], None, None, zeros, zeros, sp["wc_r"], sp["wc_i"],
                              full=False, tb=_SCAN_TB, name=name + "_local")
    gi_r, gi_i = _s5_chain(gf_r, gf_i, sp["a_r"], sp["a_i"], seg=seg, reverse=True, name=name + "_chain")
    du, d_wbr, d_wbi, d_wcr, d_wci, d_ar, d_ai, _, _ = _s5_scan_bwd(
        h, dy, sp["wb_r"], sp["wb_i"], sp["a_r"], sp["a_i"], bs_r, bs_i, gi_r, gi_i, sp["wc_r"], sp["wc_i"],
        full=True, tb=_SCAN_TB, name=name)
    return du, (d_ar, d_ai, d_wbr, d_wbi, d_wcr, d_wci)


def _ffn_fwd(x1, mod, n2g, w_gu, w_down, tag):
    F = w_down.shape[0]
    h2 = _rowwise(lambda x, g, m: _normmod(x, g, m[3:4], m[4:5]), [x1], [n2g, mod], [(x1.shape[1], bf16)], name=tag + "_norm2")[0]
    ab = _mm(h2, w_gu, name=tag + "_gu")
    act = _rowwise(lambda ab: jax.nn.silu(ab[:, :F]) * ab[:, F:], [ab], [], [(F, bf16)], name=tag + "_act")[0]
    f = _mm(act, w_down, name=tag + "_down")
    x2 = _rowwise(lambda x, f, m: x + m[5:6] * f, [x1, f], [mod], [(x1.shape[1], f32)], name=tag + "_res2")[0]
    return x2, dict(x1=x1, h2=h2, ab=ab, act=act, f=f)


def _ffn_bwd(dx2, sv, mod, n2g, w_gu, w_down, tag):
    F = w_down.shape[0]
    D = dx2.shape[1]
    df, dgate2 = _rowwise(lambda dx, f, m: (m[5:6] * dx, jnp.sum(dx * f, axis=0, keepdims=True)), [dx2, sv["f"]], [mod],
                          [(D, bf16)], [(1, D)], name=tag + "_res2_bwd")
    dact = _mm(df, w_down, tb=True, name=tag + "_down_dx")
    dw_down = _mm(sv["act"], df, ta=True, name=tag + "_down_dw")
    dab = _rowwise_bwd(lambda ab: jax.nn.silu(ab[:, :F]) * ab[:, F:], [sv["ab"]], [], [dact], [0], [bf16], name=tag + "_act_bwd")[0]
    dh2 = _mm(dab, w_gu, tb=True, name=tag + "_gu_dx")
    dw_gu = _mm(sv["h2"], dab, ta=True, name=tag + "_gu_dw")
    dx1, dn2g, dmod = _rowwise_bwd(lambda x, g, m: _normmod(x, g, m[3:4], m[4:5]), [sv["x1"]], [n2g, mod], [dh2], [0], [f32],
                                   adds={0: dx2}, name=tag + "_norm2_bwd")
    dmod = dmod.at[5:6].add(dgate2)
    return dx1, dw_gu, dw_down, dn2g, dmod


def _device_step(x, target, pos, mods, kmod, W):
    S, D = x.shape
    F = W["ffn_w_down"].shape[1]
    inv = 1.0 / (ROPE_THETA ** (jnp.arange(0, ROPE, 2, dtype=f32) / ROPE))
    inv128 = _lanes128(jnp.concatenate([inv, inv]), NOPE)
    sign128 = _lanes128(jnp.concatenate([-jnp.ones(ROPE // 2, f32), jnp.ones(ROPE // 2, f32)]), NOPE)
    cf, ss = _rowwise(lambda p, iv, sg: (jnp.cos(p * iv), jnp.sin(p * iv) * sg), [pos], [inv128, sign128],
                      [(HEAD_PAD, f32), (HEAD_PAD, f32)], name="rope_table")

    w_gu = [jnp.concatenate([W["ffn_w_gate"][l], W["ffn_w_up"][l]], axis=1) for l in range(DEPTH)]
    saved = []
    xs = _to_segments(x)

    s5p = []
    for l in range(N_A):
        tag = f"l{l}"
        mod = mods[l]
        p = {k: W["s5_" + k][l] for k in ("lam_re", "lam_im", "log_dt", "b_re", "b_im", "c_re", "c_im")}
        sp = _s5_params_fwd(p)
        s5p.append((p, sp))
        n1g = W["norm1_g"][l][None]
        dsk = W["s5_d"][l][None]
        bgl = W["s5_b_glu"][l][None]
        h = _rowwise(lambda x, g, m: _normmod(x, g, m[0:1], m[1:2]), [xs], [n1g, mod], [(D, f32)], name=tag + "_norm1")[0]
        y_ssm, scan_saved = _s5_mix_fwd(h, sp, tag + "_scan")
        g = _rowwise(lambda ys, h, d: jax.nn.gelu(ys + d * h), [y_ssm, h], [dsk], [(D, f32)], name=tag + "_gelu")[0]
        z = _mm(g, W["s5_w_glu"][l], name=tag + "_glu")
        x1 = _rowwise(lambda x, g, z, b, m: x + m[2:3] * (g * jax.nn.sigmoid(z + b)), [xs, g, z], [bgl, mod], [(D, f32)], name=tag + "_res1")[0]
        x2, fsv = _ffn_fwd(x1, mod, W["norm2_g"][l][None], w_gu[l], W["ffn_w_down"][l], tag)
        saved.append(dict(x0=xs, h=h, y_ssm=y_ssm, g=g, z=z, scan=scan_saved, ffn=fsv))
        xs = x2
    x = _from_segments(xs)

    w_kv_a = jnp.concatenate([W["w_kv_a"][:, :KV_RANK], jnp.zeros((D, NOPE), f32).astype(W["w_kv_a"].dtype), W["w_kv_a"][:, KV_RANK:],
                              jnp.zeros((D, HEAD_PAD - NOPE - ROPE), f32).astype(W["w_kv_a"].dtype)], axis=1)
    wkb = W["w_kv_b"].reshape(KV_RANK, N_HEADS, NOPE + VDIM)
    w_kb = _pad_heads(wkb[:, :, :NOPE].reshape(KV_RANK, N_HEADS * NOPE), NOPE, 1)
    w_vb = _pad_heads(wkb[:, :, NOPE:].reshape(KV_RANK, N_HEADS * VDIM), VDIM, 1)
    kvg = W["kv_norm_g"][None]
    ga = W["kv_a_norm_g"][None]
    gkr = _lanes128(W["k_rope_norm_g"], NOPE)
    gkn = _lanes128(W["k_nope_norm_g"], 0)
    hk = _rowwise(lambda x, g, m: _normmod(x, g, m[0:1], m[1:2]), [x], [kvg, kmod], [(D, bf16)], name="kv_norm")[0]
    kva = _mm(hk, w_kv_a, name="kv_a")
    ckv, kr = _rowwise(_kv_a_post, [kva, cf, ss], [ga, gkr], [(KV_RANK, f32), (HEAD_PAD, f32)], name="kv_a_post")
    kraw = _mm(ckv, w_kb, name="kv_bk")
    vpad = _mm(ckv, w_vb, out_dtype=bf16, name="kv_bv")
    kpad = _rowwise(_k_heads, [kraw, kr], [gkn], [(N_HEADS * HEAD_PAD, bf16)], name="k_heads")[0]
    kv_saved = dict(x=x, hk=hk, kva=kva, ckv=ckv, kr=kr, kraw=kraw)

    for l in range(N_A, DEPTH):
        tag = f"l{l}"
        j = l - N_A
        mod = mods[l]
        n1g = W["norm1_g"][l][None]
        w_uq = _pad_heads(W["mla_w_uq"][j], NOPE + ROPE, 1)
        w_o = _pad_heads(W["mla_w_o"][j], VDIM, 0)
        qg = W["mla_q_norm_g"][j][None]
        gqn = _lanes128(W["mla_q_nope_norm_g"][j], 0)
        gqr = _lanes128(W["mla_q_rope_norm_g"][j], NOPE)
        h = _rowwise(lambda x, g, m: _normmod(x, g, m[0:1], m[1:2]), [x], [n1g, mod], [(D, bf16)], name=tag + "_norm1")[0]
        qa = _mm(h, W["mla_w_dq"][j], name=tag + "_dq")
        qc = _rowwise(_rms, [qa], [qg], [(qa.shape[1], f32)], name=tag + "_qnorm")[0]
        qraw = _mm(qc, w_uq, name=tag + "_uq")
        q = _rowwise(_q_heads, [qraw, cf, ss], [gqn, gqr], [(N_HEADS * HEAD_PAD, bf16)], name=tag + "_q_heads")[0]
        o, lse = _attn_fwd(q, kpad, vpad, tq=256, name=tag + "_attn")
        mix = _mm(o, w_o, name=tag + "_wo")
        x1 = _rowwise(lambda x, mx, m: x + m[2:3] * mx, [x, mix], [mod], [(D, f32)], name=tag + "_res1")[0]
        x2, fsv = _ffn_fwd(x1, mod, W["norm2_g"][l][None], w_gu[l], W["ffn_w_down"][l], tag)
        saved.append(dict(x0=x, h=h, qa=qa, qc=qc, qraw=qraw, q=q, o=o, lse=lse, mix=mix, w_uq=w_uq, w_o=w_o, ffn=fsv))
        x = x2

    dx, loss = _rowwise(
        lambda y, t: ((y - t) * (1.0 / D), jnp.full((1, LANES), 0.5 * jnp.sum(jnp.mean(jnp.square(y - t), axis=-1)), f32)),
        [x, target], [], [(D, f32)], [(1, LANES)], name="loss")

    gW = {}
    dmods = [None] * DEPTH
    g_gate, g_up, g_down, g_n1, g_n2 = [None] * DEPTH, [None] * DEPTH, [None] * DEPTH, [None] * DEPTH, [None] * DEPTH
    dks, dvs = [], []
    g_dq, g_qn, g_uq, g_qnn, g_qrn, g_wo = [None] * 2, [None] * 2, [None] * 2, [None] * 2, [None] * 2, [None] * 2
    for l in range(DEPTH - 1, N_A - 1, -1):
        tag = f"l{l}"
        j = l - N_A
        sv = saved[l]
        mod = mods[l]
        dx1, dw_gu, dw_down, g_n2[l], dmod = _ffn_bwd(dx, sv["ffn"], mod, W["norm2_g"][l][None], w_gu[l], W["ffn_w_down"][l], tag)
        g_gate[l], g_up[l], g_down[l] = dw_gu[:, :F], dw_gu[:, F:], dw_down
        dmix, dgate1 = _rowwise(lambda dx, mx, m: (m[2:3] * dx, jnp.sum(dx * mx, axis=0, keepdims=True)), [dx1, sv["mix"]], [mod],
                                [(D, bf16)], [(1, D)], name=tag + "_res1_bwd")
        do = _mm(dmix, sv["w_o"], tb=True, name=tag + "_wo_dx")
        g_wo[j] = _unpad_heads(_mm(sv["o"], dmix, ta=True, name=tag + "_wo_dw"), VDIM, 0)
        dq = _attn_bwd_q(sv["q"], kpad, vpad, do, sv["o"], sv["lse"], tq=256, name=tag + "_attn_dq")
        dk, dv = _attn_bwd_kv(sv["q"], kpad, vpad, do, sv["o"], sv["lse"], tq=256, name=tag + "_attn_dkv")
        dks.append(dk)
        dvs.append(dv)
        gqn = _lanes128(W["mla_q_nope_norm_g"][j], 0)
        gqr = _lanes128(W["mla_q_rope_norm_g"][j], NOPE)
        dqraw, dgqn, dgqr = _rowwise_bwd(lambda qr, c, s, a, b: _q_heads(qr, c, s, a, b), [sv["qraw"], cf, ss], [gqn, gqr], [dq], [0], [bf16],
                                         name=tag + "_q_heads_bwd")
        g_qnn[j], g_qrn[j] = dgqn[0, :NOPE], dgqr[0, NOPE:NOPE + ROPE]
        dqc = _mm(dqraw, sv["w_uq"], tb=True, name=tag + "_uq_dx")
        g_uq[j] = _unpad_heads(_mm(sv["qc"], dqraw, ta=True, name=tag + "_uq_dw"), NOPE + ROPE, 1)
        qg = W["mla_q_norm_g"][j][None]
        dqa, dqg = _rowwise_bwd(_rms, [sv["qa"]], [qg], [dqc], [0], [bf16], name=tag + "_qnorm_bwd")
        g_qn[j] = dqg[0]
        dh = _mm(dqa, W["mla_w_dq"][j], tb=True, name=tag + "_dq_dx")
        g_dq[j] = _mm(sv["h"], dqa, ta=True, name=tag + "_dq_dw")
        dx, dn1g, dmod1 = _rowwise_bwd(lambda x, g, m: _normmod(x, g, m[0:1], m[1:2]), [sv["x0"]], [W["norm1_g"][l][None], mod], [dh], [0], [f32],
                                       adds={0: dx1}, name=tag + "_norm1_bwd")
        g_n1[l] = dn1g[0]
        dmods[l] = (dmod + dmod1).at[2:3].add(dgate1)

    gkn = _lanes128(W["k_nope_norm_g"], 0)
    dk_sum = _sum_lead(jnp.stack(dks), f32, "dk_sum")
    dv_sum = _sum_lead(jnp.stack(dvs), bf16, "dv_sum")
    dkraw, dkr, dgkn = _rowwise_bwd(lambda kr_, r, g: _k_heads(kr_, r, g), [kv_saved["kraw"], kv_saved["kr"]], [gkn], [dk_sum], [0, 1], [bf16, f32],
                                    name="k_heads_bwd")
    dckv = _sum_lead(jnp.stack([_mm(dkraw, w_kb, tb=True, name="kv_bk_dx"), _mm(dv_sum, w_vb, tb=True, name="kv_bv_dx")]), f32, "dckv_sum")
    g_kb = _unpad_heads(_mm(kv_saved["ckv"], dkraw, ta=True, name="kv_bk_dw"), NOPE, 1)
    g_vb = _unpad_heads(_mm(kv_saved["ckv"], dv_sum, ta=True, name="kv_bv_dw"), VDIM, 1)
    gW["w_kv_b"] = jnp.concatenate([g_kb.reshape(KV_RANK, N_HEADS, NOPE), g_vb.reshape(KV_RANK, N_HEADS, VDIM)], axis=2).reshape(KV_RANK, -1)
    dkva, dga, dgkr = _rowwise_bwd(_kv_a_post, [kv_saved["kva"], cf, ss], [ga, gkr], [dckv, dkr], [0], [bf16], name="kv_a_post_bwd")
    dhk = _mm(dkva, w_kv_a, tb=True, name="kv_a_dx")
    g_kva = _mm(kv_saved["hk"], dkva, ta=True, name="kv_a_dw")
    gW["w_kv_a"] = jnp.concatenate([g_kva[:, :KV_RANK], g_kva[:, KV_RANK + NOPE:KV_RANK + NOPE + ROPE]], axis=1)
    dx, dkvg, dkmod = _rowwise_bwd(lambda x, g, m: _normmod(x, g, m[0:1], m[1:2]), [kv_saved["x"]], [kvg, kmod], [dhk], [0], [f32],
                                   adds={0: dx}, name="kv_norm_bwd")
    gW["kv_norm_g"], gW["kv_a_norm_g"] = dkvg[0], dga[0]
    gW["k_nope_norm_g"], gW["k_rope_norm_g"] = dgkn[0, :NOPE], dgkr[0, NOPE:NOPE + ROPE]

    dxs = _to_segments(dx)
    g_s5 = {k: [None] * N_A for k in ("lam_re", "lam_im", "log_dt", "b_re", "b_im", "c_re", "c_im", "d", "w_glu", "b_glu")}
    for l in range(N_A - 1, -1, -1):
        tag = f"l{l}"
        sv = saved[l]
        mod = mods[l]
        p, sp = s5p[l]
        dsk = W["s5_d"][l][None]
        bgl = W["s5_b_glu"][l][None]
        dx1, dw_gu, dw_down, g_n2[l], dmod = _ffn_bwd(dxs, sv["ffn"], mod, W["norm2_g"][l][None], w_gu[l], W["ffn_w_down"][l], tag)
        g_gate[l], g_up[l], g_down[l] = dw_gu[:, :F], dw_gu[:, F:], dw_down

        def res1(g, z, b, m):
            return m[2:3] * (g * jax.nn.sigmoid(z + b))
        dg1, dz, dbgl, dmod_g = _rowwise_bwd(res1, [sv["g"], sv["z"]], [bgl, mod], [dx1], [0, 1], [f32, bf16], name=tag + "_res1_bwd")
        dg2 = _mm(dz, W["s5_w_glu"][l], tb=True, name=tag + "_glu_dx")
        g_s5["w_glu"][l] = _mm(sv["g"], dz, ta=True, name=tag + "_glu_dw")
        g_s5["b_glu"][l] = dbgl[0]

        def gelu_bwd(ys, h, dga_, dgb_, d):
            _, vjp = jax.vjp(lambda ys, h, d: jax.nn.gelu(ys + d * h), ys, h, d)
            dys, dh, dd = vjp(dga_ + dgb_)
            return dys, dh, dd
        dys, dh_a, ddsk = _rowwise(gelu_bwd, [sv["y_ssm"], sv["h"], dg1, dg2], [dsk], [(D, f32), (D, f32)], [(1, D)], name=tag + "_gelu_bwd")
        g_s5["d"][l] = ddsk[0]
        du, scan_g = _s5_mix_bwd(sv["h"], dys, sp, sv["scan"], tag + "_scan_bwd")
        pg = _s5_params_bwd(p, sp, *scan_g)
        for k in ("lam_re", "lam_im", "log_dt", "b_re", "b_im", "c_re", "c_im"):
            g_s5[k][l] = pg[k]
        dxs, dn1g, dmod1 = _rowwise_bwd(lambda x, g, m: _normmod(x, g, m[0:1], m[1:2]), [sv["x0"]], [W["norm1_g"][l][None], mod],
                                        [_sum_lead(jnp.stack([dh_a, du]), f32, tag + "_dh_sum")], [0], [f32], adds={0: dx1}, name=tag + "_norm1_bwd")
        g_n1[l] = dn1g[0]
        dmods[l] = dmod + dmod1 + dmod_g
    dx = _from_segments(dxs)

    gW.update(
        norm1_g=jnp.stack(g_n1), norm2_g=jnp.stack([g[0] for g in g_n2]),
        ffn_w_gate=jnp.stack(g_gate), ffn_w_up=jnp.stack(g_up), ffn_w_down=jnp.stack(g_down),
        mla_w_dq=jnp.stack(g_dq), mla_q_norm_g=jnp.stack(g_qn), mla_w_uq=jnp.stack(g_uq),
        mla_q_nope_norm_g=jnp.stack(g_qnn), mla_q_rope_norm_g=jnp.stack(g_qrn), mla_w_o=jnp.stack(g_wo),
    )
    for k, v in g_s5.items():
        gW["s5_" + k] = jnp.stack(v)
    return loss, dx, gW, jnp.stack(dmods), dkmod


_WEIGHTS = ['ada_w', 'ada_b', 'norm1_g', 'norm2_g', 'ffn_w_gate', 'ffn_w_up', 'ffn_w_down', 's5_lam_re', 's5_lam_im', 's5_log_dt',
            's5_b_re', 's5_b_im', 's5_c_re', 's5_c_im', 's5_d', 's5_w_glu', 's5_b_glu', 'kv_ada_w', 'kv_ada_b', 'kv_norm_g', 'w_kv_a',
            'kv_a_norm_g', 'w_kv_b', 'k_nope_norm_g', 'k_rope_norm_g', 'mla_w_dq', 'mla_q_norm_g', 'mla_w_uq', 'mla_q_nope_norm_g',
            'mla_q_rope_norm_g', 'mla_w_o']
_BIG = {"ffn_w_gate": 2, "ffn_w_up": 2, "ffn_w_down": 1, "s5_w_glu": 1, "w_kv_a": 0, "w_kv_b": 1, "mla_w_dq": 1, "mla_w_uq": 2, "mla_w_o": 1}
_TENSOR_PARALLEL = ("ada_w", "kv_ada_w")
_SHARDED_VECS = ("s5_d", "s5_b_glu")
_CHIPS = 4
_FLAT_W = 1024
_FLAT_ALIGN = 512


def _pack_rows(arrs, width, align):
    flat = jnp.concatenate([a.reshape(-1) for a in arrs])
    rows = -(-flat.shape[0] // (width * align)) * align
    flat = jnp.pad(flat, (0, rows * width - flat.shape[0]))
    return flat.reshape(rows, width)


def _unpack(flat, shapes):
    flat = flat.reshape(-1)
    out, off = [], 0
    for s in shapes:
        n = math.prod(s)
        out.append(flat[off:off + n].reshape(s))
        off += n
    return out


def _adamw(w, g, m, v, name):
    c1 = 1.0 - ADAM_B1 ** ADAM_STEP
    c2 = 1.0 - ADAM_B2 ** ADAM_STEP

    def fn(w, g, m, v):
        m = ADAM_B1 * m + (1.0 - ADAM_B1) * g
        v = ADAM_B2 * v + (1.0 - ADAM_B2) * jnp.square(g)
        delta = -ADAM_LR * ((m / c1) / (jnp.sqrt(v / c2) + ADAM_EPS) + ADAM_WD * w)
        return delta, m, v

    C = w.shape[1]
    tm = _pick(w.shape[0], (256, 128, 64, 32, 16, 8))
    return _rowwise(fn, [w, g, m, v], [], [(C, f32)] * 3, tm=tm, name=name)


def kernel(x, c, positions, ada_w, ada_b, norm1_g, norm2_g, ffn_w_gate, ffn_w_up, ffn_w_down, s5_lam_re, s5_lam_im, s5_log_dt, s5_b_re, s5_b_im, s5_c_re, s5_c_im, s5_d, s5_w_glu, s5_b_glu, kv_ada_w, kv_ada_b, kv_norm_g, w_kv_a, kv_a_norm_g, w_kv_b, k_nope_norm_g, k_rope_norm_g, mla_w_dq, mla_q_norm_g, mla_w_uq, mla_q_nope_norm_g, mla_q_rope_norm_g, mla_w_o, loss_target, m_ada_w, m_ada_b, m_norm1_g, m_norm2_g, m_ffn_w_gate, m_ffn_w_up, m_ffn_w_down, m_s5_lam_re, m_s5_lam_im, m_s5_log_dt, m_s5_b_re, m_s5_b_im, m_s5_c_re, m_s5_c_im, m_s5_d, m_s5_w_glu, m_s5_b_glu, m_kv_ada_w, m_kv_ada_b, m_kv_norm_g, m_w_kv_a, m_kv_a_norm_g, m_w_kv_b, m_k_nope_norm_g, m_k_rope_norm_g, m_mla_w_dq, m_mla_q_norm_g, m_mla_w_uq, m_mla_q_nope_norm_g, m_mla_q_rope_norm_g, m_mla_w_o, v_ada_w, v_ada_b, v_norm1_g, v_norm2_g, v_ffn_w_gate, v_ffn_w_up, v_ffn_w_down, v_s5_lam_re, v_s5_lam_im, v_s5_log_dt, v_s5_b_re, v_s5_b_im, v_s5_c_re, v_s5_c_im, v_s5_d, v_s5_w_glu, v_s5_b_glu, v_kv_ada_w, v_kv_ada_b, v_kv_norm_g, v_w_kv_a, v_kv_a_norm_g, v_w_kv_b, v_k_nope_norm_g, v_k_rope_norm_g, v_mla_w_dq, v_mla_q_norm_g, v_mla_w_uq, v_mla_q_nope_norm_g, v_mla_q_rope_norm_g, v_mla_w_o):
    given = dict(locals())
    w = {n: given[n] for n in _WEIGHTS}
    m_in = {n: given["m_" + n] for n in _WEIGHTS}
    v_in = {n: given["v_" + n] for n in _WEIGHTS}
    S, D = x.shape[1], x.shape[2]
    ax, ay, ac = lax.axis_index("x"), lax.axis_index("y"), lax.axis_index("c")
    chip = 2 * ax + ay
    me = 4 * ax + 2 * ay + ac
    n_ada = ada_w.shape[2]
    n_kada = kv_ada_w.shape[1]

    g1 = _xchg(_pack_rows([c, s5_d, s5_b_glu], LANES, SUBLANES), "all", True, "gather_cond")
    g1 = g1.reshape(8, -1)
    c_all = g1[:, :D]
    vec = g1[0::2, D:D + 2 * s5_d.size].reshape(_CHIPS, 2, N_A, s5_d.shape[1])
    full_vecs = {"s5_d": vec[:, 0].transpose(1, 0, 2).reshape(N_A, D), "s5_b_glu": vec[:, 1].transpose(1, 0, 2).reshape(N_A, D)}
    c_act = _rowwise(jax.nn.silu, [c_all], [], [(D, f32)], tm=8, name="cond_silu")[0]

    mod_cols = [_mm(c_act, ada_w[l], cast=False, name="ada_proj") for l in range(DEPTH)] + [_mm(c_act, kv_ada_w, cast=False, name="kv_ada_proj")]
    g2 = _xchg(_pack_rows([jnp.concatenate(mod_cols, axis=1)], LANES, SUBLANES), "all", True, "gather_mod")
    g2 = g2.reshape(8, 8, -1)[0::2]
    mine = lax.dynamic_index_in_dim(g2, me, axis=1, keepdims=False)
    mod_lin = mine[:, :DEPTH * n_ada].reshape(_CHIPS, DEPTH, n_ada).transpose(1, 0, 2).reshape(DEPTH, 6 * D)
    kmod_lin = mine[:, DEPTH * n_ada:].reshape(1, 2 * D)
    mods = _rowwise(lambda a, b: a + b, [mod_lin, ada_b], [], [(6 * D, f32)], tm=DEPTH, name="ada_bias")[0].reshape(DEPTH, 6, D)
    kmod = _rowwise(lambda a, b: a + b, [kmod_lin, kv_ada_b.reshape(1, 2 * D)], [], [(2 * D, f32)], tm=1, name="kv_ada_bias")[0].reshape(2, D)

    big = list(_BIG)
    flat_w = _pack_rows([w[n] for n in big], _FLAT_W, _FLAT_ALIGN)
    gathered = _xchg(flat_w.astype(bf16), "xy", True, "gather_weights")
    R = flat_w.shape[0]
    parts = _unpack_chips(gathered, [w[n].shape for n in big])
    W = {n: jnp.concatenate([parts[i][q] for q in range(_CHIPS)], axis=_BIG[n]) for i, n in enumerate(big)}
    for n in _WEIGHTS:
        if n not in _BIG and n not in _TENSOR_PARALLEL and n not in _SHARDED_VECS and n not in ("ada_b", "kv_ada_b"):
            W[n] = w[n]
    W.update(full_vecs)

    pos = positions.reshape(S, 1).astype(f32)
    loss, dx, gW, dmods, dkmod = _device_step(x[0], loss_target[0], pos, mods, kmod, W)

    g4 = _xchg(_pack_rows([dmods, dkmod], LANES, SUBLANES), "all", True, "gather_dmod")
    ada_b_sum = _sum_lead(g4, f32, "dmod_sum").reshape(-1)
    g_ada_b = ada_b_sum[:DEPTH * 6 * D].reshape(DEPTH, 6 * D)
    g_kv_ada_b = ada_b_sum[DEPTH * 6 * D:DEPTH * 6 * D + 2 * D]
    dm_all = g4.reshape(8, -1)
    dm_ada = lax.dynamic_slice_in_dim(dm_all[:, :DEPTH * 6 * D].reshape(8, DEPTH, 6 * D), chip * n_ada, n_ada, axis=2)
    dm_kv = lax.dynamic_slice_in_dim(dm_all[:, DEPTH * 6 * D:DEPTH * 6 * D + 2 * D], chip * n_kada, n_kada, axis=1)

    def outer(at, b):
        acc = at[:, 0:1] * b[0:1, :]
        for i in range(1, 8):
            acc = acc + at[:, i:i + 1] * b[i:i + 1, :]
        return acc

    c_act_t = c_act.T
    g_ada_w = jnp.stack([_rowwise(outer, [c_act_t], [dm_ada[:, l]], [(n_ada, f32)], name="ada_dw")[0] for l in range(DEPTH)])
    g_kv_ada_w = _rowwise(outer, [c_act_t], [dm_kv], [(n_kada, f32)], name="kv_ada_dw")[0]

    pieces = [[jnp.split(gW[n], _CHIPS, axis=_BIG[n])[q] for n in big] for q in range(_CHIPS)]
    gflat = jnp.stack([_pack_rows(pieces[q], _FLAT_W, _FLAT_ALIGN) for q in range(_CHIPS)]).astype(bf16)
    R2 = R // 2
    halves = gflat.reshape(_CHIPS, 2, R2, _FLAT_W).transpose(1, 0, 2, 3)
    ra = _xchg(halves, "c", False, "grads_pair")
    hsum = _sum_lead(ra.reshape(2, _CHIPS * R2, _FLAT_W), bf16, "grads_pair_sum").reshape(_CHIPS, R2, _FLAT_W)
    rb = _xchg(hsum, "xy", False, "grads_chips")
    tsum = _sum_lead(rb, f32, "grads_chips_sum")
    g_flat = _xchg(tsum, "c", True, "grads_halves").reshape(R, _FLAT_W)
    g_big = dict(zip(big, _unpack(g_flat, [w[n].shape for n in big])))

    small = [n for n in _WEIGHTS if n not in _BIG and n not in _TENSOR_PARALLEL and n not in ("ada_b", "kv_ada_b")]
    svec = _pack_rows([loss[0, :1]] + [gW[n] for n in small], LANES, 8 * SUBLANES)
    rows8 = svec.shape[0] // 8
    rs = _xchg(svec.reshape(8, rows8, LANES), "all", False, "small_scatter")
    red = _sum_lead(rs, f32, "small_sum")
    full = _xchg(red, "all", True, "small_gather").reshape(-1)
    loss_tot = full[0]
    g_small = dict(zip(small, _unpack(full[1:], [gW[n].shape for n in small])))
    for n in _SHARDED_VECS:
        g_small[n] = lax.dynamic_slice_in_dim(g_small[n], chip * w[n].shape[1], w[n].shape[1], axis=1)
    g_small["ada_b"] = g_ada_b
    g_small["kv_ada_b"] = g_kv_ada_b

    grads = dict(g_big)
    grads.update(g_small)
    grads["ada_w"] = g_ada_w
    grads["kv_ada_w"] = g_kv_ada_w

    delta, new_m, new_v = {}, {}, {}
    d_, m_, v_ = _adamw(flat_w, g_flat, _pack_rows([m_in[n] for n in big], _FLAT_W, _FLAT_ALIGN),
                        _pack_rows([v_in[n] for n in big], _FLAT_W, _FLAT_ALIGN), "adamw_matmul")
    shapes = [w[n].shape for n in big]
    for res, src in ((delta, d_), (new_m, m_), (new_v, v_)):
        res.update(zip(big, _unpack(src, shapes)))
    for n in _TENSOR_PARALLEL:
        C = w[n].shape[-1]
        d_, m_, v_ = _adamw(w[n].reshape(-1, C), grads[n].reshape(-1, C), m_in[n].reshape(-1, C), v_in[n].reshape(-1, C), "adamw_" + n)
        delta[n], new_m[n], new_v[n] = d_.reshape(w[n].shape), m_.reshape(w[n].shape), v_.reshape(w[n].shape)
    rest = [n for n in _WEIGHTS if n not in _BIG and n not in _TENSOR_PARALLEL]
    d_, m_, v_ = _adamw(*[_pack_rows([src[n] for n in rest], LANES, SUBLANES) for src in (w, grads, m_in, v_in)], "adamw_small")
    shapes = [w[n].shape for n in rest]
    for res, src in ((delta, d_), (new_m, m_), (new_v, v_)):
        res.update(zip(rest, _unpack(src, shapes)))

    return (loss_tot, dx[None], *[grads[n] for n in _WEIGHTS], *[delta[n] for n in _WEIGHTS],
            *[new_m[n] for n in _WEIGHTS], *[new_v[n] for n in _WEIGHTS])


def _unpack_chips(gathered, shapes):
    n_chip = gathered.shape[0]
    flat = gathered.reshape(n_chip, -1)
    out, off = [], 0
    for s in shapes:
        n = math.prod(s)
        out.append(flat[:, off:off + n].reshape((n_chip,) + tuple(s)))
        off += n
    return out
```

```python
import functools
import math

import jax
import jax.numpy as jnp
from jax import lax
from jax.experimental import pallas as pl
from jax.experimental.pallas import tpu as pltpu

f32 = jnp.float32
bf16 = jnp.bfloat16
SDS = jax.ShapeDtypeStruct

EPS = 1e-6
CHUNK = 64
N_HEADS = 16
NOPE = 64
ROPE = 32
VDIM = 64
HEAD_PAD = 128
KV_RANK = 256
ROPE_THETA = 10000.0
ATTN_SCALE = 1.0 / math.sqrt(NOPE + ROPE)
SSM_GROUP = 16
SSM_STATE = 64
N_A = 2
DEPTH = 4
LANES = 128
SUBLANES = 8
SEGMENTS = 8
SLAB_CH = 128
SLAB_ST = 512
ADAM_LR, ADAM_B1, ADAM_B2, ADAM_EPS, ADAM_WD, ADAM_STEP = 0.001, 0.9, 0.999, 1e-08, 0.01, 10


def _pick(n, prefs=(512, 256, 128)):
    for p in prefs:
        if n % p == 0:
            return p
    return n


V7X_VMEM_BYTES = 64 << 20
MM_VMEM_LIMIT = V7X_VMEM_BYTES - (8 << 20)
MM_VMEM_BUDGET = 40 << 20
MM_MAX_TM, MM_MAX_TN = 1536, 1536
BF16_ROWS = 16


def _largest_divisor(n, cap, unit):
    if n <= cap:
        return n
    for d in range(cap - cap % unit, 0, -unit):
        if n % d == 0:
            return d
    return n


def _mm_tiles(M, N, K, sa, sb, so, m_on_lanes):
    tm = _largest_divisor(M, MM_MAX_TM, LANES if m_on_lanes else BF16_ROWS)
    tn = _largest_divisor(N, MM_MAX_TN, LANES)
    tk = K

    def need(tm, tn, tk):
        return 2 * (tm * tk * sa + tk * tn * sb + tm * tn * so) + (0 if tk == K else tm * tn * 4)

    m_unit = LANES if m_on_lanes else BF16_ROWS
    while need(tm, tn, tk) > MM_VMEM_BUDGET:
        if tk % 256 == 0 and tk >= 1024:
            tk //= 2
        elif tm % (2 * m_unit) == 0 and tm >= 512:
            tm //= 2
        elif tn % 256 == 0:
            tn //= 2
        else:
            break
    return tm, tn, tk


def _mm(a, b, *, ta=False, tb=False, out_dtype=f32, cast=True, name="mm"):
    if ta:
        K, M = a.shape
    else:
        M, K = a.shape
    if tb:
        N, K2 = b.shape
    else:
        K2, N = b.shape
    assert K == K2, (a.shape, b.shape, ta, tb)
    tm, tn, tk = _mm_tiles(M, N, K, a.dtype.itemsize, b.dtype.itemsize, jnp.dtype(out_dtype).itemsize, ta)
    nk = K // tk
    dims = (((0 if ta else 1,), (1 if tb else 0,)), ((), ()))

    def dot(a_ref, b_ref):
        av, bv = a_ref[...], b_ref[...]
        if cast:
            return lax.dot_general(av.astype(bf16), bv.astype(bf16), dims, preferred_element_type=f32)
        return lax.dot_general(av, bv, dims, preferred_element_type=f32, precision=lax.Precision.HIGHEST)

    def body_one(a_ref, b_ref, o_ref):
        o_ref[...] = dot(a_ref, b_ref).astype(o_ref.dtype)

    def body_acc(a_ref, b_ref, o_ref, acc):
        k = pl.program_id(2)

        @pl.when(k == 0)
        def _():
            acc[...] = jnp.zeros_like(acc)

        acc[...] += dot(a_ref, b_ref)

        @pl.when(k == nk - 1)
        def _():
            o_ref[...] = acc[...].astype(o_ref.dtype)

    a_spec = pl.BlockSpec((tk, tm), lambda i, j, k: (k, i)) if ta else pl.BlockSpec((tm, tk), lambda i, j, k: (i, k))
    b_spec = pl.BlockSpec((tn, tk), lambda i, j, k: (j, k)) if tb else pl.BlockSpec((tk, tn), lambda i, j, k: (k, j))
    return pl.pallas_call(
        body_one if nk == 1 else body_acc, name=name, grid=(M // tm, N // tn, nk),
        in_specs=[a_spec, b_spec], out_specs=pl.BlockSpec((tm, tn), lambda i, j, k: (i, j)),
        out_shape=SDS((M, N), out_dtype), scratch_shapes=[] if nk == 1 else [pltpu.VMEM((tm, tn), f32)],
        compiler_params=pltpu.CompilerParams(dimension_semantics=("parallel", "parallel", "arbitrary"),
                                             vmem_limit_bytes=MM_VMEM_LIMIT),
    )(a, b)


def _rowwise(fn, rows, consts, row_outs, acc_outs=(), *, tm=256, name="rowwise"):
    S = rows[0].shape[0]
    tm = min(tm, S)
    assert S % tm == 0, (S, tm)
    n_r, n_c, n_ro, n_ao = len(rows), len(consts), len(row_outs), len(acc_outs)

    def body(*refs):
        vals = [r[...] for r in refs[:n_r + n_c]]
        outs = fn(*vals)
        if not isinstance(outs, (tuple, list)):
            outs = (outs,)
        assert len(outs) == n_ro + n_ao, (name, len(outs), n_ro, n_ao)
        o_refs = refs[n_r + n_c:]
        for ref, val in zip(o_refs[:n_ro], outs[:n_ro]):
            ref[...] = val.astype(ref.dtype)
        if n_ao:
            @pl.when(pl.program_id(0) == 0)
            def _():
                for ref in o_refs[n_ro:]:
                    ref[...] = jnp.zeros_like(ref)
            for ref, val in zip(o_refs[n_ro:], outs[n_ro:]):
                ref[...] += jnp.broadcast_to(val, ref.shape).astype(f32)

    def full(shape):
        nd = len(shape)
        return pl.BlockSpec(tuple(shape), lambda i: (0,) * nd)

    in_specs = [pl.BlockSpec((tm, r.shape[1]), lambda i: (i, 0)) for r in rows] + [full(c.shape) for c in consts]
    out_specs = [pl.BlockSpec((tm, w), lambda i: (i, 0)) for (w, _) in row_outs] + [full(s) for s in acc_outs]
    out_shape = [SDS((S, w), dt) for (w, dt) in row_outs] + [SDS(tuple(s), f32) for s in acc_outs]
    res = pl.pallas_call(
        body, name=name, grid=(S // tm,), in_specs=in_specs, out_specs=out_specs, out_shape=out_shape,
        compiler_params=pltpu.CompilerParams(dimension_semantics=("arbitrary",)),
    )(*rows, *consts)
    return res


def _rowwise_bwd(f, rows, consts, cts, want, row_dtypes, *, adds=None, tm=256, name="rowwise_bwd"):
    n_r, n_c, n_ct = len(rows), len(consts), len(cts)
    adds = adds or {}
    add_keys = sorted(adds)
    add_rows = [adds[k] for k in add_keys]

    def fn(*args):
        r = [a.astype(f32) for a in args[:n_r]]
        ct = [a.astype(f32) for a in args[n_r:n_r + n_ct]]
        ad = args[n_r + n_ct:n_r + n_ct + len(add_rows)]
        c = list(args[n_r + n_ct + len(add_rows):])
        _, vjp = jax.vjp(f, *r, *c)
        g = vjp(ct[0] if n_ct == 1 else tuple(ct))
        d_rows = []
        for pos, i in enumerate(want):
            d = g[i]
            if pos in adds:
                d = d + ad[add_keys.index(pos)].astype(f32)
            d_rows.append(d)
        return (*d_rows, *g[n_r:])

    return _rowwise(fn, list(rows) + list(cts) + add_rows, list(consts),
                    [(rows[i].shape[1], dt) for i, dt in zip(want, row_dtypes)],
                    [c.shape for c in consts], tm=tm, name=name)


def _sum_lead(arr, out_dtype, name):
    n, R, C = arr.shape
    tm = _pick(R, (256, 128, 64, 32, 16, 8))

    def body(a_ref, o_ref):
        acc = a_ref[0].astype(f32)
        for q in range(1, n):
            acc = acc + a_ref[q].astype(f32)
        o_ref[...] = acc.astype(o_ref.dtype)

    return pl.pallas_call(
        body, name=name, grid=(R // tm,), in_specs=[pl.BlockSpec((n, tm, C), lambda i: (0, i, 0))],
        out_specs=pl.BlockSpec((tm, C), lambda i: (i, 0)), out_shape=SDS((R, C), out_dtype),
    )(arr)


_GROUPS = {
    "all": [(kx, ky, kc) for kx in (0, 1) for ky in (0, 1) for kc in (0, 1)][1:],
    "xy": [(0, 1, 0), (1, 0, 0), (1, 1, 0)],
    "c": [(0, 0, 1)],
}


XCHG_CHUNK_MIN_BYTES = 1 << 20
XCHG_CHUNKS = 8


def _group_pos(group, x, y, c):
    return {"all": 4 * x + 2 * y + c, "xy": 2 * x + y, "c": c}[group]


def _xchg(send, group, bcast, name):
    flips = _GROUPS[group]
    n = len(flips) + 1
    piece = tuple(send.shape if bcast else send.shape[1:])
    if not bcast:
        assert send.shape[0] == n
    lead = piece[0]
    nch = 1
    if math.prod(piece) * send.dtype.itemsize >= XCHG_CHUNK_MIN_BYTES:
        nch = lead if len(piece) > 2 else (XCHG_CHUNKS if lead % (XCHG_CHUNKS * 2 * BF16_ROWS) == 0 else 1)
    rows = lead // nch

    def body(s_ref, r_ref, send_sems, recv_sems, own_sem):
        x, y, c = lax.axis_index("x"), lax.axis_index("y"), lax.axis_index("c")
        me = _group_pos(group, x, y, c)

        def src(p, ch):
            ref = s_ref if bcast else s_ref.at[p]
            return ref.at[pl.ds(ch * rows, rows)]

        def dst(ch):
            return r_ref.at[me].at[pl.ds(ch * rows, rows)]

        own = pltpu.make_async_copy(s_ref if bcast else s_ref.at[me], r_ref.at[me], own_sem)
        own.start()
        copies = []
        for ch in range(nch):
            for k, (kx, ky, kc) in enumerate(flips):
                tx, ty, tc = x ^ kx, y ^ ky, c ^ kc
                cp = pltpu.make_async_remote_copy(
                    src_ref=src(_group_pos(group, tx, ty, tc), ch), dst_ref=dst(ch),
                    send_sem=send_sems.at[k, ch], recv_sem=recv_sems.at[k, ch],
                    device_id=(tx, ty, tc), device_id_type=pl.DeviceIdType.MESH)
                cp.start()
                copies.append(cp)
        for cp in copies:
            cp.wait()
        own.wait()

    return pl.pallas_call(
        body, name=name, out_shape=SDS((n,) + piece, send.dtype),
        in_specs=[pl.BlockSpec(memory_space=pl.ANY)], out_specs=pl.BlockSpec(memory_space=pl.ANY),
        scratch_shapes=[pltpu.SemaphoreType.DMA((n - 1, nch)), pltpu.SemaphoreType.DMA((n - 1, nch)), pltpu.SemaphoreType.DMA],
        compiler_params=pltpu.CompilerParams(has_side_effects=True),
    )(send)


def _rms(x, g):
    return x * lax.rsqrt(jnp.mean(x * x, axis=-1, keepdims=True) + EPS) * g


def _normmod(x, g, shift, scale):
    return _rms(x, g) * (1.0 + scale) + shift


def _lane(shape):
    return lax.broadcasted_iota(jnp.int32, shape, 1)


def _partner(x):
    lane = _lane(x.shape)
    lo = (lane >= NOPE) & (lane < NOPE + ROPE // 2)
    hi = (lane >= NOPE + ROPE // 2) & (lane < NOPE + ROPE)
    return jnp.where(lo, pltpu.roll(x, HEAD_PAD - ROPE // 2, 1), jnp.where(hi, pltpu.roll(x, ROPE // 2, 1), 0.0))


@jax.custom_vjp
def _rope(x, cf, ss):
    return x * cf + _partner(x) * ss


def _rope_fwd(x, cf, ss):
    return _rope(x, cf, ss), (cf, ss)


def _rope_bwd(res, dy):
    cf, ss = res
    return dy * cf + _partner(dy * ss), jnp.zeros_like(cf), jnp.zeros_like(ss)


_rope.defvjp(_rope_fwd, _rope_bwd)


def _head_norm(xh, gn, gr):
    lane = _lane(xh.shape)
    x2 = xh * xh
    ms_n = jnp.sum(jnp.where(lane < NOPE, x2, 0.0), axis=-1, keepdims=True) * (1.0 / NOPE)
    ms_r = jnp.sum(jnp.where((lane >= NOPE) & (lane < NOPE + ROPE), x2, 0.0), axis=-1, keepdims=True) * (1.0 / ROPE)
    return xh * (lax.rsqrt(ms_n + EPS) * gn + lax.rsqrt(ms_r + EPS) * gr)


def _q_heads(qraw, cf, ss, gn, gr):
    outs = []
    for h in range(N_HEADS):
        xh = qraw[:, h * HEAD_PAD:(h + 1) * HEAD_PAD]
        outs.append(_rope(_head_norm(xh, gn, gr), cf, ss))
    return jnp.concatenate(outs, axis=1)


def _k_heads(kraw, kr, gn):
    outs = []
    zero = jnp.zeros_like(gn)
    for h in range(N_HEADS):
        xh = kraw[:, h * HEAD_PAD:(h + 1) * HEAD_PAD]
        outs.append(_head_norm(xh, gn, zero) + kr)
    return jnp.concatenate(outs, axis=1)


def _kv_a_post(kva, cf, ss, ga, gr):
    ckv = _rms(kva[:, :KV_RANK], ga)
    kr = _rope(_head_norm(kva[:, KV_RANK:], jnp.zeros_like(gr), gr), cf, ss)
    return ckv, kr


def _cmul(ar, ai, br, bi):
    return ar * br - ai * bi, ar * bi + ai * br


def _s5_scan_fwd(u, wbr, wbi, ar, ai, init_r, init_i, wcr, wci, *, want_y, tb, name):
    S, D = u.shape
    ns = D // SLAB_CH
    seg = S // SEGMENTS
    tb = min(tb, seg)
    nb = seg // tb
    R = tb * SEGMENTS
    nt = (((1,), (0,)), ((), ()))

    def body(u_ref, wbr_ref, wbi_ref, ar_ref, ai_ref, ir_ref, ii_ref, wcr_ref, wci_ref, *rest):
        if want_y:
            y_ref, er_ref, ei_ref, bsr_ref, bsi_ref, cr, ci, bur, bui, sr, si = rest
        else:
            er_ref, ei_ref, bsr_ref, bsi_ref, cr, ci, bur, bui, sr, si = rest
        t = pl.program_id(1)

        @pl.when(t == 0)
        def _():
            cr[...] = ir_ref[0]
            ci[...] = ii_ref[0]

        bsr_ref[0, 0] = cr[...]
        bsi_ref[0, 0] = ci[...]
        ub = u_ref[...].astype(bf16)
        bur[...] = lax.dot_general(ub, wbr_ref[0].astype(bf16), nt, preferred_element_type=f32)
        bui[...] = lax.dot_general(ub, wbi_ref[0].astype(bf16), nt, preferred_element_type=f32)
        a_r = jnp.broadcast_to(ar_ref[0], (SEGMENTS, SLAB_ST))
        a_i = jnp.broadcast_to(ai_ref[0], (SEGMENTS, SLAB_ST))

        def step(tau, carry):
            c_r, c_i = carry
            r = pl.multiple_of(tau * SEGMENTS, SEGMENTS)
            p_r, p_i = _cmul(a_r, a_i, c_r, c_i)
            n_r = p_r + bur[pl.ds(r, SEGMENTS), :]
            n_i = p_i + bui[pl.ds(r, SEGMENTS), :]
            sr[pl.ds(r, SEGMENTS), :] = n_r
            si[pl.ds(r, SEGMENTS), :] = n_i
            return n_r, n_i

        c_r, c_i = lax.fori_loop(0, tb, step, (cr[...], ci[...]))
        cr[...] = c_r
        ci[...] = c_i
        if want_y:
            y_ref[...] = (lax.dot_general(sr[...].astype(bf16), wcr_ref[0].astype(bf16), nt, preferred_element_type=f32)
                          - lax.dot_general(si[...].astype(bf16), wci_ref[0].astype(bf16), nt, preferred_element_type=f32))

        @pl.when(t == nb - 1)
        def _():
            er_ref[0] = c_r
            ei_ref[0] = c_i

    slab3 = lambda s: pl.BlockSpec((1,) + s, lambda k, t: (k, 0, 0))
    in_specs = [pl.BlockSpec((R, SLAB_CH), lambda k, t: (t, k)),
                slab3((SLAB_CH, SLAB_ST)), slab3((SLAB_CH, SLAB_ST)), slab3((1, SLAB_ST)), slab3((1, SLAB_ST)),
                slab3((SEGMENTS, SLAB_ST)), slab3((SEGMENTS, SLAB_ST)), slab3((SLAB_ST, SLAB_CH)), slab3((SLAB_ST, SLAB_CH))]
    out_specs = [slab3((SEGMENTS, SLAB_ST)), slab3((SEGMENTS, SLAB_ST)),
                 pl.BlockSpec((1, 1, SEGMENTS, SLAB_ST), lambda k, t: (t, k, 0, 0)),
                 pl.BlockSpec((1, 1, SEGMENTS, SLAB_ST), lambda k, t: (t, k, 0, 0))]
    out_shape = [SDS((ns, SEGMENTS, SLAB_ST), f32)] * 2 + [SDS((nb, ns, SEGMENTS, SLAB_ST), f32)] * 2
    if want_y:
        out_specs = [pl.BlockSpec((R, SLAB_CH), lambda k, t: (t, k))] + out_specs
        out_shape = [SDS((S, D), f32)] + out_shape
    return pl.pallas_call(
        body, name=name, grid=(ns, nb), in_specs=in_specs, out_specs=out_specs, out_shape=out_shape,
        scratch_shapes=[pltpu.VMEM((SEGMENTS, SLAB_ST), f32)] * 2 + [pltpu.VMEM((R, SLAB_ST), f32)] * 4,
        compiler_params=pltpu.CompilerParams(dimension_semantics=("parallel", "arbitrary")),
    )(u, wbr, wbi, ar, ai, init_r, init_i, wcr, wci)


def _s5_scan_bwd(u, dy, wbr, wbi, ar, ai, bs_r, bs_i, ginit_r, ginit_i, wcr, wci, *, full, tb, name):
    S, D = dy.shape
    ns = D // SLAB_CH
    seg = S // SEGMENTS
    tb = min(tb, seg)
    nb = seg // tb
    R = tb * SEGMENTS
    nn = (((1,), (0,)), ((), ()))
    nt = (((1,), (1,)), ((), ()))
    tn = (((0,), (0,)), ((), ()))

    def body(*refs):
        if full:
            (u_ref, dy_ref, wbr_ref, wbi_ref, ar_ref, ai_ref, bsr_ref, bsi_ref, gir_ref, gii_ref, wcr_ref, wci_ref,
             du_ref, dwbr_ref, dwbi_ref, dwcr_ref, dwci_ref, dar_ref, dai_ref, gfr_ref, gfi_ref,
             gr_c, gi_c, bur, bui, sr, si, gsr, gsi) = refs
        else:
            (dy_ref, ar_ref, ai_ref, gir_ref, gii_ref, wcr_ref, wci_ref, gfr_ref, gfi_ref, gr_c, gi_c, gsr, gsi) = refs
        t = pl.program_id(1)

        @pl.when(t == 0)
        def _():
            gr_c[...] = gir_ref[0]
            gi_c[...] = gii_ref[0]
            if full:
                for ref in (dwbr_ref, dwbi_ref, dwcr_ref, dwci_ref, dar_ref, dai_ref):
                    ref[...] = jnp.zeros_like(ref)

        a_r = jnp.broadcast_to(ar_ref[0], (SEGMENTS, SLAB_ST))
        a_i = jnp.broadcast_to(ai_ref[0], (SEGMENTS, SLAB_ST))
        dyb = dy_ref[...].astype(bf16)
        gsr[...] = lax.dot_general(dyb, wcr_ref[0].astype(bf16), nt, preferred_element_type=f32)
        gsi[...] = -lax.dot_general(dyb, wci_ref[0].astype(bf16), nt, preferred_element_type=f32)

        if full:
            ub = u_ref[...].astype(bf16)
            bur[...] = lax.dot_general(ub, wbr_ref[0].astype(bf16), nn, preferred_element_type=f32)
            bui[...] = lax.dot_general(ub, wbi_ref[0].astype(bf16), nn, preferred_element_type=f32)

            def fstep(tau, carry):
                c_r, c_i = carry
                r = pl.multiple_of(tau * SEGMENTS, SEGMENTS)
                p_r, p_i = _cmul(a_r, a_i, c_r, c_i)
                n_r = p_r + bur[pl.ds(r, SEGMENTS), :]
                n_i = p_i + bui[pl.ds(r, SEGMENTS), :]
                sr[pl.ds(r, SEGMENTS), :] = n_r
                si[pl.ds(r, SEGMENTS), :] = n_i
                return n_r, n_i

            lax.fori_loop(0, tb, fstep, (bsr_ref[0, 0], bsi_ref[0, 0]))

        def adj(g_r, g_i):
            return a_r * g_r + a_i * g_i, a_r * g_i - a_i * g_r

        def rstep(k, carry):
            tau = tb - 1 - k
            r = pl.multiple_of(tau * SEGMENTS, SEGMENTS)
            if full:
                g_r, g_i, acc_r, acc_i = carry
            else:
                g_r, g_i = carry
            b_r, b_i = adj(g_r, g_i)
            n_r = b_r + gsr[pl.ds(r, SEGMENTS), :]
            n_i = b_i + gsi[pl.ds(r, SEGMENTS), :]
            gsr[pl.ds(r, SEGMENTS), :] = n_r
            gsi[pl.ds(r, SEGMENTS), :] = n_i
            if not full:
                return n_r, n_i
            rp = pl.multiple_of(jnp.maximum(tau - 1, 0) * SEGMENTS, SEGMENTS)
            first = tau == 0
            p_r = jnp.where(first, bsr_ref[0, 0], sr[pl.ds(rp, SEGMENTS), :])
            p_i = jnp.where(first, bsi_ref[0, 0], si[pl.ds(rp, SEGMENTS), :])
            return n_r, n_i, acc_r + n_r * p_r + n_i * p_i, acc_i + n_i * p_r - n_r * p_i

        zero = jnp.zeros((SEGMENTS, SLAB_ST), f32)
        if full:
            g_r, g_i, acc_r, acc_i = lax.fori_loop(0, tb, rstep, (gr_c[...], gi_c[...], zero, zero))
            dar_ref[0] += acc_r
            dai_ref[0] += acc_i
        else:
            g_r, g_i = lax.fori_loop(0, tb, rstep, (gr_c[...], gi_c[...]))
        gr_c[...] = g_r
        gi_c[...] = g_i

        if full:
            gbr = gsr[...].astype(bf16)
            gbi = gsi[...].astype(bf16)
            du_ref[...] = (lax.dot_general(gbr, wbr_ref[0].astype(bf16), nt, preferred_element_type=f32)
                           + lax.dot_general(gbi, wbi_ref[0].astype(bf16), nt, preferred_element_type=f32))
            dwbr_ref[0] += lax.dot_general(ub, gbr, tn, preferred_element_type=f32)
            dwbi_ref[0] += lax.dot_general(ub, gbi, tn, preferred_element_type=f32)
            dwcr_ref[0] += lax.dot_general(sr[...].astype(bf16), dyb, tn, preferred_element_type=f32)
            dwci_ref[0] -= lax.dot_general(si[...].astype(bf16), dyb, tn, preferred_element_type=f32)

        @pl.when(t == nb - 1)
        def _():
            gfr_ref[0] = g_r
            gfi_ref[0] = g_i

    slab3 = lambda s: pl.BlockSpec((1,) + s, lambda k, t: (k, 0, 0))
    rev_rows = pl.BlockSpec((R, SLAB_CH), lambda k, t: (nb - 1 - t, k))
    rev_bs = pl.BlockSpec((1, 1, SEGMENTS, SLAB_ST), lambda k, t: (nb - 1 - t, k, 0, 0))
    st = slab3((SEGMENTS, SLAB_ST))
    state_sds = SDS((ns, SEGMENTS, SLAB_ST), f32)
    if full:
        in_specs = [rev_rows, rev_rows, slab3((SLAB_CH, SLAB_ST)), slab3((SLAB_CH, SLAB_ST)), slab3((1, SLAB_ST)), slab3((1, SLAB_ST)),
                    rev_bs, rev_bs, st, st, slab3((SLAB_ST, SLAB_CH)), slab3((SLAB_ST, SLAB_CH))]
        args = (u, dy, wbr, wbi, ar, ai, bs_r, bs_i, ginit_r, ginit_i, wcr, wci)
        out_specs = [rev_rows, slab3((SLAB_CH, SLAB_ST)), slab3((SLAB_CH, SLAB_ST)), slab3((SLAB_ST, SLAB_CH)), slab3((SLAB_ST, SLAB_CH)),
                     st, st, st, st]
        out_shape = [SDS((S, D), f32), SDS((ns, SLAB_CH, SLAB_ST), f32), SDS((ns, SLAB_CH, SLAB_ST), f32),
                     SDS((ns, SLAB_ST, SLAB_CH), f32), SDS((ns, SLAB_ST, SLAB_CH), f32)] + [state_sds] * 4
        scratch = [pltpu.VMEM((SEGMENTS, SLAB_ST), f32)] * 2 + [pltpu.VMEM((R, SLAB_ST), f32)] * 6
    else:
        in_specs = [rev_rows, slab3((1, SLAB_ST)), slab3((1, SLAB_ST)), st, st, slab3((SLAB_ST, SLAB_CH)), slab3((SLAB_ST, SLAB_CH))]
        args = (dy, ar, ai, ginit_r, ginit_i, wcr, wci)
        out_specs = [st, st]
        out_shape = [state_sds] * 2
        scratch = [pltpu.VMEM((SEGMENTS, SLAB_ST), f32)] * 2 + [pltpu.VMEM((R, SLAB_ST), f32)] * 2
    return pl.pallas_call(
        body, name=name, grid=(ns, nb), in_specs=in_specs, out_specs=out_specs, out_shape=out_shape, scratch_shapes=scratch,
        compiler_params=pltpu.CompilerParams(dimension_semantics=("parallel", "arbitrary")),
    )(*args)


def _s5_chain(e_r, e_i, ar, ai, *, seg, reverse, name):
    ns = e_r.shape[0]
    assert seg & (seg - 1) == 0

    def body(er_ref, ei_ref, ar_ref, ai_ref, or_ref, oi_ref):
        p_r, p_i = ar_ref[0], ai_ref[0]
        if reverse:
            p_i = -p_i
        for _ in range(seg.bit_length() - 1):
            p_r, p_i = _cmul(p_r, p_i, p_r, p_i)
        c_r = jnp.zeros((1, SLAB_ST), f32)
        c_i = jnp.zeros((1, SLAB_ST), f32)
        order = range(SEGMENTS - 1, -1, -1) if reverse else range(SEGMENTS)
        for j in order:
            or_ref[0, pl.ds(j, 1), :] = c_r
            oi_ref[0, pl.ds(j, 1), :] = c_i
            m_r, m_i = _cmul(p_r, p_i, c_r, c_i)
            c_r = er_ref[0, pl.ds(j, 1), :] + m_r
            c_i = ei_ref[0, pl.ds(j, 1), :] + m_i

    st = pl.BlockSpec((1, SEGMENTS, SLAB_ST), lambda k: (k, 0, 0))
    av = pl.BlockSpec((1, 1, SLAB_ST), lambda k: (k, 0, 0))
    return pl.pallas_call(
        body, name=name, grid=(ns,), in_specs=[st, st, av, av], out_specs=[st, st],
        out_shape=[SDS(e_r.shape, f32)] * 2,
    )(e_r, e_i, ar, ai)


def _s5_disc(lr, li, log_dt):
    dt = jnp.exp(log_dt)
    mag = jnp.exp(lr * dt)
    ab_re = mag * jnp.cos(li * dt)
    ab_im = mag * jnp.sin(li * dt)
    den = lr * lr + li * li
    nr = ab_re - 1.0
    ni = ab_im
    return ab_re, ab_im, (nr * lr + ni * li) / den, (ni * lr - nr * li) / den


def _s5_bbar(b_r, b_i, fr, fi):
    return fr * b_r - fi * b_i, fr * b_i + fi * b_r


def _blockdiag(w, rows, cols):
    g = w.shape[0]
    w = w.reshape(g // 8, 8, rows, cols)
    eye = jnp.eye(8, dtype=w.dtype)
    full = w[:, :, :, None, :] * eye[None, :, None, :, None]
    return full.reshape(g // 8, 8 * rows, 8 * cols)


def _blockdiag_take(w, rows, cols):
    ns = w.shape[0]
    w = w.reshape(ns, 8, rows, 8, cols)
    idx = jnp.arange(8)
    d = w[:, idx, :, idx, :]
    return jnp.moveaxis(d, 0, 1).reshape(ns * 8, rows, cols)


def _mask(i, j, tq, tk):
    row = i * tq + lax.broadcasted_iota(jnp.int32, (tq, tk), 0)
    col = j * tk + lax.broadcasted_iota(jnp.int32, (tq, tk), 1)
    return (row // CHUNK) >= (col // CHUNK)


_NT = (((1,), (1,)), ((), ()))
_NN = (((1,), (0,)), ((), ()))
_TN = (((0,), (0,)), ((), ()))
_NEG = -1e30


def _attn_fwd(q, k, v, *, t, name):
    S = q.shape[0]
    t = min(t, S)
    nq = S // t

    def body(q_ref, k_ref, v_ref, o_ref, lse_ref, m_s, l_s, acc):
        i = pl.program_id(1)
        m_s[...] = jnp.full_like(m_s, _NEG)
        l_s[...] = jnp.zeros_like(l_s)
        acc[...] = jnp.zeros_like(acc)
        qb = q_ref[...]

        def block(j, masked):
            r = pl.multiple_of(j * t, t)
            s = lax.dot_general(qb, k_ref[pl.ds(r, t), :], _NT, preferred_element_type=f32) * ATTN_SCALE
            if masked:
                s = jnp.where(_mask(0, 0, t, t), s, _NEG)
            m_old = m_s[...]
            m_new = jnp.maximum(m_old, jnp.max(s, axis=-1, keepdims=True))
            alpha = jnp.exp(m_old - m_new)
            p = jnp.exp(s - m_new)
            l_s[...] = alpha * l_s[...] + jnp.sum(p, axis=-1, keepdims=True)
            acc[...] = alpha * acc[...] + lax.dot_general(p.astype(bf16), v_ref[pl.ds(r, t), :], _NN, preferred_element_type=f32)
            m_s[...] = m_new

        def below(j, carry):
            block(j, False)
            return carry

        lax.fori_loop(0, i, below, 0)
        block(i, True)
        o_ref[...] = acc[...] / l_s[...]
        lse_ref[0] = m_s[...] + jnp.log(l_s[...])

    qs = pl.BlockSpec((t, HEAD_PAD), lambda h, i: (i, h))
    ks = pl.BlockSpec((S, HEAD_PAD), lambda h, i: (0, h))
    return pl.pallas_call(
        body, name=name, grid=(N_HEADS, nq), in_specs=[qs, ks, ks],
        out_specs=[qs, pl.BlockSpec((1, t, 1), lambda h, i: (h, i, 0))],
        out_shape=[SDS((S, N_HEADS * HEAD_PAD), f32), SDS((N_HEADS, S, 1), f32)],
        scratch_shapes=[pltpu.VMEM((t, 1), f32), pltpu.VMEM((t, 1), f32), pltpu.VMEM((t, HEAD_PAD), f32)],
        compiler_params=pltpu.CompilerParams(dimension_semantics=("parallel", "arbitrary")),
    )(q, k, v)


def _attn_bwd(q, k, v, do, o, lse, *, t, name):
    S = q.shape[0]
    t = min(t, S)
    nq = S // t

    def body(q_ref, k_ref, v_ref, do_ref, o_ref, lse_ref, dq_ref, dk_ref, dv_ref, d_s, dk_acc, dv_acc):
        j = pl.program_id(1)

        @pl.when(j == 0)
        def _():
            dq_ref[...] = jnp.zeros_like(dq_ref)
            d_s[...] = jnp.sum(do_ref[...].astype(f32) * o_ref[...], axis=-1, keepdims=True)

        dk_acc[...] = jnp.zeros_like(dk_acc)
        dv_acc[...] = jnp.zeros_like(dv_acc)
        kb = k_ref[...]
        vb = v_ref[...]

        def block(i, masked):
            r = pl.multiple_of(i * t, t)
            qb = q_ref[pl.ds(r, t), :]
            dob = do_ref[pl.ds(r, t), :]
            s = lax.dot_general(qb, kb, _NT, preferred_element_type=f32) * ATTN_SCALE
            if masked:
                s = jnp.where(_mask(0, 0, t, t), s, _NEG)
            p = jnp.exp(s - lse_ref[0, pl.ds(r, t), :])
            dp = lax.dot_general(dob, vb, _NT, preferred_element_type=f32)
            ds = (p * (dp - d_s[pl.ds(r, t), :]) * ATTN_SCALE).astype(bf16)
            dv_acc[...] += lax.dot_general(p.astype(bf16), dob, _TN, preferred_element_type=f32)
            dk_acc[...] += lax.dot_general(ds, qb, _TN, preferred_element_type=f32)
            dq_ref[pl.ds(r, t), :] += lax.dot_general(ds, kb, _NN, preferred_element_type=f32)

        def below(i, carry):
            block(i, False)
            return carry

        block(j, True)
        lax.fori_loop(j + 1, nq, below, 0)
        dk_ref[...] = dk_acc[...]
        dv_ref[...] = dv_acc[...]

    hs = pl.BlockSpec((S, HEAD_PAD), lambda h, j: (0, h))
    ks = pl.BlockSpec((t, HEAD_PAD), lambda h, j: (j, h))
    ls = pl.BlockSpec((1, S, 1), lambda h, j: (h, 0, 0))
    return pl.pallas_call(
        body, name=name, grid=(N_HEADS, nq), in_specs=[hs, ks, ks, hs, hs, ls], out_specs=[hs, ks, ks],
        out_shape=[SDS((S, N_HEADS * HEAD_PAD), f32)] * 3,
        scratch_shapes=[pltpu.VMEM((S, 1), f32), pltpu.VMEM((t, HEAD_PAD), f32), pltpu.VMEM((t, HEAD_PAD), f32)],
        compiler_params=pltpu.CompilerParams(dimension_semantics=("parallel", "arbitrary")),
    )(q, k, v, do, o, lse)


def _pad_heads(w, per_head, axis):
    w = jnp.moveaxis(w, axis, -1)
    lead = w.shape[:-1]
    w = w.reshape(lead + (N_HEADS, per_head))
    w = jnp.pad(w, [(0, 0)] * len(lead) + [(0, 0), (0, HEAD_PAD - per_head)])
    return jnp.moveaxis(w.reshape(lead + (N_HEADS * HEAD_PAD,)), -1, axis)


def _unpad_heads(w, per_head, axis):
    w = jnp.moveaxis(w, axis, -1)
    lead = w.shape[:-1]
    w = w.reshape(lead + (N_HEADS, HEAD_PAD))[..., :per_head]
    return jnp.moveaxis(w.reshape(lead + (N_HEADS * per_head,)), -1, axis)


def _lanes128(vec, offset):
    return jnp.zeros((1, HEAD_PAD), f32).at[0, offset:offset + vec.shape[0]].set(vec)


def _to_segments(a):
    S, D = a.shape
    return a.reshape(SEGMENTS, S // SEGMENTS, D).transpose(1, 0, 2).reshape(S, D)


def _from_segments(a):
    S, D = a.shape
    return a.reshape(S // SEGMENTS, SEGMENTS, D).transpose(1, 0, 2).reshape(S, D)


def _s5_params_fwd(p):
    G, N, P = p["b_re"].shape
    def disc_body(lr, li, ld, o1, o2, o3, o4):
        o1[...], o2[...], o3[...], o4[...] = _s5_disc(lr[...], li[...], ld[...])

    ab_r, ab_i, f_r, f_i = pl.pallas_call(disc_body, name="s5_disc", out_shape=[SDS((G, N), f32)] * 4)(
        p["lam_re"], p["lam_im"], p["log_dt"].reshape(G, 1))
    bb_r, bb_i = _rowwise(_s5_bbar, [p["b_re"].reshape(G * N, P), p["b_im"].reshape(G * N, P), f_r.reshape(G * N, 1), f_i.reshape(G * N, 1)],
                          [], [(P, f32), (P, f32)], tm=512, name="s5_bbar")
    ns = G // 8
    out = dict(
        f_r=f_r, f_i=f_i,
        a_r=ab_r.reshape(ns, 1, SLAB_ST), a_i=ab_i.reshape(ns, 1, SLAB_ST),
        wb_r=_blockdiag(bb_r.reshape(G, N, P).transpose(0, 2, 1), P, N), wb_i=_blockdiag(bb_i.reshape(G, N, P).transpose(0, 2, 1), P, N),
        wc_r=_blockdiag(p["c_re"].transpose(0, 2, 1), N, P), wc_i=_blockdiag(p["c_im"].transpose(0, 2, 1), N, P),
    )
    return out


def _s5_params_bwd(p, sp, d_ar, d_ai, d_wbr, d_wbi, d_wcr, d_wci):
    G, N, P = p["b_re"].shape
    dbb_r = _blockdiag_take(d_wbr, P, N).transpose(0, 2, 1).reshape(G * N, P)
    dbb_i = _blockdiag_take(d_wbi, P, N).transpose(0, 2, 1).reshape(G * N, P)
    d_cre = _blockdiag_take(d_wcr, N, P).transpose(0, 2, 1)
    d_cim = _blockdiag_take(d_wci, N, P).transpose(0, 2, 1)
    rows = [p["b_re"].reshape(G * N, P), p["b_im"].reshape(G * N, P), sp["f_r"].reshape(G * N, 1), sp["f_i"].reshape(G * N, 1)]
    d_br, d_bi, d_fr, d_fi = _rowwise_bwd(_s5_bbar, rows, [], [dbb_r, dbb_i], [0, 1, 2, 3], [f32] * 4, tm=512, name="s5_bbar_bwd")

    def seg_sum(d, name):
        return _sum_lead(d.transpose(1, 0, 2).reshape(SEGMENTS, G, N), f32, name)

    def body(lr, li, ld, c1, c2, c3, c4, o1, o2, o3):
        _, vjp = jax.vjp(_s5_disc, lr[...], li[...], ld[...])
        o1[...], o2[...], o3[...] = vjp((c1[...], c2[...], c3[...], c4[...]))

    d_lr, d_li, d_ld = pl.pallas_call(
        body, name="s5_disc_bwd", out_shape=[SDS((G, N), f32), SDS((G, N), f32), SDS((G, 1), f32)],
    )(p["lam_re"], p["lam_im"], p["log_dt"].reshape(G, 1), seg_sum(d_ar, "s5_da_re_sum"), seg_sum(d_ai, "s5_da_im_sum"),
      d_fr.reshape(G, N), d_fi.reshape(G, N))
    return dict(lam_re=d_lr, lam_im=d_li, log_dt=d_ld.reshape(G), b_re=d_br.reshape(G, N, P), b_im=d_bi.reshape(G, N, P),
                c_re=d_cre, c_im=d_cim)


_SCAN_TB = 32
_ATTN_T = 256


def _s5_mix_fwd(h, sp, name):
    S = h.shape[0]
    seg = S // SEGMENTS
    zeros = jnp.zeros((h.shape[1] // SLAB_CH, SEGMENTS, SLAB_ST), f32)
    common = (sp["wb_r"], sp["wb_i"], sp["a_r"], sp["a_i"])
    e_r, e_i, _, _ = _s5_scan_fwd(h, *common, zeros, zeros, sp["wc_r"], sp["wc_i"], want_y=False, tb=_SCAN_TB, name=name + "_local")
    i_r, i_i = _s5_chain(e_r, e_i, sp["a_r"], sp["a_i"], seg=seg, reverse=False, name=name + "_chain")
    y, _, _, bs_r, bs_i = _s5_scan_fwd(h, *common, i_r, i_i, sp["wc_r"], sp["wc_i"], want_y=True, tb=_SCAN_TB, name=name)
    return y, (bs_r, bs_i)


def _s5_mix_bwd(h, dy, sp, saved, name):
    S = h.shape[0]
    seg = S // SEGMENTS
    bs_r, bs_i = saved
    zeros = jnp.zeros((h.shape[1] // SLAB_CH, SEGMENTS, SLAB_ST), f32)
    gf_r, gf_i = _s5_scan_bwd(None, dy, None, None, sp["a_r"], sp["a_i"], None, None, zeros, zeros, sp["wc_r"], sp["wc_i"],
                              full=False, tb=_SCAN_TB, name=name + "_local")
    gi_r, gi_i = _s5_chain(gf_r, gf_i, sp["a_r"], sp["a_i"], seg=seg, reverse=True, name=name + "_chain")
    du, d_wbr, d_wbi, d_wcr, d_wci, d_ar, d_ai, _, _ = _s5_scan_bwd(
        h, dy, sp["wb_r"], sp["wb_i"], sp["a_r"], sp["a_i"], bs_r, bs_i, gi_r, gi_i, sp["wc_r"], sp["wc_i"],
        full=True, tb=_SCAN_TB, name=name)
    return du, (d_ar, d_ai, d_wbr, d_wbi, d_wcr, d_wci)


def _ffn_fwd(x1, mod, n2g, w_gu, w_down, tag):
    F = w_down.shape[0]
    h2 = _rowwise(lambda x, g, m: _normmod(x, g, m[3:4], m[4:5]), [x1], [n2g, mod], [(x1.shape[1], bf16)], name=tag + "_norm2")[0]
    ab = _mm(h2, w_gu, name=tag + "_gu")
    act = _rowwise(lambda ab: jax.nn.silu(ab[:, :F]) * ab[:, F:], [ab], [], [(F, bf16)], name=tag + "_act")[0]
    f = _mm(act, w_down, name=tag + "_down")
    x2 = _rowwise(lambda x, f, m: x + m[5:6] * f, [x1, f], [mod], [(x1.shape[1], f32)], name=tag + "_res2")[0]
    return x2, dict(x1=x1, h2=h2, ab=ab, act=act, f=f)


def _ffn_bwd(dx2, sv, mod, n2g, w_gu, w_down, tag):
    F = w_down.shape[0]
    D = dx2.shape[1]
    df, dgate2 = _rowwise(lambda dx, f, m: (m[5:6] * dx, jnp.sum(dx * f, axis=0, keepdims=True)), [dx2, sv["f"]], [mod],
                          [(D, bf16)], [(1, D)], name=tag + "_res2_bwd")
    dact = _mm(df, w_down, tb=True, name=tag + "_down_dx")
    dw_down = _mm(sv["act"], df, ta=True, name=tag + "_down_dw")
    dab = _rowwise_bwd(lambda ab: jax.nn.silu(ab[:, :F]) * ab[:, F:], [sv["ab"]], [], [dact], [0], [bf16], name=tag + "_act_bwd")[0]
    dh2 = _mm(dab, w_gu, tb=True, name=tag + "_gu_dx")
    dw_gu = _mm(sv["h2"], dab, ta=True, name=tag + "_gu_dw")
    dx1, dn2g, dmod = _rowwise_bwd(lambda x, g, m: _normmod(x, g, m[3:4], m[4:5]), [sv["x1"]], [n2g, mod], [dh2], [0], [f32],
                                   adds={0: dx2}, name=tag + "_norm2_bwd")
    dmod = dmod.at[5:6].add(dgate2)
    return dx1, dw_gu, dw_down, dn2g, dmod


def _device_step(x, target, pos, mods, kmod, W):
    S, D = x.shape
    F = W["ffn_w_down"].shape[1]
    inv = 1.0 / (ROPE_THETA ** (jnp.arange(0, ROPE, 2, dtype=f32) / ROPE))
    inv128 = _lanes128(jnp.concatenate([inv, inv]), NOPE)
    sign128 = _lanes128(jnp.concatenate([-jnp.ones(ROPE // 2, f32), jnp.ones(ROPE // 2, f32)]), NOPE)
    cf, ss = _rowwise(lambda p, iv, sg: (jnp.cos(p * iv), jnp.sin(p * iv) * sg), [pos], [inv128, sign128],
                      [(HEAD_PAD, f32), (HEAD_PAD, f32)], name="rope_table")

    w_gu = [jnp.concatenate([W["ffn_w_gate"][l], W["ffn_w_up"][l]], axis=1) for l in range(DEPTH)]
    saved = []
    xs = _to_segments(x)

    s5p = []
    for l in range(N_A):
        tag = f"l{l}"
        mod = mods[l]
        p = {k: W["s5_" + k][l] for k in ("lam_re", "lam_im", "log_dt", "b_re", "b_im", "c_re", "c_im")}
        sp = _s5_params_fwd(p)
        s5p.append((p, sp))
        n1g = W["norm1_g"][l][None]
        dsk = W["s5_d"][l][None]
        bgl = W["s5_b_glu"][l][None]
        h = _rowwise(lambda x, g, m: _normmod(x, g, m[0:1], m[1:2]), [xs], [n1g, mod], [(D, f32)], name=tag + "_norm1")[0]
        y_ssm, scan_saved = _s5_mix_fwd(h, sp, tag + "_scan")
        g = _rowwise(lambda ys, h, d: jax.nn.gelu(ys + d * h), [y_ssm, h], [dsk], [(D, f32)], name=tag + "_gelu")[0]
        z = _mm(g, W["s5_w_glu"][l], name=tag + "_glu")
        x1 = _rowwise(lambda x, g, z, b, m: x + m[2:3] * (g * jax.nn.sigmoid(z + b)), [xs, g, z], [bgl, mod], [(D, f32)], name=tag + "_res1")[0]
        x2, fsv = _ffn_fwd(x1, mod, W["norm2_g"][l][None], w_gu[l], W["ffn_w_down"][l], tag)
        saved.append(dict(x0=xs, h=h, y_ssm=y_ssm, g=g, z=z, scan=scan_saved, ffn=fsv))
        xs = x2
    x = _from_segments(xs)

    w_kv_a = jnp.concatenate([W["w_kv_a"][:, :KV_RANK], jnp.zeros((D, NOPE), f32).astype(W["w_kv_a"].dtype), W["w_kv_a"][:, KV_RANK:],
                              jnp.zeros((D, HEAD_PAD - NOPE - ROPE), f32).astype(W["w_kv_a"].dtype)], axis=1)
    wkb = W["w_kv_b"].reshape(KV_RANK, N_HEADS, NOPE + VDIM)
    w_kb = _pad_heads(wkb[:, :, :NOPE].reshape(KV_RANK, N_HEADS * NOPE), NOPE, 1)
    w_vb = _pad_heads(wkb[:, :, NOPE:].reshape(KV_RANK, N_HEADS * VDIM), VDIM, 1)
    kvg = W["kv_norm_g"][None]
    ga = W["kv_a_norm_g"][None]
    gkr = _lanes128(W["k_rope_norm_g"], NOPE)
    gkn = _lanes128(W["k_nope_norm_g"], 0)
    hk = _rowwise(lambda x, g, m: _normmod(x, g, m[0:1], m[1:2]), [x], [kvg, kmod], [(D, bf16)], name="kv_norm")[0]
    kva = _mm(hk, w_kv_a, name="kv_a")
    ckv, kr = _rowwise(_kv_a_post, [kva, cf, ss], [ga, gkr], [(KV_RANK, f32), (HEAD_PAD, f32)], name="kv_a_post")
    kraw = _mm(ckv, w_kb, name="kv_bk")
    vpad = _mm(ckv, w_vb, out_dtype=bf16, name="kv_bv")
    kpad = _rowwise(_k_heads, [kraw, kr], [gkn], [(N_HEADS * HEAD_PAD, bf16)], name="k_heads")[0]
    kv_saved = dict(x=x, hk=hk, kva=kva, ckv=ckv, kr=kr, kraw=kraw)

    for l in range(N_A, DEPTH):
        tag = f"l{l}"
        j = l - N_A
        mod = mods[l]
        n1g = W["norm1_g"][l][None]
        w_uq = _pad_heads(W["mla_w_uq"][j], NOPE + ROPE, 1)
        w_o = _pad_heads(W["mla_w_o"][j], VDIM, 0)
        qg = W["mla_q_norm_g"][j][None]
        gqn = _lanes128(W["mla_q_nope_norm_g"][j], 0)
        gqr = _lanes128(W["mla_q_rope_norm_g"][j], NOPE)
        h = _rowwise(lambda x, g, m: _normmod(x, g, m[0:1], m[1:2]), [x], [n1g, mod], [(D, bf16)], name=tag + "_norm1")[0]
        qa = _mm(h, W["mla_w_dq"][j], name=tag + "_dq")
        qc = _rowwise(_rms, [qa], [qg], [(qa.shape[1], f32)], name=tag + "_qnorm")[0]
        qraw = _mm(qc, w_uq, name=tag + "_uq")
        q = _rowwise(_q_heads, [qraw, cf, ss], [gqn, gqr], [(N_HEADS * HEAD_PAD, bf16)], name=tag + "_q_heads")[0]
        o, lse = _attn_fwd(q, kpad, vpad, t=_ATTN_T, name=tag + "_attn")
        mix = _mm(o, w_o, name=tag + "_wo")
        x1 = _rowwise(lambda x, mx, m: x + m[2:3] * mx, [x, mix], [mod], [(D, f32)], name=tag + "_res1")[0]
        x2, fsv = _ffn_fwd(x1, mod, W["norm2_g"][l][None], w_gu[l], W["ffn_w_down"][l], tag)
        saved.append(dict(x0=x, h=h, qa=qa, qc=qc, qraw=qraw, q=q, o=o, lse=lse, mix=mix, w_uq=w_uq, w_o=w_o, ffn=fsv))
        x = x2

    dx, loss = _rowwise(
        lambda y, t: ((y - t) * (1.0 / D), jnp.full((1, LANES), 0.5 * jnp.sum(jnp.mean(jnp.square(y - t), axis=-1)), f32)),
        [x, target], [], [(D, f32)], [(1, LANES)], name="loss")

    gW = {}
    dmods = [None] * DEPTH
    g_gate, g_up, g_down, g_n1, g_n2 = [None] * DEPTH, [None] * DEPTH, [None] * DEPTH, [None] * DEPTH, [None] * DEPTH
    dks, dvs = [], []
    g_dq, g_qn, g_uq, g_qnn, g_qrn, g_wo = [None] * 2, [None] * 2, [None] * 2, [None] * 2, [None] * 2, [None] * 2
    for l in range(DEPTH - 1, N_A - 1, -1):
        tag = f"l{l}"
        j = l - N_A
        sv = saved[l]
        mod = mods[l]
        dx1, dw_gu, dw_down, g_n2[l], dmod = _ffn_bwd(dx, sv["ffn"], mod, W["norm2_g"][l][None], w_gu[l], W["ffn_w_down"][l], tag)
        g_gate[l], g_up[l], g_down[l] = dw_gu[:, :F], dw_gu[:, F:], dw_down
        dmix, dgate1 = _rowwise(lambda dx, mx, m: (m[2:3] * dx, jnp.sum(dx * mx, axis=0, keepdims=True)), [dx1, sv["mix"]], [mod],
                                [(D, bf16)], [(1, D)], name=tag + "_res1_bwd")
        do = _mm(dmix, sv["w_o"], tb=True, out_dtype=bf16, name=tag + "_wo_dx")
        g_wo[j] = _unpad_heads(_mm(sv["o"], dmix, ta=True, name=tag + "_wo_dw"), VDIM, 0)
        dq, dk, dv = _attn_bwd(sv["q"], kpad, vpad, do, sv["o"], sv["lse"], t=_ATTN_T, name=tag + "_attn_bwd")
        dks.append(dk)
        dvs.append(dv)
        gqn = _lanes128(W["mla_q_nope_norm_g"][j], 0)
        gqr = _lanes128(W["mla_q_rope_norm_g"][j], NOPE)
        dqraw, dgqn, dgqr = _rowwise_bwd(lambda qr, c, s, a, b: _q_heads(qr, c, s, a, b), [sv["qraw"], cf, ss], [gqn, gqr], [dq], [0], [bf16],
                                         name=tag + "_q_heads_bwd")
        g_qnn[j], g_qrn[j] = dgqn[0, :NOPE], dgqr[0, NOPE:NOPE + ROPE]
        dqc = _mm(dqraw, sv["w_uq"], tb=True, name=tag + "_uq_dx")
        g_uq[j] = _unpad_heads(_mm(sv["qc"], dqraw, ta=True, name=tag + "_uq_dw"), NOPE + ROPE, 1)
        qg = W["mla_q_norm_g"][j][None]
        dqa, dqg = _rowwise_bwd(_rms, [sv["qa"]], [qg], [dqc], [0], [bf16], name=tag + "_qnorm_bwd")
        g_qn[j] = dqg[0]
        dh = _mm(dqa, W["mla_w_dq"][j], tb=True, name=tag + "_dq_dx")
        g_dq[j] = _mm(sv["h"], dqa, ta=True, name=tag + "_dq_dw")
        dx, dn1g, dmod1 = _rowwise_bwd(lambda x, g, m: _normmod(x, g, m[0:1], m[1:2]), [sv["x0"]], [W["norm1_g"][l][None], mod], [dh], [0], [f32],
                                       adds={0: dx1}, name=tag + "_norm1_bwd")
        g_n1[l] = dn1g[0]
        dmods[l] = (dmod + dmod1).at[2:3].add(dgate1)

    gkn = _lanes128(W["k_nope_norm_g"], 0)
    dk_sum = _sum_lead(jnp.stack(dks), f32, "dk_sum")
    dv_sum = _sum_lead(jnp.stack(dvs), bf16, "dv_sum")
    dkraw, dkr, dgkn = _rowwise_bwd(lambda kr_, r, g: _k_heads(kr_, r, g), [kv_saved["kraw"], kv_saved["kr"]], [gkn], [dk_sum], [0, 1], [bf16, f32],
                                    name="k_heads_bwd")
    dckv = _sum_lead(jnp.stack([_mm(dkraw, w_kb, tb=True, name="kv_bk_dx"), _mm(dv_sum, w_vb, tb=True, name="kv_bv_dx")]), f32, "dckv_sum")
    g_kb = _unpad_heads(_mm(kv_saved["ckv"], dkraw, ta=True, name="kv_bk_dw"), NOPE, 1)
    g_vb = _unpad_heads(_mm(kv_saved["ckv"], dv_sum, ta=True, name="kv_bv_dw"), VDIM, 1)
    gW["w_kv_b"] = jnp.concatenate([g_kb.reshape(KV_RANK, N_HEADS, NOPE), g_vb.reshape(KV_RANK, N_HEADS, VDIM)], axis=2).reshape(KV_RANK, -1)
    dkva, dga, dgkr = _rowwise_bwd(_kv_a_post, [kv_saved["kva"], cf, ss], [ga, gkr], [dckv, dkr], [0], [bf16], name="kv_a_post_bwd")
    dhk = _mm(dkva, w_kv_a, tb=True, name="kv_a_dx")
    g_kva = _mm(kv_saved["hk"], dkva, ta=True, name="kv_a_dw")
    gW["w_kv_a"] = jnp.concatenate([g_kva[:, :KV_RANK], g_kva[:, KV_RANK + NOPE:KV_RANK + NOPE + ROPE]], axis=1)
    dx, dkvg, dkmod = _rowwise_bwd(lambda x, g, m: _normmod(x, g, m[0:1], m[1:2]), [kv_saved["x"]], [kvg, kmod], [dhk], [0], [f32],
                                   adds={0: dx}, name="kv_norm_bwd")
    gW["kv_norm_g"], gW["kv_a_norm_g"] = dkvg[0], dga[0]
    gW["k_nope_norm_g"], gW["k_rope_norm_g"] = dgkn[0, :NOPE], dgkr[0, NOPE:NOPE + ROPE]

    dxs = _to_segments(dx)
    g_s5 = {k: [None] * N_A for k in ("lam_re", "lam_im", "log_dt", "b_re", "b_im", "c_re", "c_im", "d", "w_glu", "b_glu")}
    for l in range(N_A - 1, -1, -1):
        tag = f"l{l}"
        sv = saved[l]
        mod = mods[l]
        p, sp = s5p[l]
        dsk = W["s5_d"][l][None]
        bgl = W["s5_b_glu"][l][None]
        dx1, dw_gu, dw_down, g_n2[l], dmod = _ffn_bwd(dxs, sv["ffn"], mod, W["norm2_g"][l][None], w_gu[l], W["ffn_w_down"][l], tag)
        g_gate[l], g_up[l], g_down[l] = dw_gu[:, :F], dw_gu[:, F:], dw_down

        def res1(g, z, b, m):
            return m[2:3] * (g * jax.nn.sigmoid(z + b))
        dg1, dz, dbgl, dmod_g = _rowwise_bwd(res1, [sv["g"], sv["z"]], [bgl, mod], [dx1], [0, 1], [f32, bf16], name=tag + "_res1_bwd")
        dg2 = _mm(dz, W["s5_w_glu"][l], tb=True, name=tag + "_glu_dx")
        g_s5["w_glu"][l] = _mm(sv["g"], dz, ta=True, name=tag + "_glu_dw")
        g_s5["b_glu"][l] = dbgl[0]

        def gelu_bwd(ys, h, dga_, dgb_, d):
            _, vjp = jax.vjp(lambda ys, h, d: jax.nn.gelu(ys + d * h), ys, h, d)
            dys, dh, dd = vjp(dga_ + dgb_)
            return dys, dh, dd
        dys, dh_a, ddsk = _rowwise(gelu_bwd, [sv["y_ssm"], sv["h"], dg1, dg2], [dsk], [(D, f32), (D, f32)], [(1, D)], name=tag + "_gelu_bwd")
        g_s5["d"][l] = ddsk[0]
        du, scan_g = _s5_mix_bwd(sv["h"], dys, sp, sv["scan"], tag + "_scan_bwd")
        pg = _s5_params_bwd(p, sp, *scan_g)
        for k in ("lam_re", "lam_im", "log_dt", "b_re", "b_im", "c_re", "c_im"):
            g_s5[k][l] = pg[k]
        dxs, dn1g, dmod1 = _rowwise_bwd(lambda x, g, m: _normmod(x, g, m[0:1], m[1:2]), [sv["x0"]], [W["norm1_g"][l][None], mod],
                                        [_sum_lead(jnp.stack([dh_a, du]), f32, tag + "_dh_sum")], [0], [f32], adds={0: dx1}, name=tag + "_norm1_bwd")
        g_n1[l] = dn1g[0]
        dmods[l] = dmod + dmod1 + dmod_g
    dx = _from_segments(dxs)

    gW.update(
        norm1_g=jnp.stack(g_n1), norm2_g=jnp.stack([g[0] for g in g_n2]),
        ffn_w_gate=jnp.stack(g_gate), ffn_w_up=jnp.stack(g_up), ffn_w_down=jnp.stack(g_down),
        mla_w_dq=jnp.stack(g_dq), mla_q_norm_g=jnp.stack(g_qn), mla_w_uq=jnp.stack(g_uq),
        mla_q_nope_norm_g=jnp.stack(g_qnn), mla_q_rope_norm_g=jnp.stack(g_qrn), mla_w_o=jnp.stack(g_wo),
    )
    for k, v in g_s5.items():
        gW["s5_" + k] = jnp.stack(v)
    return loss, dx, gW, jnp.stack(dmods), dkmod


_WEIGHTS = ['ada_w', 'ada_b', 'norm1_g', 'norm2_g', 'ffn_w_gate', 'ffn_w_up', 'ffn_w_down', 's5_lam_re', 's5_lam_im', 's5_log_dt',
            's5_b_re', 's5_b_im', 's5_c_re', 's5_c_im', 's5_d', 's5_w_glu', 's5_b_glu', 'kv_ada_w', 'kv_ada_b', 'kv_norm_g', 'w_kv_a',
            'kv_a_norm_g', 'w_kv_b', 'k_nope_norm_g', 'k_rope_norm_g', 'mla_w_dq', 'mla_q_norm_g', 'mla_w_uq', 'mla_q_nope_norm_g',
            'mla_q_rope_norm_g', 'mla_w_o']
_BIG = {"ffn_w_gate": 2, "ffn_w_up": 2, "ffn_w_down": 1, "s5_w_glu": 1, "w_kv_a": 0, "w_kv_b": 1, "mla_w_dq": 1, "mla_w_uq": 2, "mla_w_o": 1}
_TENSOR_PARALLEL = ("ada_w", "kv_ada_w")
_SHARDED_VECS = ("s5_d", "s5_b_glu")
_CHIPS = 4
_FLAT_W = 1024
_FLAT_ALIGN = 512


def _pack_rows(arrs, width, align):
    flat = jnp.concatenate([a.reshape(-1) for a in arrs])
    rows = -(-flat.shape[0] // (width * align)) * align
    flat = jnp.pad(flat, (0, rows * width - flat.shape[0]))
    return flat.reshape(rows, width)


def _unpack(flat, shapes):
    flat = flat.reshape(-1)
    out, off = [], 0
    for s in shapes:
        n = math.prod(s)
        out.append(flat[off:off + n].reshape(s))
        off += n
    return out


def _adamw(w, g, m, v, name):
    c1 = 1.0 - ADAM_B1 ** ADAM_STEP
    c2 = 1.0 - ADAM_B2 ** ADAM_STEP

    def fn(w, g, m, v):
        m = ADAM_B1 * m + (1.0 - ADAM_B1) * g
        v = ADAM_B2 * v + (1.0 - ADAM_B2) * jnp.square(g)
        delta = -ADAM_LR * ((m / c1) / (jnp.sqrt(v / c2) + ADAM_EPS) + ADAM_WD * w)
        return delta, m, v

    C = w.shape[1]
    tm = _pick(w.shape[0], (256, 128, 64, 32, 16, 8))
    return _rowwise(fn, [w, g, m, v], [], [(C, f32)] * 3, tm=tm, name=name)


def kernel(x, c, positions, ada_w, ada_b, norm1_g, norm2_g, ffn_w_gate, ffn_w_up, ffn_w_down, s5_lam_re, s5_lam_im, s5_log_dt, s5_b_re, s5_b_im, s5_c_re, s5_c_im, s5_d, s5_w_glu, s5_b_glu, kv_ada_w, kv_ada_b, kv_norm_g, w_kv_a, kv_a_norm_g, w_kv_b, k_nope_norm_g, k_rope_norm_g, mla_w_dq, mla_q_norm_g, mla_w_uq, mla_q_nope_norm_g, mla_q_rope_norm_g, mla_w_o, loss_target, m_ada_w, m_ada_b, m_norm1_g, m_norm2_g, m_ffn_w_gate, m_ffn_w_up, m_ffn_w_down, m_s5_lam_re, m_s5_lam_im, m_s5_log_dt, m_s5_b_re, m_s5_b_im, m_s5_c_re, m_s5_c_im, m_s5_d, m_s5_w_glu, m_s5_b_glu, m_kv_ada_w, m_kv_ada_b, m_kv_norm_g, m_w_kv_a, m_kv_a_norm_g, m_w_kv_b, m_k_nope_norm_g, m_k_rope_norm_g, m_mla_w_dq, m_mla_q_norm_g, m_mla_w_uq, m_mla_q_nope_norm_g, m_mla_q_rope_norm_g, m_mla_w_o, v_ada_w, v_ada_b, v_norm1_g, v_norm2_g, v_ffn_w_gate, v_ffn_w_up, v_ffn_w_down, v_s5_lam_re, v_s5_lam_im, v_s5_log_dt, v_s5_b_re, v_s5_b_im, v_s5_c_re, v_s5_c_im, v_s5_d, v_s5_w_glu, v_s5_b_glu, v_kv_ada_w, v_kv_ada_b, v_kv_norm_g, v_w_kv_a, v_kv_a_norm_g, v_w_kv_b, v_k_nope_norm_g, v_k_rope_norm_g, v_mla_w_dq, v_mla_q_norm_g, v_mla_w_uq, v_mla_q_nope_norm_g, v_mla_q_rope_norm_g, v_mla_w_o):
    given = dict(locals())
    w = {n: given[n] for n in _WEIGHTS}
    m_in = {n: given["m_" + n] for n in _WEIGHTS}
    v_in = {n: given["v_" + n] for n in _WEIGHTS}
    S, D = x.shape[1], x.shape[2]
    ax, ay, ac = lax.axis_index("x"), lax.axis_index("y"), lax.axis_index("c")
    chip = 2 * ax + ay
    me = 4 * ax + 2 * ay + ac
    n_ada = ada_w.shape[2]
    n_kada = kv_ada_w.shape[1]

    g1 = _xchg(_pack_rows([c, s5_d, s5_b_glu], LANES, SUBLANES), "all", True, "gather_cond")
    g1 = g1.reshape(8, -1)
    c_all = g1[:, :D]
    vec = g1[0::2, D:D + 2 * s5_d.size].reshape(_CHIPS, 2, N_A, s5_d.shape[1])
    full_vecs = {"s5_d": vec[:, 0].transpose(1, 0, 2).reshape(N_A, D), "s5_b_glu": vec[:, 1].transpose(1, 0, 2).reshape(N_A, D)}
    c_act = _rowwise(jax.nn.silu, [c_all], [], [(D, f32)], tm=8, name="cond_silu")[0]

    mod_cols = [_mm(c_act, ada_w[l], cast=False, name="ada_proj") for l in range(DEPTH)] + [_mm(c_act, kv_ada_w, cast=False, name="kv_ada_proj")]
    g2 = _xchg(_pack_rows([jnp.concatenate(mod_cols, axis=1)], LANES, SUBLANES), "all", True, "gather_mod")
    g2 = g2.reshape(8, 8, -1)[0::2]
    mine = lax.dynamic_index_in_dim(g2, me, axis=1, keepdims=False)
    mod_lin = mine[:, :DEPTH * n_ada].reshape(_CHIPS, DEPTH, n_ada).transpose(1, 0, 2).reshape(DEPTH, 6 * D)
    kmod_lin = mine[:, DEPTH * n_ada:].reshape(1, 2 * D)
    mods = _rowwise(lambda a, b: a + b, [mod_lin, ada_b], [], [(6 * D, f32)], tm=DEPTH, name="ada_bias")[0].reshape(DEPTH, 6, D)
    kmod = _rowwise(lambda a, b: a + b, [kmod_lin, kv_ada_b.reshape(1, 2 * D)], [], [(2 * D, f32)], tm=1, name="kv_ada_bias")[0].reshape(2, D)

    big = list(_BIG)
    flat_w = _pack_rows([w[n] for n in big], _FLAT_W, _FLAT_ALIGN)
    gathered = _xchg(flat_w.astype(bf16), "xy", True, "gather_weights")
    R = flat_w.shape[0]
    parts = _unpack_chips(gathered, [w[n].shape for n in big])
    W = {n: jnp.concatenate([parts[i][q] for q in range(_CHIPS)], axis=_BIG[n]) for i, n in enumerate(big)}
    for n in _WEIGHTS:
        if n not in _BIG and n not in _TENSOR_PARALLEL and n not in _SHARDED_VECS and n not in ("ada_b", "kv_ada_b"):
            W[n] = w[n]
    W.update(full_vecs)

    pos = positions.reshape(S, 1).astype(f32)
    loss, dx, gW, dmods, dkmod = _device_step(x[0], loss_target[0], pos, mods, kmod, W)

    g4 = _xchg(_pack_rows([dmods, dkmod], LANES, SUBLANES), "all", True, "gather_dmod")
    ada_b_sum = _sum_lead(g4, f32, "dmod_sum").reshape(-1)
    g_ada_b = ada_b_sum[:DEPTH * 6 * D].reshape(DEPTH, 6 * D)
    g_kv_ada_b = ada_b_sum[DEPTH * 6 * D:DEPTH * 6 * D + 2 * D]
    dm_all = g4.reshape(8, -1)
    dm_ada = lax.dynamic_slice_in_dim(dm_all[:, :DEPTH * 6 * D].reshape(8, DEPTH, 6 * D), chip * n_ada, n_ada, axis=2)
    dm_kv = lax.dynamic_slice_in_dim(dm_all[:, DEPTH * 6 * D:DEPTH * 6 * D + 2 * D], chip * n_kada, n_kada, axis=1)

    def outer(at, b):
        acc = at[:, 0:1] * b[0:1, :]
        for i in range(1, 8):
            acc = acc + at[:, i:i + 1] * b[i:i + 1, :]
        return acc

    c_act_t = c_act.T
    g_ada_w = jnp.stack([_rowwise(outer, [c_act_t], [dm_ada[:, l]], [(n_ada, f32)], name="ada_dw")[0] for l in range(DEPTH)])
    g_kv_ada_w = _rowwise(outer, [c_act_t], [dm_kv], [(n_kada, f32)], name="kv_ada_dw")[0]

    pieces = [[jnp.split(gW[n], _CHIPS, axis=_BIG[n])[q] for n in big] for q in range(_CHIPS)]
    gflat = jnp.stack([_pack_rows(pieces[q], _FLAT_W, _FLAT_ALIGN) for q in range(_CHIPS)]).astype(bf16)
    R2 = R // 2
    halves = gflat.reshape(_CHIPS, 2, R2, _FLAT_W).transpose(1, 0, 2, 3)
    ra = _xchg(halves, "c", False, "grads_pair")
    hsum = _sum_lead(ra.reshape(2, _CHIPS * R2, _FLAT_W), bf16, "grads_pair_sum").reshape(_CHIPS, R2, _FLAT_W)
    rb = _xchg(hsum, "xy", False, "grads_chips")
    tsum = _sum_lead(rb, f32, "grads_chips_sum")
    g_flat = _xchg(tsum, "c", True, "grads_halves").reshape(R, _FLAT_W)
    g_big = dict(zip(big, _unpack(g_flat, [w[n].shape for n in big])))

    small = [n for n in _WEIGHTS if n not in _BIG and n not in _TENSOR_PARALLEL and n not in ("ada_b", "kv_ada_b")]
    svec = _pack_rows([loss[0, :1]] + [gW[n] for n in small], LANES, 8 * SUBLANES)
    rows8 = svec.shape[0] // 8
    rs = _xchg(svec.reshape(8, rows8, LANES), "all", False, "small_scatter")
    red = _sum_lead(rs, f32, "small_sum")
    full = _xchg(red, "all", True, "small_gather").reshape(-1)
    loss_tot = full[0]
    g_small = dict(zip(small, _unpack(full[1:], [gW[n].shape for n in small])))
    for n in _SHARDED_VECS:
        g_small[n] = lax.dynamic_slice_in_dim(g_small[n], chip * w[n].shape[1], w[n].shape[1], axis=1)
    g_small["ada_b"] = g_ada_b
    g_small["kv_ada_b"] = g_kv_ada_b

    grads = dict(g_big)
    grads.update(g_small)
    grads["ada_w"] = g_ada_w
    grads["kv_ada_w"] = g_kv_ada_w

    delta, new_m, new_v = {}, {}, {}
    d_, m_, v_ = _adamw(flat_w, g_flat, _pack_rows([m_in[n] for n in big], _FLAT_W, _FLAT_ALIGN),
                        _pack_rows([v_in[n] for n in big], _FLAT_W, _FLAT_ALIGN), "adamw_matmul")
    shapes = [w[n].shape for n in big]
    for res, src in ((delta, d_), (new_m, m_), (new_v, v_)):
        res.update(zip(big, _unpack(src, shapes)))
    for n in _TENSOR_PARALLEL:
        C = w[n].shape[-1]
        d_, m_, v_ = _adamw(w[n].reshape(-1, C), grads[n].reshape(-1, C), m_in[n].reshape(-1, C), v_in[n].reshape(-1, C), "adamw_" + n)
        delta[n], new_m[n], new_v[n] = d_.reshape(w[n].shape), m_.reshape(w[n].shape), v_.reshape(w[n].shape)
    rest = [n for n in _WEIGHTS if n not in _BIG and n not in _TENSOR_PARALLEL]
    d_, m_, v_ = _adamw(*[_pack_rows([src[n] for n in rest], LANES, SUBLANES) for src in (w, grads, m_in, v_in)], "adamw_small")
    shapes = [w[n].shape for n in rest]
    for res, src in ((delta, d_), (new_m, m_), (new_v, v_)):
        res.update(zip(rest, _unpack(src, shapes)))

    return (loss_tot, dx[None], *[grads[n] for n in _WEIGHTS], *[delta[n] for n in _WEIGHTS],
            *[new_m[n] for n in _WEIGHTS], *[new_v[n] for n in _WEIGHTS])


def _unpack_chips(gathered, shapes):
    n_chip = gathered.shape[0]
    flat = gathered.reshape(n_chip, -1)
    out, off = [], 0
    for s in shapes:
        n = math.prod(s)
        out.append(flat[:, off:off + n].reshape((n_chip,) + tuple(s)))
        off += n
    return out
```

```python
import functools
import math

import jax
import jax.numpy as jnp
from jax import lax
from jax.experimental import pallas as pl
from jax.experimental.pallas import tpu as pltpu

f32 = jnp.float32
bf16 = jnp.bfloat16
SDS = jax.ShapeDtypeStruct

EPS = 1e-6
CHUNK = 64
N_HEADS = 16
NOPE = 64
ROPE = 32
VDIM = 64
HEAD_PAD = 128
KV_RANK = 256
ROPE_THETA = 10000.0
ATTN_SCALE = 1.0 / math.sqrt(NOPE + ROPE)
SSM_GROUP = 16
SSM_STATE = 64
N_A = 2
DEPTH = 4
LANES = 128
SUBLANES = 8
SEGMENTS = 8
SLAB_CH = 128
SLAB_ST = 512
ADAM_LR, ADAM_B1, ADAM_B2, ADAM_EPS, ADAM_WD, ADAM_STEP = 0.001, 0.9, 0.999, 1e-08, 0.01, 10


def _pick(n, prefs=(512, 256, 128)):
    for p in prefs:
        if n % p == 0:
            return p
    return n


V7X_VMEM_BYTES = 64 << 20
MM_VMEM_LIMIT = V7X_VMEM_BYTES - (8 << 20)
MM_VMEM_BUDGET = 40 << 20
MM_MAX_TM, MM_MAX_TN = 1536, 1536
BF16_ROWS = 16


def _largest_divisor(n, cap, unit):
    if n <= cap:
        return n
    for d in range(cap - cap % unit, 0, -unit):
        if n % d == 0:
            return d
    return n


def _mm_tiles(M, N, K, sa, sb, so, m_on_lanes):
    tm = _largest_divisor(M, MM_MAX_TM, LANES if m_on_lanes else BF16_ROWS)
    tn = _largest_divisor(N, MM_MAX_TN, LANES)
    tk = K

    def need(tm, tn, tk):
        return 2 * (tm * tk * sa + tk * tn * sb + tm * tn * so) + (0 if tk == K else tm * tn * 4)

    m_unit = LANES if m_on_lanes else BF16_ROWS
    while need(tm, tn, tk) > MM_VMEM_BUDGET:
        if tk % 256 == 0 and tk >= 1024:
            tk //= 2
        elif tm % (2 * m_unit) == 0 and tm >= 512:
            tm //= 2
        elif tn % 256 == 0:
            tn //= 2
        else:
            break
    return tm, tn, tk


def _mm(a, b, *, ta=False, tb=False, out_dtype=f32, cast=True, name="mm"):
    if ta:
        K, M = a.shape
    else:
        M, K = a.shape
    if tb:
        N, K2 = b.shape
    else:
        K2, N = b.shape
    assert K == K2, (a.shape, b.shape, ta, tb)
    tm, tn, tk = _mm_tiles(M, N, K, a.dtype.itemsize, b.dtype.itemsize, jnp.dtype(out_dtype).itemsize, ta)
    nk = K // tk
    dims = (((0 if ta else 1,), (1 if tb else 0,)), ((), ()))

    def dot(a_ref, b_ref):
        av, bv = a_ref[...], b_ref[...]
        if cast:
            return lax.dot_general(av.astype(bf16), bv.astype(bf16), dims, preferred_element_type=f32)
        return lax.dot_general(av, bv, dims, preferred_element_type=f32, precision=lax.Precision.HIGHEST)

    def body_one(a_ref, b_ref, o_ref):
        o_ref[...] = dot(a_ref, b_ref).astype(o_ref.dtype)

    def body_acc(a_ref, b_ref, o_ref, acc):
        k = pl.program_id(2)

        @pl.when(k == 0)
        def _():
            acc[...] = jnp.zeros_like(acc)

        acc[...] += dot(a_ref, b_ref)

        @pl.when(k == nk - 1)
        def _():
            o_ref[...] = acc[...].astype(o_ref.dtype)

    a_spec = pl.BlockSpec((tk, tm), lambda i, j, k: (k, i)) if ta else pl.BlockSpec((tm, tk), lambda i, j, k: (i, k))
    b_spec = pl.BlockSpec((tn, tk), lambda i, j, k: (j, k)) if tb else pl.BlockSpec((tk, tn), lambda i, j, k: (k, j))
    return pl.pallas_call(
        body_one if nk == 1 else body_acc, name=name, grid=(M // tm, N // tn, nk),
        in_specs=[a_spec, b_spec], out_specs=pl.BlockSpec((tm, tn), lambda i, j, k: (i, j)),
        out_shape=SDS((M, N), out_dtype), scratch_shapes=[] if nk == 1 else [pltpu.VMEM((tm, tn), f32)],
        compiler_params=pltpu.CompilerParams(dimension_semantics=("parallel", "parallel", "arbitrary"),
                                             vmem_limit_bytes=MM_VMEM_LIMIT),
    )(a, b)


def _rowwise(fn, rows, consts, row_outs, acc_outs=(), *, tm=256, name="rowwise"):
    S = rows[0].shape[0]
    tm = min(tm, S)
    assert S % tm == 0, (S, tm)
    n_r, n_c, n_ro, n_ao = len(rows), len(consts), len(row_outs), len(acc_outs)

    def body(*refs):
        vals = [r[...] for r in refs[:n_r + n_c]]
        outs = fn(*vals)
        if not isinstance(outs, (tuple, list)):
            outs = (outs,)
        assert len(outs) == n_ro + n_ao, (name, len(outs), n_ro, n_ao)
        o_refs = refs[n_r + n_c:]
        for ref, val in zip(o_refs[:n_ro], outs[:n_ro]):
            ref[...] = val.astype(ref.dtype)
        if n_ao:
            @pl.when(pl.program_id(0) == 0)
            def _():
                for ref in o_refs[n_ro:]:
                    ref[...] = jnp.zeros_like(ref)
            for ref, val in zip(o_refs[n_ro:], outs[n_ro:]):
                ref[...] += jnp.broadcast_to(val, ref.shape).astype(f32)

    def full(shape):
        nd = len(shape)
        return pl.BlockSpec(tuple(shape), lambda i: (0,) * nd)

    in_specs = [pl.BlockSpec((tm, r.shape[1]), lambda i: (i, 0)) for r in rows] + [full(c.shape) for c in consts]
    out_specs = [pl.BlockSpec((tm, w), lambda i: (i, 0)) for (w, _) in row_outs] + [full(s) for s in acc_outs]
    out_shape = [SDS((S, w), dt) for (w, dt) in row_outs] + [SDS(tuple(s), f32) for s in acc_outs]
    res = pl.pallas_call(
        body, name=name, grid=(S // tm,), in_specs=in_specs, out_specs=out_specs, out_shape=out_shape,
        compiler_params=pltpu.CompilerParams(dimension_semantics=("arbitrary",)),
    )(*rows, *consts)
    return res


def _rowwise_bwd(f, rows, consts, cts, want, row_dtypes, *, adds=None, tm=256, name="rowwise_bwd"):
    n_r, n_c, n_ct = len(rows), len(consts), len(cts)
    adds = adds or {}
    add_keys = sorted(adds)
    add_rows = [adds[k] for k in add_keys]

    def fn(*args):
        r = [a.astype(f32) for a in args[:n_r]]
        ct = [a.astype(f32) for a in args[n_r:n_r + n_ct]]
        ad = args[n_r + n_ct:n_r + n_ct + len(add_rows)]
        c = list(args[n_r + n_ct + len(add_rows):])
        _, vjp = jax.vjp(f, *r, *c)
        g = vjp(ct[0] if n_ct == 1 else tuple(ct))
        d_rows = []
        for pos, i in enumerate(want):
            d = g[i]
            if pos in adds:
                d = d + ad[add_keys.index(pos)].astype(f32)
            d_rows.append(d)
        return (*d_rows, *g[n_r:])

    return _rowwise(fn, list(rows) + list(cts) + add_rows, list(consts),
                    [(rows[i].shape[1], dt) for i, dt in zip(want, row_dtypes)],
                    [c.shape for c in consts], tm=tm, name=name)


def _sum_lead(arr, out_dtype, name):
    n, R, C = arr.shape
    tm = _pick(R, (256, 128, 64, 32, 16, 8))

    def body(a_ref, o_ref):
        acc = a_ref[0].astype(f32)
        for q in range(1, n):
            acc = acc + a_ref[q].astype(f32)
        o_ref[...] = acc.astype(o_ref.dtype)

    return pl.pallas_call(
        body, name=name, grid=(R // tm,), in_specs=[pl.BlockSpec((n, tm, C), lambda i: (0, i, 0))],
        out_specs=pl.BlockSpec((tm, C), lambda i: (i, 0)), out_shape=SDS((R, C), out_dtype),
    )(arr)


_GROUPS = {
    "all": [(kx, ky, kc) for kx in (0, 1) for ky in (0, 1) for kc in (0, 1)][1:],
    "xy": [(0, 1, 0), (1, 0, 0), (1, 1, 0)],
    "c": [(0, 0, 1)],
}


XCHG_CHUNK_MIN_BYTES = 1 << 20
XCHG_CHUNKS = 16


def _group_pos(group, x, y, c):
    return {"all": 4 * x + 2 * y + c, "xy": 2 * x + y, "c": c}[group]


def _xchg(send, group, bcast, name):
    flips = _GROUPS[group]
    n = len(flips) + 1
    piece = tuple(send.shape if bcast else send.shape[1:])
    if not bcast:
        assert send.shape[0] == n
    lead = piece[0]
    nch = 1
    if math.prod(piece) * send.dtype.itemsize >= XCHG_CHUNK_MIN_BYTES:
        nch = lead if len(piece) > 2 else (XCHG_CHUNKS if lead % (XCHG_CHUNKS * 2 * BF16_ROWS) == 0 else 1)
    rows = lead // nch

    def body(s_ref, r_ref, send_sems, recv_sems, own_sems):
        x, y, c = lax.axis_index("x"), lax.axis_index("y"), lax.axis_index("c")
        me = _group_pos(group, x, y, c)

        def src(p, ch):
            ref = s_ref if bcast else s_ref.at[p]
            return ref.at[pl.ds(ch * rows, rows)]

        def dst(ch):
            return r_ref.at[me].at[pl.ds(ch * rows, rows)]

        owns = [pltpu.make_async_copy(src(me, ch), dst(ch), own_sems.at[ch]) for ch in range(nch)]
        for cp in owns:
            cp.start()
        copies = []
        for ch in range(nch):
            for k, (kx, ky, kc) in enumerate(flips):
                tx, ty, tc = x ^ kx, y ^ ky, c ^ kc
                cp = pltpu.make_async_remote_copy(
                    src_ref=src(_group_pos(group, tx, ty, tc), ch), dst_ref=dst(ch),
                    send_sem=send_sems.at[k, ch], recv_sem=recv_sems.at[k, ch],
                    device_id=(tx, ty, tc), device_id_type=pl.DeviceIdType.MESH)
                cp.start()
                copies.append(cp)
        for cp in copies + owns:
            cp.wait()

    return pl.pallas_call(
        body, name=name, out_shape=SDS((n,) + piece, send.dtype),
        in_specs=[pl.BlockSpec(memory_space=pl.ANY)], out_specs=pl.BlockSpec(memory_space=pl.ANY),
        scratch_shapes=[pltpu.SemaphoreType.DMA((n - 1, nch)), pltpu.SemaphoreType.DMA((n - 1, nch)), pltpu.SemaphoreType.DMA((nch,))],
        compiler_params=pltpu.CompilerParams(has_side_effects=True),
    )(send)


def _rms(x, g):
    return x * lax.rsqrt(jnp.mean(x * x, axis=-1, keepdims=True) + EPS) * g


def _normmod(x, g, shift, scale):
    return _rms(x, g) * (1.0 + scale) + shift


def _lane(shape):
    return lax.broadcasted_iota(jnp.int32, shape, 1)


def _partner(x):
    lane = _lane(x.shape)
    lo = (lane >= NOPE) & (lane < NOPE + ROPE // 2)
    hi = (lane >= NOPE + ROPE // 2) & (lane < NOPE + ROPE)
    return jnp.where(lo, pltpu.roll(x, HEAD_PAD - ROPE // 2, 1), jnp.where(hi, pltpu.roll(x, ROPE // 2, 1), 0.0))


@jax.custom_vjp
def _rope(x, cf, ss):
    return x * cf + _partner(x) * ss


def _rope_fwd(x, cf, ss):
    return _rope(x, cf, ss), (cf, ss)


def _rope_bwd(res, dy):
    cf, ss = res
    return dy * cf + _partner(dy * ss), jnp.zeros_like(cf), jnp.zeros_like(ss)


_rope.defvjp(_rope_fwd, _rope_bwd)


def _head_norm(xh, gn, gr):
    lane = _lane(xh.shape)
    x2 = xh * xh
    ms_n = jnp.sum(jnp.where(lane < NOPE, x2, 0.0), axis=-1, keepdims=True) * (1.0 / NOPE)
    ms_r = jnp.sum(jnp.where((lane >= NOPE) & (lane < NOPE + ROPE), x2, 0.0), axis=-1, keepdims=True) * (1.0 / ROPE)
    return xh * (lax.rsqrt(ms_n + EPS) * gn + lax.rsqrt(ms_r + EPS) * gr)


def _q_heads(qraw, cf, ss, gn, gr):
    outs = []
    for h in range(N_HEADS):
        xh = qraw[:, h * HEAD_PAD:(h + 1) * HEAD_PAD]
        outs.append(_rope(_head_norm(xh, gn, gr), cf, ss))
    return jnp.concatenate(outs, axis=1)


def _k_heads(kraw, kr, gn):
    outs = []
    zero = jnp.zeros_like(gn)
    for h in range(N_HEADS):
        xh = kraw[:, h * HEAD_PAD:(h + 1) * HEAD_PAD]
        outs.append(_head_norm(xh, gn, zero) + kr)
    return jnp.concatenate(outs, axis=1)


def _kv_a_post(kva, cf, ss, ga, gr):
    ckv = _rms(kva[:, :KV_RANK], ga)
    kr = _rope(_head_norm(kva[:, KV_RANK:], jnp.zeros_like(gr), gr), cf, ss)
    return ckv, kr


def _cmul(ar, ai, br, bi):
    return ar * br - ai * bi, ar * bi + ai * br


def _s5_scan_fwd(u, wbr, wbi, ar, ai, init_r, init_i, wcr, wci, *, want_y, tb, name):
    S, D = u.shape
    ns = D // SLAB_CH
    seg = S // SEGMENTS
    tb = min(tb, seg)
    nb = seg // tb
    R = tb * SEGMENTS
    nt = (((1,), (0,)), ((), ()))

    def body(u_ref, wbr_ref, wbi_ref, ar_ref, ai_ref, ir_ref, ii_ref, wcr_ref, wci_ref, *rest):
        if want_y:
            y_ref, er_ref, ei_ref, bsr_ref, bsi_ref, cr, ci, bur, bui, sr, si = rest
        else:
            er_ref, ei_ref, bsr_ref, bsi_ref, cr, ci, bur, bui, sr, si = rest
        t = pl.program_id(1)

        @pl.when(t == 0)
        def _():
            cr[...] = ir_ref[0]
            ci[...] = ii_ref[0]

        bsr_ref[0, 0] = cr[...]
        bsi_ref[0, 0] = ci[...]
        ub = u_ref[...].astype(bf16)
        bur[...] = lax.dot_general(ub, wbr_ref[0].astype(bf16), nt, preferred_element_type=f32)
        bui[...] = lax.dot_general(ub, wbi_ref[0].astype(bf16), nt, preferred_element_type=f32)
        a_r = jnp.broadcast_to(ar_ref[0], (SEGMENTS, SLAB_ST))
        a_i = jnp.broadcast_to(ai_ref[0], (SEGMENTS, SLAB_ST))

        def step(tau, carry):
            c_r, c_i = carry
            r = pl.multiple_of(tau * SEGMENTS, SEGMENTS)
            p_r, p_i = _cmul(a_r, a_i, c_r, c_i)
            n_r = p_r + bur[pl.ds(r, SEGMENTS), :]
            n_i = p_i + bui[pl.ds(r, SEGMENTS), :]
            sr[pl.ds(r, SEGMENTS), :] = n_r
            si[pl.ds(r, SEGMENTS), :] = n_i
            return n_r, n_i

        c_r, c_i = lax.fori_loop(0, tb, step, (cr[...], ci[...]))
        cr[...] = c_r
        ci[...] = c_i
        if want_y:
            y_ref[...] = (lax.dot_general(sr[...].astype(bf16), wcr_ref[0].astype(bf16), nt, preferred_element_type=f32)
                          - lax.dot_general(si[...].astype(bf16), wci_ref[0].astype(bf16), nt, preferred_element_type=f32))

        @pl.when(t == nb - 1)
        def _():
            er_ref[0] = c_r
            ei_ref[0] = c_i

    slab3 = lambda s: pl.BlockSpec((1,) + s, lambda k, t: (k, 0, 0))
    in_specs = [pl.BlockSpec((R, SLAB_CH), lambda k, t: (t, k)),
                slab3((SLAB_CH, SLAB_ST)), slab3((SLAB_CH, SLAB_ST)), slab3((1, SLAB_ST)), slab3((1, SLAB_ST)),
                slab3((SEGMENTS, SLAB_ST)), slab3((SEGMENTS, SLAB_ST)), slab3((SLAB_ST, SLAB_CH)), slab3((SLAB_ST, SLAB_CH))]
    out_specs = [slab3((SEGMENTS, SLAB_ST)), slab3((SEGMENTS, SLAB_ST)),
                 pl.BlockSpec((1, 1, SEGMENTS, SLAB_ST), lambda k, t: (t, k, 0, 0)),
                 pl.BlockSpec((1, 1, SEGMENTS, SLAB_ST), lambda k, t: (t, k, 0, 0))]
    out_shape = [SDS((ns, SEGMENTS, SLAB_ST), f32)] * 2 + [SDS((nb, ns, SEGMENTS, SLAB_ST), f32)] * 2
    if want_y:
        out_specs = [pl.BlockSpec((R, SLAB_CH), lambda k, t: (t, k))] + out_specs
        out_shape = [SDS((S, D), f32)] + out_shape
    return pl.pallas_call(
        body, name=name, grid=(ns, nb), in_specs=in_specs, out_specs=out_specs, out_shape=out_shape,
        scratch_shapes=[pltpu.VMEM((SEGMENTS, SLAB_ST), f32)] * 2 + [pltpu.VMEM((R, SLAB_ST), f32)] * 4,
        compiler_params=pltpu.CompilerParams(dimension_semantics=("parallel", "arbitrary")),
    )(u, wbr, wbi, ar, ai, init_r, init_i, wcr, wci)


def _s5_scan_bwd(u, dy, wbr, wbi, ar, ai, bs_r, bs_i, ginit_r, ginit_i, wcr, wci, *, full, tb, name):
    S, D = dy.shape
    ns = D // SLAB_CH
    seg = S // SEGMENTS
    tb = min(tb, seg)
    nb = seg // tb
    R = tb * SEGMENTS
    nn = (((1,), (0,)), ((), ()))
    nt = (((1,), (1,)), ((), ()))
    tn = (((0,), (0,)), ((), ()))

    def body(*refs):
        if full:
            (u_ref, dy_ref, wbr_ref, wbi_ref, ar_ref, ai_ref, bsr_ref, bsi_ref, gir_ref, gii_ref, wcr_ref, wci_ref,
             du_ref, dwbr_ref, dwbi_ref, dwcr_ref, dwci_ref, dar_ref, dai_ref, gfr_ref, gfi_ref,
             gr_c, gi_c, bur, bui, sr, si, gsr, gsi) = refs
        else:
            (dy_ref, ar_ref, ai_ref, gir_ref, gii_ref, wcr_ref, wci_ref, gfr_ref, gfi_ref, gr_c, gi_c, gsr, gsi) = refs
        t = pl.program_id(1)

        @pl.when(t == 0)
        def _():
            gr_c[...] = gir_ref[0]
            gi_c[...] = gii_ref[0]
            if full:
                for ref in (dwbr_ref, dwbi_ref, dwcr_ref, dwci_ref, dar_ref, dai_ref):
                    ref[...] = jnp.zeros_like(ref)

        a_r = jnp.broadcast_to(ar_ref[0], (SEGMENTS, SLAB_ST))
        a_i = jnp.broadcast_to(ai_ref[0], (SEGMENTS, SLAB_ST))
        dyb = dy_ref[...].astype(bf16)
        gsr[...] = lax.dot_general(dyb, wcr_ref[0].astype(bf16), nt, preferred_element_type=f32)
        gsi[...] = -lax.dot_general(dyb, wci_ref[0].astype(bf16), nt, preferred_element_type=f32)

        if full:
            ub = u_ref[...].astype(bf16)
            bur[...] = lax.dot_general(ub, wbr_ref[0].astype(bf16), nn, preferred_element_type=f32)
            bui[...] = lax.dot_general(ub, wbi_ref[0].astype(bf16), nn, preferred_element_type=f32)

            def fstep(tau, carry):
                c_r, c_i = carry
                r = pl.multiple_of(tau * SEGMENTS, SEGMENTS)
                p_r, p_i = _cmul(a_r, a_i, c_r, c_i)
                n_r = p_r + bur[pl.ds(r, SEGMENTS), :]
                n_i = p_i + bui[pl.ds(r, SEGMENTS), :]
                sr[pl.ds(r, SEGMENTS), :] = n_r
                si[pl.ds(r, SEGMENTS), :] = n_i
                return n_r, n_i

            lax.fori_loop(0, tb, fstep, (bsr_ref[0, 0], bsi_ref[0, 0]))

        def adj(g_r, g_i):
            return a_r * g_r + a_i * g_i, a_r * g_i - a_i * g_r

        def rstep(k, carry):
            tau = tb - 1 - k
            r = pl.multiple_of(tau * SEGMENTS, SEGMENTS)
            if full:
                g_r, g_i, acc_r, acc_i = carry
            else:
                g_r, g_i = carry
            b_r, b_i = adj(g_r, g_i)
            n_r = b_r + gsr[pl.ds(r, SEGMENTS), :]
            n_i = b_i + gsi[pl.ds(r, SEGMENTS), :]
            gsr[pl.ds(r, SEGMENTS), :] = n_r
            gsi[pl.ds(r, SEGMENTS), :] = n_i
            if not full:
                return n_r, n_i
            rp = pl.multiple_of(jnp.maximum(tau - 1, 0) * SEGMENTS, SEGMENTS)
            first = tau == 0
            p_r = jnp.where(first, bsr_ref[0, 0], sr[pl.ds(rp, SEGMENTS), :])
            p_i = jnp.where(first, bsi_ref[0, 0], si[pl.ds(rp, SEGMENTS), :])
            return n_r, n_i, acc_r + n_r * p_r + n_i * p_i, acc_i + n_i * p_r - n_r * p_i

        zero = jnp.zeros((SEGMENTS, SLAB_ST), f32)
        if full:
            g_r, g_i, acc_r, acc_i = lax.fori_loop(0, tb, rstep, (gr_c[...], gi_c[...], zero, zero))
            dar_ref[0] += acc_r
            dai_ref[0] += acc_i
        else:
            g_r, g_i = lax.fori_loop(0, tb, rstep, (gr_c[...], gi_c[...]))
        gr_c[...] = g_r
        gi_c[...] = g_i

        if full:
            gbr = gsr[...].astype(bf16)
            gbi = gsi[...].astype(bf16)
            du_ref[...] = (lax.dot_general(gbr, wbr_ref[0].astype(bf16), nt, preferred_element_type=f32)
                           + lax.dot_general(gbi, wbi_ref[0].astype(bf16), nt, preferred_element_type=f32))
            dwbr_ref[0] += lax.dot_general(ub, gbr, tn, preferred_element_type=f32)
            dwbi_ref[0] += lax.dot_general(ub, gbi, tn, preferred_element_type=f32)
            dwcr_ref[0] += lax.dot_general(sr[...].astype(bf16), dyb, tn, preferred_element_type=f32)
            dwci_ref[0] -= lax.dot_general(si[...].astype(bf16), dyb, tn, preferred_element_type=f32)

        @pl.when(t == nb - 1)
        def _():
            gfr_ref[0] = g_r
            gfi_ref[0] = g_i

    slab3 = lambda s: pl.BlockSpec((1,) + s, lambda k, t: (k, 0, 0))
    rev_rows = pl.BlockSpec((R, SLAB_CH), lambda k, t: (nb - 1 - t, k))
    rev_bs = pl.BlockSpec((1, 1, SEGMENTS, SLAB_ST), lambda k, t: (nb - 1 - t, k, 0, 0))
    st = slab3((SEGMENTS, SLAB_ST))
    state_sds = SDS((ns, SEGMENTS, SLAB_ST), f32)
    if full:
        in_specs = [rev_rows, rev_rows, slab3((SLAB_CH, SLAB_ST)), slab3((SLAB_CH, SLAB_ST)), slab3((1, SLAB_ST)), slab3((1, SLAB_ST)),
                    rev_bs, rev_bs, st, st, slab3((SLAB_ST, SLAB_CH)), slab3((SLAB_ST, SLAB_CH))]
        args = (u, dy, wbr, wbi, ar, ai, bs_r, bs_i, ginit_r, ginit_i, wcr, wci)
        out_specs = [rev_rows, slab3((SLAB_CH, SLAB_ST)), slab3((SLAB_CH, SLAB_ST)), slab3((SLAB_ST, SLAB_CH)), slab3((SLAB_ST, SLAB_CH)),
                     st, st, st, st]
        out_shape = [SDS((S, D), f32), SDS((ns, SLAB_CH, SLAB_ST), f32), SDS((ns, SLAB_CH, SLAB_ST), f32),
                     SDS((ns, SLAB_ST, SLAB_CH), f32), SDS((ns, SLAB_ST, SLAB_CH), f32)] + [state_sds] * 4
        scratch = [pltpu.VMEM((SEGMENTS, SLAB_ST), f32)] * 2 + [pltpu.VMEM((R, SLAB_ST), f32)] * 6
    else:
        in_specs = [rev_rows, slab3((1, SLAB_ST)), slab3((1, SLAB_ST)), st, st, slab3((SLAB_ST, SLAB_CH)), slab3((SLAB_ST, SLAB_CH))]
        args = (dy, ar, ai, ginit_r, ginit_i, wcr, wci)
        out_specs = [st, st]
        out_shape = [state_sds] * 2
        scratch = [pltpu.VMEM((SEGMENTS, SLAB_ST), f32)] * 2 + [pltpu.VMEM((R, SLAB_ST), f32)] * 2
    return pl.pallas_call(
        body, name=name, grid=(ns, nb), in_specs=in_specs, out_specs=out_specs, out_shape=out_shape, scratch_shapes=scratch,
        compiler_params=pltpu.CompilerParams(dimension_semantics=("parallel", "arbitrary")),
    )(*args)


def _s5_chain(e_r, e_i, ar, ai, *, seg, reverse, name):
    ns = e_r.shape[0]
    assert seg & (seg - 1) == 0

    def body(er_ref, ei_ref, ar_ref, ai_ref, or_ref, oi_ref):
        p_r, p_i = ar_ref[0], ai_ref[0]
        if reverse:
            p_i = -p_i
        for _ in range(seg.bit_length() - 1):
            p_r, p_i = _cmul(p_r, p_i, p_r, p_i)
        c_r = jnp.zeros((1, SLAB_ST), f32)
        c_i = jnp.zeros((1, SLAB_ST), f32)
        order = range(SEGMENTS - 1, -1, -1) if reverse else range(SEGMENTS)
        for j in order:
            or_ref[0, pl.ds(j, 1), :] = c_r
            oi_ref[0, pl.ds(j, 1), :] = c_i
            m_r, m_i = _cmul(p_r, p_i, c_r, c_i)
            c_r = er_ref[0, pl.ds(j, 1), :] + m_r
            c_i = ei_ref[0, pl.ds(j, 1), :] + m_i

    st = pl.BlockSpec((1, SEGMENTS, SLAB_ST), lambda k: (k, 0, 0))
    av = pl.BlockSpec((1, 1, SLAB_ST), lambda k: (k, 0, 0))
    return pl.pallas_call(
        body, name=name, grid=(ns,), in_specs=[st, st, av, av], out_specs=[st, st],
        out_shape=[SDS(e_r.shape, f32)] * 2,
    )(e_r, e_i, ar, ai)


def _s5_disc(lr, li, log_dt):
    dt = jnp.exp(log_dt)
    mag = jnp.exp(lr * dt)
    ab_re = mag * jnp.cos(li * dt)
    ab_im = mag * jnp.sin(li * dt)
    den = lr * lr + li * li
    nr = ab_re - 1.0
    ni = ab_im
    return ab_re, ab_im, (nr * lr + ni * li) / den, (ni * lr - nr * li) / den


def _s5_bbar(b_r, b_i, fr, fi):
    return fr * b_r - fi * b_i, fr * b_i + fi * b_r


def _blockdiag(w, rows, cols):
    g = w.shape[0]
    w = w.reshape(g // 8, 8, rows, cols)
    eye = jnp.eye(8, dtype=w.dtype)
    full = w[:, :, :, None, :] * eye[None, :, None, :, None]
    return full.reshape(g // 8, 8 * rows, 8 * cols)


def _blockdiag_take(w, rows, cols):
    ns = w.shape[0]
    w = w.reshape(ns, 8, rows, 8, cols)
    idx = jnp.arange(8)
    d = w[:, idx, :, idx, :]
    return jnp.moveaxis(d, 0, 1).reshape(ns * 8, rows, cols)


def _mask(i, j, tq, tk):
    row = i * tq + lax.broadcasted_iota(jnp.int32, (tq, tk), 0)
    col = j * tk + lax.broadcasted_iota(jnp.int32, (tq, tk), 1)
    return (row // CHUNK) >= (col // CHUNK)


_NT = (((1,), (1,)), ((), ()))
_NN = (((1,), (0,)), ((), ()))
_TN = (((0,), (0,)), ((), ()))
_NEG = -1e30


def _attn_fwd(q, k, v, *, t, name):
    S = q.shape[0]
    t = min(t, S)
    nq = S // t

    def body(q_ref, k_ref, v_ref, o_ref, lse_ref, m_s, l_s, acc):
        i = pl.program_id(1)
        m_s[...] = jnp.full_like(m_s, _NEG)
        l_s[...] = jnp.zeros_like(l_s)
        acc[...] = jnp.zeros_like(acc)
        qb = q_ref[...]

        def block(j, masked):
            r = pl.multiple_of(j * t, t)
            s = lax.dot_general(qb, k_ref[pl.ds(r, t), :], _NT, preferred_element_type=f32) * ATTN_SCALE
            if masked:
                s = jnp.where(_mask(0, 0, t, t), s, _NEG)
            m_old = m_s[...]
            m_new = jnp.maximum(m_old, jnp.max(s, axis=-1, keepdims=True))
            alpha = jnp.exp(m_old - m_new)
            p = jnp.exp(s - m_new)
            l_s[...] = alpha * l_s[...] + jnp.sum(p, axis=-1, keepdims=True)
            acc[...] = alpha * acc[...] + lax.dot_general(p.astype(bf16), v_ref[pl.ds(r, t), :], _NN, preferred_element_type=f32)
            m_s[...] = m_new

        def below(j, carry):
            block(j, False)
            return carry

        lax.fori_loop(0, i, below, 0)
        block(i, True)
        o_ref[...] = acc[...] / l_s[...]
        lse_ref[0] = m_s[...] + jnp.log(l_s[...])

    qs = pl.BlockSpec((t, HEAD_PAD), lambda h, i: (i, h))
    ks = pl.BlockSpec((S, HEAD_PAD), lambda h, i: (0, h))
    return pl.pallas_call(
        body, name=name, grid=(N_HEADS, nq), in_specs=[qs, ks, ks],
        out_specs=[qs, pl.BlockSpec((1, t, 1), lambda h, i: (h, i, 0))],
        out_shape=[SDS((S, N_HEADS * HEAD_PAD), f32), SDS((N_HEADS, S, 1), f32)],
        scratch_shapes=[pltpu.VMEM((t, 1), f32), pltpu.VMEM((t, 1), f32), pltpu.VMEM((t, HEAD_PAD), f32)],
        compiler_params=pltpu.CompilerParams(dimension_semantics=("parallel", "arbitrary")),
    )(q, k, v)


def _attn_bwd(q, k, v, do, o, lse, *, t, name):
    S = q.shape[0]
    t = min(t, S)
    nq = S // t

    def body(q_ref, k_ref, v_ref, do_ref, o_ref, lse_ref, dq_ref, dk_ref, dv_ref, d_s, dk_acc, dv_acc):
        j = pl.program_id(1)

        @pl.when(j == 0)
        def _():
            dq_ref[...] = jnp.zeros_like(dq_ref)
            d_s[...] = jnp.sum(do_ref[...].astype(f32) * o_ref[...], axis=-1, keepdims=True)

        dk_acc[...] = jnp.zeros_like(dk_acc)
        dv_acc[...] = jnp.zeros_like(dv_acc)
        kb = k_ref[...]
        vb = v_ref[...]

        def block(i, masked):
            r = pl.multiple_of(i * t, t)
            qb = q_ref[pl.ds(r, t), :]
            dob = do_ref[pl.ds(r, t), :]
            s = lax.dot_general(qb, kb, _NT, preferred_element_type=f32) * ATTN_SCALE
            if masked:
                s = jnp.where(_mask(0, 0, t, t), s, _NEG)
            p = jnp.exp(s - lse_ref[0, pl.ds(r, t), :])
            dp = lax.dot_general(dob, vb, _NT, preferred_element_type=f32)
            ds = (p * (dp - d_s[pl.ds(r, t), :]) * ATTN_SCALE).astype(bf16)
            dv_acc[...] += lax.dot_general(p.astype(bf16), dob, _TN, preferred_element_type=f32)
            dk_acc[...] += lax.dot_general(ds, qb, _TN, preferred_element_type=f32)
            dq_ref[pl.ds(r, t), :] += lax.dot_general(ds, kb, _NN, preferred_element_type=f32)

        def below(i, carry):
            block(i, False)
            return carry

        block(j, True)
        lax.fori_loop(j + 1, nq, below, 0)
        dk_ref[...] = dk_acc[...]
        dv_ref[...] = dv_acc[...]

    hs = pl.BlockSpec((S, HEAD_PAD), lambda h, j: (0, h))
    ks = pl.BlockSpec((t, HEAD_PAD), lambda h, j: (j, h))
    ls = pl.BlockSpec((1, S, 1), lambda h, j: (h, 0, 0))
    return pl.pallas_call(
        body, name=name, grid=(N_HEADS, nq), in_specs=[hs, ks, ks, hs, hs, ls], out_specs=[hs, ks, ks],
        out_shape=[SDS((S, N_HEADS * HEAD_PAD), f32)] * 3,
        scratch_shapes=[pltpu.VMEM((S, 1), f32), pltpu.VMEM((t, HEAD_PAD), f32), pltpu.VMEM((t, HEAD_PAD), f32)],
        compiler_params=pltpu.CompilerParams(dimension_semantics=("parallel", "arbitrary")),
    )(q, k, v, do, o, lse)


def _pad_heads(w, per_head, axis):
    w = jnp.moveaxis(w, axis, -1)
    lead = w.shape[:-1]
    w = w.reshape(lead + (N_HEADS, per_head))
    w = jnp.pad(w, [(0, 0)] * len(lead) + [(0, 0), (0, HEAD_PAD - per_head)])
    return jnp.moveaxis(w.reshape(lead + (N_HEADS * HEAD_PAD,)), -1, axis)


def _unpad_heads(w, per_head, axis):
    w = jnp.moveaxis(w, axis, -1)
    lead = w.shape[:-1]
    w = w.reshape(lead + (N_HEADS, HEAD_PAD))[..., :per_head]
    return jnp.moveaxis(w.reshape(lead + (N_HEADS * per_head,)), -1, axis)


def _lanes128(vec, offset):
    return jnp.zeros((1, HEAD_PAD), f32).at[0, offset:offset + vec.shape[0]].set(vec)


def _to_segments(a):
    S, D = a.shape
    return a.reshape(SEGMENTS, S // SEGMENTS, D).transpose(1, 0, 2).reshape(S, D)


def _from_segments(a):
    S, D = a.shape
    return a.reshape(S // SEGMENTS, SEGMENTS, D).transpose(1, 0, 2).reshape(S, D)


def _s5_params_fwd(p):
    G, N, P = p["b_re"].shape
    def disc_body(lr, li, ld, o1, o2, o3, o4):
        o1[...], o2[...], o3[...], o4[...] = _s5_disc(lr[...], li[...], ld[...])

    ab_r, ab_i, f_r, f_i = pl.pallas_call(disc_body, name="s5_disc", out_shape=[SDS((G, N), f32)] * 4)(
        p["lam_re"], p["lam_im"], p["log_dt"].reshape(G, 1))
    bb_r, bb_i = _rowwise(_s5_bbar, [p["b_re"].reshape(G * N, P), p["b_im"].reshape(G * N, P), f_r.reshape(G * N, 1), f_i.reshape(G * N, 1)],
                          [], [(P, f32), (P, f32)], tm=512, name="s5_bbar")
    ns = G // 8
    out = dict(
        f_r=f_r, f_i=f_i,
        a_r=ab_r.reshape(ns, 1, SLAB_ST), a_i=ab_i.reshape(ns, 1, SLAB_ST),
        wb_r=_blockdiag(bb_r.reshape(G, N, P).transpose(0, 2, 1), P, N), wb_i=_blockdiag(bb_i.reshape(G, N, P).transpose(0, 2, 1), P, N),
        wc_r=_blockdiag(p["c_re"].transpose(0, 2, 1), N, P), wc_i=_blockdiag(p["c_im"].transpose(0, 2, 1), N, P),
    )
    return out


def _s5_params_bwd(p, sp, d_ar, d_ai, d_wbr, d_wbi, d_wcr, d_wci):
    G, N, P = p["b_re"].shape
    dbb_r = _blockdiag_take(d_wbr, P, N).transpose(0, 2, 1).reshape(G * N, P)
    dbb_i = _blockdiag_take(d_wbi, P, N).transpose(0, 2, 1).reshape(G * N, P)
    d_cre = _blockdiag_take(d_wcr, N, P).transpose(0, 2, 1)
    d_cim = _blockdiag_take(d_wci, N, P).transpose(0, 2, 1)
    rows = [p["b_re"].reshape(G * N, P), p["b_im"].reshape(G * N, P), sp["f_r"].reshape(G * N, 1), sp["f_i"].reshape(G * N, 1)]
    d_br, d_bi, d_fr, d_fi = _rowwise_bwd(_s5_bbar, rows, [], [dbb_r, dbb_i], [0, 1, 2, 3], [f32] * 4, tm=512, name="s5_bbar_bwd")

    def seg_sum(d, name):
        return _sum_lead(d.transpose(1, 0, 2).reshape(SEGMENTS, G, N), f32, name)

    def body(lr, li, ld, c1, c2, c3, c4, o1, o2, o3):
        _, vjp = jax.vjp(_s5_disc, lr[...], li[...], ld[...])
        o1[...], o2[...], o3[...] = vjp((c1[...], c2[...], c3[...], c4[...]))

    d_lr, d_li, d_ld = pl.pallas_call(
        body, name="s5_disc_bwd", out_shape=[SDS((G, N), f32), SDS((G, N), f32), SDS((G, 1), f32)],
    )(p["lam_re"], p["lam_im"], p["log_dt"].reshape(G, 1), seg_sum(d_ar, "s5_da_re_sum"), seg_sum(d_ai, "s5_da_im_sum"),
      d_fr.reshape(G, N), d_fi.reshape(G, N))
    return dict(lam_re=d_lr, lam_im=d_li, log_dt=d_ld.reshape(G), b_re=d_br.reshape(G, N, P), b_im=d_bi.reshape(G, N, P),
                c_re=d_cre, c_im=d_cim)


_SCAN_TB = 32
_ATTN_T = 512


def _s5_mix_fwd(h, sp, name):
    S = h.shape[0]
    seg = S // SEGMENTS
    zeros = jnp.zeros((h.shape[1] // SLAB_CH, SEGMENTS, SLAB_ST), f32)
    common = (sp["wb_r"], sp["wb_i"], sp["a_r"], sp["a_i"])
    e_r, e_i, _, _ = _s5_scan_fwd(h, *common, zeros, zeros, sp["wc_r"], sp["wc_i"], want_y=False, tb=_SCAN_TB, name=name + "_local")
    i_r, i_i = _s5_chain(e_r, e_i, sp["a_r"], sp["a_i"], seg=seg, reverse=False, name=name + "_chain")
    y, _, _, bs_r, bs_i = _s5_scan_fwd(h, *common, i_r, i_i, sp["wc_r"], sp["wc_i"], want_y=True, tb=_SCAN_TB, name=name)
    return y, (bs_r, bs_i)


def _s5_mix_bwd(h, dy, sp, saved, name):
    S = h.shape[0]
    seg = S // SEGMENTS
    bs_r, bs_i = saved
    zeros = jnp.zeros((h.shape[1] // SLAB_CH, SEGMENTS, SLAB_ST), f32)
    gf_r, gf_i = _s5_scan_bwd(None, dy, None, None, sp["a_r"], sp["a_i"], None, None, zeros, zeros, sp["wc_r"], sp["wc_i"],
                              full=False, tb=_SCAN_TB, name=name + "_local")
    gi_r, gi_i = _s5_chain(gf_r, gf_i, sp["a_r"], sp["a_i"], seg=seg, reverse=True, name=name + "_chain")
    du, d_wbr, d_wbi, d_wcr, d_wci, d_ar, d_ai, _, _ = _s5_scan_bwd(
        h, dy, sp["wb_r"], sp["wb_i"], sp["a_r"], sp["a_i"], bs_r, bs_i, gi_r, gi_i, sp["wc_r"], sp["wc_i"],
        full=True, tb=_SCAN_TB, name=name)
    return du, (d_ar, d_ai, d_wbr, d_wbi, d_wcr, d_wci)


def _ffn_fwd(x1, mod, n2g, w_gu, w_down, tag):
    F = w_down.shape[0]
    h2 = _rowwise(lambda x, g, m: _normmod(x, g, m[3:4], m[4:5]), [x1], [n2g, mod], [(x1.shape[1], bf16)], name=tag + "_norm2")[0]
    ab = _mm(h2, w_gu, name=tag + "_gu")
    act = _rowwise(lambda ab: jax.nn.silu(ab[:, :F]) * ab[:, F:], [ab], [], [(F, bf16)], name=tag + "_act")[0]
    f = _mm(act, w_down, name=tag + "_down")
    x2 = _rowwise(lambda x, f, m: x + m[5:6] * f, [x1, f], [mod], [(x1.shape[1], f32)], name=tag + "_res2")[0]
    return x2, dict(x1=x1, h2=h2, ab=ab, act=act, f=f)


def _ffn_bwd(dx2, sv, mod, n2g, w_gu, w_down, tag):
    F = w_down.shape[0]
    D = dx2.shape[1]
    df, dgate2 = _rowwise(lambda dx, f, m: (m[5:6] * dx, jnp.sum(dx * f, axis=0, keepdims=True)), [dx2, sv["f"]], [mod],
                          [(D, bf16)], [(1, D)], name=tag + "_res2_bwd")
    dact = _mm(df, w_down, tb=True, name=tag + "_down_dx")
    dw_down = _mm(sv["act"], df, ta=True, name=tag + "_down_dw")
    dab = _rowwise_bwd(lambda ab: jax.nn.silu(ab[:, :F]) * ab[:, F:], [sv["ab"]], [], [dact], [0], [bf16], name=tag + "_act_bwd")[0]
    dh2 = _mm(dab, w_gu, tb=True, name=tag + "_gu_dx")
    dw_gu = _mm(sv["h2"], dab, ta=True, name=tag + "_gu_dw")
    dx1, dn2g, dmod = _rowwise_bwd(lambda x, g, m: _normmod(x, g, m[3:4], m[4:5]), [sv["x1"]], [n2g, mod], [dh2], [0], [f32],
                                   adds={0: dx2}, name=tag + "_norm2_bwd")
    dmod = dmod.at[5:6].add(dgate2)
    return dx1, dw_gu, dw_down, dn2g, dmod


def _device_step(x, target, pos, mods, kmod, W):
    S, D = x.shape
    F = W["ffn_w_down"].shape[1]
    inv = 1.0 / (ROPE_THETA ** (jnp.arange(0, ROPE, 2, dtype=f32) / ROPE))
    inv128 = _lanes128(jnp.concatenate([inv, inv]), NOPE)
    sign128 = _lanes128(jnp.concatenate([-jnp.ones(ROPE // 2, f32), jnp.ones(ROPE // 2, f32)]), NOPE)
    cf, ss = _rowwise(lambda p, iv, sg: (jnp.cos(p * iv), jnp.sin(p * iv) * sg), [pos], [inv128, sign128],
                      [(HEAD_PAD, f32), (HEAD_PAD, f32)], name="rope_table")

    w_gu = [jnp.concatenate([W["ffn_w_gate"][l], W["ffn_w_up"][l]], axis=1) for l in range(DEPTH)]
    saved = []
    xs = _to_segments(x)

    s5p = []
    for l in range(N_A):
        tag = f"l{l}"
        mod = mods[l]
        p = {k: W["s5_" + k][l] for k in ("lam_re", "lam_im", "log_dt", "b_re", "b_im", "c_re", "c_im")}
        sp = _s5_params_fwd(p)
        s5p.append((p, sp))
        n1g = W["norm1_g"][l][None]
        dsk = W["s5_d"][l][None]
        bgl = W["s5_b_glu"][l][None]
        h = _rowwise(lambda x, g, m: _normmod(x, g, m[0:1], m[1:2]), [xs], [n1g, mod], [(D, f32)], name=tag + "_norm1")[0]
        y_ssm, scan_saved = _s5_mix_fwd(h, sp, tag + "_scan")
        g = _rowwise(lambda ys, h, d: jax.nn.gelu(ys + d * h), [y_ssm, h], [dsk], [(D, f32)], name=tag + "_gelu")[0]
        z = _mm(g, W["s5_w_glu"][l], name=tag + "_glu")
        x1 = _rowwise(lambda x, g, z, b, m: x + m[2:3] * (g * jax.nn.sigmoid(z + b)), [xs, g, z], [bgl, mod], [(D, f32)], name=tag + "_res1")[0]
        x2, fsv = _ffn_fwd(x1, mod, W["norm2_g"][l][None], w_gu[l], W["ffn_w_down"][l], tag)
        saved.append(dict(x0=xs, h=h, y_ssm=y_ssm, g=g, z=z, scan=scan_saved, ffn=fsv))
        xs = x2
    x = _from_segments(xs)

    w_kv_a = jnp.concatenate([W["w_kv_a"][:, :KV_RANK], jnp.zeros((D, NOPE), f32).astype(W["w_kv_a"].dtype), W["w_kv_a"][:, KV_RANK:],
                              jnp.zeros((D, HEAD_PAD - NOPE - ROPE), f32).astype(W["w_kv_a"].dtype)], axis=1)
    wkb = W["w_kv_b"].reshape(KV_RANK, N_HEADS, NOPE + VDIM)
    w_kb = _pad_heads(wkb[:, :, :NOPE].reshape(KV_RANK, N_HEADS * NOPE), NOPE, 1)
    w_vb = _pad_heads(wkb[:, :, NOPE:].reshape(KV_RANK, N_HEADS * VDIM), VDIM, 1)
    kvg = W["kv_norm_g"][None]
    ga = W["kv_a_norm_g"][None]
    gkr = _lanes128(W["k_rope_norm_g"], NOPE)
    gkn = _lanes128(W["k_nope_norm_g"], 0)
    hk = _rowwise(lambda x, g, m: _normmod(x, g, m[0:1], m[1:2]), [x], [kvg, kmod], [(D, bf16)], name="kv_norm")[0]
    kva = _mm(hk, w_kv_a, name="kv_a")
    ckv, kr = _rowwise(_kv_a_post, [kva, cf, ss], [ga, gkr], [(KV_RANK, f32), (HEAD_PAD, f32)], name="kv_a_post")
    kraw = _mm(ckv, w_kb, name="kv_bk")
    vpad = _mm(ckv, w_vb, out_dtype=bf16, name="kv_bv")
    kpad = _rowwise(_k_heads, [kraw, kr], [gkn], [(N_HEADS * HEAD_PAD, bf16)], name="k_heads")[0]
    kv_saved = dict(x=x, hk=hk, kva=kva, ckv=ckv, kr=kr, kraw=kraw)

    for l in range(N_A, DEPTH):
        tag = f"l{l}"
        j = l - N_A
        mod = mods[l]
        n1g = W["norm1_g"][l][None]
        w_uq = _pad_heads(W["mla_w_uq"][j], NOPE + ROPE, 1)
        w_o = _pad_heads(W["mla_w_o"][j], VDIM, 0)
        qg = W["mla_q_norm_g"][j][None]
        gqn = _lanes128(W["mla_q_nope_norm_g"][j], 0)
        gqr = _lanes128(W["mla_q_rope_norm_g"][j], NOPE)
        h = _rowwise(lambda x, g, m: _normmod(x, g, m[0:1], m[1:2]), [x], [n1g, mod], [(D, bf16)], name=tag + "_norm1")[0]
        qa = _mm(h, W["mla_w_dq"][j], name=tag + "_dq")
        qc = _rowwise(_rms, [qa], [qg], [(qa.shape[1], f32)], name=tag + "_qnorm")[0]
        qraw = _mm(qc, w_uq, name=tag + "_uq")
        q = _rowwise(_q_heads, [qraw, cf, ss], [gqn, gqr], [(N_HEADS * HEAD_PAD, bf16)], name=tag + "_q_heads")[0]
        o, lse = _attn_fwd(q, kpad, vpad, t=_ATTN_T, name=tag + "_attn")
        mix = _mm(o, w_o, name=tag + "_wo")
        x1 = _rowwise(lambda x, mx, m: x + m[2:3] * mx, [x, mix], [mod], [(D, f32)], name=tag + "_res1")[0]
        x2, fsv = _ffn_fwd(x1, mod, W["norm2_g"][l][None], w_gu[l], W["ffn_w_down"][l], tag)
        saved.append(dict(x0=x, h=h, qa=qa, qc=qc, qraw=qraw, q=q, o=o, lse=lse, mix=mix, w_uq=w_uq, w_o=w_o, ffn=fsv))
        x = x2

    dx, loss = _rowwise(
        lambda y, t: ((y - t) * (1.0 / D), jnp.full((1, LANES), 0.5 * jnp.sum(jnp.mean(jnp.square(y - t), axis=-1)), f32)),
        [x, target], [], [(D, f32)], [(1, LANES)], name="loss")

    gW = {}
    dmods = [None] * DEPTH
    g_gate, g_up, g_down, g_n1, g_n2 = [None] * DEPTH, [None] * DEPTH, [None] * DEPTH, [None] * DEPTH, [None] * DEPTH
    dks, dvs = [], []
    g_dq, g_qn, g_uq, g_qnn, g_qrn, g_wo = [None] * 2, [None] * 2, [None] * 2, [None] * 2, [None] * 2, [None] * 2
    for l in range(DEPTH - 1, N_A - 1, -1):
        tag = f"l{l}"
        j = l - N_A
        sv = saved[l]
        mod = mods[l]
        dx1, dw_gu, dw_down, g_n2[l], dmod = _ffn_bwd(dx, sv["ffn"], mod, W["norm2_g"][l][None], w_gu[l], W["ffn_w_down"][l], tag)
        g_gate[l], g_up[l], g_down[l] = dw_gu[:, :F], dw_gu[:, F:], dw_down
        dmix, dgate1 = _rowwise(lambda dx, mx, m: (m[2:3] * dx, jnp.sum(dx * mx, axis=0, keepdims=True)), [dx1, sv["mix"]], [mod],
                                [(D, bf16)], [(1, D)], name=tag + "_res1_bwd")
        do = _mm(dmix, sv["w_o"], tb=True, out_dtype=bf16, name=tag + "_wo_dx")
        g_wo[j] = _unpad_heads(_mm(sv["o"], dmix, ta=True, name=tag + "_wo_dw"), VDIM, 0)
        dq, dk, dv = _attn_bwd(sv["q"], kpad, vpad, do, sv["o"], sv["lse"], t=_ATTN_T, name=tag + "_attn_bwd")
        dks.append(dk)
        dvs.append(dv)
        gqn = _lanes128(W["mla_q_nope_norm_g"][j], 0)
        gqr = _lanes128(W["mla_q_rope_norm_g"][j], NOPE)
        dqraw, dgqn, dgqr = _rowwise_bwd(lambda qr, c, s, a, b: _q_heads(qr, c, s, a, b), [sv["qraw"], cf, ss], [gqn, gqr], [dq], [0], [bf16],
                                         name=tag + "_q_heads_bwd")
        g_qnn[j], g_qrn[j] = dgqn[0, :NOPE], dgqr[0, NOPE:NOPE + ROPE]
        dqc = _mm(dqraw, sv["w_uq"], tb=True, name=tag + "_uq_dx")
        g_uq[j] = _unpad_heads(_mm(sv["qc"], dqraw, ta=True, name=tag + "_uq_dw"), NOPE + ROPE, 1)
        qg = W["mla_q_norm_g"][j][None]
        dqa, dqg = _rowwise_bwd(_rms, [sv["qa"]], [qg], [dqc], [0], [bf16], name=tag + "_qnorm_bwd")
        g_qn[j] = dqg[0]
        dh = _mm(dqa, W["mla_w_dq"][j], tb=True, name=tag + "_dq_dx")
        g_dq[j] = _mm(sv["h"], dqa, ta=True, name=tag + "_dq_dw")
        dx, dn1g, dmod1 = _rowwise_bwd(lambda x, g, m: _normmod(x, g, m[0:1], m[1:2]), [sv["x0"]], [W["norm1_g"][l][None], mod], [dh], [0], [f32],
                                       adds={0: dx1}, name=tag + "_norm1_bwd")
        g_n1[l] = dn1g[0]
        dmods[l] = (dmod + dmod1).at[2:3].add(dgate1)

    gkn = _lanes128(W["k_nope_norm_g"], 0)
    dk_sum = _sum_lead(jnp.stack(dks), f32, "dk_sum")
    dv_sum = _sum_lead(jnp.stack(dvs), bf16, "dv_sum")
    dkraw, dkr, dgkn = _rowwise_bwd(lambda kr_, r, g: _k_heads(kr_, r, g), [kv_saved["kraw"], kv_saved["kr"]], [gkn], [dk_sum], [0, 1], [bf16, f32],
                                    name="k_heads_bwd")
    dckv = _sum_lead(jnp.stack([_mm(dkraw, w_kb, tb=True, name="kv_bk_dx"), _mm(dv_sum, w_vb, tb=True, name="kv_bv_dx")]), f32, "dckv_sum")
    g_kb = _unpad_heads(_mm(kv_saved["ckv"], dkraw, ta=True, name="kv_bk_dw"), NOPE, 1)
    g_vb = _unpad_heads(_mm(kv_saved["ckv"], dv_sum, ta=True, name="kv_bv_dw"), VDIM, 1)
    gW["w_kv_b"] = jnp.concatenate([g_kb.reshape(KV_RANK, N_HEADS, NOPE), g_vb.reshape(KV_RANK, N_HEADS, VDIM)], axis=2).reshape(KV_RANK, -1)
    dkva, dga, dgkr = _rowwise_bwd(_kv_a_post, [kv_saved["kva"], cf, ss], [ga, gkr], [dckv, dkr], [0], [bf16], name="kv_a_post_bwd")
    dhk = _mm(dkva, w_kv_a, tb=True, name="kv_a_dx")
    g_kva = _mm(kv_saved["hk"], dkva, ta=True, name="kv_a_dw")
    gW["w_kv_a"] = jnp.concatenate([g_kva[:, :KV_RANK], g_kva[:, KV_RANK + NOPE:KV_RANK + NOPE + ROPE]], axis=1)
    dx, dkvg, dkmod = _rowwise_bwd(lambda x, g, m: _normmod(x, g, m[0:1], m[1:2]), [kv_saved["x"]], [kvg, kmod], [dhk], [0], [f32],
                                   adds={0: dx}, name="kv_norm_bwd")
    gW["kv_norm_g"], gW["kv_a_norm_g"] = dkvg[0], dga[0]
    gW["k_nope_norm_g"], gW["k_rope_norm_g"] = dgkn[0, :NOPE], dgkr[0, NOPE:NOPE + ROPE]

    dxs = _to_segments(dx)
    g_s5 = {k: [None] * N_A for k in ("lam_re", "lam_im", "log_dt", "b_re", "b_im", "c_re", "c_im", "d", "w_glu", "b_glu")}
    for l in range(N_A - 1, -1, -1):
        tag = f"l{l}"
        sv = saved[l]
        mod = mods[l]
        p, sp = s5p[l]
        dsk = W["s5_d"][l][None]
        bgl = W["s5_b_glu"][l][None]
        dx1, dw_gu, dw_down, g_n2[l], dmod = _ffn_bwd(dxs, sv["ffn"], mod, W["norm2_g"][l][None], w_gu[l], W["ffn_w_down"][l], tag)
        g_gate[l], g_up[l], g_down[l] = dw_gu[:, :F], dw_gu[:, F:], dw_down

        def res1(g, z, b, m):
            return m[2:3] * (g * jax.nn.sigmoid(z + b))
        dg1, dz, dbgl, dmod_g = _rowwise_bwd(res1, [sv["g"], sv["z"]], [bgl, mod], [dx1], [0, 1], [f32, bf16], name=tag + "_res1_bwd")
        dg2 = _mm(dz, W["s5_w_glu"][l], tb=True, name=tag + "_glu_dx")
        g_s5["w_glu"][l] = _mm(sv["g"], dz, ta=True, name=tag + "_glu_dw")
        g_s5["b_glu"][l] = dbgl[0]

        def gelu_bwd(ys, h, dga_, dgb_, d):
            _, vjp = jax.vjp(lambda ys, h, d: jax.nn.gelu(ys + d * h), ys, h, d)
            dys, dh, dd = vjp(dga_ + dgb_)
            return dys, dh, dd
        dys, dh_a, ddsk = _rowwise(gelu_bwd, [sv["y_ssm"], sv["h"], dg1, dg2], [dsk], [(D, f32), (D, f32)], [(1, D)], name=tag + "_gelu_bwd")
        g_s5["d"][l] = ddsk[0]
        du, scan_g = _s5_mix_bwd(sv["h"], dys, sp, sv["scan"], tag + "_scan_bwd")
        pg = _s5_params_bwd(p, sp, *scan_g)
        for k in ("lam_re", "lam_im", "log_dt", "b_re", "b_im", "c_re", "c_im"):
            g_s5[k][l] = pg[k]
        dxs, dn1g, dmod1 = _rowwise_bwd(lambda x, g, m: _normmod(x, g, m[0:1], m[1:2]), [sv["x0"]], [W["norm1_g"][l][None], mod],
                                        [_sum_lead(jnp.stack([dh_a, du]), f32, tag + "_dh_sum")], [0], [f32], adds={0: dx1}, name=tag + "_norm1_bwd")
        g_n1[l] = dn1g[0]
        dmods[l] = dmod + dmod1 + dmod_g
    dx = _from_segments(dxs)

    gW.update(
        norm1_g=jnp.stack(g_n1), norm2_g=jnp.stack([g[0] for g in g_n2]),
        ffn_w_gate=jnp.stack(g_gate), ffn_w_up=jnp.stack(g_up), ffn_w_down=jnp.stack(g_down),
        mla_w_dq=jnp.stack(g_dq), mla_q_norm_g=jnp.stack(g_qn), mla_w_uq=jnp.stack(g_uq),
        mla_q_nope_norm_g=jnp.stack(g_qnn), mla_q_rope_norm_g=jnp.stack(g_qrn), mla_w_o=jnp.stack(g_wo),
    )
    for k, v in g_s5.items():
        gW["s5_" + k] = jnp.stack(v)
    return loss, dx, gW, jnp.stack(dmods), dkmod


_WEIGHTS = ['ada_w', 'ada_b', 'norm1_g', 'norm2_g', 'ffn_w_gate', 'ffn_w_up', 'ffn_w_down', 's5_lam_re', 's5_lam_im', 's5_log_dt',
            's5_b_re', 's5_b_im', 's5_c_re', 's5_c_im', 's5_d', 's5_w_glu', 's5_b_glu', 'kv_ada_w', 'kv_ada_b', 'kv_norm_g', 'w_kv_a',
            'kv_a_norm_g', 'w_kv_b', 'k_nope_norm_g', 'k_rope_norm_g', 'mla_w_dq', 'mla_q_norm_g', 'mla_w_uq', 'mla_q_nope_norm_g',
            'mla_q_rope_norm_g', 'mla_w_o']
_BIG = {"ffn_w_gate": 2, "ffn_w_up": 2, "ffn_w_down": 1, "s5_w_glu": 1, "w_kv_a": 0, "w_kv_b": 1, "mla_w_dq": 1, "mla_w_uq": 2, "mla_w_o": 1}
_TENSOR_PARALLEL = ("ada_w", "kv_ada_w")
_SHARDED_VECS = ("s5_d", "s5_b_glu")
_CHIPS = 4
_FLAT_W = 1024
_FLAT_ALIGN = 512


def _pack_rows(arrs, width, align):
    flat = jnp.concatenate([a.reshape(-1) for a in arrs])
    rows = -(-flat.shape[0] // (width * align)) * align
    flat = jnp.pad(flat, (0, rows * width - flat.shape[0]))
    return flat.reshape(rows, width)


def _unpack(flat, shapes):
    flat = flat.reshape(-1)
    out, off = [], 0
    for s in shapes:
        n = math.prod(s)
        out.append(flat[off:off + n].reshape(s))
        off += n
    return out


def _adamw(w, g, m, v, name):
    c1 = 1.0 - ADAM_B1 ** ADAM_STEP
    c2 = 1.0 - ADAM_B2 ** ADAM_STEP

    def fn(w, g, m, v):
        m = ADAM_B1 * m + (1.0 - ADAM_B1) * g
        v = ADAM_B2 * v + (1.0 - ADAM_B2) * jnp.square(g)
        delta = -ADAM_LR * ((m / c1) / (jnp.sqrt(v / c2) + ADAM_EPS) + ADAM_WD * w)
        return delta, m, v

    C = w.shape[1]
    tm = _pick(w.shape[0], (256, 128, 64, 32, 16, 8))
    return _rowwise(fn, [w, g, m, v], [], [(C, f32)] * 3, tm=tm, name=name)


def kernel(x, c, positions, ada_w, ada_b, norm1_g, norm2_g, ffn_w_gate, ffn_w_up, ffn_w_down, s5_lam_re, s5_lam_im, s5_log_dt, s5_b_re, s5_b_im, s5_c_re, s5_c_im, s5_d, s5_w_glu, s5_b_glu, kv_ada_w, kv_ada_b, kv_norm_g, w_kv_a, kv_a_norm_g, w_kv_b, k_nope_norm_g, k_rope_norm_g, mla_w_dq, mla_q_norm_g, mla_w_uq, mla_q_nope_norm_g, mla_q_rope_norm_g, mla_w_o, loss_target, m_ada_w, m_ada_b, m_norm1_g, m_norm2_g, m_ffn_w_gate, m_ffn_w_up, m_ffn_w_down, m_s5_lam_re, m_s5_lam_im, m_s5_log_dt, m_s5_b_re, m_s5_b_im, m_s5_c_re, m_s5_c_im, m_s5_d, m_s5_w_glu, m_s5_b_glu, m_kv_ada_w, m_kv_ada_b, m_kv_norm_g, m_w_kv_a, m_kv_a_norm_g, m_w_kv_b, m_k_nope_norm_g, m_k_rope_norm_g, m_mla_w_dq, m_mla_q_norm_g, m_mla_w_uq, m_mla_q_nope_norm_g, m_mla_q_rope_norm_g, m_mla_w_o, v_ada_w, v_ada_b, v_norm1_g, v_norm2_g, v_ffn_w_gate, v_ffn_w_up, v_ffn_w_down, v_s5_lam_re, v_s5_lam_im, v_s5_log_dt, v_s5_b_re, v_s5_b_im, v_s5_c_re, v_s5_c_im, v_s5_d, v_s5_w_glu, v_s5_b_glu, v_kv_ada_w, v_kv_ada_b, v_kv_norm_g, v_w_kv_a, v_kv_a_norm_g, v_w_kv_b, v_k_nope_norm_g, v_k_rope_norm_g, v_mla_w_dq, v_mla_q_norm_g, v_mla_w_uq, v_mla_q_nope_norm_g, v_mla_q_rope_norm_g, v_mla_w_o):
    given = dict(locals())
    w = {n: given[n] for n in _WEIGHTS}
    m_in = {n: given["m_" + n] for n in _WEIGHTS}
    v_in = {n: given["v_" + n] for n in _WEIGHTS}
    S, D = x.shape[1], x.shape[2]
    ax, ay, ac = lax.axis_index("x"), lax.axis_index("y"), lax.axis_index("c")
    chip = 2 * ax + ay
    me = 4 * ax + 2 * ay + ac
    n_ada = ada_w.shape[2]
    n_kada = kv_ada_w.shape[1]

    g1 = _xchg(_pack_rows([c, s5_d, s5_b_glu], LANES, SUBLANES), "all", True, "gather_cond")
    g1 = g1.reshape(8, -1)
    c_all = g1[:, :D]
    vec = g1[0::2, D:D + 2 * s5_d.size].reshape(_CHIPS, 2, N_A, s5_d.shape[1])
    full_vecs = {"s5_d": vec[:, 0].transpose(1, 0, 2).reshape(N_A, D), "s5_b_glu": vec[:, 1].transpose(1, 0, 2).reshape(N_A, D)}
    c_act = _rowwise(jax.nn.silu, [c_all], [], [(D, f32)], tm=8, name="cond_silu")[0]

    mod_cols = [_mm(c_act, ada_w[l], cast=False, name="ada_proj") for l in range(DEPTH)] + [_mm(c_act, kv_ada_w, cast=False, name="kv_ada_proj")]
    g2 = _xchg(_pack_rows([jnp.concatenate(mod_cols, axis=1)], LANES, SUBLANES), "all", True, "gather_mod")
    g2 = g2.reshape(8, 8, -1)[0::2]
    mine = lax.dynamic_index_in_dim(g2, me, axis=1, keepdims=False)
    mod_lin = mine[:, :DEPTH * n_ada].reshape(_CHIPS, DEPTH, n_ada).transpose(1, 0, 2).reshape(DEPTH, 6 * D)
    kmod_lin = mine[:, DEPTH * n_ada:].reshape(1, 2 * D)
    mods = _rowwise(lambda a, b: a + b, [mod_lin, ada_b], [], [(6 * D, f32)], tm=DEPTH, name="ada_bias")[0].reshape(DEPTH, 6, D)
    kmod = _rowwise(lambda a, b: a + b, [kmod_lin, kv_ada_b.reshape(1, 2 * D)], [], [(2 * D, f32)], tm=1, name="kv_ada_bias")[0].reshape(2, D)

    big = list(_BIG)
    flat_w = _pack_rows([w[n] for n in big], _FLAT_W, _FLAT_ALIGN)
    gathered = _xchg(flat_w.astype(bf16), "xy", True, "gather_weights")
    R = flat_w.shape[0]
    parts = _unpack_chips(gathered, [w[n].shape for n in big])
    W = {n: jnp.concatenate([parts[i][q] for q in range(_CHIPS)], axis=_BIG[n]) for i, n in enumerate(big)}
    for n in _WEIGHTS:
        if n not in _BIG and n not in _TENSOR_PARALLEL and n not in _SHARDED_VECS and n not in ("ada_b", "kv_ada_b"):
            W[n] = w[n]
    W.update(full_vecs)

    pos = positions.reshape(S, 1).astype(f32)
    loss, dx, gW, dmods, dkmod = _device_step(x[0], loss_target[0], pos, mods, kmod, W)

    g4 = _xchg(_pack_rows([dmods, dkmod], LANES, SUBLANES), "all", True, "gather_dmod")
    ada_b_sum = _sum_lead(g4, f32, "dmod_sum").reshape(-1)
    g_ada_b = ada_b_sum[:DEPTH * 6 * D].reshape(DEPTH, 6 * D)
    g_kv_ada_b = ada_b_sum[DEPTH * 6 * D:DEPTH * 6 * D + 2 * D]
    dm_all = g4.reshape(8, -1)
    dm_ada = lax.dynamic_slice_in_dim(dm_all[:, :DEPTH * 6 * D].reshape(8, DEPTH, 6 * D), chip * n_ada, n_ada, axis=2)
    dm_kv = lax.dynamic_slice_in_dim(dm_all[:, DEPTH * 6 * D:DEPTH * 6 * D + 2 * D], chip * n_kada, n_kada, axis=1)

    def outer(at, b):
        acc = at[:, 0:1] * b[0:1, :]
        for i in range(1, 8):
            acc = acc + at[:, i:i + 1] * b[i:i + 1, :]
        return acc

    c_act_t = c_act.T
    g_ada_w = jnp.stack([_rowwise(outer, [c_act_t], [dm_ada[:, l]], [(n_ada, f32)], name="ada_dw")[0] for l in range(DEPTH)])
    g_kv_ada_w = _rowwise(outer, [c_act_t], [dm_kv], [(n_kada, f32)], name="kv_ada_dw")[0]

    pieces = [[jnp.split(gW[n], _CHIPS, axis=_BIG[n])[q] for n in big] for q in range(_CHIPS)]
    gflat = jnp.stack([_pack_rows(pieces[q], _FLAT_W, _FLAT_ALIGN) for q in range(_CHIPS)]).astype(bf16)
    R2 = R // 2
    halves = gflat.reshape(_CHIPS, 2, R2, _FLAT_W).transpose(1, 0, 2, 3).reshape(2, _CHIPS * R2, _FLAT_W)
    ra = _xchg(halves, "c", False, "grads_pair")
    hsum = _sum_lead(ra, bf16, "grads_pair_sum").reshape(_CHIPS, R2, _FLAT_W)
    rb = _xchg(hsum, "xy", False, "grads_chips")
    tsum = _sum_lead(rb, f32, "grads_chips_sum")
    g_flat = _xchg(tsum, "c", True, "grads_halves").reshape(R, _FLAT_W)
    g_big = dict(zip(big, _unpack(g_flat, [w[n].shape for n in big])))

    small = [n for n in _WEIGHTS if n not in _BIG and n not in _TENSOR_PARALLEL and n not in ("ada_b", "kv_ada_b")]
    svec = _pack_rows([loss[0, :1]] + [gW[n] for n in small], LANES, 8 * SUBLANES)
    rows8 = svec.shape[0] // 8
    rs = _xchg(svec.reshape(8, rows8, LANES), "all", False, "small_scatter")
    red = _sum_lead(rs, f32, "small_sum")
    full = _xchg(red, "all", True, "small_gather").reshape(-1)
    loss_tot = full[0]
    g_small = dict(zip(small, _unpack(full[1:], [gW[n].shape for n in small])))
    for n in _SHARDED_VECS:
        g_small[n] = lax.dynamic_slice_in_dim(g_small[n], chip * w[n].shape[1], w[n].shape[1], axis=1)
    g_small["ada_b"] = g_ada_b
    g_small["kv_ada_b"] = g_kv_ada_b

    grads = dict(g_big)
    grads.update(g_small)
    grads["ada_w"] = g_ada_w
    grads["kv_ada_w"] = g_kv_ada_w

    delta, new_m, new_v = {}, {}, {}
    d_, m_, v_ = _adamw(flat_w, g_flat, _pack_rows([m_in[n] for n in big], _FLAT_W, _FLAT_ALIGN),
                        _pack_rows([v_in[n] for n in big], _FLAT_W, _FLAT_ALIGN), "adamw_matmul")
    shapes = [w[n].shape for n in big]
    for res, src in ((delta, d_), (new_m, m_), (new_v, v_)):
        res.update(zip(big, _unpack(src, shapes)))
    for n in _TENSOR_PARALLEL:
        C = w[n].shape[-1]
        d_, m_, v_ = _adamw(w[n].reshape(-1, C), grads[n].reshape(-1, C), m_in[n].reshape(-1, C), v_in[n].reshape(-1, C), "adamw_" + n)
        delta[n], new_m[n], new_v[n] = d_.reshape(w[n].shape), m_.reshape(w[n].shape), v_.reshape(w[n].shape)
    rest = [n for n in _WEIGHTS if n not in _BIG and n not in _TENSOR_PARALLEL]
    d_, m_, v_ = _adamw(*[_pack_rows([src[n] for n in rest], LANES, SUBLANES) for src in (w, grads, m_in, v_in)], "adamw_small")
    shapes = [w[n].shape for n in rest]
    for res, src in ((delta, d_), (new_m, m_), (new_v, v_)):
        res.update(zip(rest, _unpack(src, shapes)))

    return (loss_tot, dx[None], *[grads[n] for n in _WEIGHTS], *[delta[n] for n in _WEIGHTS],
            *[new_m[n] for n in _WEIGHTS], *[new_v[n] for n in _WEIGHTS])


def _unpack_chips(gathered, shapes):
    n_chip = gathered.shape[0]
    flat = gathered.reshape(n_chip, -1)
    out, off = [], 0
    for s in shapes:
        n = math.prod(s)
        out.append(flat[:, off:off + n].reshape((n_chip,) + tuple(s)))
        off += n
    return out
```

```python
import functools
import math

import jax
import jax.numpy as jnp
from jax import lax
from jax.experimental import pallas as pl
from jax.experimental.pallas import tpu as pltpu

f32 = jnp.float32
bf16 = jnp.bfloat16
SDS = jax.ShapeDtypeStruct

EPS = 1e-6
CHUNK = 64
N_HEADS = 16
NOPE = 64
ROPE = 32
VDIM = 64
HEAD_PAD = 128
KV_RANK = 256
ROPE_THETA = 10000.0
ATTN_SCALE = 1.0 / math.sqrt(NOPE + ROPE)
SSM_GROUP = 16
SSM_STATE = 64
N_A = 2
DEPTH = 4
LANES = 128
SUBLANES = 8
SEGMENTS = 8
SLAB_CH = 128
SLAB_ST = 512
ADAM_LR, ADAM_B1, ADAM_B2, ADAM_EPS, ADAM_WD, ADAM_STEP = 0.001, 0.9, 0.999, 1e-08, 0.01, 10


def _pick(n, prefs=(512, 256, 128)):
    for p in prefs:
        if n % p == 0:
            return p
    return n


V7X_VMEM_BYTES = 64 << 20
MM_VMEM_LIMIT = V7X_VMEM_BYTES - (8 << 20)
MM_VMEM_BUDGET = 40 << 20
MM_MAX_TM, MM_MAX_TN = 1536, 1536
BF16_ROWS = 16


def _largest_divisor(n, cap, unit):
    if n <= cap:
        return n
    for d in range(cap - cap % unit, 0, -unit):
        if n % d == 0:
            return d
    return n


def _mm_tiles(M, N, K, sa, sb, so, m_on_lanes):
    tm = _largest_divisor(M, MM_MAX_TM, LANES if m_on_lanes else BF16_ROWS)
    tn = _largest_divisor(N, MM_MAX_TN, LANES)
    tk = K

    def need(tm, tn, tk):
        return 2 * (tm * tk * sa + tk * tn * sb + tm * tn * so) + (0 if tk == K else tm * tn * 4)

    m_unit = LANES if m_on_lanes else BF16_ROWS
    while need(tm, tn, tk) > MM_VMEM_BUDGET:
        if tk % 256 == 0 and tk >= 1024:
            tk //= 2
        elif tm % (2 * m_unit) == 0 and tm >= 512:
            tm //= 2
        elif tn % 256 == 0:
            tn //= 2
        else:
            break
    return tm, tn, tk


def _mm(a, b, *, ta=False, tb=False, out_dtype=f32, cast=True, name="mm"):
    if ta:
        K, M = a.shape
    else:
        M, K = a.shape
    if tb:
        N, K2 = b.shape
    else:
        K2, N = b.shape
    assert K == K2, (a.shape, b.shape, ta, tb)
    tm, tn, tk = _mm_tiles(M, N, K, a.dtype.itemsize, b.dtype.itemsize, jnp.dtype(out_dtype).itemsize, ta)
    nk = K // tk
    dims = (((0 if ta else 1,), (1 if tb else 0,)), ((), ()))

    def dot(a_ref, b_ref):
        av, bv = a_ref[...], b_ref[...]
        if cast:
            return lax.dot_general(av.astype(bf16), bv.astype(bf16), dims, preferred_element_type=f32)
        return lax.dot_general(av, bv, dims, preferred_element_type=f32, precision=lax.Precision.HIGHEST)

    def body_one(a_ref, b_ref, o_ref):
        o_ref[...] = dot(a_ref, b_ref).astype(o_ref.dtype)

    def body_acc(a_ref, b_ref, o_ref, acc):
        k = pl.program_id(2)

        @pl.when(k == 0)
        def _():
            acc[...] = jnp.zeros_like(acc)

        acc[...] += dot(a_ref, b_ref)

        @pl.when(k == nk - 1)
        def _():
            o_ref[...] = acc[...].astype(o_ref.dtype)

    a_spec = pl.BlockSpec((tk, tm), lambda i, j, k: (k, i)) if ta else pl.BlockSpec((tm, tk), lambda i, j, k: (i, k))
    b_spec = pl.BlockSpec((tn, tk), lambda i, j, k: (j, k)) if tb else pl.BlockSpec((tk, tn), lambda i, j, k: (k, j))
    return pl.pallas_call(
        body_one if nk == 1 else body_acc, name=name, grid=(M // tm, N // tn, nk),
        in_specs=[a_spec, b_spec], out_specs=pl.BlockSpec((tm, tn), lambda i, j, k: (i, j)),
        out_shape=SDS((M, N), out_dtype), scratch_shapes=[] if nk == 1 else [pltpu.VMEM((tm, tn), f32)],
        compiler_params=pltpu.CompilerParams(dimension_semantics=("parallel", "parallel", "arbitrary"),
                                             vmem_limit_bytes=MM_VMEM_LIMIT),
    )(a, b)


def _rowwise(fn, rows, consts, row_outs, acc_outs=(), *, tm=256, name="rowwise"):
    S = rows[0].shape[0]
    tm = min(tm, S)
    assert S % tm == 0, (S, tm)
    n_r, n_c, n_ro, n_ao = len(rows), len(consts), len(row_outs), len(acc_outs)

    def body(*refs):
        vals = [r[...] for r in refs[:n_r + n_c]]
        outs = fn(*vals)
        if not isinstance(outs, (tuple, list)):
            outs = (outs,)
        assert len(outs) == n_ro + n_ao, (name, len(outs), n_ro, n_ao)
        o_refs = refs[n_r + n_c:]
        for ref, val in zip(o_refs[:n_ro], outs[:n_ro]):
            ref[...] = val.astype(ref.dtype)
        if n_ao:
            @pl.when(pl.program_id(0) == 0)
            def _():
                for ref in o_refs[n_ro:]:
                    ref[...] = jnp.zeros_like(ref)
            for ref, val in zip(o_refs[n_ro:], outs[n_ro:]):
                ref[...] += jnp.broadcast_to(val, ref.shape).astype(f32)

    def full(shape):
        nd = len(shape)
        return pl.BlockSpec(tuple(shape), lambda i: (0,) * nd)

    in_specs = [pl.BlockSpec((tm, r.shape[1]), lambda i: (i, 0)) for r in rows] + [full(c.shape) for c in consts]
    out_specs = [pl.BlockSpec((tm, w), lambda i: (i, 0)) for (w, _) in row_outs] + [full(s) for s in acc_outs]
    out_shape = [SDS((S, w), dt) for (w, dt) in row_outs] + [SDS(tuple(s), f32) for s in acc_outs]
    res = pl.pallas_call(
        body, name=name, grid=(S // tm,), in_specs=in_specs, out_specs=out_specs, out_shape=out_shape,
        compiler_params=pltpu.CompilerParams(dimension_semantics=("arbitrary",)),
    )(*rows, *consts)
    return res


def _rowwise_bwd(f, rows, consts, cts, want, row_dtypes, *, adds=None, tm=256, name="rowwise_bwd"):
    n_r, n_c, n_ct = len(rows), len(consts), len(cts)
    adds = adds or {}
    add_keys = sorted(adds)
    add_rows = [adds[k] for k in add_keys]

    def fn(*args):
        r = [a.astype(f32) for a in args[:n_r]]
        ct = [a.astype(f32) for a in args[n_r:n_r + n_ct]]
        ad = args[n_r + n_ct:n_r + n_ct + len(add_rows)]
        c = list(args[n_r + n_ct + len(add_rows):])
        _, vjp = jax.vjp(f, *r, *c)
        g = vjp(ct[0] if n_ct == 1 else tuple(ct))
        d_rows = []
        for pos, i in enumerate(want):
            d = g[i]
            if pos in adds:
                d = d + ad[add_keys.index(pos)].astype(f32)
            d_rows.append(d)
        return (*d_rows, *g[n_r:])

    return _rowwise(fn, list(rows) + list(cts) + add_rows, list(consts),
                    [(rows[i].shape[1], dt) for i, dt in zip(want, row_dtypes)],
                    [c.shape for c in consts], tm=tm, name=name)


def _sum_lead(arr, out_dtype, name):
    n, R, C = arr.shape
    tm = _pick(R, (256, 128, 64, 32, 16, 8))

    def body(a_ref, o_ref):
        acc = a_ref[0].astype(f32)
        for q in range(1, n):
            acc = acc + a_ref[q].astype(f32)
        o_ref[...] = acc.astype(o_ref.dtype)

    return pl.pallas_call(
        body, name=name, grid=(R // tm,), in_specs=[pl.BlockSpec((n, tm, C), lambda i: (0, i, 0))],
        out_specs=pl.BlockSpec((tm, C), lambda i: (i, 0)), out_shape=SDS((R, C), out_dtype),
    )(arr)


_GROUPS = {
    "all": [(kx, ky, kc) for kx in (0, 1) for ky in (0, 1) for kc in (0, 1)][1:],
    "xy": [(0, 1, 0), (1, 0, 0), (1, 1, 0)],
    "c": [(0, 0, 1)],
}


XCHG_CHUNK_MIN_BYTES = 1 << 20
XCHG_CHUNKS = 16


def _group_pos(group, x, y, c):
    return {"all": 4 * x + 2 * y + c, "xy": 2 * x + y, "c": c}[group]


def _xchg(send, group, bcast, name):
    flips = _GROUPS[group]
    n = len(flips) + 1
    piece = tuple(send.shape if bcast else send.shape[1:])
    if not bcast:
        assert send.shape[0] == n
    lead = piece[0]
    nch = 1
    if math.prod(piece) * send.dtype.itemsize >= XCHG_CHUNK_MIN_BYTES:
        nch = lead if len(piece) > 2 else (XCHG_CHUNKS if lead % (XCHG_CHUNKS * 2 * BF16_ROWS) == 0 else 1)
    rows = lead // nch

    def body(s_ref, r_ref, send_sems, recv_sems, own_sems):
        x, y, c = lax.axis_index("x"), lax.axis_index("y"), lax.axis_index("c")
        me = _group_pos(group, x, y, c)

        def src(p, ch):
            ref = s_ref if bcast else s_ref.at[p]
            return ref.at[pl.ds(ch * rows, rows)]

        def dst(ch):
            return r_ref.at[me].at[pl.ds(ch * rows, rows)]

        owns = [pltpu.make_async_copy(src(me, ch), dst(ch), own_sems.at[ch]) for ch in range(nch)]
        for cp in owns:
            cp.start()
        copies = []
        for ch in range(nch):
            for k, (kx, ky, kc) in enumerate(flips):
                tx, ty, tc = x ^ kx, y ^ ky, c ^ kc
                cp = pltpu.make_async_remote_copy(
                    src_ref=src(_group_pos(group, tx, ty, tc), ch), dst_ref=dst(ch),
                    send_sem=send_sems.at[k, ch], recv_sem=recv_sems.at[k, ch],
                    device_id=(tx, ty, tc), device_id_type=pl.DeviceIdType.MESH)
                cp.start()
                copies.append(cp)
        for cp in copies + owns:
            cp.wait()

    return pl.pallas_call(
        body, name=name, out_shape=SDS((n,) + piece, send.dtype),
        in_specs=[pl.BlockSpec(memory_space=pl.ANY)], out_specs=pl.BlockSpec(memory_space=pl.ANY),
        scratch_shapes=[pltpu.SemaphoreType.DMA((n - 1, nch)), pltpu.SemaphoreType.DMA((n - 1, nch)), pltpu.SemaphoreType.DMA((nch,))],
        compiler_params=pltpu.CompilerParams(has_side_effects=True),
    )(send)


PAIR_ROWS = 256


def _pair(arr, mode, out_dtype, name):
    P, N, C = arr.shape
    tm = min(PAIR_ROWS, N)
    nb = N // tm
    assert N % tm == 0
    add = mode == "add"
    assert not add or P == 2

    def body(core_ref, *refs):
        if add:
            keep_ref, send_ref, o_ref, slots, send_sems, recv_sems, credit = refs
        else:
            a_ref, o_ref, stage, slots, send_sems, recv_sems, credit = refs
        i = pl.program_id(0)
        x, y, c = lax.axis_index("x"), lax.axis_index("y"), lax.axis_index("c")
        sib = (x, y, 1 - c)
        slot = lax.rem(i, 2)
        if add:
            src = send_ref.at[0]
        else:
            t = a_ref[0].astype(f32)
            for q in range(1, P):
                t = t + a_ref[q].astype(f32)
            stage[...] = t.astype(stage.dtype)
            o_ref[c] = stage[...]
            src = stage

        @pl.when(i >= 2)
        def _():
            pl.semaphore_wait(credit.at[slot], 1)

        cp = pltpu.make_async_remote_copy(src_ref=src, dst_ref=slots.at[slot], send_sem=send_sems.at[slot], recv_sem=recv_sems.at[slot],
                                          device_id=sib, device_id_type=pl.DeviceIdType.MESH)
        cp.start()
        cp.wait_recv()
        if add:
            o_ref[...] = (keep_ref[0].astype(f32) + slots[slot].astype(f32)).astype(o_ref.dtype)
        else:
            o_ref[1 - c] = slots[slot]
        cp.wait_send()

        @pl.when(i + 2 < nb)
        def _():
            pl.semaphore_signal(credit.at[slot], 1, device_id=sib, device_id_type=pl.DeviceIdType.MESH)

    core = lax.axis_index("c").astype(jnp.int32).reshape(1)
    scratch = [pltpu.VMEM((2, tm, C), arr.dtype if add else out_dtype), pltpu.SemaphoreType.DMA((2,)), pltpu.SemaphoreType.DMA((2,)),
               pltpu.SemaphoreType.REGULAR((2,))]
    if add:
        in_specs = [pl.BlockSpec((1, tm, C), lambda i, core: (core[0], i, 0)), pl.BlockSpec((1, tm, C), lambda i, core: (1 - core[0], i, 0))]
        out_specs = pl.BlockSpec((tm, C), lambda i, core: (i, 0))
        out_shape = SDS((N, C), out_dtype)
        args = (arr, arr)
    else:
        in_specs = [pl.BlockSpec((P, tm, C), lambda i, core: (0, i, 0))]
        out_specs = pl.BlockSpec((2, tm, C), lambda i, core: (0, i, 0))
        out_shape = SDS((2, N, C), out_dtype)
        scratch = [pltpu.VMEM((tm, C), out_dtype)] + scratch
        args = (arr,)
    return pl.pallas_call(
        body, name=name, out_shape=out_shape,
        grid_spec=pltpu.PrefetchScalarGridSpec(num_scalar_prefetch=1, grid=(nb,), in_specs=in_specs, out_specs=out_specs, scratch_shapes=scratch),
        compiler_params=pltpu.CompilerParams(dimension_semantics=("arbitrary",), has_side_effects=True),
    )(core, *args)


def _rms(x, g):
    return x * lax.rsqrt(jnp.mean(x * x, axis=-1, keepdims=True) + EPS) * g


def _normmod(x, g, shift, scale):
    return _rms(x, g) * (1.0 + scale) + shift


def _lane(shape):
    return lax.broadcasted_iota(jnp.int32, shape, 1)


def _partner(x):
    lane = _lane(x.shape)
    lo = (lane >= NOPE) & (lane < NOPE + ROPE // 2)
    hi = (lane >= NOPE + ROPE // 2) & (lane < NOPE + ROPE)
    return jnp.where(lo, pltpu.roll(x, HEAD_PAD - ROPE // 2, 1), jnp.where(hi, pltpu.roll(x, ROPE // 2, 1), 0.0))


@jax.custom_vjp
def _rope(x, cf, ss):
    return x * cf + _partner(x) * ss


def _rope_fwd(x, cf, ss):
    return _rope(x, cf, ss), (cf, ss)


def _rope_bwd(res, dy):
    cf, ss = res
    return dy * cf + _partner(dy * ss), jnp.zeros_like(cf), jnp.zeros_like(ss)


_rope.defvjp(_rope_fwd, _rope_bwd)


def _head_norm(xh, gn, gr):
    lane = _lane(xh.shape)
    x2 = xh * xh
    ms_n = jnp.sum(jnp.where(lane < NOPE, x2, 0.0), axis=-1, keepdims=True) * (1.0 / NOPE)
    ms_r = jnp.sum(jnp.where((lane >= NOPE) & (lane < NOPE + ROPE), x2, 0.0), axis=-1, keepdims=True) * (1.0 / ROPE)
    return xh * (lax.rsqrt(ms_n + EPS) * gn + lax.rsqrt(ms_r + EPS) * gr)


def _q_heads(qraw, cf, ss, gn, gr):
    outs = []
    for h in range(N_HEADS):
        xh = qraw[:, h * HEAD_PAD:(h + 1) * HEAD_PAD]
        outs.append(_rope(_head_norm(xh, gn, gr), cf, ss))
    return jnp.concatenate(outs, axis=1)


def _k_heads(kraw, kr, gn):
    outs = []
    zero = jnp.zeros_like(gn)
    for h in range(N_HEADS):
        xh = kraw[:, h * HEAD_PAD:(h + 1) * HEAD_PAD]
        outs.append(_head_norm(xh, gn, zero) + kr)
    return jnp.concatenate(outs, axis=1)


def _kv_a_post(kva, cf, ss, ga, gr):
    ckv = _rms(kva[:, :KV_RANK], ga)
    kr = _rope(_head_norm(kva[:, KV_RANK:], jnp.zeros_like(gr), gr), cf, ss)
    return ckv, kr


def _cmul(ar, ai, br, bi):
    return ar * br - ai * bi, ar * bi + ai * br


def _s5_scan_fwd(u, wbr, wbi, ar, ai, init_r, init_i, wcr, wci, *, want_y, tb, name):
    S, D = u.shape
    ns = D // SLAB_CH
    seg = S // SEGMENTS
    tb = min(tb, seg)
    nb = seg // tb
    R = tb * SEGMENTS
    nt = (((1,), (0,)), ((), ()))

    def body(u_ref, wbr_ref, wbi_ref, ar_ref, ai_ref, ir_ref, ii_ref, wcr_ref, wci_ref, *rest):
        if want_y:
            y_ref, er_ref, ei_ref, bsr_ref, bsi_ref, cr, ci, bur, bui, sr, si = rest
        else:
            er_ref, ei_ref, bsr_ref, bsi_ref, cr, ci, bur, bui, sr, si = rest
        t = pl.program_id(1)

        @pl.when(t == 0)
        def _():
            cr[...] = ir_ref[0]
            ci[...] = ii_ref[0]

        bsr_ref[0, 0] = cr[...]
        bsi_ref[0, 0] = ci[...]
        ub = u_ref[...].astype(bf16)
        bur[...] = lax.dot_general(ub, wbr_ref[0].astype(bf16), nt, preferred_element_type=f32)
        bui[...] = lax.dot_general(ub, wbi_ref[0].astype(bf16), nt, preferred_element_type=f32)
        a_r = jnp.broadcast_to(ar_ref[0], (SEGMENTS, SLAB_ST))
        a_i = jnp.broadcast_to(ai_ref[0], (SEGMENTS, SLAB_ST))

        def step(tau, carry):
            c_r, c_i = carry
            r = pl.multiple_of(tau * SEGMENTS, SEGMENTS)
            p_r, p_i = _cmul(a_r, a_i, c_r, c_i)
            n_r = p_r + bur[pl.ds(r, SEGMENTS), :]
            n_i = p_i + bui[pl.ds(r, SEGMENTS), :]
            sr[pl.ds(r, SEGMENTS), :] = n_r
            si[pl.ds(r, SEGMENTS), :] = n_i
            return n_r, n_i

        c_r, c_i = lax.fori_loop(0, tb, step, (cr[...], ci[...]))
        cr[...] = c_r
        ci[...] = c_i
        if want_y:
            y_ref[...] = (lax.dot_general(sr[...].astype(bf16), wcr_ref[0].astype(bf16), nt, preferred_element_type=f32)
                          - lax.dot_general(si[...].astype(bf16), wci_ref[0].astype(bf16), nt, preferred_element_type=f32))

        @pl.when(t == nb - 1)
        def _():
            er_ref[0] = c_r
            ei_ref[0] = c_i

    slab3 = lambda s: pl.BlockSpec((1,) + s, lambda k, t: (k, 0, 0))
    in_specs = [pl.BlockSpec((R, SLAB_CH), lambda k, t: (t, k)),
                slab3((SLAB_CH, SLAB_ST)), slab3((SLAB_CH, SLAB_ST)), slab3((1, SLAB_ST)), slab3((1, SLAB_ST)),
                slab3((SEGMENTS, SLAB_ST)), slab3((SEGMENTS, SLAB_ST)), slab3((SLAB_ST, SLAB_CH)), slab3((SLAB_ST, SLAB_CH))]
    out_specs = [slab3((SEGMENTS, SLAB_ST)), slab3((SEGMENTS, SLAB_ST)),
                 pl.BlockSpec((1, 1, SEGMENTS, SLAB_ST), lambda k, t: (t, k, 0, 0)),
                 pl.BlockSpec((1, 1, SEGMENTS, SLAB_ST), lambda k, t: (t, k, 0, 0))]
    out_shape = [SDS((ns, SEGMENTS, SLAB_ST), f32)] * 2 + [SDS((nb, ns, SEGMENTS, SLAB_ST), f32)] * 2
    if want_y:
        out_specs = [pl.BlockSpec((R, SLAB_CH), lambda k, t: (t, k))] + out_specs
        out_shape = [SDS((S, D), f32)] + out_shape
    return pl.pallas_call(
        body, name=name, grid=(ns, nb), in_specs=in_specs, out_specs=out_specs, out_shape=out_shape,
        scratch_shapes=[pltpu.VMEM((SEGMENTS, SLAB_ST), f32)] * 2 + [pltpu.VMEM((R, SLAB_ST), f32)] * 4,
        compiler_params=pltpu.CompilerParams(dimension_semantics=("parallel", "arbitrary")),
    )(u, wbr, wbi, ar, ai, init_r, init_i, wcr, wci)


def _s5_scan_bwd(u, dy, wbr, wbi, ar, ai, bs_r, bs_i, ginit_r, ginit_i, wcr, wci, *, full, tb, name):
    S, D = dy.shape
    ns = D // SLAB_CH
    seg = S // SEGMENTS
    tb = min(tb, seg)
    nb = seg // tb
    R = tb * SEGMENTS
    nn = (((1,), (0,)), ((), ()))
    nt = (((1,), (1,)), ((), ()))
    tn = (((0,), (0,)), ((), ()))

    def body(*refs):
        if full:
            (u_ref, dy_ref, wbr_ref, wbi_ref, ar_ref, ai_ref, bsr_ref, bsi_ref, gir_ref, gii_ref, wcr_ref, wci_ref,
             du_ref, dwbr_ref, dwbi_ref, dwcr_ref, dwci_ref, dar_ref, dai_ref, gfr_ref, gfi_ref,
             gr_c, gi_c, bur, bui, sr, si, gsr, gsi) = refs
        else:
            (dy_ref, ar_ref, ai_ref, gir_ref, gii_ref, wcr_ref, wci_ref, gfr_ref, gfi_ref, gr_c, gi_c, gsr, gsi) = refs
        t = pl.program_id(1)

        @pl.when(t == 0)
        def _():
            gr_c[...] = gir_ref[0]
            gi_c[...] = gii_ref[0]
            if full:
                for ref in (dwbr_ref, dwbi_ref, dwcr_ref, dwci_ref, dar_ref, dai_ref):
                    ref[...] = jnp.zeros_like(ref)

        a_r = jnp.broadcast_to(ar_ref[0], (SEGMENTS, SLAB_ST))
        a_i = jnp.broadcast_to(ai_ref[0], (SEGMENTS, SLAB_ST))
        dyb = dy_ref[...].astype(bf16)
        gsr[...] = lax.dot_general(dyb, wcr_ref[0].astype(bf16), nt, preferred_element_type=f32)
        gsi[...] = -lax.dot_general(dyb, wci_ref[0].astype(bf16), nt, preferred_element_type=f32)

        if full:
            ub = u_ref[...].astype(bf16)
            bur[...] = lax.dot_general(ub, wbr_ref[0].astype(bf16), nn, preferred_element_type=f32)
            bui[...] = lax.dot_general(ub, wbi_ref[0].astype(bf16), nn, preferred_element_type=f32)

            def fstep(tau, carry):
                c_r, c_i = carry
                r = pl.multiple_of(tau * SEGMENTS, SEGMENTS)
                p_r, p_i = _cmul(a_r, a_i, c_r, c_i)
                n_r = p_r + bur[pl.ds(r, SEGMENTS), :]
                n_i = p_i + bui[pl.ds(r, SEGMENTS), :]
                sr[pl.ds(r, SEGMENTS), :] = n_r
                si[pl.ds(r, SEGMENTS), :] = n_i
                return n_r, n_i

            lax.fori_loop(0, tb, fstep, (bsr_ref[0, 0], bsi_ref[0, 0]))

        def adj(g_r, g_i):
            return a_r * g_r + a_i * g_i, a_r * g_i - a_i * g_r

        def rstep(k, carry):
            tau = tb - 1 - k
            r = pl.multiple_of(tau * SEGMENTS, SEGMENTS)
            if full:
                g_r, g_i, acc_r, acc_i = carry
            else:
                g_r, g_i = carry
            b_r, b_i = adj(g_r, g_i)
            n_r = b_r + gsr[pl.ds(r, SEGMENTS), :]
            n_i = b_i + gsi[pl.ds(r, SEGMENTS), :]
            gsr[pl.ds(r, SEGMENTS), :] = n_r
            gsi[pl.ds(r, SEGMENTS), :] = n_i
            if not full:
                return n_r, n_i
            rp = pl.multiple_of(jnp.maximum(tau - 1, 0) * SEGMENTS, SEGMENTS)
            first = tau == 0
            p_r = jnp.where(first, bsr_ref[0, 0], sr[pl.ds(rp, SEGMENTS), :])
            p_i = jnp.where(first, bsi_ref[0, 0], si[pl.ds(rp, SEGMENTS), :])
            return n_r, n_i, acc_r + n_r * p_r + n_i * p_i, acc_i + n_i * p_r - n_r * p_i

        zero = jnp.zeros((SEGMENTS, SLAB_ST), f32)
        if full:
            g_r, g_i, acc_r, acc_i = lax.fori_loop(0, tb, rstep, (gr_c[...], gi_c[...], zero, zero))
            dar_ref[0] += acc_r
            dai_ref[0] += acc_i
        else:
            g_r, g_i = lax.fori_loop(0, tb, rstep, (gr_c[...], gi_c[...]))
        gr_c[...] = g_r
        gi_c[...] = g_i

        if full:
            gbr = gsr[...].astype(bf16)
            gbi = gsi[...].astype(bf16)
            du_ref[...] = (lax.dot_general(gbr, wbr_ref[0].astype(bf16), nt, preferred_element_type=f32)
                           + lax.dot_general(gbi, wbi_ref[0].astype(bf16), nt, preferred_element_type=f32))
            dwbr_ref[0] += lax.dot_general(ub, gbr, tn, preferred_element_type=f32)
            dwbi_ref[0] += lax.dot_general(ub, gbi, tn, preferred_element_type=f32)
            dwcr_ref[0] += lax.dot_general(sr[...].astype(bf16), dyb, tn, preferred_element_type=f32)
            dwci_ref[0] -= lax.dot_general(si[...].astype(bf16), dyb, tn, preferred_element_type=f32)

        @pl.when(t == nb - 1)
        def _():
            gfr_ref[0] = g_r
            gfi_ref[0] = g_i

    slab3 = lambda s: pl.BlockSpec((1,) + s, lambda k, t: (k, 0, 0))
    rev_rows = pl.BlockSpec((R, SLAB_CH), lambda k, t: (nb - 1 - t, k))
    rev_bs = pl.BlockSpec((1, 1, SEGMENTS, SLAB_ST), lambda k, t: (nb - 1 - t, k, 0, 0))
    st = slab3((SEGMENTS, SLAB_ST))
    state_sds = SDS((ns, SEGMENTS, SLAB_ST), f32)
    if full:
        in_specs = [rev_rows, rev_rows, slab3((SLAB_CH, SLAB_ST)), slab3((SLAB_CH, SLAB_ST)), slab3((1, SLAB_ST)), slab3((1, SLAB_ST)),
                    rev_bs, rev_bs, st, st, slab3((SLAB_ST, SLAB_CH)), slab3((SLAB_ST, SLAB_CH))]
        args = (u, dy, wbr, wbi, ar, ai, bs_r, bs_i, ginit_r, ginit_i, wcr, wci)
        out_specs = [rev_rows, slab3((SLAB_CH, SLAB_ST)), slab3((SLAB_CH, SLAB_ST)), slab3((SLAB_ST, SLAB_CH)), slab3((SLAB_ST, SLAB_CH)),
                     st, st, st, st]
        out_shape = [SDS((S, D), f32), SDS((ns, SLAB_CH, SLAB_ST), f32), SDS((ns, SLAB_CH, SLAB_ST), f32),
                     SDS((ns, SLAB_ST, SLAB_CH), f32), SDS((ns, SLAB_ST, SLAB_CH), f32)] + [state_sds] * 4
        scratch = [pltpu.VMEM((SEGMENTS, SLAB_ST), f32)] * 2 + [pltpu.VMEM((R, SLAB_ST), f32)] * 6
    else:
        in_specs = [rev_rows, slab3((1, SLAB_ST)), slab3((1, SLAB_ST)), st, st, slab3((SLAB_ST, SLAB_CH)), slab3((SLAB_ST, SLAB_CH))]
        args = (dy, ar, ai, ginit_r, ginit_i, wcr, wci)
        out_specs = [st, st]
        out_shape = [state_sds] * 2
        scratch = [pltpu.VMEM((SEGMENTS, SLAB_ST), f32)] * 2 + [pltpu.VMEM((R, SLAB_ST), f32)] * 2
    return pl.pallas_call(
        body, name=name, grid=(ns, nb), in_specs=in_specs, out_specs=out_specs, out_shape=out_shape, scratch_shapes=scratch,
        compiler_params=pltpu.CompilerParams(dimension_semantics=("parallel", "arbitrary")),
    )(*args)


def _s5_chain(e_r, e_i, ar, ai, *, seg, reverse, name):
    ns = e_r.shape[0]
    assert seg & (seg - 1) == 0

    def body(er_ref, ei_ref, ar_ref, ai_ref, or_ref, oi_ref):
        p_r, p_i = ar_ref[0], ai_ref[0]
        if reverse:
            p_i = -p_i
        for _ in range(seg.bit_length() - 1):
            p_r, p_i = _cmul(p_r, p_i, p_r, p_i)
        c_r = jnp.zeros((1, SLAB_ST), f32)
        c_i = jnp.zeros((1, SLAB_ST), f32)
        order = range(SEGMENTS - 1, -1, -1) if reverse else range(SEGMENTS)
        for j in order:
            or_ref[0, pl.ds(j, 1), :] = c_r
            oi_ref[0, pl.ds(j, 1), :] = c_i
            m_r, m_i = _cmul(p_r, p_i, c_r, c_i)
            c_r = er_ref[0, pl.ds(j, 1), :] + m_r
            c_i = ei_ref[0, pl.ds(j, 1), :] + m_i

    st = pl.BlockSpec((1, SEGMENTS, SLAB_ST), lambda k: (k, 0, 0))
    av = pl.BlockSpec((1, 1, SLAB_ST), lambda k: (k, 0, 0))
    return pl.pallas_call(
        body, name=name, grid=(ns,), in_specs=[st, st, av, av], out_specs=[st, st],
        out_shape=[SDS(e_r.shape, f32)] * 2,
    )(e_r, e_i, ar, ai)


def _s5_disc(lr, li, log_dt):
    dt = jnp.exp(log_dt)
    mag = jnp.exp(lr * dt)
    ab_re = mag * jnp.cos(li * dt)
    ab_im = mag * jnp.sin(li * dt)
    den = lr * lr + li * li
    nr = ab_re - 1.0
    ni = ab_im
    return ab_re, ab_im, (nr * lr + ni * li) / den, (ni * lr - nr * li) / den


def _s5_bbar(b_r, b_i, fr, fi):
    return fr * b_r - fi * b_i, fr * b_i + fi * b_r


def _blockdiag(w, rows, cols):
    g = w.shape[0]
    w = w.reshape(g // 8, 8, rows, cols)
    eye = jnp.eye(8, dtype=w.dtype)
    full = w[:, :, :, None, :] * eye[None, :, None, :, None]
    return full.reshape(g // 8, 8 * rows, 8 * cols)


def _blockdiag_take(w, rows, cols):
    ns = w.shape[0]
    w = w.reshape(ns, 8, rows, 8, cols)
    idx = jnp.arange(8)
    d = w[:, idx, :, idx, :]
    return jnp.moveaxis(d, 0, 1).reshape(ns * 8, rows, cols)


def _mask(i, j, tq, tk):
    row = i * tq + lax.broadcasted_iota(jnp.int32, (tq, tk), 0)
    col = j * tk + lax.broadcasted_iota(jnp.int32, (tq, tk), 1)
    return (row // CHUNK) >= (col // CHUNK)


_NT = (((1,), (1,)), ((), ()))
_NN = (((1,), (0,)), ((), ()))
_TN = (((0,), (0,)), ((), ()))
_NEG = -1e30
ATTN_HEADS_PER_STEP = 2


def _attn_fwd(q, k, v, *, t, name):
    S = q.shape[0]
    t = min(t, S)
    nq = S // t

    hp = ATTN_HEADS_PER_STEP

    def body(q_ref, k_ref, v_ref, o_ref, lse_ref, m_s, l_s, acc):
        i = pl.program_id(1)
        m_s[...] = jnp.full_like(m_s, _NEG)
        l_s[...] = jnp.zeros_like(l_s)
        acc[...] = jnp.zeros_like(acc)

        def block(j, masked):
            r = pl.multiple_of(j * t, t)
            for hh in range(hp):
                lanes = slice(hh * HEAD_PAD, (hh + 1) * HEAD_PAD)
                s = lax.dot_general(q_ref[:, lanes], k_ref[pl.ds(r, t), lanes], _NT, preferred_element_type=f32) * ATTN_SCALE
                if masked:
                    s = jnp.where(_mask(0, 0, t, t), s, _NEG)
                m_old = m_s[hh]
                m_new = jnp.maximum(m_old, jnp.max(s, axis=-1, keepdims=True))
                alpha = jnp.exp(m_old - m_new)
                p = jnp.exp(s - m_new)
                l_s[hh] = alpha * l_s[hh] + jnp.sum(p, axis=-1, keepdims=True)
                acc[hh] = alpha * acc[hh] + lax.dot_general(p.astype(bf16), v_ref[pl.ds(r, t), lanes], _NN, preferred_element_type=f32)
                m_s[hh] = m_new

        def below(j, carry):
            block(j, False)
            return carry

        lax.fori_loop(0, i, below, 0)
        block(i, True)
        for hh in range(hp):
            o_ref[:, hh * HEAD_PAD:(hh + 1) * HEAD_PAD] = acc[hh] / l_s[hh]
            lse_ref[hh] = m_s[hh] + jnp.log(l_s[hh])

    qs = pl.BlockSpec((t, hp * HEAD_PAD), lambda h, i: (i, h))
    ks = pl.BlockSpec((S, hp * HEAD_PAD), lambda h, i: (0, h))
    return pl.pallas_call(
        body, name=name, grid=(N_HEADS // hp, nq), in_specs=[qs, ks, ks],
        out_specs=[qs, pl.BlockSpec((hp, t, 1), lambda h, i: (h, i, 0))],
        out_shape=[SDS((S, N_HEADS * HEAD_PAD), f32), SDS((N_HEADS, S, 1), f32)],
        scratch_shapes=[pltpu.VMEM((hp, t, 1), f32), pltpu.VMEM((hp, t, 1), f32), pltpu.VMEM((hp, t, HEAD_PAD), f32)],
        compiler_params=pltpu.CompilerParams(dimension_semantics=("parallel", "arbitrary")),
    )(q, k, v)


def _attn_bwd(q, k, v, do, o, lse, *, t, name):
    S = q.shape[0]
    t = min(t, S)
    nq = S // t

    def body(q_ref, k_ref, v_ref, do_ref, o_ref, lse_ref, dq_ref, dk_ref, dv_ref, d_s, dk_acc, dv_acc):
        j = pl.program_id(1)

        @pl.when(j == 0)
        def _():
            dq_ref[...] = jnp.zeros_like(dq_ref)
            d_s[...] = jnp.sum(do_ref[...].astype(f32) * o_ref[...], axis=-1, keepdims=True)

        dk_acc[...] = jnp.zeros_like(dk_acc)
        dv_acc[...] = jnp.zeros_like(dv_acc)
        kb = k_ref[...]
        vb = v_ref[...]

        def block(i, masked):
            r = pl.multiple_of(i * t, t)
            qb = q_ref[pl.ds(r, t), :]
            dob = do_ref[pl.ds(r, t), :]
            s = lax.dot_general(qb, kb, _NT, preferred_element_type=f32) * ATTN_SCALE
            if masked:
                s = jnp.where(_mask(0, 0, t, t), s, _NEG)
            p = jnp.exp(s - lse_ref[0, pl.ds(r, t), :])
            dp = lax.dot_general(dob, vb, _NT, preferred_element_type=f32)
            ds = (p * (dp - d_s[pl.ds(r, t), :]) * ATTN_SCALE).astype(bf16)
            dv_acc[...] += lax.dot_general(p.astype(bf16), dob, _TN, preferred_element_type=f32)
            dk_acc[...] += lax.dot_general(ds, qb, _TN, preferred_element_type=f32)
            dq_ref[pl.ds(r, t), :] += lax.dot_general(ds, kb, _NN, preferred_element_type=f32)

        def below(i, carry):
            block(i, False)
            return carry

        block(j, True)
        lax.fori_loop(j + 1, nq, below, 0)
        dk_ref[...] = dk_acc[...]
        dv_ref[...] = dv_acc[...]

    hs = pl.BlockSpec((S, HEAD_PAD), lambda h, j: (0, h))
    ks = pl.BlockSpec((t, HEAD_PAD), lambda h, j: (j, h))
    ls = pl.BlockSpec((1, S, 1), lambda h, j: (h, 0, 0))
    return pl.pallas_call(
        body, name=name, grid=(N_HEADS, nq), in_specs=[hs, ks, ks, hs, hs, ls], out_specs=[hs, ks, ks],
        out_shape=[SDS((S, N_HEADS * HEAD_PAD), f32)] * 3,
        scratch_shapes=[pltpu.VMEM((S, 1), f32), pltpu.VMEM((t, HEAD_PAD), f32), pltpu.VMEM((t, HEAD_PAD), f32)],
        compiler_params=pltpu.CompilerParams(dimension_semantics=("parallel", "arbitrary")),
    )(q, k, v, do, o, lse)


def _pad_heads(w, per_head, axis):
    w = jnp.moveaxis(w, axis, -1)
    lead = w.shape[:-1]
    w = w.reshape(lead + (N_HEADS, per_head))
    w = jnp.pad(w, [(0, 0)] * len(lead) + [(0, 0), (0, HEAD_PAD - per_head)])
    return jnp.moveaxis(w.reshape(lead + (N_HEADS * HEAD_PAD,)), -1, axis)


def _unpad_heads(w, per_head, axis):
    w = jnp.moveaxis(w, axis, -1)
    lead = w.shape[:-1]
    w = w.reshape(lead + (N_HEADS, HEAD_PAD))[..., :per_head]
    return jnp.moveaxis(w.reshape(lead + (N_HEADS * per_head,)), -1, axis)


def _lanes128(vec, offset):
    return jnp.zeros((1, HEAD_PAD), f32).at[0, offset:offset + vec.shape[0]].set(vec)


def _to_segments(a):
    S, D = a.shape
    return a.reshape(SEGMENTS, S // SEGMENTS, D).transpose(1, 0, 2).reshape(S, D)


def _from_segments(a):
    S, D = a.shape
    return a.reshape(S // SEGMENTS, SEGMENTS, D).transpose(1, 0, 2).reshape(S, D)


def _s5_params_fwd(p):
    G, N, P = p["b_re"].shape
    def disc_body(lr, li, ld, o1, o2, o3, o4):
        o1[...], o2[...], o3[...], o4[...] = _s5_disc(lr[...], li[...], ld[...])

    ab_r, ab_i, f_r, f_i = pl.pallas_call(disc_body, name="s5_disc", out_shape=[SDS((G, N), f32)] * 4)(
        p["lam_re"], p["lam_im"], p["log_dt"].reshape(G, 1))
    bb_r, bb_i = _rowwise(_s5_bbar, [p["b_re"].reshape(G * N, P), p["b_im"].reshape(G * N, P), f_r.reshape(G * N, 1), f_i.reshape(G * N, 1)],
                          [], [(P, f32), (P, f32)], tm=512, name="s5_bbar")
    ns = G // 8
    out = dict(
        f_r=f_r, f_i=f_i,
        a_r=ab_r.reshape(ns, 1, SLAB_ST), a_i=ab_i.reshape(ns, 1, SLAB_ST),
        wb_r=_blockdiag(bb_r.reshape(G, N, P).transpose(0, 2, 1), P, N), wb_i=_blockdiag(bb_i.reshape(G, N, P).transpose(0, 2, 1), P, N),
        wc_r=_blockdiag(p["c_re"].transpose(0, 2, 1), N, P), wc_i=_blockdiag(p["c_im"].transpose(0, 2, 1), N, P),
    )
    return out


def _s5_params_bwd(p, sp, d_ar, d_ai, d_wbr, d_wbi, d_wcr, d_wci):
    G, N, P = p["b_re"].shape
    dbb_r = _blockdiag_take(d_wbr, P, N).transpose(0, 2, 1).reshape(G * N, P)
    dbb_i = _blockdiag_take(d_wbi, P, N).transpose(0, 2, 1).reshape(G * N, P)
    d_cre = _blockdiag_take(d_wcr, N, P).transpose(0, 2, 1)
    d_cim = _blockdiag_take(d_wci, N, P).transpose(0, 2, 1)
    rows = [p["b_re"].reshape(G * N, P), p["b_im"].reshape(G * N, P), sp["f_r"].reshape(G * N, 1), sp["f_i"].reshape(G * N, 1)]
    d_br, d_bi, d_fr, d_fi = _rowwise_bwd(_s5_bbar, rows, [], [dbb_r, dbb_i], [0, 1, 2, 3], [f32] * 4, tm=512, name="s5_bbar_bwd")

    def seg_sum(d, name):
        return _sum_lead(d.transpose(1, 0, 2).reshape(SEGMENTS, G, N), f32, name)

    def body(lr, li, ld, c1, c2, c3, c4, o1, o2, o3):
        _, vjp = jax.vjp(_s5_disc, lr[...], li[...], ld[...])
        o1[...], o2[...], o3[...] = vjp((c1[...], c2[...], c3[...], c4[...]))

    d_lr, d_li, d_ld = pl.pallas_call(
        body, name="s5_disc_bwd", out_shape=[SDS((G, N), f32), SDS((G, N), f32), SDS((G, 1), f32)],
    )(p["lam_re"], p["lam_im"], p["log_dt"].reshape(G, 1), seg_sum(d_ar, "s5_da_re_sum"), seg_sum(d_ai, "s5_da_im_sum"),
      d_fr.reshape(G, N), d_fi.reshape(G, N))
    return dict(lam_re=d_lr, lam_im=d_li, log_dt=d_ld.reshape(G), b_re=d_br.reshape(G, N, P), b_im=d_bi.reshape(G, N, P),
                c_re=d_cre, c_im=d_cim)


_SCAN_TB = 32
_ATTN_T = 512


def _s5_mix_fwd(h, sp, name):
    S = h.shape[0]
    seg = S // SEGMENTS
    zeros = jnp.zeros((h.shape[1] // SLAB_CH, SEGMENTS, SLAB_ST), f32)
    common = (sp["wb_r"], sp["wb_i"], sp["a_r"], sp["a_i"])
    e_r, e_i, _, _ = _s5_scan_fwd(h, *common, zeros, zeros, sp["wc_r"], sp["wc_i"], want_y=False, tb=_SCAN_TB, name=name + "_local")
    i_r, i_i = _s5_chain(e_r, e_i, sp["a_r"], sp["a_i"], seg=seg, reverse=False, name=name + "_chain")
    y, _, _, bs_r, bs_i = _s5_scan_fwd(h, *common, i_r, i_i, sp["wc_r"], sp["wc_i"], want_y=True, tb=_SCAN_TB, name=name)
    return y, (bs_r, bs_i)


def _s5_mix_bwd(h, dy, sp, saved, name):
    S = h.shape[0]
    seg = S // SEGMENTS
    bs_r, bs_i = saved
    zeros = jnp.zeros((h.shape[1] // SLAB_CH, SEGMENTS, SLAB_ST), f32)
    gf_r, gf_i = _s5_scan_bwd(None, dy, None, None, sp["a_r"], sp["a_i"], None, None, zeros, zeros, sp["wc_r"], sp["wc_i"],
                              full=False, tb=_SCAN_TB, name=name + "_local")
    gi_r, gi_i = _s5_chain(gf_r, gf_i, sp["a_r"], sp["a_i"], seg=seg, reverse=True, name=name + "_chain")
    du, d_wbr, d_wbi, d_wcr, d_wci, d_ar, d_ai, _, _ = _s5_scan_bwd(
        h, dy, sp["wb_r"], sp["wb_i"], sp["a_r"], sp["a_i"], bs_r, bs_i, gi_r, gi_i, sp["wc_r"], sp["wc_i"],
        full=True, tb=_SCAN_TB, name=name)
    return du, (d_ar, d_ai, d_wbr, d_wbi, d_wcr, d_wci)


def _ffn_fwd(x1, mod, n2g, w_gu, w_down, tag):
    F = w_down.shape[0]
    h2 = _rowwise(lambda x, g, m: _normmod(x, g, m[3:4], m[4:5]), [x1], [n2g, mod], [(x1.shape[1], bf16)], name=tag + "_norm2")[0]
    ab = _mm(h2, w_gu, name=tag + "_gu")
    act = _rowwise(lambda ab: jax.nn.silu(ab[:, :F]) * ab[:, F:], [ab], [], [(F, bf16)], name=tag + "_act")[0]
    f = _mm(act, w_down, name=tag + "_down")
    x2 = _rowwise(lambda x, f, m: x + m[5:6] * f, [x1, f], [mod], [(x1.shape[1], f32)], name=tag + "_res2")[0]
    return x2, dict(x1=x1, h2=h2, ab=ab, act=act, f=f)


def _ffn_bwd(dx2, sv, mod, n2g, w_gu, w_down, tag):
    F = w_down.shape[0]
    D = dx2.shape[1]
    df, dgate2 = _rowwise(lambda dx, f, m: (m[5:6] * dx, jnp.sum(dx * f, axis=0, keepdims=True)), [dx2, sv["f"]], [mod],
                          [(D, bf16)], [(1, D)], name=tag + "_res2_bwd")
    dact = _mm(df, w_down, tb=True, name=tag + "_down_dx")
    dw_down = _mm(sv["act"], df, ta=True, name=tag + "_down_dw")
    dab = _rowwise_bwd(lambda ab: jax.nn.silu(ab[:, :F]) * ab[:, F:], [sv["ab"]], [], [dact], [0], [bf16], name=tag + "_act_bwd")[0]
    dh2 = _mm(dab, w_gu, tb=True, name=tag + "_gu_dx")
    dw_gu = _mm(sv["h2"], dab, ta=True, name=tag + "_gu_dw")
    dx1, dn2g, dmod = _rowwise_bwd(lambda x, g, m: _normmod(x, g, m[3:4], m[4:5]), [sv["x1"]], [n2g, mod], [dh2], [0], [f32],
                                   adds={0: dx2}, name=tag + "_norm2_bwd")
    dmod = dmod.at[5:6].add(dgate2)
    return dx1, dw_gu, dw_down, dn2g, dmod


def _device_step(x, target, pos, mods, kmod, W):
    S, D = x.shape
    F = W["ffn_w_down"].shape[1]
    inv = 1.0 / (ROPE_THETA ** (jnp.arange(0, ROPE, 2, dtype=f32) / ROPE))
    inv128 = _lanes128(jnp.concatenate([inv, inv]), NOPE)
    sign128 = _lanes128(jnp.concatenate([-jnp.ones(ROPE // 2, f32), jnp.ones(ROPE // 2, f32)]), NOPE)
    cf, ss = _rowwise(lambda p, iv, sg: (jnp.cos(p * iv), jnp.sin(p * iv) * sg), [pos], [inv128, sign128],
                      [(HEAD_PAD, f32), (HEAD_PAD, f32)], name="rope_table")

    w_gu = [jnp.concatenate([W["ffn_w_gate"][l], W["ffn_w_up"][l]], axis=1) for l in range(DEPTH)]
    saved = []
    xs = _to_segments(x)

    s5p = []
    for l in range(N_A):
        tag = f"l{l}"
        mod = mods[l]
        p = {k: W["s5_" + k][l] for k in ("lam_re", "lam_im", "log_dt", "b_re", "b_im", "c_re", "c_im")}
        sp = _s5_params_fwd(p)
        s5p.append((p, sp))
        n1g = W["norm1_g"][l][None]
        dsk = W["s5_d"][l][None]
        bgl = W["s5_b_glu"][l][None]
        h = _rowwise(lambda x, g, m: _normmod(x, g, m[0:1], m[1:2]), [xs], [n1g, mod], [(D, f32)], name=tag + "_norm1")[0]
        y_ssm, scan_saved = _s5_mix_fwd(h, sp, tag + "_scan")
        g = _rowwise(lambda ys, h, d: jax.nn.gelu(ys + d * h), [y_ssm, h], [dsk], [(D, f32)], name=tag + "_gelu")[0]
        z = _mm(g, W["s5_w_glu"][l], name=tag + "_glu")
        x1 = _rowwise(lambda x, g, z, b, m: x + m[2:3] * (g * jax.nn.sigmoid(z + b)), [xs, g, z], [bgl, mod], [(D, f32)], name=tag + "_res1")[0]
        x2, fsv = _ffn_fwd(x1, mod, W["norm2_g"][l][None], w_gu[l], W["ffn_w_down"][l], tag)
        saved.append(dict(x0=xs, h=h, y_ssm=y_ssm, g=g, z=z, scan=scan_saved, ffn=fsv))
        xs = x2
    x = _from_segments(xs)

    w_kv_a = jnp.concatenate([W["w_kv_a"][:, :KV_RANK], jnp.zeros((D, NOPE), f32).astype(W["w_kv_a"].dtype), W["w_kv_a"][:, KV_RANK:],
                              jnp.zeros((D, HEAD_PAD - NOPE - ROPE), f32).astype(W["w_kv_a"].dtype)], axis=1)
    wkb = W["w_kv_b"].reshape(KV_RANK, N_HEADS, NOPE + VDIM)
    w_kb = _pad_heads(wkb[:, :, :NOPE].reshape(KV_RANK, N_HEADS * NOPE), NOPE, 1)
    w_vb = _pad_heads(wkb[:, :, NOPE:].reshape(KV_RANK, N_HEADS * VDIM), VDIM, 1)
    kvg = W["kv_norm_g"][None]
    ga = W["kv_a_norm_g"][None]
    gkr = _lanes128(W["k_rope_norm_g"], NOPE)
    gkn = _lanes128(W["k_nope_norm_g"], 0)
    hk = _rowwise(lambda x, g, m: _normmod(x, g, m[0:1], m[1:2]), [x], [kvg, kmod], [(D, bf16)], name="kv_norm")[0]
    kva = _mm(hk, w_kv_a, name="kv_a")
    ckv, kr = _rowwise(_kv_a_post, [kva, cf, ss], [ga, gkr], [(KV_RANK, f32), (HEAD_PAD, f32)], name="kv_a_post")
    kraw = _mm(ckv, w_kb, name="kv_bk")
    vpad = _mm(ckv, w_vb, out_dtype=bf16, name="kv_bv")
    kpad = _rowwise(_k_heads, [kraw, kr], [gkn], [(N_HEADS * HEAD_PAD, bf16)], name="k_heads")[0]
    kv_saved = dict(x=x, hk=hk, kva=kva, ckv=ckv, kr=kr, kraw=kraw)

    for l in range(N_A, DEPTH):
        tag = f"l{l}"
        j = l - N_A
        mod = mods[l]
        n1g = W["norm1_g"][l][None]
        w_uq = _pad_heads(W["mla_w_uq"][j], NOPE + ROPE, 1)
        w_o = _pad_heads(W["mla_w_o"][j], VDIM, 0)
        qg = W["mla_q_norm_g"][j][None]
        gqn = _lanes128(W["mla_q_nope_norm_g"][j], 0)
        gqr = _lanes128(W["mla_q_rope_norm_g"][j], NOPE)
        h = _rowwise(lambda x, g, m: _normmod(x, g, m[0:1], m[1:2]), [x], [n1g, mod], [(D, bf16)], name=tag + "_norm1")[0]
        qa = _mm(h, W["mla_w_dq"][j], name=tag + "_dq")
        qc = _rowwise(_rms, [qa], [qg], [(qa.shape[1], f32)], name=tag + "_qnorm")[0]
        qraw = _mm(qc, w_uq, name=tag + "_uq")
        q = _rowwise(_q_heads, [qraw, cf, ss], [gqn, gqr], [(N_HEADS * HEAD_PAD, bf16)], name=tag + "_q_heads")[0]
        o, lse = _attn_fwd(q, kpad, vpad, t=_ATTN_T, name=tag + "_attn")
        mix = _mm(o, w_o, name=tag + "_wo")
        x1 = _rowwise(lambda x, mx, m: x + m[2:3] * mx, [x, mix], [mod], [(D, f32)], name=tag + "_res1")[0]
        x2, fsv = _ffn_fwd(x1, mod, W["norm2_g"][l][None], w_gu[l], W["ffn_w_down"][l], tag)
        saved.append(dict(x0=x, h=h, qa=qa, qc=qc, qraw=qraw, q=q, o=o, lse=lse, mix=mix, w_uq=w_uq, w_o=w_o, ffn=fsv))
        x = x2

    dx, loss = _rowwise(
        lambda y, t: ((y - t) * (1.0 / D), jnp.full((1, LANES), 0.5 * jnp.sum(jnp.mean(jnp.square(y - t), axis=-1)), f32)),
        [x, target], [], [(D, f32)], [(1, LANES)], name="loss")

    gW = {}
    dmods = [None] * DEPTH
    g_gate, g_up, g_down, g_n1, g_n2 = [None] * DEPTH, [None] * DEPTH, [None] * DEPTH, [None] * DEPTH, [None] * DEPTH
    dks, dvs = [], []
    g_dq, g_qn, g_uq, g_qnn, g_qrn, g_wo = [None] * 2, [None] * 2, [None] * 2, [None] * 2, [None] * 2, [None] * 2
    for l in range(DEPTH - 1, N_A - 1, -1):
        tag = f"l{l}"
        j = l - N_A
        sv = saved[l]
        mod = mods[l]
        dx1, dw_gu, dw_down, g_n2[l], dmod = _ffn_bwd(dx, sv["ffn"], mod, W["norm2_g"][l][None], w_gu[l], W["ffn_w_down"][l], tag)
        g_gate[l], g_up[l], g_down[l] = dw_gu[:, :F], dw_gu[:, F:], dw_down
        dmix, dgate1 = _rowwise(lambda dx, mx, m: (m[2:3] * dx, jnp.sum(dx * mx, axis=0, keepdims=True)), [dx1, sv["mix"]], [mod],
                                [(D, bf16)], [(1, D)], name=tag + "_res1_bwd")
        do = _mm(dmix, sv["w_o"], tb=True, out_dtype=bf16, name=tag + "_wo_dx")
        g_wo[j] = _unpad_heads(_mm(sv["o"], dmix, ta=True, name=tag + "_wo_dw"), VDIM, 0)
        dq, dk, dv = _attn_bwd(sv["q"], kpad, vpad, do, sv["o"], sv["lse"], t=_ATTN_T, name=tag + "_attn_bwd")
        dks.append(dk)
        dvs.append(dv)
        gqn = _lanes128(W["mla_q_nope_norm_g"][j], 0)
        gqr = _lanes128(W["mla_q_rope_norm_g"][j], NOPE)
        dqraw, dgqn, dgqr = _rowwise_bwd(lambda qr, c, s, a, b: _q_heads(qr, c, s, a, b), [sv["qraw"], cf, ss], [gqn, gqr], [dq], [0], [bf16],
                                         name=tag + "_q_heads_bwd")
        g_qnn[j], g_qrn[j] = dgqn[0, :NOPE], dgqr[0, NOPE:NOPE + ROPE]
        dqc = _mm(dqraw, sv["w_uq"], tb=True, name=tag + "_uq_dx")
        g_uq[j] = _unpad_heads(_mm(sv["qc"], dqraw, ta=True, name=tag + "_uq_dw"), NOPE + ROPE, 1)
        qg = W["mla_q_norm_g"][j][None]
        dqa, dqg = _rowwise_bwd(_rms, [sv["qa"]], [qg], [dqc], [0], [bf16], name=tag + "_qnorm_bwd")
        g_qn[j] = dqg[0]
        dh = _mm(dqa, W["mla_w_dq"][j], tb=True, name=tag + "_dq_dx")
        g_dq[j] = _mm(sv["h"], dqa, ta=True, name=tag + "_dq_dw")
        dx, dn1g, dmod1 = _rowwise_bwd(lambda x, g, m: _normmod(x, g, m[0:1], m[1:2]), [sv["x0"]], [W["norm1_g"][l][None], mod], [dh], [0], [f32],
                                       adds={0: dx1}, name=tag + "_norm1_bwd")
        g_n1[l] = dn1g[0]
        dmods[l] = (dmod + dmod1).at[2:3].add(dgate1)

    gkn = _lanes128(W["k_nope_norm_g"], 0)
    dk_sum = _sum_lead(jnp.stack(dks), f32, "dk_sum")
    dv_sum = _sum_lead(jnp.stack(dvs), bf16, "dv_sum")
    dkraw, dkr, dgkn = _rowwise_bwd(lambda kr_, r, g: _k_heads(kr_, r, g), [kv_saved["kraw"], kv_saved["kr"]], [gkn], [dk_sum], [0, 1], [bf16, f32],
                                    name="k_heads_bwd")
    dckv = _sum_lead(jnp.stack([_mm(dkraw, w_kb, tb=True, name="kv_bk_dx"), _mm(dv_sum, w_vb, tb=True, name="kv_bv_dx")]), f32, "dckv_sum")
    g_kb = _unpad_heads(_mm(kv_saved["ckv"], dkraw, ta=True, name="kv_bk_dw"), NOPE, 1)
    g_vb = _unpad_heads(_mm(kv_saved["ckv"], dv_sum, ta=True, name="kv_bv_dw"), VDIM, 1)
    gW["w_kv_b"] = jnp.concatenate([g_kb.reshape(KV_RANK, N_HEADS, NOPE), g_vb.reshape(KV_RANK, N_HEADS, VDIM)], axis=2).reshape(KV_RANK, -1)
    dkva, dga, dgkr = _rowwise_bwd(_kv_a_post, [kv_saved["kva"], cf, ss], [ga, gkr], [dckv, dkr], [0], [bf16], name="kv_a_post_bwd")
    dhk = _mm(dkva, w_kv_a, tb=True, name="kv_a_dx")
    g_kva = _mm(kv_saved["hk"], dkva, ta=True, name="kv_a_dw")
    gW["w_kv_a"] = jnp.concatenate([g_kva[:, :KV_RANK], g_kva[:, KV_RANK + NOPE:KV_RANK + NOPE + ROPE]], axis=1)
    dx, dkvg, dkmod = _rowwise_bwd(lambda x, g, m: _normmod(x, g, m[0:1], m[1:2]), [kv_saved["x"]], [kvg, kmod], [dhk], [0], [f32],
                                   adds={0: dx}, name="kv_norm_bwd")
    gW["kv_norm_g"], gW["kv_a_norm_g"] = dkvg[0], dga[0]
    gW["k_nope_norm_g"], gW["k_rope_norm_g"] = dgkn[0, :NOPE], dgkr[0, NOPE:NOPE + ROPE]

    dxs = _to_segments(dx)
    g_s5 = {k: [None] * N_A for k in ("lam_re", "lam_im", "log_dt", "b_re", "b_im", "c_re", "c_im", "d", "w_glu", "b_glu")}
    for l in range(N_A - 1, -1, -1):
        tag = f"l{l}"
        sv = saved[l]
        mod = mods[l]
        p, sp = s5p[l]
        dsk = W["s5_d"][l][None]
        bgl = W["s5_b_glu"][l][None]
        dx1, dw_gu, dw_down, g_n2[l], dmod = _ffn_bwd(dxs, sv["ffn"], mod, W["norm2_g"][l][None], w_gu[l], W["ffn_w_down"][l], tag)
        g_gate[l], g_up[l], g_down[l] = dw_gu[:, :F], dw_gu[:, F:], dw_down

        def res1(g, z, b, m):
            return m[2:3] * (g * jax.nn.sigmoid(z + b))
        dg1, dz, dbgl, dmod_g = _rowwise_bwd(res1, [sv["g"], sv["z"]], [bgl, mod], [dx1], [0, 1], [f32, bf16], name=tag + "_res1_bwd")
        dg2 = _mm(dz, W["s5_w_glu"][l], tb=True, name=tag + "_glu_dx")
        g_s5["w_glu"][l] = _mm(sv["g"], dz, ta=True, name=tag + "_glu_dw")
        g_s5["b_glu"][l] = dbgl[0]

        def gelu_bwd(ys, h, dga_, dgb_, d):
            _, vjp = jax.vjp(lambda ys, h, d: jax.nn.gelu(ys + d * h), ys, h, d)
            dys, dh, dd = vjp(dga_ + dgb_)
            return dys, dh, dd
        dys, dh_a, ddsk = _rowwise(gelu_bwd, [sv["y_ssm"], sv["h"], dg1, dg2], [dsk], [(D, f32), (D, f32)], [(1, D)], name=tag + "_gelu_bwd")
        g_s5["d"][l] = ddsk[0]
        du, scan_g = _s5_mix_bwd(sv["h"], dys, sp, sv["scan"], tag + "_scan_bwd")
        pg = _s5_params_bwd(p, sp, *scan_g)
        for k in ("lam_re", "lam_im", "log_dt", "b_re", "b_im", "c_re", "c_im"):
            g_s5[k][l] = pg[k]
        dxs, dn1g, dmod1 = _rowwise_bwd(lambda x, g, m: _normmod(x, g, m[0:1], m[1:2]), [sv["x0"]], [W["norm1_g"][l][None], mod],
                                        [_sum_lead(jnp.stack([dh_a, du]), f32, tag + "_dh_sum")], [0], [f32], adds={0: dx1}, name=tag + "_norm1_bwd")
        g_n1[l] = dn1g[0]
        dmods[l] = dmod + dmod1 + dmod_g
    dx = _from_segments(dxs)

    gW.update(
        norm1_g=jnp.stack(g_n1), norm2_g=jnp.stack([g[0] for g in g_n2]),
        ffn_w_gate=jnp.stack(g_gate), ffn_w_up=jnp.stack(g_up), ffn_w_down=jnp.stack(g_down),
        mla_w_dq=jnp.stack(g_dq), mla_q_norm_g=jnp.stack(g_qn), mla_w_uq=jnp.stack(g_uq),
        mla_q_nope_norm_g=jnp.stack(g_qnn), mla_q_rope_norm_g=jnp.stack(g_qrn), mla_w_o=jnp.stack(g_wo),
    )
    for k, v in g_s5.items():
        gW["s5_" + k] = jnp.stack(v)
    return loss, dx, gW, jnp.stack(dmods), dkmod


_WEIGHTS = ['ada_w', 'ada_b', 'norm1_g', 'norm2_g', 'ffn_w_gate', 'ffn_w_up', 'ffn_w_down', 's5_lam_re', 's5_lam_im', 's5_log_dt',
            's5_b_re', 's5_b_im', 's5_c_re', 's5_c_im', 's5_d', 's5_w_glu', 's5_b_glu', 'kv_ada_w', 'kv_ada_b', 'kv_norm_g', 'w_kv_a',
            'kv_a_norm_g', 'w_kv_b', 'k_nope_norm_g', 'k_rope_norm_g', 'mla_w_dq', 'mla_q_norm_g', 'mla_w_uq', 'mla_q_nope_norm_g',
            'mla_q_rope_norm_g', 'mla_w_o']
_BIG = {"ffn_w_gate": 2, "ffn_w_up": 2, "ffn_w_down": 1, "s5_w_glu": 1, "w_kv_a": 0, "w_kv_b": 1, "mla_w_dq": 1, "mla_w_uq": 2, "mla_w_o": 1}
_TENSOR_PARALLEL = ("ada_w", "kv_ada_w")
_SHARDED_VECS = ("s5_d", "s5_b_glu")
_CHIPS = 4
_FLAT_W = 1024
_FLAT_ALIGN = 512


def _pack_rows(arrs, width, align):
    flat = jnp.concatenate([a.reshape(-1) for a in arrs])
    rows = -(-flat.shape[0] // (width * align)) * align
    flat = jnp.pad(flat, (0, rows * width - flat.shape[0]))
    return flat.reshape(rows, width)


def _unpack(flat, shapes):
    flat = flat.reshape(-1)
    out, off = [], 0
    for s in shapes:
        n = math.prod(s)
        out.append(flat[off:off + n].reshape(s))
        off += n
    return out


def _adamw(w, g, m, v, name):
    c1 = 1.0 - ADAM_B1 ** ADAM_STEP
    c2 = 1.0 - ADAM_B2 ** ADAM_STEP

    def fn(w, g, m, v):
        m = ADAM_B1 * m + (1.0 - ADAM_B1) * g
        v = ADAM_B2 * v + (1.0 - ADAM_B2) * jnp.square(g)
        delta = -ADAM_LR * ((m / c1) / (jnp.sqrt(v / c2) + ADAM_EPS) + ADAM_WD * w)
        return delta, m, v

    C = w.shape[1]
    tm = _pick(w.shape[0], (256, 128, 64, 32, 16, 8))
    return _rowwise(fn, [w, g, m, v], [], [(C, f32)] * 3, tm=tm, name=name)


def kernel(x, c, positions, ada_w, ada_b, norm1_g, norm2_g, ffn_w_gate, ffn_w_up, ffn_w_down, s5_lam_re, s5_lam_im, s5_log_dt, s5_b_re, s5_b_im, s5_c_re, s5_c_im, s5_d, s5_w_glu, s5_b_glu, kv_ada_w, kv_ada_b, kv_norm_g, w_kv_a, kv_a_norm_g, w_kv_b, k_nope_norm_g, k_rope_norm_g, mla_w_dq, mla_q_norm_g, mla_w_uq, mla_q_nope_norm_g, mla_q_rope_norm_g, mla_w_o, loss_target, m_ada_w, m_ada_b, m_norm1_g, m_norm2_g, m_ffn_w_gate, m_ffn_w_up, m_ffn_w_down, m_s5_lam_re, m_s5_lam_im, m_s5_log_dt, m_s5_b_re, m_s5_b_im, m_s5_c_re, m_s5_c_im, m_s5_d, m_s5_w_glu, m_s5_b_glu, m_kv_ada_w, m_kv_ada_b, m_kv_norm_g, m_w_kv_a, m_kv_a_norm_g, m_w_kv_b, m_k_nope_norm_g, m_k_rope_norm_g, m_mla_w_dq, m_mla_q_norm_g, m_mla_w_uq, m_mla_q_nope_norm_g, m_mla_q_rope_norm_g, m_mla_w_o, v_ada_w, v_ada_b, v_norm1_g, v_norm2_g, v_ffn_w_gate, v_ffn_w_up, v_ffn_w_down, v_s5_lam_re, v_s5_lam_im, v_s5_log_dt, v_s5_b_re, v_s5_b_im, v_s5_c_re, v_s5_c_im, v_s5_d, v_s5_w_glu, v_s5_b_glu, v_kv_ada_w, v_kv_ada_b, v_kv_norm_g, v_w_kv_a, v_kv_a_norm_g, v_w_kv_b, v_k_nope_norm_g, v_k_rope_norm_g, v_mla_w_dq, v_mla_q_norm_g, v_mla_w_uq, v_mla_q_nope_norm_g, v_mla_q_rope_norm_g, v_mla_w_o):
    given = dict(locals())
    w = {n: given[n] for n in _WEIGHTS}
    m_in = {n: given["m_" + n] for n in _WEIGHTS}
    v_in = {n: given["v_" + n] for n in _WEIGHTS}
    S, D = x.shape[1], x.shape[2]
    ax, ay, ac = lax.axis_index("x"), lax.axis_index("y"), lax.axis_index("c")
    chip = 2 * ax + ay
    me = 4 * ax + 2 * ay + ac
    n_ada = ada_w.shape[2]
    n_kada = kv_ada_w.shape[1]

    g1 = _xchg(_pack_rows([c, s5_d, s5_b_glu], LANES, SUBLANES), "all", True, "gather_cond")
    g1 = g1.reshape(8, -1)
    c_all = g1[:, :D]
    vec = g1[0::2, D:D + 2 * s5_d.size].reshape(_CHIPS, 2, N_A, s5_d.shape[1])
    full_vecs = {"s5_d": vec[:, 0].transpose(1, 0, 2).reshape(N_A, D), "s5_b_glu": vec[:, 1].transpose(1, 0, 2).reshape(N_A, D)}
    c_act = _rowwise(jax.nn.silu, [c_all], [], [(D, f32)], tm=8, name="cond_silu")[0]

    mod_cols = [_mm(c_act, ada_w[l], cast=False, name="ada_proj") for l in range(DEPTH)] + [_mm(c_act, kv_ada_w, cast=False, name="kv_ada_proj")]
    g2 = _xchg(_pack_rows([jnp.concatenate(mod_cols, axis=1)], LANES, SUBLANES), "all", True, "gather_mod")
    g2 = g2.reshape(8, 8, -1)[0::2]
    mine = lax.dynamic_index_in_dim(g2, me, axis=1, keepdims=False)
    mod_lin = mine[:, :DEPTH * n_ada].reshape(_CHIPS, DEPTH, n_ada).transpose(1, 0, 2).reshape(DEPTH, 6 * D)
    kmod_lin = mine[:, DEPTH * n_ada:].reshape(1, 2 * D)
    mods = _rowwise(lambda a, b: a + b, [mod_lin, ada_b], [], [(6 * D, f32)], tm=DEPTH, name="ada_bias")[0].reshape(DEPTH, 6, D)
    kmod = _rowwise(lambda a, b: a + b, [kmod_lin, kv_ada_b.reshape(1, 2 * D)], [], [(2 * D, f32)], tm=1, name="kv_ada_bias")[0].reshape(2, D)

    big = list(_BIG)
    flat_w = _pack_rows([w[n] for n in big], _FLAT_W, _FLAT_ALIGN)
    R = flat_w.shape[0]
    R2 = R // 2
    my_half = lax.dynamic_slice_in_dim(flat_w.astype(bf16), ac * R2, R2, axis=0)
    halves_in = _xchg(my_half, "xy", True, "gather_weights")
    both = _pair(halves_in.reshape(1, _CHIPS * R2, _FLAT_W), "swap", bf16, "gather_pair")
    parts = _unpack_chips(both.reshape(2, _CHIPS, R2 * _FLAT_W), [w[n].shape for n in big])
    W = {n: jnp.concatenate([parts[i][q] for q in range(_CHIPS)], axis=_BIG[n]) for i, n in enumerate(big)}
    for n in _WEIGHTS:
        if n not in _BIG and n not in _TENSOR_PARALLEL and n not in _SHARDED_VECS and n not in ("ada_b", "kv_ada_b"):
            W[n] = w[n]
    W.update(full_vecs)

    pos = positions.reshape(S, 1).astype(f32)
    loss, dx, gW, dmods, dkmod = _device_step(x[0], loss_target[0], pos, mods, kmod, W)

    g4 = _xchg(_pack_rows([dmods, dkmod], LANES, SUBLANES), "all", True, "gather_dmod")
    ada_b_sum = _sum_lead(g4, f32, "dmod_sum").reshape(-1)
    g_ada_b = ada_b_sum[:DEPTH * 6 * D].reshape(DEPTH, 6 * D)
    g_kv_ada_b = ada_b_sum[DEPTH * 6 * D:DEPTH * 6 * D + 2 * D]
    dm_all = g4.reshape(8, -1)
    dm_ada = lax.dynamic_slice_in_dim(dm_all[:, :DEPTH * 6 * D].reshape(8, DEPTH, 6 * D), chip * n_ada, n_ada, axis=2)
    dm_kv = lax.dynamic_slice_in_dim(dm_all[:, DEPTH * 6 * D:DEPTH * 6 * D + 2 * D], chip * n_kada, n_kada, axis=1)

    def outer(at, b):
        acc = at[:, 0:1] * b[0:1, :]
        for i in range(1, 8):
            acc = acc + at[:, i:i + 1] * b[i:i + 1, :]
        return acc

    c_act_t = c_act.T
    g_ada_w = jnp.stack([_rowwise(outer, [c_act_t], [dm_ada[:, l]], [(n_ada, f32)], name="ada_dw")[0] for l in range(DEPTH)])
    g_kv_ada_w = _rowwise(outer, [c_act_t], [dm_kv], [(n_kada, f32)], name="kv_ada_dw")[0]

    pieces = [[jnp.split(gW[n], _CHIPS, axis=_BIG[n])[q] for n in big] for q in range(_CHIPS)]
    gflat = jnp.stack([_pack_rows(pieces[q], _FLAT_W, _FLAT_ALIGN) for q in range(_CHIPS)]).astype(bf16)
    R2 = R // 2
    halves = gflat.reshape(_CHIPS, 2, R2, _FLAT_W).transpose(1, 0, 2, 3).reshape(2, _CHIPS * R2, _FLAT_W)
    hsum = _pair(halves, "add", bf16, "grads_pair").reshape(_CHIPS, R2, _FLAT_W)
    rb = _xchg(hsum, "xy", False, "grads_chips")
    g_flat = _pair(rb, "swap", f32, "grads_halves").reshape(R, _FLAT_W)
    g_big = dict(zip(big, _unpack(g_flat, [w[n].shape for n in big])))

    small = [n for n in _WEIGHTS if n not in _BIG and n not in _TENSOR_PARALLEL and n not in ("ada_b", "kv_ada_b")]
    svec = _pack_rows([loss[0, :1]] + [gW[n] for n in small], LANES, 8 * SUBLANES)
    rows8 = svec.shape[0] // 8
    rs = _xchg(svec.reshape(8, rows8, LANES), "all", False, "small_scatter")
    red = _sum_lead(rs, f32, "small_sum")
    full = _xchg(red, "all", True, "small_gather").reshape(-1)
    loss_tot = full[0]
    g_small = dict(zip(small, _unpack(full[1:], [gW[n].shape for n in small])))
    for n in _SHARDED_VECS:
        g_small[n] = lax.dynamic_slice_in_dim(g_small[n], chip * w[n].shape[1], w[n].shape[1], axis=1)
    g_small["ada_b"] = g_ada_b
    g_small["kv_ada_b"] = g_kv_ada_b

    grads = dict(g_big)
    grads.update(g_small)
    grads["ada_w"] = g_ada_w
    grads["kv_ada_w"] = g_kv_ada_w

    delta, new_m, new_v = {}, {}, {}
    d_, m_, v_ = _adamw(flat_w, g_flat, _pack_rows([m_in[n] for n in big], _FLAT_W, _FLAT_ALIGN),
                        _pack_rows([v_in[n] for n in big], _FLAT_W, _FLAT_ALIGN), "adamw_matmul")
    shapes = [w[n].shape for n in big]
    for res, src in ((delta, d_), (new_m, m_), (new_v, v_)):
        res.update(zip(big, _unpack(src, shapes)))
    for n in _TENSOR_PARALLEL:
        C = w[n].shape[-1]
        d_, m_, v_ = _adamw(w[n].reshape(-1, C), grads[n].reshape(-1, C), m_in[n].reshape(-1, C), v_in[n].reshape(-1, C), "adamw_" + n)
        delta[n], new_m[n], new_v[n] = d_.reshape(w[n].shape), m_.reshape(w[n].shape), v_.reshape(w[n].shape)
    rest = [n for n in _WEIGHTS if n not in _BIG and n not in _TENSOR_PARALLEL]
    d_, m_, v_ = _adamw(*[_pack_rows([src[n] for n in rest], LANES, SUBLANES) for src in (w, grads, m_in, v_in)], "adamw_small")
    shapes = [w[n].shape for n in rest]
    for res, src in ((delta, d_), (new_m, m_), (new_v, v_)):
        res.update(zip(rest, _unpack(src, shapes)))

    return (loss_tot, dx[None], *[grads[n] for n in _WEIGHTS], *[delta[n] for n in _WEIGHTS],
            *[new_m[n] for n in _WEIGHTS], *[new_v[n] for n in _WEIGHTS])


def _unpack_chips(halves, shapes):
    _, n_chip, L = halves.shape
    out, off = [], 0
    for s in shapes:
        n = math.prod(s)
        pieces = []
        if off < L:
            pieces.append(halves[0][:, off:min(off + n, L)])
        if off + n > L:
            pieces.append(halves[1][:, max(off, L) - L:off + n - L])
        flat = pieces[0] if len(pieces) == 1 else jnp.concatenate(pieces, axis=1)
        out.append(flat.reshape((n_chip,) + tuple(s)))
        off += n
    return out
```

```python
import functools
import math

import jax
import jax.numpy as jnp
from jax import lax
from jax.experimental import pallas as pl
from jax.experimental.pallas import tpu as pltpu

f32 = jnp.float32
bf16 = jnp.bfloat16
SDS = jax.ShapeDtypeStruct

EPS = 1e-6
CHUNK = 64
N_HEADS = 16
NOPE = 64
ROPE = 32
VDIM = 64
HEAD_PAD = 128
KV_RANK = 256
ROPE_THETA = 10000.0
ATTN_SCALE = 1.0 / math.sqrt(NOPE + ROPE)
SSM_GROUP = 16
SSM_STATE = 64
N_A = 2
DEPTH = 4
LANES = 128
SUBLANES = 8
SEGMENTS = 8
SLAB_CH = 128
SLAB_ST = 512
ADAM_LR, ADAM_B1, ADAM_B2, ADAM_EPS, ADAM_WD, ADAM_STEP = 0.001, 0.9, 0.999, 1e-08, 0.01, 10


def _pick(n, prefs=(512, 256, 128)):
    for p in prefs:
        if n % p == 0:
            return p
    return n


V7X_VMEM_BYTES = 64 << 20
MM_VMEM_LIMIT = V7X_VMEM_BYTES - (8 << 20)
MM_VMEM_BUDGET = 40 << 20
MM_MAX_TM, MM_MAX_TN = 1536, 1536
BF16_ROWS = 16


def _largest_divisor(n, cap, unit):
    if n <= cap:
        return n
    for d in range(cap - cap % unit, 0, -unit):
        if n % d == 0:
            return d
    return n


def _mm_tiles(M, N, K, sa, sb, so, m_on_lanes):
    tm = _largest_divisor(M, MM_MAX_TM, LANES if m_on_lanes else BF16_ROWS)
    tn = _largest_divisor(N, MM_MAX_TN, LANES)
    tk = K

    def need(tm, tn, tk):
        return 2 * (tm * tk * sa + tk * tn * sb + tm * tn * so) + (0 if tk == K else tm * tn * 4)

    m_unit = LANES if m_on_lanes else BF16_ROWS
    while need(tm, tn, tk) > MM_VMEM_BUDGET:
        if tk % 256 == 0 and tk >= 1024:
            tk //= 2
        elif tm % (2 * m_unit) == 0 and tm >= 512:
            tm //= 2
        elif tn % 256 == 0:
            tn //= 2
        else:
            break
    return tm, tn, tk


def _mm(a, b, *, ta=False, tb=False, out_dtype=f32, cast=True, name="mm"):
    if ta:
        K, M = a.shape
    else:
        M, K = a.shape
    if tb:
        N, K2 = b.shape
    else:
        K2, N = b.shape
    assert K == K2, (a.shape, b.shape, ta, tb)
    tm, tn, tk = _mm_tiles(M, N, K, a.dtype.itemsize, b.dtype.itemsize, jnp.dtype(out_dtype).itemsize, ta)
    nk = K // tk
    dims = (((0 if ta else 1,), (1 if tb else 0,)), ((), ()))

    def dot(a_ref, b_ref):
        av, bv = a_ref[...], b_ref[...]
        if cast:
            return lax.dot_general(av.astype(bf16), bv.astype(bf16), dims, preferred_element_type=f32)
        return lax.dot_general(av, bv, dims, preferred_element_type=f32, precision=lax.Precision.HIGHEST)

    def body_one(a_ref, b_ref, o_ref):
        o_ref[...] = dot(a_ref, b_ref).astype(o_ref.dtype)

    def body_acc(a_ref, b_ref, o_ref, acc):
        k = pl.program_id(2)

        @pl.when(k == 0)
        def _():
            acc[...] = jnp.zeros_like(acc)

        acc[...] += dot(a_ref, b_ref)

        @pl.when(k == nk - 1)
        def _():
            o_ref[...] = acc[...].astype(o_ref.dtype)

    a_spec = pl.BlockSpec((tk, tm), lambda i, j, k: (k, i)) if ta else pl.BlockSpec((tm, tk), lambda i, j, k: (i, k))
    b_spec = pl.BlockSpec((tn, tk), lambda i, j, k: (j, k)) if tb else pl.BlockSpec((tk, tn), lambda i, j, k: (k, j))
    return pl.pallas_call(
        body_one if nk == 1 else body_acc, name=name, grid=(M // tm, N // tn, nk),
        in_specs=[a_spec, b_spec], out_specs=pl.BlockSpec((tm, tn), lambda i, j, k: (i, j)),
        out_shape=SDS((M, N), out_dtype), scratch_shapes=[] if nk == 1 else [pltpu.VMEM((tm, tn), f32)],
        compiler_params=pltpu.CompilerParams(dimension_semantics=("parallel", "parallel", "arbitrary"),
                                             vmem_limit_bytes=MM_VMEM_LIMIT),
    )(a, b)


def _rowwise(fn, rows, consts, row_outs, acc_outs=(), *, tm=256, name="rowwise"):
    S = rows[0].shape[0]
    tm = min(tm, S)
    assert S % tm == 0, (S, tm)
    n_r, n_c, n_ro, n_ao = len(rows), len(consts), len(row_outs), len(acc_outs)

    def body(*refs):
        vals = [r[...] for r in refs[:n_r + n_c]]
        outs = fn(*vals)
        if not isinstance(outs, (tuple, list)):
            outs = (outs,)
        assert len(outs) == n_ro + n_ao, (name, len(outs), n_ro, n_ao)
        o_refs = refs[n_r + n_c:]
        for ref, val in zip(o_refs[:n_ro], outs[:n_ro]):
            ref[...] = val.astype(ref.dtype)
        if n_ao:
            @pl.when(pl.program_id(0) == 0)
            def _():
                for ref in o_refs[n_ro:]:
                    ref[...] = jnp.zeros_like(ref)
            for ref, val in zip(o_refs[n_ro:], outs[n_ro:]):
                ref[...] += jnp.broadcast_to(val, ref.shape).astype(f32)

    def full(shape):
        nd = len(shape)
        return pl.BlockSpec(tuple(shape), lambda i: (0,) * nd)

    in_specs = [pl.BlockSpec((tm, r.shape[1]), lambda i: (i, 0)) for r in rows] + [full(c.shape) for c in consts]
    out_specs = [pl.BlockSpec((tm, w), lambda i: (i, 0)) for (w, _) in row_outs] + [full(s) for s in acc_outs]
    out_shape = [SDS((S, w), dt) for (w, dt) in row_outs] + [SDS(tuple(s), f32) for s in acc_outs]
    res = pl.pallas_call(
        body, name=name, grid=(S // tm,), in_specs=in_specs, out_specs=out_specs, out_shape=out_shape,
        compiler_params=pltpu.CompilerParams(dimension_semantics=("arbitrary",)),
    )(*rows, *consts)
    return res


def _rowwise_bwd(f, rows, consts, cts, want, row_dtypes, *, adds=None, tm=256, name="rowwise_bwd"):
    n_r, n_c, n_ct = len(rows), len(consts), len(cts)
    adds = adds or {}
    add_keys = sorted(adds)
    add_rows = [adds[k] for k in add_keys]

    def fn(*args):
        r = [a.astype(f32) for a in args[:n_r]]
        ct = [a.astype(f32) for a in args[n_r:n_r + n_ct]]
        ad = args[n_r + n_ct:n_r + n_ct + len(add_rows)]
        c = list(args[n_r + n_ct + len(add_rows):])
        _, vjp = jax.vjp(f, *r, *c)
        g = vjp(ct[0] if n_ct == 1 else tuple(ct))
        d_rows = []
        for pos, i in enumerate(want):
            d = g[i]
            if pos in adds:
                d = d + ad[add_keys.index(pos)].astype(f32)
            d_rows.append(d)
        return (*d_rows, *g[n_r:])

    return _rowwise(fn, list(rows) + list(cts) + add_rows, list(consts),
                    [(rows[i].shape[1], dt) for i, dt in zip(want, row_dtypes)],
                    [c.shape for c in consts], tm=tm, name=name)


def _sum_lead(arr, out_dtype, name):
    n, R, C = arr.shape
    tm = _pick(R, (256, 128, 64, 32, 16, 8))

    def body(a_ref, o_ref):
        acc = a_ref[0].astype(f32)
        for q in range(1, n):
            acc = acc + a_ref[q].astype(f32)
        o_ref[...] = acc.astype(o_ref.dtype)

    return pl.pallas_call(
        body, name=name, grid=(R // tm,), in_specs=[pl.BlockSpec((n, tm, C), lambda i: (0, i, 0))],
        out_specs=pl.BlockSpec((tm, C), lambda i: (i, 0)), out_shape=SDS((R, C), out_dtype),
    )(arr)


_GROUPS = {
    "all": [(kx, ky, kc) for kx in (0, 1) for ky in (0, 1) for kc in (0, 1)][1:],
    "xy": [(0, 1, 0), (1, 0, 0), (1, 1, 0)],
    "c": [(0, 0, 1)],
}


XCHG_CHUNK_MIN_BYTES = 1 << 20
XCHG_CHUNKS = 16


def _group_pos(group, x, y, c):
    return {"all": 4 * x + 2 * y + c, "xy": 2 * x + y, "c": c}[group]


def _xchg(send, group, bcast, name):
    flips = _GROUPS[group]
    n = len(flips) + 1
    piece = tuple(send.shape if bcast else send.shape[1:])
    if not bcast:
        assert send.shape[0] == n
    lead = piece[0]
    nch = 1
    if math.prod(piece) * send.dtype.itemsize >= XCHG_CHUNK_MIN_BYTES:
        nch = lead if len(piece) > 2 else (XCHG_CHUNKS if lead % (XCHG_CHUNKS * 2 * BF16_ROWS) == 0 else 1)
    rows = lead // nch

    def body(s_ref, r_ref, send_sems, recv_sems, own_sems):
        x, y, c = lax.axis_index("x"), lax.axis_index("y"), lax.axis_index("c")
        me = _group_pos(group, x, y, c)

        def src(p, ch):
            ref = s_ref if bcast else s_ref.at[p]
            return ref.at[pl.ds(ch * rows, rows)]

        def dst(ch):
            return r_ref.at[me].at[pl.ds(ch * rows, rows)]

        owns = [pltpu.make_async_copy(src(me, ch), dst(ch), own_sems.at[ch]) for ch in range(nch)]
        for cp in owns:
            cp.start()
        copies = []
        for ch in range(nch):
            for k, (kx, ky, kc) in enumerate(flips):
                tx, ty, tc = x ^ kx, y ^ ky, c ^ kc
                cp = pltpu.make_async_remote_copy(
                    src_ref=src(_group_pos(group, tx, ty, tc), ch), dst_ref=dst(ch),
                    send_sem=send_sems.at[k, ch], recv_sem=recv_sems.at[k, ch],
                    device_id=(tx, ty, tc), device_id_type=pl.DeviceIdType.MESH)
                cp.start()
                copies.append(cp)
        for cp in copies + owns:
            cp.wait()

    return pl.pallas_call(
        body, name=name, out_shape=SDS((n,) + piece, send.dtype),
        in_specs=[pl.BlockSpec(memory_space=pl.ANY)], out_specs=pl.BlockSpec(memory_space=pl.ANY),
        scratch_shapes=[pltpu.SemaphoreType.DMA((n - 1, nch)), pltpu.SemaphoreType.DMA((n - 1, nch)), pltpu.SemaphoreType.DMA((nch,))],
        compiler_params=pltpu.CompilerParams(has_side_effects=True),
    )(send)


PAIR_ROWS = 256


def _pair(arr, mode, out_dtype, name):
    P, N, C = arr.shape
    tm = _largest_divisor(N, PAIR_ROWS, BF16_ROWS)
    nb = N // tm
    assert N % tm == 0 and tm <= PAIR_ROWS
    add = mode == "add"
    assert not add or P == 2

    def body(core_ref, *refs):
        if add:
            keep_ref, send_ref, o_ref, slots, send_sems, recv_sems, credit = refs
        else:
            a_ref, o_ref, stage, slots, send_sems, recv_sems, credit = refs
        i = pl.program_id(0)
        x, y, c = lax.axis_index("x"), lax.axis_index("y"), lax.axis_index("c")
        sib = (x, y, 1 - c)
        slot = lax.rem(i, 2)
        if add:
            src = send_ref.at[0]
        else:
            t = a_ref[0].astype(f32)
            for q in range(1, P):
                t = t + a_ref[q].astype(f32)
            stage[...] = t.astype(stage.dtype)
            o_ref[c] = stage[...]
            src = stage

        @pl.when(i >= 2)
        def _():
            pl.semaphore_wait(credit.at[slot], 1)

        cp = pltpu.make_async_remote_copy(src_ref=src, dst_ref=slots.at[slot], send_sem=send_sems.at[slot], recv_sem=recv_sems.at[slot],
                                          device_id=sib, device_id_type=pl.DeviceIdType.MESH)
        cp.start()
        cp.wait_recv()
        if add:
            o_ref[...] = (keep_ref[0].astype(f32) + slots[slot].astype(f32)).astype(o_ref.dtype)
        else:
            o_ref[1 - c] = slots[slot]
        cp.wait_send()

        @pl.when(i + 2 < nb)
        def _():
            pl.semaphore_signal(credit.at[slot], 1, device_id=sib, device_id_type=pl.DeviceIdType.MESH)

    core = lax.axis_index("c").astype(jnp.int32).reshape(1)
    scratch = [pltpu.VMEM((2, tm, C), arr.dtype if add else out_dtype), pltpu.SemaphoreType.DMA((2,)), pltpu.SemaphoreType.DMA((2,)),
               pltpu.SemaphoreType.REGULAR((2,))]
    if add:
        in_specs = [pl.BlockSpec((1, tm, C), lambda i, core: (core[0], i, 0)), pl.BlockSpec((1, tm, C), lambda i, core: (1 - core[0], i, 0))]
        out_specs = pl.BlockSpec((tm, C), lambda i, core: (i, 0))
        out_shape = SDS((N, C), out_dtype)
        args = (arr, arr)
    else:
        in_specs = [pl.BlockSpec((P, tm, C), lambda i, core: (0, i, 0))]
        out_specs = pl.BlockSpec((2, tm, C), lambda i, core: (0, i, 0))
        out_shape = SDS((2, N, C), out_dtype)
        scratch = [pltpu.VMEM((tm, C), out_dtype)] + scratch
        args = (arr,)
    return pl.pallas_call(
        body, name=name, out_shape=out_shape,
        grid_spec=pltpu.PrefetchScalarGridSpec(num_scalar_prefetch=1, grid=(nb,), in_specs=in_specs, out_specs=out_specs, scratch_shapes=scratch),
        compiler_params=pltpu.CompilerParams(dimension_semantics=("arbitrary",), has_side_effects=True),
    )(core, *args)


def _rms(x, g):
    return x * lax.rsqrt(jnp.mean(x * x, axis=-1, keepdims=True) + EPS) * g


def _normmod(x, g, shift, scale):
    return _rms(x, g) * (1.0 + scale) + shift


def _lane(shape):
    return lax.broadcasted_iota(jnp.int32, shape, 1)


def _partner(x):
    lane = _lane(x.shape)
    lo = (lane >= NOPE) & (lane < NOPE + ROPE // 2)
    hi = (lane >= NOPE + ROPE // 2) & (lane < NOPE + ROPE)
    return jnp.where(lo, pltpu.roll(x, HEAD_PAD - ROPE // 2, 1), jnp.where(hi, pltpu.roll(x, ROPE // 2, 1), 0.0))


@jax.custom_vjp
def _rope(x, cf, ss):
    return x * cf + _partner(x) * ss


def _rope_fwd(x, cf, ss):
    return _rope(x, cf, ss), (cf, ss)


def _rope_bwd(res, dy):
    cf, ss = res
    return dy * cf + _partner(dy * ss), jnp.zeros_like(cf), jnp.zeros_like(ss)


_rope.defvjp(_rope_fwd, _rope_bwd)


def _head_norm(xh, gn, gr):
    lane = _lane(xh.shape)
    x2 = xh * xh
    ms_n = jnp.sum(jnp.where(lane < NOPE, x2, 0.0), axis=-1, keepdims=True) * (1.0 / NOPE)
    ms_r = jnp.sum(jnp.where((lane >= NOPE) & (lane < NOPE + ROPE), x2, 0.0), axis=-1, keepdims=True) * (1.0 / ROPE)
    return xh * (lax.rsqrt(ms_n + EPS) * gn + lax.rsqrt(ms_r + EPS) * gr)


def _q_heads(qraw, cf, ss, gn, gr):
    outs = []
    for h in range(N_HEADS):
        xh = qraw[:, h * HEAD_PAD:(h + 1) * HEAD_PAD]
        outs.append(_rope(_head_norm(xh, gn, gr), cf, ss))
    return jnp.concatenate(outs, axis=1)


def _k_heads(kraw, kr, gn):
    outs = []
    zero = jnp.zeros_like(gn)
    for h in range(N_HEADS):
        xh = kraw[:, h * HEAD_PAD:(h + 1) * HEAD_PAD]
        outs.append(_head_norm(xh, gn, zero) + kr)
    return jnp.concatenate(outs, axis=1)


def _kv_a_post(kva, cf, ss, ga, gr):
    ckv = _rms(kva[:, :KV_RANK], ga)
    kr = _rope(_head_norm(kva[:, KV_RANK:], jnp.zeros_like(gr), gr), cf, ss)
    return ckv, kr


def _cmul(ar, ai, br, bi):
    return ar * br - ai * bi, ar * bi + ai * br


def _s5_scan_fwd(u, wbr, wbi, ar, ai, init_r, init_i, wcr, wci, *, want_y, tb, name):
    S, D = u.shape
    ns = D // SLAB_CH
    seg = S // SEGMENTS
    tb = min(tb, seg)
    nb = seg // tb
    R = tb * SEGMENTS
    nt = (((1,), (0,)), ((), ()))

    def body(u_ref, wbr_ref, wbi_ref, ar_ref, ai_ref, ir_ref, ii_ref, wcr_ref, wci_ref, *rest):
        if want_y:
            y_ref, er_ref, ei_ref, bsr_ref, bsi_ref, cr, ci, bur, bui, sr, si = rest
        else:
            er_ref, ei_ref, bsr_ref, bsi_ref, cr, ci, bur, bui, sr, si = rest
        t = pl.program_id(1)

        @pl.when(t == 0)
        def _():
            cr[...] = ir_ref[0]
            ci[...] = ii_ref[0]

        bsr_ref[0, 0] = cr[...]
        bsi_ref[0, 0] = ci[...]
        ub = u_ref[...].astype(bf16)
        bur[...] = lax.dot_general(ub, wbr_ref[0].astype(bf16), nt, preferred_element_type=f32)
        bui[...] = lax.dot_general(ub, wbi_ref[0].astype(bf16), nt, preferred_element_type=f32)
        a_r = jnp.broadcast_to(ar_ref[0], (SEGMENTS, SLAB_ST))
        a_i = jnp.broadcast_to(ai_ref[0], (SEGMENTS, SLAB_ST))

        def step(tau, carry):
            c_r, c_i = carry
            r = pl.multiple_of(tau * SEGMENTS, SEGMENTS)
            p_r, p_i = _cmul(a_r, a_i, c_r, c_i)
            n_r = p_r + bur[pl.ds(r, SEGMENTS), :]
            n_i = p_i + bui[pl.ds(r, SEGMENTS), :]
            sr[pl.ds(r, SEGMENTS), :] = n_r
            si[pl.ds(r, SEGMENTS), :] = n_i
            return n_r, n_i

        c_r, c_i = lax.fori_loop(0, tb, step, (cr[...], ci[...]))
        cr[...] = c_r
        ci[...] = c_i
        if want_y:
            y_ref[...] = (lax.dot_general(sr[...].astype(bf16), wcr_ref[0].astype(bf16), nt, preferred_element_type=f32)
                          - lax.dot_general(si[...].astype(bf16), wci_ref[0].astype(bf16), nt, preferred_element_type=f32))

        @pl.when(t == nb - 1)
        def _():
            er_ref[0] = c_r
            ei_ref[0] = c_i

    slab3 = lambda s: pl.BlockSpec((1,) + s, lambda k, t: (k, 0, 0))
    in_specs = [pl.BlockSpec((R, SLAB_CH), lambda k, t: (t, k)),
                slab3((SLAB_CH, SLAB_ST)), slab3((SLAB_CH, SLAB_ST)), slab3((1, SLAB_ST)), slab3((1, SLAB_ST)),
                slab3((SEGMENTS, SLAB_ST)), slab3((SEGMENTS, SLAB_ST)), slab3((SLAB_ST, SLAB_CH)), slab3((SLAB_ST, SLAB_CH))]
    out_specs = [slab3((SEGMENTS, SLAB_ST)), slab3((SEGMENTS, SLAB_ST)),
                 pl.BlockSpec((1, 1, SEGMENTS, SLAB_ST), lambda k, t: (t, k, 0, 0)),
                 pl.BlockSpec((1, 1, SEGMENTS, SLAB_ST), lambda k, t: (t, k, 0, 0))]
    out_shape = [SDS((ns, SEGMENTS, SLAB_ST), f32)] * 2 + [SDS((nb, ns, SEGMENTS, SLAB_ST), f32)] * 2
    if want_y:
        out_specs = [pl.BlockSpec((R, SLAB_CH), lambda k, t: (t, k))] + out_specs
        out_shape = [SDS((S, D), f32)] + out_shape
    return pl.pallas_call(
        body, name=name, grid=(ns, nb), in_specs=in_specs, out_specs=out_specs, out_shape=out_shape,
        scratch_shapes=[pltpu.VMEM((SEGMENTS, SLAB_ST), f32)] * 2 + [pltpu.VMEM((R, SLAB_ST), f32)] * 4,
        compiler_params=pltpu.CompilerParams(dimension_semantics=("parallel", "arbitrary")),
    )(u, wbr, wbi, ar, ai, init_r, init_i, wcr, wci)


def _s5_scan_bwd(u, dy, wbr, wbi, ar, ai, bs_r, bs_i, ginit_r, ginit_i, wcr, wci, *, full, tb, name):
    S, D = dy.shape
    ns = D // SLAB_CH
    seg = S // SEGMENTS
    tb = min(tb, seg)
    nb = seg // tb
    R = tb * SEGMENTS
    nn = (((1,), (0,)), ((), ()))
    nt = (((1,), (1,)), ((), ()))
    tn = (((0,), (0,)), ((), ()))

    def body(*refs):
        if full:
            (u_ref, dy_ref, wbr_ref, wbi_ref, ar_ref, ai_ref, bsr_ref, bsi_ref, gir_ref, gii_ref, wcr_ref, wci_ref,
             du_ref, dwbr_ref, dwbi_ref, dwcr_ref, dwci_ref, dar_ref, dai_ref, gfr_ref, gfi_ref,
             gr_c, gi_c, bur, bui, sr, si, gsr, gsi) = refs
        else:
            (dy_ref, ar_ref, ai_ref, gir_ref, gii_ref, wcr_ref, wci_ref, gfr_ref, gfi_ref, gr_c, gi_c, gsr, gsi) = refs
        t = pl.program_id(1)

        @pl.when(t == 0)
        def _():
            gr_c[...] = gir_ref[0]
            gi_c[...] = gii_ref[0]
            if full:
                for ref in (dwbr_ref, dwbi_ref, dwcr_ref, dwci_ref, dar_ref, dai_ref):
                    ref[...] = jnp.zeros_like(ref)

        a_r = jnp.broadcast_to(ar_ref[0], (SEGMENTS, SLAB_ST))
        a_i = jnp.broadcast_to(ai_ref[0], (SEGMENTS, SLAB_ST))
        dyb = dy_ref[...].astype(bf16)
        gsr[...] = lax.dot_general(dyb, wcr_ref[0].astype(bf16), nt, preferred_element_type=f32)
        gsi[...] = -lax.dot_general(dyb, wci_ref[0].astype(bf16), nt, preferred_element_type=f32)

        if full:
            ub = u_ref[...].astype(bf16)
            bur[...] = lax.dot_general(ub, wbr_ref[0].astype(bf16), nn, preferred_element_type=f32)
            bui[...] = lax.dot_general(ub, wbi_ref[0].astype(bf16), nn, preferred_element_type=f32)

            def fstep(tau, carry):
                c_r, c_i = carry
                r = pl.multiple_of(tau * SEGMENTS, SEGMENTS)
                p_r, p_i = _cmul(a_r, a_i, c_r, c_i)
                n_r = p_r + bur[pl.ds(r, SEGMENTS), :]
                n_i = p_i + bui[pl.ds(r, SEGMENTS), :]
                sr[pl.ds(r, SEGMENTS), :] = n_r
                si[pl.ds(r, SEGMENTS), :] = n_i
                return n_r, n_i

            lax.fori_loop(0, tb, fstep, (bsr_ref[0, 0], bsi_ref[0, 0]))

        def adj(g_r, g_i):
            return a_r * g_r + a_i * g_i, a_r * g_i - a_i * g_r

        def rstep(k, carry):
            tau = tb - 1 - k
            r = pl.multiple_of(tau * SEGMENTS, SEGMENTS)
            if full:
                g_r, g_i, acc_r, acc_i = carry
            else:
                g_r, g_i = carry
            b_r, b_i = adj(g_r, g_i)
            n_r = b_r + gsr[pl.ds(r, SEGMENTS), :]
            n_i = b_i + gsi[pl.ds(r, SEGMENTS), :]
            gsr[pl.ds(r, SEGMENTS), :] = n_r
            gsi[pl.ds(r, SEGMENTS), :] = n_i
            if not full:
                return n_r, n_i
            rp = pl.multiple_of(jnp.maximum(tau - 1, 0) * SEGMENTS, SEGMENTS)
            first = tau == 0
            p_r = jnp.where(first, bsr_ref[0, 0], sr[pl.ds(rp, SEGMENTS), :])
            p_i = jnp.where(first, bsi_ref[0, 0], si[pl.ds(rp, SEGMENTS), :])
            return n_r, n_i, acc_r + n_r * p_r + n_i * p_i, acc_i + n_i * p_r - n_r * p_i

        zero = jnp.zeros((SEGMENTS, SLAB_ST), f32)
        if full:
            g_r, g_i, acc_r, acc_i = lax.fori_loop(0, tb, rstep, (gr_c[...], gi_c[...], zero, zero))
            dar_ref[0] += acc_r
            dai_ref[0] += acc_i
        else:
            g_r, g_i = lax.fori_loop(0, tb, rstep, (gr_c[...], gi_c[...]))
        gr_c[...] = g_r
        gi_c[...] = g_i

        if full:
            gbr = gsr[...].astype(bf16)
            gbi = gsi[...].astype(bf16)
            du_ref[...] = (lax.dot_general(gbr, wbr_ref[0].astype(bf16), nt, preferred_element_type=f32)
                           + lax.dot_general(gbi, wbi_ref[0].astype(bf16), nt, preferred_element_type=f32))
            dwbr_ref[0] += lax.dot_general(ub, gbr, tn, preferred_element_type=f32)
            dwbi_ref[0] += lax.dot_general(ub, gbi, tn, preferred_element_type=f32)
            dwcr_ref[0] += lax.dot_general(sr[...].astype(bf16), dyb, tn, preferred_element_type=f32)
            dwci_ref[0] -= lax.dot_general(si[...].astype(bf16), dyb, tn, preferred_element_type=f32)

        @pl.when(t == nb - 1)
        def _():
            gfr_ref[0] = g_r
            gfi_ref[0] = g_i

    slab3 = lambda s: pl.BlockSpec((1,) + s, lambda k, t: (k, 0, 0))
    rev_rows = pl.BlockSpec((R, SLAB_CH), lambda k, t: (nb - 1 - t, k))
    rev_bs = pl.BlockSpec((1, 1, SEGMENTS, SLAB_ST), lambda k, t: (nb - 1 - t, k, 0, 0))
    st = slab3((SEGMENTS, SLAB_ST))
    state_sds = SDS((ns, SEGMENTS, SLAB_ST), f32)
    if full:
        in_specs = [rev_rows, rev_rows, slab3((SLAB_CH, SLAB_ST)), slab3((SLAB_CH, SLAB_ST)), slab3((1, SLAB_ST)), slab3((1, SLAB_ST)),
                    rev_bs, rev_bs, st, st, slab3((SLAB_ST, SLAB_CH)), slab3((SLAB_ST, SLAB_CH))]
        args = (u, dy, wbr, wbi, ar, ai, bs_r, bs_i, ginit_r, ginit_i, wcr, wci)
        out_specs = [rev_rows, slab3((SLAB_CH, SLAB_ST)), slab3((SLAB_CH, SLAB_ST)), slab3((SLAB_ST, SLAB_CH)), slab3((SLAB_ST, SLAB_CH)),
                     st, st, st, st]
        out_shape = [SDS((S, D), f32), SDS((ns, SLAB_CH, SLAB_ST), f32), SDS((ns, SLAB_CH, SLAB_ST), f32),
                     SDS((ns, SLAB_ST, SLAB_CH), f32), SDS((ns, SLAB_ST, SLAB_CH), f32)] + [state_sds] * 4
        scratch = [pltpu.VMEM((SEGMENTS, SLAB_ST), f32)] * 2 + [pltpu.VMEM((R, SLAB_ST), f32)] * 6
    else:
        in_specs = [rev_rows, slab3((1, SLAB_ST)), slab3((1, SLAB_ST)), st, st, slab3((SLAB_ST, SLAB_CH)), slab3((SLAB_ST, SLAB_CH))]
        args = (dy, ar, ai, ginit_r, ginit_i, wcr, wci)
        out_specs = [st, st]
        out_shape = [state_sds] * 2
        scratch = [pltpu.VMEM((SEGMENTS, SLAB_ST), f32)] * 2 + [pltpu.VMEM((R, SLAB_ST), f32)] * 2
    return pl.pallas_call(
        body, name=name, grid=(ns, nb), in_specs=in_specs, out_specs=out_specs, out_shape=out_shape, scratch_shapes=scratch,
        compiler_params=pltpu.CompilerParams(dimension_semantics=("parallel", "arbitrary")),
    )(*args)


def _s5_chain(e_r, e_i, ar, ai, *, seg, reverse, name):
    ns = e_r.shape[0]
    assert seg & (seg - 1) == 0

    def body(er_ref, ei_ref, ar_ref, ai_ref, or_ref, oi_ref):
        p_r, p_i = ar_ref[0], ai_ref[0]
        if reverse:
            p_i = -p_i
        for _ in range(seg.bit_length() - 1):
            p_r, p_i = _cmul(p_r, p_i, p_r, p_i)
        c_r = jnp.zeros((1, SLAB_ST), f32)
        c_i = jnp.zeros((1, SLAB_ST), f32)
        order = range(SEGMENTS - 1, -1, -1) if reverse else range(SEGMENTS)
        for j in order:
            or_ref[0, pl.ds(j, 1), :] = c_r
            oi_ref[0, pl.ds(j, 1), :] = c_i
            m_r, m_i = _cmul(p_r, p_i, c_r, c_i)
            c_r = er_ref[0, pl.ds(j, 1), :] + m_r
            c_i = ei_ref[0, pl.ds(j, 1), :] + m_i

    st = pl.BlockSpec((1, SEGMENTS, SLAB_ST), lambda k: (k, 0, 0))
    av = pl.BlockSpec((1, 1, SLAB_ST), lambda k: (k, 0, 0))
    return pl.pallas_call(
        body, name=name, grid=(ns,), in_specs=[st, st, av, av], out_specs=[st, st],
        out_shape=[SDS(e_r.shape, f32)] * 2,
    )(e_r, e_i, ar, ai)


def _s5_disc(lr, li, log_dt):
    dt = jnp.exp(log_dt)
    mag = jnp.exp(lr * dt)
    ab_re = mag * jnp.cos(li * dt)
    ab_im = mag * jnp.sin(li * dt)
    den = lr * lr + li * li
    nr = ab_re - 1.0
    ni = ab_im
    return ab_re, ab_im, (nr * lr + ni * li) / den, (ni * lr - nr * li) / den


def _s5_bbar(b_r, b_i, fr, fi):
    return fr * b_r - fi * b_i, fr * b_i + fi * b_r


def _blockdiag(w, rows, cols):
    g = w.shape[0]
    w = w.reshape(g // 8, 8, rows, cols)
    eye = jnp.eye(8, dtype=w.dtype)
    full = w[:, :, :, None, :] * eye[None, :, None, :, None]
    return full.reshape(g // 8, 8 * rows, 8 * cols)


def _blockdiag_take(w, rows, cols):
    ns = w.shape[0]
    w = w.reshape(ns, 8, rows, 8, cols)
    idx = jnp.arange(8)
    d = w[:, idx, :, idx, :]
    return jnp.moveaxis(d, 0, 1).reshape(ns * 8, rows, cols)


def _mask(i, j, tq, tk):
    row = i * tq + lax.broadcasted_iota(jnp.int32, (tq, tk), 0)
    col = j * tk + lax.broadcasted_iota(jnp.int32, (tq, tk), 1)
    return (row // CHUNK) >= (col // CHUNK)


_NT = (((1,), (1,)), ((), ()))
_NN = (((1,), (0,)), ((), ()))
_TN = (((0,), (0,)), ((), ()))
_NEG = -1e30
ATTN_HEADS_PER_STEP = 2


def _attn_fwd(q, k, v, *, t, name):
    S = q.shape[0]
    t = min(t, S)
    nq = S // t

    hp = ATTN_HEADS_PER_STEP

    def body(q_ref, k_ref, v_ref, o_ref, lse_ref, m_s, l_s, acc):
        i = pl.program_id(1)
        m_s[...] = jnp.full_like(m_s, _NEG)
        l_s[...] = jnp.zeros_like(l_s)
        acc[...] = jnp.zeros_like(acc)

        def block(j, masked):
            r = pl.multiple_of(j * t, t)
            for hh in range(hp):
                lanes = slice(hh * HEAD_PAD, (hh + 1) * HEAD_PAD)
                s = lax.dot_general(q_ref[:, lanes], k_ref[pl.ds(r, t), lanes], _NT, preferred_element_type=f32) * ATTN_SCALE
                if masked:
                    s = jnp.where(_mask(0, 0, t, t), s, _NEG)
                m_old = m_s[hh]
                m_new = jnp.maximum(m_old, jnp.max(s, axis=-1, keepdims=True))
                alpha = jnp.exp(m_old - m_new)
                p = jnp.exp(s - m_new)
                l_s[hh] = alpha * l_s[hh] + jnp.sum(p, axis=-1, keepdims=True)
                acc[hh] = alpha * acc[hh] + lax.dot_general(p.astype(bf16), v_ref[pl.ds(r, t), lanes], _NN, preferred_element_type=f32)
                m_s[hh] = m_new

        def below(j, carry):
            block(j, False)
            return carry

        lax.fori_loop(0, i, below, 0)
        block(i, True)
        for hh in range(hp):
            o_ref[:, hh * HEAD_PAD:(hh + 1) * HEAD_PAD] = acc[hh] / l_s[hh]
            lse_ref[hh] = m_s[hh] + jnp.log(l_s[hh])

    qs = pl.BlockSpec((t, hp * HEAD_PAD), lambda h, i: (i, h))
    ks = pl.BlockSpec((S, hp * HEAD_PAD), lambda h, i: (0, h))
    return pl.pallas_call(
        body, name=name, grid=(N_HEADS // hp, nq), in_specs=[qs, ks, ks],
        out_specs=[qs, pl.BlockSpec((hp, t, 1), lambda h, i: (h, i, 0))],
        out_shape=[SDS((S, N_HEADS * HEAD_PAD), f32), SDS((N_HEADS, S, 1), f32)],
        scratch_shapes=[pltpu.VMEM((hp, t, 1), f32), pltpu.VMEM((hp, t, 1), f32), pltpu.VMEM((hp, t, HEAD_PAD), f32)],
        compiler_params=pltpu.CompilerParams(dimension_semantics=("parallel", "arbitrary")),
    )(q, k, v)


def _attn_bwd(q, k, v, do, o, lse, *, t, name):
    S = q.shape[0]
    t = min(t, S)
    nq = S // t

    def body(q_ref, k_ref, v_ref, do_ref, o_ref, lse_ref, dq_ref, dk_ref, dv_ref, d_s, dk_acc, dv_acc):
        j = pl.program_id(1)

        @pl.when(j == 0)
        def _():
            dq_ref[...] = jnp.zeros_like(dq_ref)
            d_s[...] = jnp.sum(do_ref[...].astype(f32) * o_ref[...], axis=-1, keepdims=True)

        dk_acc[...] = jnp.zeros_like(dk_acc)
        dv_acc[...] = jnp.zeros_like(dv_acc)
        kb = k_ref[...]
        vb = v_ref[...]

        def block(i, masked):
            r = pl.multiple_of(i * t, t)
            qb = q_ref[pl.ds(r, t), :]
            dob = do_ref[pl.ds(r, t), :]
            s = lax.dot_general(qb, kb, _NT, preferred_element_type=f32) * ATTN_SCALE
            if masked:
                s = jnp.where(_mask(0, 0, t, t), s, _NEG)
            p = jnp.exp(s - lse_ref[0, pl.ds(r, t), :])
            dp = lax.dot_general(dob, vb, _NT, preferred_element_type=f32)
            ds = (p * (dp - d_s[pl.ds(r, t), :]) * ATTN_SCALE).astype(bf16)
            dv_acc[...] += lax.dot_general(p.astype(bf16), dob, _TN, preferred_element_type=f32)
            dk_acc[...] += lax.dot_general(ds, qb, _TN, preferred_element_type=f32)
            dq_ref[pl.ds(r, t), :] += lax.dot_general(ds, kb, _NN, preferred_element_type=f32)

        def below(i, carry):
            block(i, False)
            return carry

        block(j, True)
        lax.fori_loop(j + 1, nq, below, 0)
        dk_ref[...] = dk_acc[...]
        dv_ref[...] = dv_acc[...]

    hs = pl.BlockSpec((S, HEAD_PAD), lambda h, j: (0, h))
    ks = pl.BlockSpec((t, HEAD_PAD), lambda h, j: (j, h))
    ls = pl.BlockSpec((1, S, 1), lambda h, j: (h, 0, 0))
    return pl.pallas_call(
        body, name=name, grid=(N_HEADS, nq), in_specs=[hs, ks, ks, hs, hs, ls], out_specs=[hs, ks, ks],
        out_shape=[SDS((S, N_HEADS * HEAD_PAD), f32)] * 3,
        scratch_shapes=[pltpu.VMEM((S, 1), f32), pltpu.VMEM((t, HEAD_PAD), f32), pltpu.VMEM((t, HEAD_PAD), f32)],
        compiler_params=pltpu.CompilerParams(dimension_semantics=("parallel", "arbitrary")),
    )(q, k, v, do, o, lse)


def _pad_heads(w, per_head, axis):
    w = jnp.moveaxis(w, axis, -1)
    lead = w.shape[:-1]
    w = w.reshape(lead + (N_HEADS, per_head))
    w = jnp.pad(w, [(0, 0)] * len(lead) + [(0, 0), (0, HEAD_PAD - per_head)])
    return jnp.moveaxis(w.reshape(lead + (N_HEADS * HEAD_PAD,)), -1, axis)


def _unpad_heads(w, per_head, axis):
    w = jnp.moveaxis(w, axis, -1)
    lead = w.shape[:-1]
    w = w.reshape(lead + (N_HEADS, HEAD_PAD))[..., :per_head]
    return jnp.moveaxis(w.reshape(lead + (N_HEADS * per_head,)), -1, axis)


def _lanes128(vec, offset):
    return jnp.zeros((1, HEAD_PAD), f32).at[0, offset:offset + vec.shape[0]].set(vec)


def _to_segments(a):
    S, D = a.shape
    return a.reshape(SEGMENTS, S // SEGMENTS, D).transpose(1, 0, 2).reshape(S, D)


def _from_segments(a):
    S, D = a.shape
    return a.reshape(S // SEGMENTS, SEGMENTS, D).transpose(1, 0, 2).reshape(S, D)


def _s5_params_fwd(p):
    G, N, P = p["b_re"].shape
    def disc_body(lr, li, ld, o1, o2, o3, o4):
        o1[...], o2[...], o3[...], o4[...] = _s5_disc(lr[...], li[...], ld[...])

    ab_r, ab_i, f_r, f_i = pl.pallas_call(disc_body, name="s5_disc", out_shape=[SDS((G, N), f32)] * 4)(
        p["lam_re"], p["lam_im"], p["log_dt"].reshape(G, 1))
    bb_r, bb_i = _rowwise(_s5_bbar, [p["b_re"].reshape(G * N, P), p["b_im"].reshape(G * N, P), f_r.reshape(G * N, 1), f_i.reshape(G * N, 1)],
                          [], [(P, f32), (P, f32)], tm=512, name="s5_bbar")
    ns = G // 8
    out = dict(
        f_r=f_r, f_i=f_i,
        a_r=ab_r.reshape(ns, 1, SLAB_ST), a_i=ab_i.reshape(ns, 1, SLAB_ST),
        wb_r=_blockdiag(bb_r.reshape(G, N, P).transpose(0, 2, 1), P, N), wb_i=_blockdiag(bb_i.reshape(G, N, P).transpose(0, 2, 1), P, N),
        wc_r=_blockdiag(p["c_re"].transpose(0, 2, 1), N, P), wc_i=_blockdiag(p["c_im"].transpose(0, 2, 1), N, P),
    )
    return out


def _s5_params_bwd(p, sp, d_ar, d_ai, d_wbr, d_wbi, d_wcr, d_wci):
    G, N, P = p["b_re"].shape
    dbb_r = _blockdiag_take(d_wbr, P, N).transpose(0, 2, 1).reshape(G * N, P)
    dbb_i = _blockdiag_take(d_wbi, P, N).transpose(0, 2, 1).reshape(G * N, P)
    d_cre = _blockdiag_take(d_wcr, N, P).transpose(0, 2, 1)
    d_cim = _blockdiag_take(d_wci, N, P).transpose(0, 2, 1)
    rows = [p["b_re"].reshape(G * N, P), p["b_im"].reshape(G * N, P), sp["f_r"].reshape(G * N, 1), sp["f_i"].reshape(G * N, 1)]
    d_br, d_bi, d_fr, d_fi = _rowwise_bwd(_s5_bbar, rows, [], [dbb_r, dbb_i], [0, 1, 2, 3], [f32] * 4, tm=512, name="s5_bbar_bwd")

    def seg_sum(d, name):
        return _sum_lead(d.transpose(1, 0, 2).reshape(SEGMENTS, G, N), f32, name)

    def body(lr, li, ld, c1, c2, c3, c4, o1, o2, o3):
        _, vjp = jax.vjp(_s5_disc, lr[...], li[...], ld[...])
        o1[...], o2[...], o3[...] = vjp((c1[...], c2[...], c3[...], c4[...]))

    d_lr, d_li, d_ld = pl.pallas_call(
        body, name="s5_disc_bwd", out_shape=[SDS((G, N), f32), SDS((G, N), f32), SDS((G, 1), f32)],
    )(p["lam_re"], p["lam_im"], p["log_dt"].reshape(G, 1), seg_sum(d_ar, "s5_da_re_sum"), seg_sum(d_ai, "s5_da_im_sum"),
      d_fr.reshape(G, N), d_fi.reshape(G, N))
    return dict(lam_re=d_lr, lam_im=d_li, log_dt=d_ld.reshape(G), b_re=d_br.reshape(G, N, P), b_im=d_bi.reshape(G, N, P),
                c_re=d_cre, c_im=d_cim)


_SCAN_TB = 32
_ATTN_T = 512


def _s5_mix_fwd(h, sp, name):
    S = h.shape[0]
    seg = S // SEGMENTS
    zeros = jnp.zeros((h.shape[1] // SLAB_CH, SEGMENTS, SLAB_ST), f32)
    common = (sp["wb_r"], sp["wb_i"], sp["a_r"], sp["a_i"])
    e_r, e_i, _, _ = _s5_scan_fwd(h, *common, zeros, zeros, sp["wc_r"], sp["wc_i"], want_y=False, tb=_SCAN_TB, name=name + "_local")
    i_r, i_i = _s5_chain(e_r, e_i, sp["a_r"], sp["a_i"], seg=seg, reverse=False, name=name + "_chain")
    y, _, _, bs_r, bs_i = _s5_scan_fwd(h, *common, i_r, i_i, sp["wc_r"], sp["wc_i"], want_y=True, tb=_SCAN_TB, name=name)
    return y, (bs_r, bs_i)


def _s5_mix_bwd(h, dy, sp, saved, name):
    S = h.shape[0]
    seg = S // SEGMENTS
    bs_r, bs_i = saved
    zeros = jnp.zeros((h.shape[1] // SLAB_CH, SEGMENTS, SLAB_ST), f32)
    gf_r, gf_i = _s5_scan_bwd(None, dy, None, None, sp["a_r"], sp["a_i"], None, None, zeros, zeros, sp["wc_r"], sp["wc_i"],
                              full=False, tb=_SCAN_TB, name=name + "_local")
    gi_r, gi_i = _s5_chain(gf_r, gf_i, sp["a_r"], sp["a_i"], seg=seg, reverse=True, name=name + "_chain")
    du, d_wbr, d_wbi, d_wcr, d_wci, d_ar, d_ai, _, _ = _s5_scan_bwd(
        h, dy, sp["wb_r"], sp["wb_i"], sp["a_r"], sp["a_i"], bs_r, bs_i, gi_r, gi_i, sp["wc_r"], sp["wc_i"],
        full=True, tb=_SCAN_TB, name=name)
    return du, (d_ar, d_ai, d_wbr, d_wbi, d_wcr, d_wci)


def _ffn_fwd(x1, mod, n2g, w_gu, w_down, tag):
    F = w_down.shape[0]
    h2 = _rowwise(lambda x, g, m: _normmod(x, g, m[3:4], m[4:5]), [x1], [n2g, mod], [(x1.shape[1], bf16)], name=tag + "_norm2")[0]
    ab = _mm(h2, w_gu, name=tag + "_gu")
    act = _rowwise(lambda ab: jax.nn.silu(ab[:, :F]) * ab[:, F:], [ab], [], [(F, bf16)], name=tag + "_act")[0]
    f = _mm(act, w_down, name=tag + "_down")
    x2 = _rowwise(lambda x, f, m: x + m[5:6] * f, [x1, f], [mod], [(x1.shape[1], f32)], name=tag + "_res2")[0]
    return x2, dict(x1=x1, h2=h2, ab=ab, act=act, f=f)


def _ffn_bwd(dx2, sv, mod, n2g, w_gu, w_down, tag):
    F = w_down.shape[0]
    D = dx2.shape[1]
    df, dgate2 = _rowwise(lambda dx, f, m: (m[5:6] * dx, jnp.sum(dx * f, axis=0, keepdims=True)), [dx2, sv["f"]], [mod],
                          [(D, bf16)], [(1, D)], name=tag + "_res2_bwd")
    dact = _mm(df, w_down, tb=True, name=tag + "_down_dx")
    dw_down = _mm(sv["act"], df, ta=True, name=tag + "_down_dw")
    dab = _rowwise_bwd(lambda ab: jax.nn.silu(ab[:, :F]) * ab[:, F:], [sv["ab"]], [], [dact], [0], [bf16], name=tag + "_act_bwd")[0]
    dh2 = _mm(dab, w_gu, tb=True, name=tag + "_gu_dx")
    dw_gu = _mm(sv["h2"], dab, ta=True, name=tag + "_gu_dw")
    dx1, dn2g, dmod = _rowwise_bwd(lambda x, g, m: _normmod(x, g, m[3:4], m[4:5]), [sv["x1"]], [n2g, mod], [dh2], [0], [f32],
                                   adds={0: dx2}, name=tag + "_norm2_bwd")
    dmod = dmod.at[5:6].add(dgate2)
    return dx1, dw_gu, dw_down, dn2g, dmod


def _device_step(x, target, pos, mods, kmod, W):
    S, D = x.shape
    F = W["ffn_w_down"].shape[1]
    inv = 1.0 / (ROPE_THETA ** (jnp.arange(0, ROPE, 2, dtype=f32) / ROPE))
    inv128 = _lanes128(jnp.concatenate([inv, inv]), NOPE)
    sign128 = _lanes128(jnp.concatenate([-jnp.ones(ROPE // 2, f32), jnp.ones(ROPE // 2, f32)]), NOPE)
    cf, ss = _rowwise(lambda p, iv, sg: (jnp.cos(p * iv), jnp.sin(p * iv) * sg), [pos], [inv128, sign128],
                      [(HEAD_PAD, f32), (HEAD_PAD, f32)], name="rope_table")

    w_gu = W.get("ffn_w_gu") or [jnp.concatenate([W["ffn_w_gate"][l], W["ffn_w_up"][l]], axis=1) for l in range(DEPTH)]
    g_gu = [None] * DEPTH
    saved = []
    xs = _to_segments(x)

    s5p = []
    for l in range(N_A):
        tag = f"l{l}"
        mod = mods[l]
        p = {k: W["s5_" + k][l] for k in ("lam_re", "lam_im", "log_dt", "b_re", "b_im", "c_re", "c_im")}
        sp = _s5_params_fwd(p)
        s5p.append((p, sp))
        n1g = W["norm1_g"][l][None]
        dsk = W["s5_d"][l][None]
        bgl = W["s5_b_glu"][l][None]
        h = _rowwise(lambda x, g, m: _normmod(x, g, m[0:1], m[1:2]), [xs], [n1g, mod], [(D, f32)], name=tag + "_norm1")[0]
        y_ssm, scan_saved = _s5_mix_fwd(h, sp, tag + "_scan")
        g = _rowwise(lambda ys, h, d: jax.nn.gelu(ys + d * h), [y_ssm, h], [dsk], [(D, f32)], name=tag + "_gelu")[0]
        z = _mm(g, W["s5_w_glu"][l], name=tag + "_glu")
        x1 = _rowwise(lambda x, g, z, b, m: x + m[2:3] * (g * jax.nn.sigmoid(z + b)), [xs, g, z], [bgl, mod], [(D, f32)], name=tag + "_res1")[0]
        x2, fsv = _ffn_fwd(x1, mod, W["norm2_g"][l][None], w_gu[l], W["ffn_w_down"][l], tag)
        saved.append(dict(x0=xs, h=h, y_ssm=y_ssm, g=g, z=z, scan=scan_saved, ffn=fsv))
        xs = x2
    x = _from_segments(xs)

    w_kv_a = jnp.concatenate([W["w_kv_a"][:, :KV_RANK], jnp.zeros((D, NOPE), f32).astype(W["w_kv_a"].dtype), W["w_kv_a"][:, KV_RANK:],
                              jnp.zeros((D, HEAD_PAD - NOPE - ROPE), f32).astype(W["w_kv_a"].dtype)], axis=1)
    wkb = W["w_kv_b"].reshape(KV_RANK, N_HEADS, NOPE + VDIM)
    w_kb = _pad_heads(wkb[:, :, :NOPE].reshape(KV_RANK, N_HEADS * NOPE), NOPE, 1)
    w_vb = _pad_heads(wkb[:, :, NOPE:].reshape(KV_RANK, N_HEADS * VDIM), VDIM, 1)
    kvg = W["kv_norm_g"][None]
    ga = W["kv_a_norm_g"][None]
    gkr = _lanes128(W["k_rope_norm_g"], NOPE)
    gkn = _lanes128(W["k_nope_norm_g"], 0)
    hk = _rowwise(lambda x, g, m: _normmod(x, g, m[0:1], m[1:2]), [x], [kvg, kmod], [(D, bf16)], name="kv_norm")[0]
    kva = _mm(hk, w_kv_a, name="kv_a")
    ckv, kr = _rowwise(_kv_a_post, [kva, cf, ss], [ga, gkr], [(KV_RANK, f32), (HEAD_PAD, f32)], name="kv_a_post")
    kraw = _mm(ckv, w_kb, name="kv_bk")
    vpad = _mm(ckv, w_vb, out_dtype=bf16, name="kv_bv")
    kpad = _rowwise(_k_heads, [kraw, kr], [gkn], [(N_HEADS * HEAD_PAD, bf16)], name="k_heads")[0]
    kv_saved = dict(x=x, hk=hk, kva=kva, ckv=ckv, kr=kr, kraw=kraw)

    for l in range(N_A, DEPTH):
        tag = f"l{l}"
        j = l - N_A
        mod = mods[l]
        n1g = W["norm1_g"][l][None]
        w_uq = _pad_heads(W["mla_w_uq"][j], NOPE + ROPE, 1)
        w_o = _pad_heads(W["mla_w_o"][j], VDIM, 0)
        qg = W["mla_q_norm_g"][j][None]
        gqn = _lanes128(W["mla_q_nope_norm_g"][j], 0)
        gqr = _lanes128(W["mla_q_rope_norm_g"][j], NOPE)
        h = _rowwise(lambda x, g, m: _normmod(x, g, m[0:1], m[1:2]), [x], [n1g, mod], [(D, bf16)], name=tag + "_norm1")[0]
        qa = _mm(h, W["mla_w_dq"][j], name=tag + "_dq")
        qc = _rowwise(_rms, [qa], [qg], [(qa.shape[1], f32)], name=tag + "_qnorm")[0]
        qraw = _mm(qc, w_uq, name=tag + "_uq")
        q = _rowwise(_q_heads, [qraw, cf, ss], [gqn, gqr], [(N_HEADS * HEAD_PAD, bf16)], name=tag + "_q_heads")[0]
        o, lse = _attn_fwd(q, kpad, vpad, t=_ATTN_T, name=tag + "_attn")
        mix = _mm(o, w_o, name=tag + "_wo")
        x1 = _rowwise(lambda x, mx, m: x + m[2:3] * mx, [x, mix], [mod], [(D, f32)], name=tag + "_res1")[0]
        x2, fsv = _ffn_fwd(x1, mod, W["norm2_g"][l][None], w_gu[l], W["ffn_w_down"][l], tag)
        saved.append(dict(x0=x, h=h, qa=qa, qc=qc, qraw=qraw, q=q, o=o, lse=lse, mix=mix, w_uq=w_uq, w_o=w_o, ffn=fsv))
        x = x2

    dx, loss = _rowwise(
        lambda y, t: ((y - t) * (1.0 / D), jnp.full((1, LANES), 0.5 * jnp.sum(jnp.mean(jnp.square(y - t), axis=-1)), f32)),
        [x, target], [], [(D, f32)], [(1, LANES)], name="loss")

    gW = {}
    dmods = [None] * DEPTH
    g_gate, g_up, g_down, g_n1, g_n2 = [None] * DEPTH, [None] * DEPTH, [None] * DEPTH, [None] * DEPTH, [None] * DEPTH
    dks, dvs = [], []
    g_dq, g_qn, g_uq, g_qnn, g_qrn, g_wo = [None] * 2, [None] * 2, [None] * 2, [None] * 2, [None] * 2, [None] * 2
    for l in range(DEPTH - 1, N_A - 1, -1):
        tag = f"l{l}"
        j = l - N_A
        sv = saved[l]
        mod = mods[l]
        dx1, dw_gu, dw_down, g_n2[l], dmod = _ffn_bwd(dx, sv["ffn"], mod, W["norm2_g"][l][None], w_gu[l], W["ffn_w_down"][l], tag)
        g_gate[l], g_up[l], g_down[l], g_gu[l] = dw_gu[:, :F], dw_gu[:, F:], dw_down, dw_gu
        dmix, dgate1 = _rowwise(lambda dx, mx, m: (m[2:3] * dx, jnp.sum(dx * mx, axis=0, keepdims=True)), [dx1, sv["mix"]], [mod],
                                [(D, bf16)], [(1, D)], name=tag + "_res1_bwd")
        do = _mm(dmix, sv["w_o"], tb=True, out_dtype=bf16, name=tag + "_wo_dx")
        g_wo[j] = _unpad_heads(_mm(sv["o"], dmix, ta=True, name=tag + "_wo_dw"), VDIM, 0)
        dq, dk, dv = _attn_bwd(sv["q"], kpad, vpad, do, sv["o"], sv["lse"], t=_ATTN_T, name=tag + "_attn_bwd")
        dks.append(dk)
        dvs.append(dv)
        gqn = _lanes128(W["mla_q_nope_norm_g"][j], 0)
        gqr = _lanes128(W["mla_q_rope_norm_g"][j], NOPE)
        dqraw, dgqn, dgqr = _rowwise_bwd(lambda qr, c, s, a, b: _q_heads(qr, c, s, a, b), [sv["qraw"], cf, ss], [gqn, gqr], [dq], [0], [bf16],
                                         name=tag + "_q_heads_bwd")
        g_qnn[j], g_qrn[j] = dgqn[0, :NOPE], dgqr[0, NOPE:NOPE + ROPE]
        dqc = _mm(dqraw, sv["w_uq"], tb=True, name=tag + "_uq_dx")
        g_uq[j] = _unpad_heads(_mm(sv["qc"], dqraw, ta=True, name=tag + "_uq_dw"), NOPE + ROPE, 1)
        qg = W["mla_q_norm_g"][j][None]
        dqa, dqg = _rowwise_bwd(_rms, [sv["qa"]], [qg], [dqc], [0], [bf16], name=tag + "_qnorm_bwd")
        g_qn[j] = dqg[0]
        dh = _mm(dqa, W["mla_w_dq"][j], tb=True, name=tag + "_dq_dx")
        g_dq[j] = _mm(sv["h"], dqa, ta=True, name=tag + "_dq_dw")
        dx, dn1g, dmod1 = _rowwise_bwd(lambda x, g, m: _normmod(x, g, m[0:1], m[1:2]), [sv["x0"]], [W["norm1_g"][l][None], mod], [dh], [0], [f32],
                                       adds={0: dx1}, name=tag + "_norm1_bwd")
        g_n1[l] = dn1g[0]
        dmods[l] = (dmod + dmod1).at[2:3].add(dgate1)

    gkn = _lanes128(W["k_nope_norm_g"], 0)
    dk_sum = _sum_lead(jnp.stack(dks), f32, "dk_sum")
    dv_sum = _sum_lead(jnp.stack(dvs), bf16, "dv_sum")
    dkraw, dkr, dgkn = _rowwise_bwd(lambda kr_, r, g: _k_heads(kr_, r, g), [kv_saved["kraw"], kv_saved["kr"]], [gkn], [dk_sum], [0, 1], [bf16, f32],
                                    name="k_heads_bwd")
    dckv = _sum_lead(jnp.stack([_mm(dkraw, w_kb, tb=True, name="kv_bk_dx"), _mm(dv_sum, w_vb, tb=True, name="kv_bv_dx")]), f32, "dckv_sum")
    g_kb = _unpad_heads(_mm(kv_saved["ckv"], dkraw, ta=True, name="kv_bk_dw"), NOPE, 1)
    g_vb = _unpad_heads(_mm(kv_saved["ckv"], dv_sum, ta=True, name="kv_bv_dw"), VDIM, 1)
    gW["w_kv_b"] = jnp.concatenate([g_kb.reshape(KV_RANK, N_HEADS, NOPE), g_vb.reshape(KV_RANK, N_HEADS, VDIM)], axis=2).reshape(KV_RANK, -1)
    dkva, dga, dgkr = _rowwise_bwd(_kv_a_post, [kv_saved["kva"], cf, ss], [ga, gkr], [dckv, dkr], [0], [bf16], name="kv_a_post_bwd")
    dhk = _mm(dkva, w_kv_a, tb=True, name="kv_a_dx")
    g_kva = _mm(kv_saved["hk"], dkva, ta=True, name="kv_a_dw")
    gW["w_kv_a"] = jnp.concatenate([g_kva[:, :KV_RANK], g_kva[:, KV_RANK + NOPE:KV_RANK + NOPE + ROPE]], axis=1)
    dx, dkvg, dkmod = _rowwise_bwd(lambda x, g, m: _normmod(x, g, m[0:1], m[1:2]), [kv_saved["x"]], [kvg, kmod], [dhk], [0], [f32],
                                   adds={0: dx}, name="kv_norm_bwd")
    gW["kv_norm_g"], gW["kv_a_norm_g"] = dkvg[0], dga[0]
    gW["k_nope_norm_g"], gW["k_rope_norm_g"] = dgkn[0, :NOPE], dgkr[0, NOPE:NOPE + ROPE]

    dxs = _to_segments(dx)
    g_s5 = {k: [None] * N_A for k in ("lam_re", "lam_im", "log_dt", "b_re", "b_im", "c_re", "c_im", "d", "w_glu", "b_glu")}
    for l in range(N_A - 1, -1, -1):
        tag = f"l{l}"
        sv = saved[l]
        mod = mods[l]
        p, sp = s5p[l]
        dsk = W["s5_d"][l][None]
        bgl = W["s5_b_glu"][l][None]
        dx1, dw_gu, dw_down, g_n2[l], dmod = _ffn_bwd(dxs, sv["ffn"], mod, W["norm2_g"][l][None], w_gu[l], W["ffn_w_down"][l], tag)
        g_gate[l], g_up[l], g_down[l], g_gu[l] = dw_gu[:, :F], dw_gu[:, F:], dw_down, dw_gu

        def res1(g, z, b, m):
            return m[2:3] * (g * jax.nn.sigmoid(z + b))
        dg1, dz, dbgl, dmod_g = _rowwise_bwd(res1, [sv["g"], sv["z"]], [bgl, mod], [dx1], [0, 1], [f32, bf16], name=tag + "_res1_bwd")
        dg2 = _mm(dz, W["s5_w_glu"][l], tb=True, name=tag + "_glu_dx")
        g_s5["w_glu"][l] = _mm(sv["g"], dz, ta=True, name=tag + "_glu_dw")
        g_s5["b_glu"][l] = dbgl[0]

        def gelu_bwd(ys, h, dga_, dgb_, d):
            _, vjp = jax.vjp(lambda ys, h, d: jax.nn.gelu(ys + d * h), ys, h, d)
            dys, dh, dd = vjp(dga_ + dgb_)
            return dys, dh, dd
        dys, dh_a, ddsk = _rowwise(gelu_bwd, [sv["y_ssm"], sv["h"], dg1, dg2], [dsk], [(D, f32), (D, f32)], [(1, D)], name=tag + "_gelu_bwd")
        g_s5["d"][l] = ddsk[0]
        du, scan_g = _s5_mix_bwd(sv["h"], dys, sp, sv["scan"], tag + "_scan_bwd")
        pg = _s5_params_bwd(p, sp, *scan_g)
        for k in ("lam_re", "lam_im", "log_dt", "b_re", "b_im", "c_re", "c_im"):
            g_s5[k][l] = pg[k]
        dxs, dn1g, dmod1 = _rowwise_bwd(lambda x, g, m: _normmod(x, g, m[0:1], m[1:2]), [sv["x0"]], [W["norm1_g"][l][None], mod],
                                        [_sum_lead(jnp.stack([dh_a, du]), f32, tag + "_dh_sum")], [0], [f32], adds={0: dx1}, name=tag + "_norm1_bwd")
        g_n1[l] = dn1g[0]
        dmods[l] = dmod + dmod1 + dmod_g
    dx = _from_segments(dxs)

    gW.update(
        norm1_g=jnp.stack(g_n1), norm2_g=jnp.stack([g[0] for g in g_n2]),
        ffn_w_gate=jnp.stack(g_gate), ffn_w_up=jnp.stack(g_up), ffn_w_down=jnp.stack(g_down), ffn_w_gu=g_gu,
        mla_w_dq=jnp.stack(g_dq), mla_q_norm_g=jnp.stack(g_qn), mla_w_uq=jnp.stack(g_uq),
        mla_q_nope_norm_g=jnp.stack(g_qnn), mla_q_rope_norm_g=jnp.stack(g_qrn), mla_w_o=jnp.stack(g_wo),
    )
    for k, v in g_s5.items():
        gW["s5_" + k] = jnp.stack(v)
    return loss, dx, gW, jnp.stack(dmods), dkmod


_WEIGHTS = ['ada_w', 'ada_b', 'norm1_g', 'norm2_g', 'ffn_w_gate', 'ffn_w_up', 'ffn_w_down', 's5_lam_re', 's5_lam_im', 's5_log_dt',
            's5_b_re', 's5_b_im', 's5_c_re', 's5_c_im', 's5_d', 's5_w_glu', 's5_b_glu', 'kv_ada_w', 'kv_ada_b', 'kv_norm_g', 'w_kv_a',
            'kv_a_norm_g', 'w_kv_b', 'k_nope_norm_g', 'k_rope_norm_g', 'mla_w_dq', 'mla_q_norm_g', 'mla_w_uq', 'mla_q_nope_norm_g',
            'mla_q_rope_norm_g', 'mla_w_o']
_BIG = {"ffn_w_gate": 2, "ffn_w_up": 2, "ffn_w_down": 1, "s5_w_glu": 1, "w_kv_a": 0, "w_kv_b": 1, "mla_w_dq": 1, "mla_w_uq": 2, "mla_w_o": 1}
_TENSOR_PARALLEL = ("ada_w", "kv_ada_w")
_SHARDED_VECS = ("s5_d", "s5_b_glu")
_CHIPS = 4
_FLAT_W = 1024
_FLAT_ALIGN = 512
_W_GROUPS = (("gu", ("ffn_w_gate", "ffn_w_up")), ("rows", ("ffn_w_down", "s5_w_glu", "mla_w_o")),
             ("misc", ("w_kv_a", "w_kv_b", "mla_w_dq", "mla_w_uq")))


def _shard_rows(arrs, flat):
    if flat:
        return _pack_rows(arrs, _FLAT_W, _FLAT_ALIGN)
    C = arrs[0].shape[-1]
    return jnp.concatenate([a.reshape(-1, C) for a in arrs], axis=0)


def _take_rows(halves, r0, r1):
    R2 = halves.shape[2]
    pieces = []
    if r0 < R2:
        pieces.append(halves[0][:, r0:min(r1, R2)])
    if r1 > R2:
        pieces.append(halves[1][:, max(r0, R2) - R2:r1 - R2])
    return pieces[0] if len(pieces) == 1 else jnp.concatenate(pieces, axis=1)


def _pack_rows(arrs, width, align):
    flat = jnp.concatenate([a.reshape(-1) for a in arrs])
    rows = -(-flat.shape[0] // (width * align)) * align
    flat = jnp.pad(flat, (0, rows * width - flat.shape[0]))
    return flat.reshape(rows, width)


def _unpack(flat, shapes):
    flat = flat.reshape(-1)
    out, off = [], 0
    for s in shapes:
        n = math.prod(s)
        out.append(flat[off:off + n].reshape(s))
        off += n
    return out


def _adamw(w, g, m, v, name):
    c1 = 1.0 - ADAM_B1 ** ADAM_STEP
    c2 = 1.0 - ADAM_B2 ** ADAM_STEP

    def fn(w, g, m, v):
        m = ADAM_B1 * m + (1.0 - ADAM_B1) * g
        v = ADAM_B2 * v + (1.0 - ADAM_B2) * jnp.square(g)
        delta = -ADAM_LR * ((m / c1) / (jnp.sqrt(v / c2) + ADAM_EPS) + ADAM_WD * w)
        return delta, m, v

    C = w.shape[1]
    tm = _pick(w.shape[0], (256, 128, 64, 32, 16, 8))
    return _rowwise(fn, [w, g, m, v], [], [(C, f32)] * 3, tm=tm, name=name)


def kernel(x, c, positions, ada_w, ada_b, norm1_g, norm2_g, ffn_w_gate, ffn_w_up, ffn_w_down, s5_lam_re, s5_lam_im, s5_log_dt, s5_b_re, s5_b_im, s5_c_re, s5_c_im, s5_d, s5_w_glu, s5_b_glu, kv_ada_w, kv_ada_b, kv_norm_g, w_kv_a, kv_a_norm_g, w_kv_b, k_nope_norm_g, k_rope_norm_g, mla_w_dq, mla_q_norm_g, mla_w_uq, mla_q_nope_norm_g, mla_q_rope_norm_g, mla_w_o, loss_target, m_ada_w, m_ada_b, m_norm1_g, m_norm2_g, m_ffn_w_gate, m_ffn_w_up, m_ffn_w_down, m_s5_lam_re, m_s5_lam_im, m_s5_log_dt, m_s5_b_re, m_s5_b_im, m_s5_c_re, m_s5_c_im, m_s5_d, m_s5_w_glu, m_s5_b_glu, m_kv_ada_w, m_kv_ada_b, m_kv_norm_g, m_w_kv_a, m_kv_a_norm_g, m_w_kv_b, m_k_nope_norm_g, m_k_rope_norm_g, m_mla_w_dq, m_mla_q_norm_g, m_mla_w_uq, m_mla_q_nope_norm_g, m_mla_q_rope_norm_g, m_mla_w_o, v_ada_w, v_ada_b, v_norm1_g, v_norm2_g, v_ffn_w_gate, v_ffn_w_up, v_ffn_w_down, v_s5_lam_re, v_s5_lam_im, v_s5_log_dt, v_s5_b_re, v_s5_b_im, v_s5_c_re, v_s5_c_im, v_s5_d, v_s5_w_glu, v_s5_b_glu, v_kv_ada_w, v_kv_ada_b, v_kv_norm_g, v_w_kv_a, v_kv_a_norm_g, v_w_kv_b, v_k_nope_norm_g, v_k_rope_norm_g, v_mla_w_dq, v_mla_q_norm_g, v_mla_w_uq, v_mla_q_nope_norm_g, v_mla_q_rope_norm_g, v_mla_w_o):
    given = dict(locals())
    w = {n: given[n] for n in _WEIGHTS}
    m_in = {n: given["m_" + n] for n in _WEIGHTS}
    v_in = {n: given["v_" + n] for n in _WEIGHTS}
    S, D = x.shape[1], x.shape[2]
    ax, ay, ac = lax.axis_index("x"), lax.axis_index("y"), lax.axis_index("c")
    chip = 2 * ax + ay
    me = 4 * ax + 2 * ay + ac
    n_ada = ada_w.shape[2]
    n_kada = kv_ada_w.shape[1]

    g1 = _xchg(_pack_rows([c, s5_d, s5_b_glu], LANES, SUBLANES), "all", True, "gather_cond")
    g1 = g1.reshape(8, -1)
    c_all = g1[:, :D]
    vec = g1[0::2, D:D + 2 * s5_d.size].reshape(_CHIPS, 2, N_A, s5_d.shape[1])
    full_vecs = {"s5_d": vec[:, 0].transpose(1, 0, 2).reshape(N_A, D), "s5_b_glu": vec[:, 1].transpose(1, 0, 2).reshape(N_A, D)}
    c_act = _rowwise(jax.nn.silu, [c_all], [], [(D, f32)], tm=8, name="cond_silu")[0]

    mod_cols = [_mm(c_act, ada_w[l], cast=False, name="ada_proj") for l in range(DEPTH)] + [_mm(c_act, kv_ada_w, cast=False, name="kv_ada_proj")]
    g2 = _xchg(_pack_rows([jnp.concatenate(mod_cols, axis=1)], LANES, SUBLANES), "all", True, "gather_mod")
    g2 = g2.reshape(8, 8, -1)[0::2]
    mine = lax.dynamic_index_in_dim(g2, me, axis=1, keepdims=False)
    mod_lin = mine[:, :DEPTH * n_ada].reshape(_CHIPS, DEPTH, n_ada).transpose(1, 0, 2).reshape(DEPTH, 6 * D)
    kmod_lin = mine[:, DEPTH * n_ada:].reshape(1, 2 * D)
    mods = _rowwise(lambda a, b: a + b, [mod_lin, ada_b], [], [(6 * D, f32)], tm=DEPTH, name="ada_bias")[0].reshape(DEPTH, 6, D)
    kmod = _rowwise(lambda a, b: a + b, [kmod_lin, kv_ada_b.reshape(1, 2 * D)], [], [(2 * D, f32)], tm=1, name="kv_ada_bias")[0].reshape(2, D)

    chip_shards = {}
    for g, names in _W_GROUPS:
        buf = _shard_rows([w[n] for n in names], g == "misc").astype(bf16)
        Rg, C = buf.shape
        R2 = Rg // 2
        my_half = lax.dynamic_slice_in_dim(buf, ac * R2, R2, axis=0)
        halves_in = _xchg(my_half, "xy", True, "gather_" + g)
        both = _pair(halves_in.reshape(1, _CHIPS * R2, C), "swap", bf16, "gather_pair_" + g).reshape(2, _CHIPS, R2, C)
        if g == "misc":
            chip_shards.update(zip(names, _unpack_chips(both.reshape(2, _CHIPS, R2 * C), [w[n].shape for n in names])))
        else:
            r = 0
            for n in names:
                rows = math.prod(w[n].shape[:-1])
                chip_shards[n] = _take_rows(both, r, r + rows).reshape((_CHIPS,) + w[n].shape)
                r += rows
    W = {n: jnp.concatenate([chip_shards[n][q] for q in range(_CHIPS)], axis=_BIG[n]) for n in _BIG if n not in _W_GROUPS[0][1]}
    W["ffn_w_gu"] = [jnp.concatenate([chip_shards[n][q, l] for n in _W_GROUPS[0][1] for q in range(_CHIPS)], axis=1) for l in range(DEPTH)]
    for n in _WEIGHTS:
        if n not in _BIG and n not in _TENSOR_PARALLEL and n not in _SHARDED_VECS and n not in ("ada_b", "kv_ada_b"):
            W[n] = w[n]
    W.update(full_vecs)

    pos = positions.reshape(S, 1).astype(f32)
    loss, dx, gW, dmods, dkmod = _device_step(x[0], loss_target[0], pos, mods, kmod, W)

    g4 = _xchg(_pack_rows([dmods, dkmod], LANES, SUBLANES), "all", True, "gather_dmod")
    ada_b_sum = _sum_lead(g4, f32, "dmod_sum").reshape(-1)
    g_ada_b = ada_b_sum[:DEPTH * 6 * D].reshape(DEPTH, 6 * D)
    g_kv_ada_b = ada_b_sum[DEPTH * 6 * D:DEPTH * 6 * D + 2 * D]
    dm_all = g4.reshape(8, -1)
    dm_ada = lax.dynamic_slice_in_dim(dm_all[:, :DEPTH * 6 * D].reshape(8, DEPTH, 6 * D), chip * n_ada, n_ada, axis=2)
    dm_kv = lax.dynamic_slice_in_dim(dm_all[:, DEPTH * 6 * D:DEPTH * 6 * D + 2 * D], chip * n_kada, n_kada, axis=1)

    def outer(at, b):
        acc = at[:, 0:1] * b[0:1, :]
        for i in range(1, 8):
            acc = acc + at[:, i:i + 1] * b[i:i + 1, :]
        return acc

    c_act_t = c_act.T
    g_ada_w = jnp.stack([_rowwise(outer, [c_act_t], [dm_ada[:, l]], [(n_ada, f32)], name="ada_dw")[0] for l in range(DEPTH)])
    g_kv_ada_w = _rowwise(outer, [c_act_t], [dm_kv], [(n_kada, f32)], name="kv_ada_dw")[0]

    g_big = {}
    for g, names in _W_GROUPS:
        if g == "gu":
            Fq = w[names[0]].shape[-1]
            halves = jnp.stack([jnp.stack([jnp.concatenate([gW["ffn_w_gu"][l][:, (h * _CHIPS + q) * Fq:(h * _CHIPS + q + 1) * Fq]
                                                            for l in range(DEPTH)], axis=0) for q in range(_CHIPS)]) for h in range(2)])
            halves = halves.astype(bf16)
        else:
            per_chip = [_shard_rows([jnp.split(gW[n], _CHIPS, axis=_BIG[n])[q] for n in names], g == "misc") for q in range(_CHIPS)]
            full = jnp.stack(per_chip).astype(bf16)
            halves = full.reshape(_CHIPS, 2, full.shape[1] // 2, full.shape[2]).transpose(1, 0, 2, 3)
        _, _, R2, C = halves.shape
        hsum = _pair(halves.reshape(2, _CHIPS * R2, C), "add", bf16, "grads_pair_" + g).reshape(_CHIPS, R2, C)
        rb = _xchg(hsum, "xy", False, "grads_chips_" + g)
        red = _pair(rb, "swap", f32, "grads_halves_" + g).reshape(2 * R2, C)
        if g == "misc":
            g_big.update(zip(names, _unpack(red, [w[n].shape for n in names])))
        else:
            r = 0
            for n in names:
                rows = math.prod(w[n].shape[:-1])
                g_big[n] = red[r:r + rows].reshape(w[n].shape)
                r += rows

    small = [n for n in _WEIGHTS if n not in _BIG and n not in _TENSOR_PARALLEL and n not in ("ada_b", "kv_ada_b")]
    svec = _pack_rows([loss[0, :1]] + [gW[n] for n in small], LANES, 8 * SUBLANES)
    rows8 = svec.shape[0] // 8
    rs = _xchg(svec.reshape(8, rows8, LANES), "all", False, "small_scatter")
    red = _sum_lead(rs, f32, "small_sum")
    full = _xchg(red, "all", True, "small_gather").reshape(-1)
    loss_tot = full[0]
    g_small = dict(zip(small, _unpack(full[1:], [gW[n].shape for n in small])))
    for n in _SHARDED_VECS:
        g_small[n] = lax.dynamic_slice_in_dim(g_small[n], chip * w[n].shape[1], w[n].shape[1], axis=1)
    g_small["ada_b"] = g_ada_b
    g_small["kv_ada_b"] = g_kv_ada_b

    grads = dict(g_big)
    grads.update(g_small)
    grads["ada_w"] = g_ada_w
    grads["kv_ada_w"] = g_kv_ada_w

    delta, new_m, new_v = {}, {}, {}
    natural = _TENSOR_PARALLEL + _W_GROUPS[0][1] + _W_GROUPS[1][1]
    for n in natural:
        C = w[n].shape[-1]
        d_, m_, v_ = _adamw(w[n].reshape(-1, C), grads[n].reshape(-1, C), m_in[n].reshape(-1, C), v_in[n].reshape(-1, C), "adamw_" + n)
        delta[n], new_m[n], new_v[n] = d_.reshape(w[n].shape), m_.reshape(w[n].shape), v_.reshape(w[n].shape)
    rest = [n for n in _WEIGHTS if n not in natural]
    d_, m_, v_ = _adamw(*[_pack_rows([src[n] for n in rest], LANES, SUBLANES) for src in (w, grads, m_in, v_in)], "adamw_small")
    shapes = [w[n].shape for n in rest]
    for res, src in ((delta, d_), (new_m, m_), (new_v, v_)):
        res.update(zip(rest, _unpack(src, shapes)))

    return (loss_tot, dx[None], *[grads[n] for n in _WEIGHTS], *[delta[n] for n in _WEIGHTS],
            *[new_m[n] for n in _WEIGHTS], *[new_v[n] for n in _WEIGHTS])


def _unpack_chips(halves, shapes):
    _, n_chip, L = halves.shape
    out, off = [], 0
    for s in shapes:
        n = math.prod(s)
        pieces = []
        if off < L:
            pieces.append(halves[0][:, off:min(off + n, L)])
        if off + n > L:
            pieces.append(halves[1][:, max(off, L) - L:off + n - L])
        flat = pieces[0] if len(pieces) == 1 else jnp.concatenate(pieces, axis=1)
        out.append(flat.reshape((n_chip,) + tuple(s)))
        off += n
    return out
```

```python
import functools
import math

import jax
import jax.numpy as jnp
from jax import lax
from jax.experimental import pallas as pl
from jax.experimental.pallas import tpu as pltpu

f32 = jnp.float32
bf16 = jnp.bfloat16
SDS = jax.ShapeDtypeStruct

EPS = 1e-6
CHUNK = 64
N_HEADS = 16
NOPE = 64
ROPE = 32
VDIM = 64
HEAD_PAD = 128
KV_RANK = 256
ROPE_THETA = 10000.0
ATTN_SCALE = 1.0 / math.sqrt(NOPE + ROPE)
SSM_GROUP = 16
SSM_STATE = 64
N_A = 2
DEPTH = 4
LANES = 128
SUBLANES = 8
SEGMENTS = 8
SLAB_CH = 128
SLAB_ST = 512
ADAM_LR, ADAM_B1, ADAM_B2, ADAM_EPS, ADAM_WD, ADAM_STEP = 0.001, 0.9, 0.999, 1e-08, 0.01, 10


def _pick(n, prefs=(512, 256, 128)):
    for p in prefs:
        if n % p == 0:
            return p
    return n


V7X_VMEM_BYTES = 64 << 20
MM_VMEM_LIMIT = V7X_VMEM_BYTES - (8 << 20)
MM_VMEM_BUDGET = 40 << 20
MM_MAX_TM, MM_MAX_TN = 1536, 1536
BF16_ROWS = 16


def _largest_divisor(n, cap, unit):
    if n <= cap:
        return n
    for d in range(cap - cap % unit, 0, -unit):
        if n % d == 0:
            return d
    return n


def _mm_tiles(M, N, K, sa, sb, so, m_on_lanes):
    tm = _largest_divisor(M, MM_MAX_TM, LANES if m_on_lanes else BF16_ROWS)
    tn = _largest_divisor(N, MM_MAX_TN, LANES)
    tk = K

    def need(tm, tn, tk):
        return 2 * (tm * tk * sa + tk * tn * sb + tm * tn * so) + (0 if tk == K else tm * tn * 4)

    m_unit = LANES if m_on_lanes else BF16_ROWS
    while need(tm, tn, tk) > MM_VMEM_BUDGET:
        if tk % 256 == 0 and tk >= 1024:
            tk //= 2
        elif tm % (2 * m_unit) == 0 and tm >= 512:
            tm //= 2
        elif tn % 256 == 0:
            tn //= 2
        else:
            break
    return tm, tn, tk


def _mm(a, b, *, ta=False, tb=False, out_dtype=f32, cast=True, name="mm"):
    if ta:
        K, M = a.shape
    else:
        M, K = a.shape
    if tb:
        N, K2 = b.shape
    else:
        K2, N = b.shape
    assert K == K2, (a.shape, b.shape, ta, tb)
    tm, tn, tk = _mm_tiles(M, N, K, a.dtype.itemsize, b.dtype.itemsize, jnp.dtype(out_dtype).itemsize, ta)
    nk = K // tk
    dims = (((0 if ta else 1,), (1 if tb else 0,)), ((), ()))

    def dot(a_ref, b_ref):
        av, bv = a_ref[...], b_ref[...]
        if cast:
            return lax.dot_general(av.astype(bf16), bv.astype(bf16), dims, preferred_element_type=f32)
        return lax.dot_general(av, bv, dims, preferred_element_type=f32, precision=lax.Precision.HIGHEST)

    def body_one(a_ref, b_ref, o_ref):
        o_ref[...] = dot(a_ref, b_ref).astype(o_ref.dtype)

    def body_acc(a_ref, b_ref, o_ref, acc):
        k = pl.program_id(2)

        @pl.when(k == 0)
        def _():
            acc[...] = jnp.zeros_like(acc)

        acc[...] += dot(a_ref, b_ref)

        @pl.when(k == nk - 1)
        def _():
            o_ref[...] = acc[...].astype(o_ref.dtype)

    a_spec = pl.BlockSpec((tk, tm), lambda i, j, k: (k, i)) if ta else pl.BlockSpec((tm, tk), lambda i, j, k: (i, k))
    b_spec = pl.BlockSpec((tn, tk), lambda i, j, k: (j, k)) if tb else pl.BlockSpec((tk, tn), lambda i, j, k: (k, j))
    return pl.pallas_call(
        body_one if nk == 1 else body_acc, name=name, grid=(M // tm, N // tn, nk),
        in_specs=[a_spec, b_spec], out_specs=pl.BlockSpec((tm, tn), lambda i, j, k: (i, j)),
        out_shape=SDS((M, N), out_dtype), scratch_shapes=[] if nk == 1 else [pltpu.VMEM((tm, tn), f32)],
        compiler_params=pltpu.CompilerParams(dimension_semantics=("parallel", "parallel", "arbitrary"),
                                             vmem_limit_bytes=MM_VMEM_LIMIT),
    )(a, b)


def _rowwise(fn, rows, consts, row_outs, acc_outs=(), *, tm=256, name="rowwise"):
    S = rows[0].shape[0]
    tm = min(tm, S)
    assert S % tm == 0, (S, tm)
    n_r, n_c, n_ro, n_ao = len(rows), len(consts), len(row_outs), len(acc_outs)

    def body(*refs):
        vals = [r[...] for r in refs[:n_r + n_c]]
        outs = fn(*vals)
        if not isinstance(outs, (tuple, list)):
            outs = (outs,)
        assert len(outs) == n_ro + n_ao, (name, len(outs), n_ro, n_ao)
        o_refs = refs[n_r + n_c:]
        for ref, val in zip(o_refs[:n_ro], outs[:n_ro]):
            ref[...] = val.astype(ref.dtype)
        if n_ao:
            @pl.when(pl.program_id(0) == 0)
            def _():
                for ref in o_refs[n_ro:]:
                    ref[...] = jnp.zeros_like(ref)
            for ref, val in zip(o_refs[n_ro:], outs[n_ro:]):
                ref[...] += jnp.broadcast_to(val, ref.shape).astype(f32)

    def full(shape):
        nd = len(shape)
        return pl.BlockSpec(tuple(shape), lambda i: (0,) * nd)

    in_specs = [pl.BlockSpec((tm, r.shape[1]), lambda i: (i, 0)) for r in rows] + [full(c.shape) for c in consts]
    out_specs = [pl.BlockSpec((tm, w), lambda i: (i, 0)) for (w, _) in row_outs] + [full(s) for s in acc_outs]
    out_shape = [SDS((S, w), dt) for (w, dt) in row_outs] + [SDS(tuple(s), f32) for s in acc_outs]
    res = pl.pallas_call(
        body, name=name, grid=(S // tm,), in_specs=in_specs, out_specs=out_specs, out_shape=out_shape,
        compiler_params=pltpu.CompilerParams(dimension_semantics=("arbitrary",)),
    )(*rows, *consts)
    return res


def _rowwise_bwd(f, rows, consts, cts, want, row_dtypes, *, adds=None, tm=256, name="rowwise_bwd"):
    n_r, n_c, n_ct = len(rows), len(consts), len(cts)
    adds = adds or {}
    add_keys = sorted(adds)
    add_rows = [adds[k] for k in add_keys]

    def fn(*args):
        r = [a.astype(f32) for a in args[:n_r]]
        ct = [a.astype(f32) for a in args[n_r:n_r + n_ct]]
        ad = args[n_r + n_ct:n_r + n_ct + len(add_rows)]
        c = list(args[n_r + n_ct + len(add_rows):])
        _, vjp = jax.vjp(f, *r, *c)
        g = vjp(ct[0] if n_ct == 1 else tuple(ct))
        d_rows = []
        for pos, i in enumerate(want):
            d = g[i]
            if pos in adds:
                d = d + ad[add_keys.index(pos)].astype(f32)
            d_rows.append(d)
        return (*d_rows, *g[n_r:])

    return _rowwise(fn, list(rows) + list(cts) + add_rows, list(consts),
                    [(rows[i].shape[1], dt) for i, dt in zip(want, row_dtypes)],
                    [c.shape for c in consts], tm=tm, name=name)


def _sum_lead(arr, out_dtype, name):
    n, R, C = arr.shape
    tm = _pick(R, (256, 128, 64, 32, 16, 8))

    def body(a_ref, o_ref):
        acc = a_ref[0].astype(f32)
        for q in range(1, n):
            acc = acc + a_ref[q].astype(f32)
        o_ref[...] = acc.astype(o_ref.dtype)

    return pl.pallas_call(
        body, name=name, grid=(R // tm,), in_specs=[pl.BlockSpec((n, tm, C), lambda i: (0, i, 0))],
        out_specs=pl.BlockSpec((tm, C), lambda i: (i, 0)), out_shape=SDS((R, C), out_dtype),
    )(arr)


_GROUPS = {
    "all": [(kx, ky, kc) for kx in (0, 1) for ky in (0, 1) for kc in (0, 1)][1:],
    "xy": [(0, 1, 0), (1, 0, 0), (1, 1, 0)],
    "c": [(0, 0, 1)],
}


XCHG_CHUNK_MIN_BYTES = 1 << 20
XCHG_CHUNKS = 16


def _group_pos(group, x, y, c):
    return {"all": 4 * x + 2 * y + c, "xy": 2 * x + y, "c": c}[group]


def _xchg(send, group, bcast, name):
    flips = _GROUPS[group]
    n = len(flips) + 1
    piece = tuple(send.shape if bcast else send.shape[1:])
    if not bcast:
        assert send.shape[0] == n
    lead = piece[0]
    nch = 1
    if math.prod(piece) * send.dtype.itemsize >= XCHG_CHUNK_MIN_BYTES:
        nch = lead if len(piece) > 2 else (XCHG_CHUNKS if lead % (XCHG_CHUNKS * 2 * BF16_ROWS) == 0 else 1)
    rows = lead // nch

    def body(s_ref, r_ref, send_sems, recv_sems, own_sems):
        x, y, c = lax.axis_index("x"), lax.axis_index("y"), lax.axis_index("c")
        me = _group_pos(group, x, y, c)

        def src(p, ch):
            ref = s_ref if bcast else s_ref.at[p]
            return ref.at[pl.ds(ch * rows, rows)]

        def dst(ch):
            return r_ref.at[me].at[pl.ds(ch * rows, rows)]

        owns = [pltpu.make_async_copy(src(me, ch), dst(ch), own_sems.at[ch]) for ch in range(nch)]
        for cp in owns:
            cp.start()
        copies = []
        for ch in range(nch):
            for k, (kx, ky, kc) in enumerate(flips):
                tx, ty, tc = x ^ kx, y ^ ky, c ^ kc
                cp = pltpu.make_async_remote_copy(
                    src_ref=src(_group_pos(group, tx, ty, tc), ch), dst_ref=dst(ch),
                    send_sem=send_sems.at[k, ch], recv_sem=recv_sems.at[k, ch],
                    device_id=(tx, ty, tc), device_id_type=pl.DeviceIdType.MESH)
                cp.start()
                copies.append(cp)
        for cp in copies + owns:
            cp.wait()

    return pl.pallas_call(
        body, name=name, out_shape=SDS((n,) + piece, send.dtype),
        in_specs=[pl.BlockSpec(memory_space=pl.ANY)], out_specs=pl.BlockSpec(memory_space=pl.ANY),
        scratch_shapes=[pltpu.SemaphoreType.DMA((n - 1, nch)), pltpu.SemaphoreType.DMA((n - 1, nch)), pltpu.SemaphoreType.DMA((nch,))],
        compiler_params=pltpu.CompilerParams(has_side_effects=True),
    )(send)


PAIR_ROWS = 256


def _pair(arr, mode, out_dtype, name):
    P, N, C = arr.shape
    tm = _largest_divisor(N, PAIR_ROWS, BF16_ROWS)
    nb = N // tm
    assert N % tm == 0 and tm <= PAIR_ROWS
    add = mode == "add"
    assert not add or P == 2

    def body(core_ref, *refs):
        if add:
            keep_ref, send_ref, o_ref, slots, send_sems, recv_sems, credit = refs
        else:
            a_ref, o_ref, stage, slots, send_sems, recv_sems, credit = refs
        i = pl.program_id(0)
        x, y, c = lax.axis_index("x"), lax.axis_index("y"), lax.axis_index("c")
        sib = (x, y, 1 - c)
        slot = lax.rem(i, 2)
        if add:
            src = send_ref.at[0]
        else:
            t = a_ref[0].astype(f32)
            for q in range(1, P):
                t = t + a_ref[q].astype(f32)
            stage[...] = t.astype(stage.dtype)
            o_ref[c] = stage[...]
            src = stage

        @pl.when(i >= 2)
        def _():
            pl.semaphore_wait(credit.at[slot], 1)

        cp = pltpu.make_async_remote_copy(src_ref=src, dst_ref=slots.at[slot], send_sem=send_sems.at[slot], recv_sem=recv_sems.at[slot],
                                          device_id=sib, device_id_type=pl.DeviceIdType.MESH)
        cp.start()
        cp.wait_recv()
        if add:
            o_ref[...] = (keep_ref[0].astype(f32) + slots[slot].astype(f32)).astype(o_ref.dtype)
        else:
            o_ref[1 - c] = slots[slot]
        cp.wait_send()

        @pl.when(i + 2 < nb)
        def _():
            pl.semaphore_signal(credit.at[slot], 1, device_id=sib, device_id_type=pl.DeviceIdType.MESH)

    core = lax.axis_index("c").astype(jnp.int32).reshape(1)
    scratch = [pltpu.VMEM((2, tm, C), arr.dtype if add else out_dtype), pltpu.SemaphoreType.DMA((2,)), pltpu.SemaphoreType.DMA((2,)),
               pltpu.SemaphoreType.REGULAR((2,))]
    if add:
        in_specs = [pl.BlockSpec((1, tm, C), lambda i, core: (core[0], i, 0)), pl.BlockSpec((1, tm, C), lambda i, core: (1 - core[0], i, 0))]
        out_specs = pl.BlockSpec((tm, C), lambda i, core: (i, 0))
        out_shape = SDS((N, C), out_dtype)
        args = (arr, arr)
    else:
        in_specs = [pl.BlockSpec((P, tm, C), lambda i, core: (0, i, 0))]
        out_specs = pl.BlockSpec((2, tm, C), lambda i, core: (0, i, 0))
        out_shape = SDS((2, N, C), out_dtype)
        scratch = [pltpu.VMEM((tm, C), out_dtype)] + scratch
        args = (arr,)
    return pl.pallas_call(
        body, name=name, out_shape=out_shape,
        grid_spec=pltpu.PrefetchScalarGridSpec(num_scalar_prefetch=1, grid=(nb,), in_specs=in_specs, out_specs=out_specs, scratch_shapes=scratch),
        compiler_params=pltpu.CompilerParams(dimension_semantics=("arbitrary",), has_side_effects=True),
    )(core, *args)


def _rms(x, g):
    return x * lax.rsqrt(jnp.mean(x * x, axis=-1, keepdims=True) + EPS) * g


def _normmod(x, g, shift, scale):
    return _rms(x, g) * (1.0 + scale) + shift


def _lane(shape):
    return lax.broadcasted_iota(jnp.int32, shape, 1)


def _partner(x):
    lane = _lane(x.shape)
    lo = (lane >= NOPE) & (lane < NOPE + ROPE // 2)
    hi = (lane >= NOPE + ROPE // 2) & (lane < NOPE + ROPE)
    return jnp.where(lo, pltpu.roll(x, HEAD_PAD - ROPE // 2, 1), jnp.where(hi, pltpu.roll(x, ROPE // 2, 1), 0.0))


@jax.custom_vjp
def _rope(x, cf, ss):
    return x * cf + _partner(x) * ss


def _rope_fwd(x, cf, ss):
    return _rope(x, cf, ss), (cf, ss)


def _rope_bwd(res, dy):
    cf, ss = res
    return dy * cf + _partner(dy * ss), jnp.zeros_like(cf), jnp.zeros_like(ss)


_rope.defvjp(_rope_fwd, _rope_bwd)


def _head_norm(xh, gn, gr):
    lane = _lane(xh.shape)
    x2 = xh * xh
    ms_n = jnp.sum(jnp.where(lane < NOPE, x2, 0.0), axis=-1, keepdims=True) * (1.0 / NOPE)
    ms_r = jnp.sum(jnp.where((lane >= NOPE) & (lane < NOPE + ROPE), x2, 0.0), axis=-1, keepdims=True) * (1.0 / ROPE)
    return xh * (lax.rsqrt(ms_n + EPS) * gn + lax.rsqrt(ms_r + EPS) * gr)


def _q_heads(qraw, cf, ss, gn, gr):
    outs = []
    for h in range(N_HEADS):
        xh = qraw[:, h * HEAD_PAD:(h + 1) * HEAD_PAD]
        outs.append(_rope(_head_norm(xh, gn, gr), cf, ss))
    return jnp.concatenate(outs, axis=1)


def _k_heads(kraw, kr, gn):
    outs = []
    zero = jnp.zeros_like(gn)
    for h in range(N_HEADS):
        xh = kraw[:, h * HEAD_PAD:(h + 1) * HEAD_PAD]
        outs.append(_head_norm(xh, gn, zero) + kr)
    return jnp.concatenate(outs, axis=1)


def _kv_a_post(kva, cf, ss, ga, gr):
    ckv = _rms(kva[:, :KV_RANK], ga)
    kr = _rope(_head_norm(kva[:, KV_RANK:], jnp.zeros_like(gr), gr), cf, ss)
    return ckv, kr


SCAN_UNROLL = 8


def _cmul(ar, ai, br, bi):
    return ar * br - ai * bi, ar * bi + ai * br


def _s5_scan_fwd(u, wbr, wbi, ar, ai, init_r, init_i, wcr, wci, *, want_y, tb, name):
    S, D = u.shape
    ns = D // SLAB_CH
    seg = S // SEGMENTS
    tb = min(tb, seg)
    nb = seg // tb
    R = tb * SEGMENTS
    nt = (((1,), (0,)), ((), ()))

    def body(u_ref, wbr_ref, wbi_ref, ar_ref, ai_ref, ir_ref, ii_ref, wcr_ref, wci_ref, *rest):
        if want_y:
            y_ref, er_ref, ei_ref, bsr_ref, bsi_ref, cr, ci, bur, bui, sr, si = rest
        else:
            er_ref, ei_ref, bsr_ref, bsi_ref, cr, ci, bur, bui, sr, si = rest
        t = pl.program_id(1)

        @pl.when(t == 0)
        def _():
            cr[...] = ir_ref[0]
            ci[...] = ii_ref[0]

        bsr_ref[0, 0] = cr[...]
        bsi_ref[0, 0] = ci[...]
        ub = u_ref[...].astype(bf16)
        bur[...] = lax.dot_general(ub, wbr_ref[0].astype(bf16), nt, preferred_element_type=f32)
        bui[...] = lax.dot_general(ub, wbi_ref[0].astype(bf16), nt, preferred_element_type=f32)
        a_r = jnp.broadcast_to(ar_ref[0], (SEGMENTS, SLAB_ST))
        a_i = jnp.broadcast_to(ai_ref[0], (SEGMENTS, SLAB_ST))

        def step(tau, carry):
            c_r, c_i = carry
            r = pl.multiple_of(tau * SEGMENTS, SEGMENTS)
            p_r, p_i = _cmul(a_r, a_i, c_r, c_i)
            n_r = p_r + bur[pl.ds(r, SEGMENTS), :]
            n_i = p_i + bui[pl.ds(r, SEGMENTS), :]
            sr[pl.ds(r, SEGMENTS), :] = n_r
            si[pl.ds(r, SEGMENTS), :] = n_i
            return n_r, n_i

        c_r, c_i = lax.fori_loop(0, tb, step, (cr[...], ci[...]), unroll=SCAN_UNROLL)
        cr[...] = c_r
        ci[...] = c_i
        if want_y:
            y_ref[...] = (lax.dot_general(sr[...].astype(bf16), wcr_ref[0].astype(bf16), nt, preferred_element_type=f32)
                          - lax.dot_general(si[...].astype(bf16), wci_ref[0].astype(bf16), nt, preferred_element_type=f32))

        @pl.when(t == nb - 1)
        def _():
            er_ref[0] = c_r
            ei_ref[0] = c_i

    slab3 = lambda s: pl.BlockSpec((1,) + s, lambda k, t: (k, 0, 0))
    in_specs = [pl.BlockSpec((R, SLAB_CH), lambda k, t: (t, k)),
                slab3((SLAB_CH, SLAB_ST)), slab3((SLAB_CH, SLAB_ST)), slab3((1, SLAB_ST)), slab3((1, SLAB_ST)),
                slab3((SEGMENTS, SLAB_ST)), slab3((SEGMENTS, SLAB_ST)), slab3((SLAB_ST, SLAB_CH)), slab3((SLAB_ST, SLAB_CH))]
    out_specs = [slab3((SEGMENTS, SLAB_ST)), slab3((SEGMENTS, SLAB_ST)),
                 pl.BlockSpec((1, 1, SEGMENTS, SLAB_ST), lambda k, t: (t, k, 0, 0)),
                 pl.BlockSpec((1, 1, SEGMENTS, SLAB_ST), lambda k, t: (t, k, 0, 0))]
    out_shape = [SDS((ns, SEGMENTS, SLAB_ST), f32)] * 2 + [SDS((nb, ns, SEGMENTS, SLAB_ST), f32)] * 2
    if want_y:
        out_specs = [pl.BlockSpec((R, SLAB_CH), lambda k, t: (t, k))] + out_specs
        out_shape = [SDS((S, D), f32)] + out_shape
    return pl.pallas_call(
        body, name=name, grid=(ns, nb), in_specs=in_specs, out_specs=out_specs, out_shape=out_shape,
        scratch_shapes=[pltpu.VMEM((SEGMENTS, SLAB_ST), f32)] * 2 + [pltpu.VMEM((R, SLAB_ST), f32)] * 4,
        compiler_params=pltpu.CompilerParams(dimension_semantics=("parallel", "arbitrary")),
    )(u, wbr, wbi, ar, ai, init_r, init_i, wcr, wci)


def _s5_scan_bwd(u, dy, wbr, wbi, ar, ai, bs_r, bs_i, ginit_r, ginit_i, wcr, wci, *, full, tb, name):
    S, D = dy.shape
    ns = D // SLAB_CH
    seg = S // SEGMENTS
    tb = min(tb, seg)
    nb = seg // tb
    R = tb * SEGMENTS
    nn = (((1,), (0,)), ((), ()))
    nt = (((1,), (1,)), ((), ()))
    tn = (((0,), (0,)), ((), ()))

    def body(*refs):
        if full:
            (u_ref, dy_ref, wbr_ref, wbi_ref, ar_ref, ai_ref, bsr_ref, bsi_ref, gir_ref, gii_ref, wcr_ref, wci_ref,
             du_ref, dwbr_ref, dwbi_ref, dwcr_ref, dwci_ref, dar_ref, dai_ref, gfr_ref, gfi_ref,
             gr_c, gi_c, bur, bui, sr, si, gsr, gsi) = refs
        else:
            (dy_ref, ar_ref, ai_ref, gir_ref, gii_ref, wcr_ref, wci_ref, gfr_ref, gfi_ref, gr_c, gi_c, gsr, gsi) = refs
        t = pl.program_id(1)

        @pl.when(t == 0)
        def _():
            gr_c[...] = gir_ref[0]
            gi_c[...] = gii_ref[0]
            if full:
                for ref in (dwbr_ref, dwbi_ref, dwcr_ref, dwci_ref, dar_ref, dai_ref):
                    ref[...] = jnp.zeros_like(ref)

        a_r = jnp.broadcast_to(ar_ref[0], (SEGMENTS, SLAB_ST))
        a_i = jnp.broadcast_to(ai_ref[0], (SEGMENTS, SLAB_ST))
        dyb = dy_ref[...].astype(bf16)
        gsr[...] = lax.dot_general(dyb, wcr_ref[0].astype(bf16), nt, preferred_element_type=f32)
        gsi[...] = -lax.dot_general(dyb, wci_ref[0].astype(bf16), nt, preferred_element_type=f32)

        if full:
            ub = u_ref[...].astype(bf16)
            bur[...] = lax.dot_general(ub, wbr_ref[0].astype(bf16), nn, preferred_element_type=f32)
            bui[...] = lax.dot_general(ub, wbi_ref[0].astype(bf16), nn, preferred_element_type=f32)

            def fstep(tau, carry):
                c_r, c_i = carry
                r = pl.multiple_of(tau * SEGMENTS, SEGMENTS)
                p_r, p_i = _cmul(a_r, a_i, c_r, c_i)
                n_r = p_r + bur[pl.ds(r, SEGMENTS), :]
                n_i = p_i + bui[pl.ds(r, SEGMENTS), :]
                sr[pl.ds(r, SEGMENTS), :] = n_r
                si[pl.ds(r, SEGMENTS), :] = n_i
                return n_r, n_i

            lax.fori_loop(0, tb, fstep, (bsr_ref[0, 0], bsi_ref[0, 0]), unroll=SCAN_UNROLL)

        def adj(g_r, g_i):
            return a_r * g_r + a_i * g_i, a_r * g_i - a_i * g_r

        def rstep(k, carry):
            tau = tb - 1 - k
            r = pl.multiple_of(tau * SEGMENTS, SEGMENTS)
            if full:
                g_r, g_i, acc_r, acc_i = carry
            else:
                g_r, g_i = carry
            b_r, b_i = adj(g_r, g_i)
            n_r = b_r + gsr[pl.ds(r, SEGMENTS), :]
            n_i = b_i + gsi[pl.ds(r, SEGMENTS), :]
            gsr[pl.ds(r, SEGMENTS), :] = n_r
            gsi[pl.ds(r, SEGMENTS), :] = n_i
            if not full:
                return n_r, n_i
            rp = pl.multiple_of(jnp.maximum(tau - 1, 0) * SEGMENTS, SEGMENTS)
            first = tau == 0
            p_r = jnp.where(first, bsr_ref[0, 0], sr[pl.ds(rp, SEGMENTS), :])
            p_i = jnp.where(first, bsi_ref[0, 0], si[pl.ds(rp, SEGMENTS), :])
            return n_r, n_i, acc_r + n_r * p_r + n_i * p_i, acc_i + n_i * p_r - n_r * p_i

        zero = jnp.zeros((SEGMENTS, SLAB_ST), f32)
        if full:
            g_r, g_i, acc_r, acc_i = lax.fori_loop(0, tb, rstep, (gr_c[...], gi_c[...], zero, zero), unroll=SCAN_UNROLL)
            dar_ref[0] += acc_r
            dai_ref[0] += acc_i
        else:
            g_r, g_i = lax.fori_loop(0, tb, rstep, (gr_c[...], gi_c[...]), unroll=SCAN_UNROLL)
        gr_c[...] = g_r
        gi_c[...] = g_i

        if full:
            gbr = gsr[...].astype(bf16)
            gbi = gsi[...].astype(bf16)
            du_ref[...] = (lax.dot_general(gbr, wbr_ref[0].astype(bf16), nt, preferred_element_type=f32)
                           + lax.dot_general(gbi, wbi_ref[0].astype(bf16), nt, preferred_element_type=f32))
            dwbr_ref[0] += lax.dot_general(ub, gbr, tn, preferred_element_type=f32)
            dwbi_ref[0] += lax.dot_general(ub, gbi, tn, preferred_element_type=f32)
            dwcr_ref[0] += lax.dot_general(sr[...].astype(bf16), dyb, tn, preferred_element_type=f32)
            dwci_ref[0] -= lax.dot_general(si[...].astype(bf16), dyb, tn, preferred_element_type=f32)

        @pl.when(t == nb - 1)
        def _():
            gfr_ref[0] = g_r
            gfi_ref[0] = g_i

    slab3 = lambda s: pl.BlockSpec((1,) + s, lambda k, t: (k, 0, 0))
    rev_rows = pl.BlockSpec((R, SLAB_CH), lambda k, t: (nb - 1 - t, k))
    rev_bs = pl.BlockSpec((1, 1, SEGMENTS, SLAB_ST), lambda k, t: (nb - 1 - t, k, 0, 0))
    st = slab3((SEGMENTS, SLAB_ST))
    state_sds = SDS((ns, SEGMENTS, SLAB_ST), f32)
    if full:
        in_specs = [rev_rows, rev_rows, slab3((SLAB_CH, SLAB_ST)), slab3((SLAB_CH, SLAB_ST)), slab3((1, SLAB_ST)), slab3((1, SLAB_ST)),
                    rev_bs, rev_bs, st, st, slab3((SLAB_ST, SLAB_CH)), slab3((SLAB_ST, SLAB_CH))]
        args = (u, dy, wbr, wbi, ar, ai, bs_r, bs_i, ginit_r, ginit_i, wcr, wci)
        out_specs = [rev_rows, slab3((SLAB_CH, SLAB_ST)), slab3((SLAB_CH, SLAB_ST)), slab3((SLAB_ST, SLAB_CH)), slab3((SLAB_ST, SLAB_CH)),
                     st, st, st, st]
        out_shape = [SDS((S, D), f32), SDS((ns, SLAB_CH, SLAB_ST), f32), SDS((ns, SLAB_CH, SLAB_ST), f32),
                     SDS((ns, SLAB_ST, SLAB_CH), f32), SDS((ns, SLAB_ST, SLAB_CH), f32)] + [state_sds] * 4
        scratch = [pltpu.VMEM((SEGMENTS, SLAB_ST), f32)] * 2 + [pltpu.VMEM((R, SLAB_ST), f32)] * 6
    else:
        in_specs = [rev_rows, slab3((1, SLAB_ST)), slab3((1, SLAB_ST)), st, st, slab3((SLAB_ST, SLAB_CH)), slab3((SLAB_ST, SLAB_CH))]
        args = (dy, ar, ai, ginit_r, ginit_i, wcr, wci)
        out_specs = [st, st]
        out_shape = [state_sds] * 2
        scratch = [pltpu.VMEM((SEGMENTS, SLAB_ST), f32)] * 2 + [pltpu.VMEM((R, SLAB_ST), f32)] * 2
    return pl.pallas_call(
        body, name=name, grid=(ns, nb), in_specs=in_specs, out_specs=out_specs, out_shape=out_shape, scratch_shapes=scratch,
        compiler_params=pltpu.CompilerParams(dimension_semantics=("parallel", "arbitrary")),
    )(*args)


def _s5_chain(e_r, e_i, ar, ai, *, seg, reverse, name):
    ns = e_r.shape[0]
    assert seg & (seg - 1) == 0

    def body(er_ref, ei_ref, ar_ref, ai_ref, or_ref, oi_ref):
        p_r, p_i = ar_ref[0], ai_ref[0]
        if reverse:
            p_i = -p_i
        for _ in range(seg.bit_length() - 1):
            p_r, p_i = _cmul(p_r, p_i, p_r, p_i)
        c_r = jnp.zeros((1, SLAB_ST), f32)
        c_i = jnp.zeros((1, SLAB_ST), f32)
        order = range(SEGMENTS - 1, -1, -1) if reverse else range(SEGMENTS)
        for j in order:
            or_ref[0, pl.ds(j, 1), :] = c_r
            oi_ref[0, pl.ds(j, 1), :] = c_i
            m_r, m_i = _cmul(p_r, p_i, c_r, c_i)
            c_r = er_ref[0, pl.ds(j, 1), :] + m_r
            c_i = ei_ref[0, pl.ds(j, 1), :] + m_i

    st = pl.BlockSpec((1, SEGMENTS, SLAB_ST), lambda k: (k, 0, 0))
    av = pl.BlockSpec((1, 1, SLAB_ST), lambda k: (k, 0, 0))
    return pl.pallas_call(
        body, name=name, grid=(ns,), in_specs=[st, st, av, av], out_specs=[st, st],
        out_shape=[SDS(e_r.shape, f32)] * 2,
    )(e_r, e_i, ar, ai)


def _s5_disc(lr, li, log_dt):
    dt = jnp.exp(log_dt)
    mag = jnp.exp(lr * dt)
    ab_re = mag * jnp.cos(li * dt)
    ab_im = mag * jnp.sin(li * dt)
    den = lr * lr + li * li
    nr = ab_re - 1.0
    ni = ab_im
    return ab_re, ab_im, (nr * lr + ni * li) / den, (ni * lr - nr * li) / den


def _s5_bbar(b_r, b_i, fr, fi):
    return fr * b_r - fi * b_i, fr * b_i + fi * b_r


def _blockdiag(w, rows, cols):
    g = w.shape[0]
    w = w.reshape(g // 8, 8, rows, cols)
    eye = jnp.eye(8, dtype=w.dtype)
    full = w[:, :, :, None, :] * eye[None, :, None, :, None]
    return full.reshape(g // 8, 8 * rows, 8 * cols)


def _blockdiag_take(w, rows, cols):
    ns = w.shape[0]
    w = w.reshape(ns, 8, rows, 8, cols)
    idx = jnp.arange(8)
    d = w[:, idx, :, idx, :]
    return jnp.moveaxis(d, 0, 1).reshape(ns * 8, rows, cols)


def _mask(i, j, tq, tk):
    row = i * tq + lax.broadcasted_iota(jnp.int32, (tq, tk), 0)
    col = j * tk + lax.broadcasted_iota(jnp.int32, (tq, tk), 1)
    return (row // CHUNK) >= (col // CHUNK)


_NT = (((1,), (1,)), ((), ()))
_NN = (((1,), (0,)), ((), ()))
_TN = (((0,), (0,)), ((), ()))
_NEG = -1e30
ATTN_HEADS_PER_STEP = 2


def _attn_fwd(q, k, v, *, t, name):
    S = q.shape[0]
    t = min(t, S)
    nq = S // t

    hp = ATTN_HEADS_PER_STEP

    def body(q_ref, k_ref, v_ref, o_ref, lse_ref, m_s, l_s, acc):
        i = pl.program_id(1)
        m_s[...] = jnp.full_like(m_s, _NEG)
        l_s[...] = jnp.zeros_like(l_s)
        acc[...] = jnp.zeros_like(acc)

        def block(j, masked):
            r = pl.multiple_of(j * t, t)
            for hh in range(hp):
                lanes = slice(hh * HEAD_PAD, (hh + 1) * HEAD_PAD)
                s = lax.dot_general(q_ref[:, lanes], k_ref[pl.ds(r, t), lanes], _NT, preferred_element_type=f32) * ATTN_SCALE
                if masked:
                    s = jnp.where(_mask(0, 0, t, t), s, _NEG)
                m_old = m_s[hh]
                m_new = jnp.maximum(m_old, jnp.max(s, axis=-1, keepdims=True))
                alpha = jnp.exp(m_old - m_new)
                p = jnp.exp(s - m_new)
                l_s[hh] = alpha * l_s[hh] + jnp.sum(p, axis=-1, keepdims=True)
                acc[hh] = alpha * acc[hh] + lax.dot_general(p.astype(bf16), v_ref[pl.ds(r, t), lanes], _NN, preferred_element_type=f32)
                m_s[hh] = m_new

        def below(j, carry):
            block(j, False)
            return carry

        lax.fori_loop(0, i, below, 0)
        block(i, True)
        for hh in range(hp):
            o_ref[:, hh * HEAD_PAD:(hh + 1) * HEAD_PAD] = acc[hh] / l_s[hh]
            lse_ref[hh] = m_s[hh] + jnp.log(l_s[hh])

    qs = pl.BlockSpec((t, hp * HEAD_PAD), lambda h, i: (i, h))
    ks = pl.BlockSpec((S, hp * HEAD_PAD), lambda h, i: (0, h))
    return pl.pallas_call(
        body, name=name, grid=(N_HEADS // hp, nq), in_specs=[qs, ks, ks],
        out_specs=[qs, pl.BlockSpec((hp, t, 1), lambda h, i: (h, i, 0))],
        out_shape=[SDS((S, N_HEADS * HEAD_PAD), f32), SDS((N_HEADS, S, 1), f32)],
        scratch_shapes=[pltpu.VMEM((hp, t, 1), f32), pltpu.VMEM((hp, t, 1), f32), pltpu.VMEM((hp, t, HEAD_PAD), f32)],
        compiler_params=pltpu.CompilerParams(dimension_semantics=("parallel", "arbitrary")),
    )(q, k, v)


def _attn_bwd(q, k, v, do, o, lse, *, t, name):
    S = q.shape[0]
    t = min(t, S)
    nq = S // t

    def body(q_ref, k_ref, v_ref, do_ref, o_ref, lse_ref, dq_ref, dk_ref, dv_ref, d_s, dk_acc, dv_acc):
        j = pl.program_id(1)

        @pl.when(j == 0)
        def _():
            dq_ref[...] = jnp.zeros_like(dq_ref)
            d_s[...] = jnp.sum(do_ref[...].astype(f32) * o_ref[...], axis=-1, keepdims=True)

        dk_acc[...] = jnp.zeros_like(dk_acc)
        dv_acc[...] = jnp.zeros_like(dv_acc)
        kb = k_ref[...]
        vb = v_ref[...]

        def block(i, masked):
            r = pl.multiple_of(i * t, t)
            qb = q_ref[pl.ds(r, t), :]
            dob = do_ref[pl.ds(r, t), :]
            s = lax.dot_general(qb, kb, _NT, preferred_element_type=f32) * ATTN_SCALE
            if masked:
                s = jnp.where(_mask(0, 0, t, t), s, _NEG)
            p = jnp.exp(s - lse_ref[0, pl.ds(r, t), :])
            dp = lax.dot_general(dob, vb, _NT, preferred_element_type=f32)
            ds = (p * (dp - d_s[pl.ds(r, t), :]) * ATTN_SCALE).astype(bf16)
            dv_acc[...] += lax.dot_general(p.astype(bf16), dob, _TN, preferred_element_type=f32)
            dk_acc[...] += lax.dot_general(ds, qb, _TN, preferred_element_type=f32)
            dq_ref[pl.ds(r, t), :] += lax.dot_general(ds, kb, _NN, preferred_element_type=f32)

        def below(i, carry):
            block(i, False)
            return carry

        block(j, True)
        lax.fori_loop(j + 1, nq, below, 0)
        dk_ref[...] = dk_acc[...]
        dv_ref[...] = dv_acc[...]

    hs = pl.BlockSpec((S, HEAD_PAD), lambda h, j: (0, h))
    ks = pl.BlockSpec((t, HEAD_PAD), lambda h, j: (j, h))
    ls = pl.BlockSpec((1, S, 1), lambda h, j: (h, 0, 0))
    return pl.pallas_call(
        body, name=name, grid=(N_HEADS, nq), in_specs=[hs, ks, ks, hs, hs, ls], out_specs=[hs, ks, ks],
        out_shape=[SDS((S, N_HEADS * HEAD_PAD), f32)] * 3,
        scratch_shapes=[pltpu.VMEM((S, 1), f32), pltpu.VMEM((t, HEAD_PAD), f32), pltpu.VMEM((t, HEAD_PAD), f32)],
        compiler_params=pltpu.CompilerParams(dimension_semantics=("parallel", "arbitrary")),
    )(q, k, v, do, o, lse)


def _pad_heads(w, per_head, axis):
    w = jnp.moveaxis(w, axis, -1)
    lead = w.shape[:-1]
    w = w.reshape(lead + (N_HEADS, per_head))
    w = jnp.pad(w, [(0, 0)] * len(lead) + [(0, 0), (0, HEAD_PAD - per_head)])
    return jnp.moveaxis(w.reshape(lead + (N_HEADS * HEAD_PAD,)), -1, axis)


def _unpad_heads(w, per_head, axis):
    w = jnp.moveaxis(w, axis, -1)
    lead = w.shape[:-1]
    w = w.reshape(lead + (N_HEADS, HEAD_PAD))[..., :per_head]
    return jnp.moveaxis(w.reshape(lead + (N_HEADS * per_head,)), -1, axis)


def _lanes128(vec, offset):
    return jnp.zeros((1, HEAD_PAD), f32).at[0, offset:offset + vec.shape[0]].set(vec)


def _to_segments(a):
    S, D = a.shape
    return a.reshape(SEGMENTS, S // SEGMENTS, D).transpose(1, 0, 2).reshape(S, D)


def _from_segments(a):
    S, D = a.shape
    return a.reshape(S // SEGMENTS, SEGMENTS, D).transpose(1, 0, 2).reshape(S, D)


def _s5_params_fwd(p):
    G, N, P = p["b_re"].shape
    def disc_body(lr, li, ld, o1, o2, o3, o4):
        o1[...], o2[...], o3[...], o4[...] = _s5_disc(lr[...], li[...], ld[...])

    ab_r, ab_i, f_r, f_i = pl.pallas_call(disc_body, name="s5_disc", out_shape=[SDS((G, N), f32)] * 4)(
        p["lam_re"], p["lam_im"], p["log_dt"].reshape(G, 1))
    bb_r, bb_i = _rowwise(_s5_bbar, [p["b_re"].reshape(G * N, P), p["b_im"].reshape(G * N, P), f_r.reshape(G * N, 1), f_i.reshape(G * N, 1)],
                          [], [(P, f32), (P, f32)], tm=512, name="s5_bbar")
    ns = G // 8
    out = dict(
        f_r=f_r, f_i=f_i,
        a_r=ab_r.reshape(ns, 1, SLAB_ST), a_i=ab_i.reshape(ns, 1, SLAB_ST),
        wb_r=_blockdiag(bb_r.reshape(G, N, P).transpose(0, 2, 1), P, N), wb_i=_blockdiag(bb_i.reshape(G, N, P).transpose(0, 2, 1), P, N),
        wc_r=_blockdiag(p["c_re"].transpose(0, 2, 1), N, P), wc_i=_blockdiag(p["c_im"].transpose(0, 2, 1), N, P),
    )
    return out


def _s5_params_bwd(p, sp, d_ar, d_ai, d_wbr, d_wbi, d_wcr, d_wci):
    G, N, P = p["b_re"].shape
    dbb_r = _blockdiag_take(d_wbr, P, N).transpose(0, 2, 1).reshape(G * N, P)
    dbb_i = _blockdiag_take(d_wbi, P, N).transpose(0, 2, 1).reshape(G * N, P)
    d_cre = _blockdiag_take(d_wcr, N, P).transpose(0, 2, 1)
    d_cim = _blockdiag_take(d_wci, N, P).transpose(0, 2, 1)
    rows = [p["b_re"].reshape(G * N, P), p["b_im"].reshape(G * N, P), sp["f_r"].reshape(G * N, 1), sp["f_i"].reshape(G * N, 1)]
    d_br, d_bi, d_fr, d_fi = _rowwise_bwd(_s5_bbar, rows, [], [dbb_r, dbb_i], [0, 1, 2, 3], [f32] * 4, tm=512, name="s5_bbar_bwd")

    def seg_sum(d, name):
        return _sum_lead(d.transpose(1, 0, 2).reshape(SEGMENTS, G, N), f32, name)

    def body(lr, li, ld, c1, c2, c3, c4, o1, o2, o3):
        _, vjp = jax.vjp(_s5_disc, lr[...], li[...], ld[...])
        o1[...], o2[...], o3[...] = vjp((c1[...], c2[...], c3[...], c4[...]))

    d_lr, d_li, d_ld = pl.pallas_call(
        body, name="s5_disc_bwd", out_shape=[SDS((G, N), f32), SDS((G, N), f32), SDS((G, 1), f32)],
    )(p["lam_re"], p["lam_im"], p["log_dt"].reshape(G, 1), seg_sum(d_ar, "s5_da_re_sum"), seg_sum(d_ai, "s5_da_im_sum"),
      d_fr.reshape(G, N), d_fi.reshape(G, N))
    return dict(lam_re=d_lr, lam_im=d_li, log_dt=d_ld.reshape(G), b_re=d_br.reshape(G, N, P), b_im=d_bi.reshape(G, N, P),
                c_re=d_cre, c_im=d_cim)


_SCAN_TB = 64
_ATTN_T = 512


def _s5_mix_fwd(h, sp, name):
    S = h.shape[0]
    seg = S // SEGMENTS
    zeros = jnp.zeros((h.shape[1] // SLAB_CH, SEGMENTS, SLAB_ST), f32)
    common = (sp["wb_r"], sp["wb_i"], sp["a_r"], sp["a_i"])
    e_r, e_i, _, _ = _s5_scan_fwd(h, *common, zeros, zeros, sp["wc_r"], sp["wc_i"], want_y=False, tb=_SCAN_TB, name=name + "_local")
    i_r, i_i = _s5_chain(e_r, e_i, sp["a_r"], sp["a_i"], seg=seg, reverse=False, name=name + "_chain")
    y, _, _, bs_r, bs_i = _s5_scan_fwd(h, *common, i_r, i_i, sp["wc_r"], sp["wc_i"], want_y=True, tb=_SCAN_TB, name=name)
    return y, (bs_r, bs_i)


def _s5_mix_bwd(h, dy, sp, saved, name):
    S = h.shape[0]
    seg = S // SEGMENTS
    bs_r, bs_i = saved
    zeros = jnp.zeros((h.shape[1] // SLAB_CH, SEGMENTS, SLAB_ST), f32)
    gf_r, gf_i = _s5_scan_bwd(None, dy, None, None, sp["a_r"], sp["a_i"], None, None, zeros, zeros, sp["wc_r"], sp["wc_i"],
                              full=False, tb=_SCAN_TB, name=name + "_local")
    gi_r, gi_i = _s5_chain(gf_r, gf_i, sp["a_r"], sp["a_i"], seg=seg, reverse=True, name=name + "_chain")
    du, d_wbr, d_wbi, d_wcr, d_wci, d_ar, d_ai, _, _ = _s5_scan_bwd(
        h, dy, sp["wb_r"], sp["wb_i"], sp["a_r"], sp["a_i"], bs_r, bs_i, gi_r, gi_i, sp["wc_r"], sp["wc_i"],
        full=True, tb=_SCAN_TB, name=name)
    return du, (d_ar, d_ai, d_wbr, d_wbi, d_wcr, d_wci)


def _ffn_fwd(x1, mod, n2g, w_gu, w_down, tag):
    F = w_down.shape[0]
    h2 = _rowwise(lambda x, g, m: _normmod(x, g, m[3:4], m[4:5]), [x1], [n2g, mod], [(x1.shape[1], bf16)], name=tag + "_norm2")[0]
    ab = _mm(h2, w_gu, out_dtype=bf16, name=tag + "_gu")
    act = _rowwise(lambda ab: jax.nn.silu(ab[:, :F].astype(f32)) * ab[:, F:].astype(f32), [ab], [], [(F, bf16)], name=tag + "_act")[0]
    f = _mm(act, w_down, name=tag + "_down")
    x2 = _rowwise(lambda x, f, m: x + m[5:6] * f, [x1, f], [mod], [(x1.shape[1], f32)], name=tag + "_res2")[0]
    return x2, dict(x1=x1, h2=h2, ab=ab, act=act, f=f)


def _ffn_bwd(dx2, sv, mod, n2g, w_gu, w_down, tag):
    F = w_down.shape[0]
    D = dx2.shape[1]
    df, dgate2 = _rowwise(lambda dx, f, m: (m[5:6] * dx, jnp.sum(dx * f, axis=0, keepdims=True)), [dx2, sv["f"]], [mod],
                          [(D, bf16)], [(1, D)], name=tag + "_res2_bwd")
    dact = _mm(df, w_down, tb=True, out_dtype=bf16, name=tag + "_down_dx")
    dw_down = _mm(sv["act"], df, ta=True, name=tag + "_down_dw")
    dab = _rowwise_bwd(lambda ab: jax.nn.silu(ab[:, :F]) * ab[:, F:], [sv["ab"]], [], [dact], [0], [bf16], name=tag + "_act_bwd")[0]
    dh2 = _mm(dab, w_gu, tb=True, name=tag + "_gu_dx")
    dw_gu = _mm(sv["h2"], dab, ta=True, name=tag + "_gu_dw")
    dx1, dn2g, dmod = _rowwise_bwd(lambda x, g, m: _normmod(x, g, m[3:4], m[4:5]), [sv["x1"]], [n2g, mod], [dh2], [0], [f32],
                                   adds={0: dx2}, name=tag + "_norm2_bwd")
    dmod = dmod.at[5:6].add(dgate2)
    return dx1, dw_gu, dw_down, dn2g, dmod


def _device_step(x, target, pos, mods, kmod, W):
    S, D = x.shape
    F = W["ffn_w_down"].shape[1]
    inv = 1.0 / (ROPE_THETA ** (jnp.arange(0, ROPE, 2, dtype=f32) / ROPE))
    inv128 = _lanes128(jnp.concatenate([inv, inv]), NOPE)
    sign128 = _lanes128(jnp.concatenate([-jnp.ones(ROPE // 2, f32), jnp.ones(ROPE // 2, f32)]), NOPE)
    cf, ss = _rowwise(lambda p, iv, sg: (jnp.cos(p * iv), jnp.sin(p * iv) * sg), [pos], [inv128, sign128],
                      [(HEAD_PAD, f32), (HEAD_PAD, f32)], name="rope_table")

    w_gu = W.get("ffn_w_gu") or [jnp.concatenate([W["ffn_w_gate"][l], W["ffn_w_up"][l]], axis=1) for l in range(DEPTH)]
    g_gu = [None] * DEPTH
    saved = []
    xs = _to_segments(x)

    s5p = []
    for l in range(N_A):
        tag = f"l{l}"
        mod = mods[l]
        p = {k: W["s5_" + k][l] for k in ("lam_re", "lam_im", "log_dt", "b_re", "b_im", "c_re", "c_im")}
        sp = _s5_params_fwd(p)
        s5p.append((p, sp))
        n1g = W["norm1_g"][l][None]
        dsk = W["s5_d"][l][None]
        bgl = W["s5_b_glu"][l][None]
        h = _rowwise(lambda x, g, m: _normmod(x, g, m[0:1], m[1:2]), [xs], [n1g, mod], [(D, f32)], name=tag + "_norm1")[0]
        y_ssm, scan_saved = _s5_mix_fwd(h, sp, tag + "_scan")
        g = _rowwise(lambda ys, h, d: jax.nn.gelu(ys + d * h), [y_ssm, h], [dsk], [(D, f32)], name=tag + "_gelu")[0]
        z = _mm(g, W["s5_w_glu"][l], name=tag + "_glu")
        x1 = _rowwise(lambda x, g, z, b, m: x + m[2:3] * (g * jax.nn.sigmoid(z + b)), [xs, g, z], [bgl, mod], [(D, f32)], name=tag + "_res1")[0]
        x2, fsv = _ffn_fwd(x1, mod, W["norm2_g"][l][None], w_gu[l], W["ffn_w_down"][l], tag)
        saved.append(dict(x0=xs, h=h, y_ssm=y_ssm, g=g, z=z, scan=scan_saved, ffn=fsv))
        xs = x2
    x = _from_segments(xs)

    w_kv_a = jnp.concatenate([W["w_kv_a"][:, :KV_RANK], jnp.zeros((D, NOPE), f32).astype(W["w_kv_a"].dtype), W["w_kv_a"][:, KV_RANK:],
                              jnp.zeros((D, HEAD_PAD - NOPE - ROPE), f32).astype(W["w_kv_a"].dtype)], axis=1)
    wkb = W["w_kv_b"].reshape(KV_RANK, N_HEADS, NOPE + VDIM)
    w_kb = _pad_heads(wkb[:, :, :NOPE].reshape(KV_RANK, N_HEADS * NOPE), NOPE, 1)
    w_vb = _pad_heads(wkb[:, :, NOPE:].reshape(KV_RANK, N_HEADS * VDIM), VDIM, 1)
    kvg = W["kv_norm_g"][None]
    ga = W["kv_a_norm_g"][None]
    gkr = _lanes128(W["k_rope_norm_g"], NOPE)
    gkn = _lanes128(W["k_nope_norm_g"], 0)
    hk = _rowwise(lambda x, g, m: _normmod(x, g, m[0:1], m[1:2]), [x], [kvg, kmod], [(D, bf16)], name="kv_norm")[0]
    kva = _mm(hk, w_kv_a, name="kv_a")
    ckv, kr = _rowwise(_kv_a_post, [kva, cf, ss], [ga, gkr], [(KV_RANK, f32), (HEAD_PAD, f32)], name="kv_a_post")
    kraw = _mm(ckv, w_kb, name="kv_bk")
    vpad = _mm(ckv, w_vb, out_dtype=bf16, name="kv_bv")
    kpad = _rowwise(_k_heads, [kraw, kr], [gkn], [(N_HEADS * HEAD_PAD, bf16)], name="k_heads")[0]
    kv_saved = dict(x=x, hk=hk, kva=kva, ckv=ckv, kr=kr, kraw=kraw)

    for l in range(N_A, DEPTH):
        tag = f"l{l}"
        j = l - N_A
        mod = mods[l]
        n1g = W["norm1_g"][l][None]
        w_uq = _pad_heads(W["mla_w_uq"][j], NOPE + ROPE, 1)
        w_o = _pad_heads(W["mla_w_o"][j], VDIM, 0)
        qg = W["mla_q_norm_g"][j][None]
        gqn = _lanes128(W["mla_q_nope_norm_g"][j], 0)
        gqr = _lanes128(W["mla_q_rope_norm_g"][j], NOPE)
        h = _rowwise(lambda x, g, m: _normmod(x, g, m[0:1], m[1:2]), [x], [n1g, mod], [(D, bf16)], name=tag + "_norm1")[0]
        qa = _mm(h, W["mla_w_dq"][j], name=tag + "_dq")
        qc = _rowwise(_rms, [qa], [qg], [(qa.shape[1], f32)], name=tag + "_qnorm")[0]
        qraw = _mm(qc, w_uq, name=tag + "_uq")
        q = _rowwise(_q_heads, [qraw, cf, ss], [gqn, gqr], [(N_HEADS * HEAD_PAD, bf16)], name=tag + "_q_heads")[0]
        o, lse = _attn_fwd(q, kpad, vpad, t=_ATTN_T, name=tag + "_attn")
        mix = _mm(o, w_o, name=tag + "_wo")
        x1 = _rowwise(lambda x, mx, m: x + m[2:3] * mx, [x, mix], [mod], [(D, f32)], name=tag + "_res1")[0]
        x2, fsv = _ffn_fwd(x1, mod, W["norm2_g"][l][None], w_gu[l], W["ffn_w_down"][l], tag)
        saved.append(dict(x0=x, h=h, qa=qa, qc=qc, qraw=qraw, q=q, o=o, lse=lse, mix=mix, w_uq=w_uq, w_o=w_o, ffn=fsv))
        x = x2

    dx, loss = _rowwise(
        lambda y, t: ((y - t) * (1.0 / D), jnp.full((1, LANES), 0.5 * jnp.sum(jnp.mean(jnp.square(y - t), axis=-1)), f32)),
        [x, target], [], [(D, f32)], [(1, LANES)], name="loss")

    gW = {}
    dmods = [None] * DEPTH
    g_gate, g_up, g_down, g_n1, g_n2 = [None] * DEPTH, [None] * DEPTH, [None] * DEPTH, [None] * DEPTH, [None] * DEPTH
    dks, dvs = [], []
    g_dq, g_qn, g_uq, g_qnn, g_qrn, g_wo = [None] * 2, [None] * 2, [None] * 2, [None] * 2, [None] * 2, [None] * 2
    for l in range(DEPTH - 1, N_A - 1, -1):
        tag = f"l{l}"
        j = l - N_A
        sv = saved[l]
        mod = mods[l]
        dx1, dw_gu, dw_down, g_n2[l], dmod = _ffn_bwd(dx, sv["ffn"], mod, W["norm2_g"][l][None], w_gu[l], W["ffn_w_down"][l], tag)
        g_gate[l], g_up[l], g_down[l], g_gu[l] = dw_gu[:, :F], dw_gu[:, F:], dw_down, dw_gu
        dmix, dgate1 = _rowwise(lambda dx, mx, m: (m[2:3] * dx, jnp.sum(dx * mx, axis=0, keepdims=True)), [dx1, sv["mix"]], [mod],
                                [(D, bf16)], [(1, D)], name=tag + "_res1_bwd")
        do = _mm(dmix, sv["w_o"], tb=True, out_dtype=bf16, name=tag + "_wo_dx")
        g_wo[j] = _unpad_heads(_mm(sv["o"], dmix, ta=True, name=tag + "_wo_dw"), VDIM, 0)
        dq, dk, dv = _attn_bwd(sv["q"], kpad, vpad, do, sv["o"], sv["lse"], t=_ATTN_T, name=tag + "_attn_bwd")
        dks.append(dk)
        dvs.append(dv)
        gqn = _lanes128(W["mla_q_nope_norm_g"][j], 0)
        gqr = _lanes128(W["mla_q_rope_norm_g"][j], NOPE)
        dqraw, dgqn, dgqr = _rowwise_bwd(lambda qr, c, s, a, b: _q_heads(qr, c, s, a, b), [sv["qraw"], cf, ss], [gqn, gqr], [dq], [0], [bf16],
                                         name=tag + "_q_heads_bwd")
        g_qnn[j], g_qrn[j] = dgqn[0, :NOPE], dgqr[0, NOPE:NOPE + ROPE]
        dqc = _mm(dqraw, sv["w_uq"], tb=True, name=tag + "_uq_dx")
        g_uq[j] = _unpad_heads(_mm(sv["qc"], dqraw, ta=True, name=tag + "_uq_dw"), NOPE + ROPE, 1)
        qg = W["mla_q_norm_g"][j][None]
        dqa, dqg = _rowwise_bwd(_rms, [sv["qa"]], [qg], [dqc], [0], [bf16], name=tag + "_qnorm_bwd")
        g_qn[j] = dqg[0]
        dh = _mm(dqa, W["mla_w_dq"][j], tb=True, name=tag + "_dq_dx")
        g_dq[j] = _mm(sv["h"], dqa, ta=True, name=tag + "_dq_dw")
        dx, dn1g, dmod1 = _rowwise_bwd(lambda x, g, m: _normmod(x, g, m[0:1], m[1:2]), [sv["x0"]], [W["norm1_g"][l][None], mod], [dh], [0], [f32],
                                       adds={0: dx1}, name=tag + "_norm1_bwd")
        g_n1[l] = dn1g[0]
        dmods[l] = (dmod + dmod1).at[2:3].add(dgate1)

    gkn = _lanes128(W["k_nope_norm_g"], 0)
    dk_sum = _sum_lead(jnp.stack(dks), f32, "dk_sum")
    dv_sum = _sum_lead(jnp.stack(dvs), bf16, "dv_sum")
    dkraw, dkr, dgkn = _rowwise_bwd(lambda kr_, r, g: _k_heads(kr_, r, g), [kv_saved["kraw"], kv_saved["kr"]], [gkn], [dk_sum], [0, 1], [bf16, f32],
                                    name="k_heads_bwd")
    dckv = _sum_lead(jnp.stack([_mm(dkraw, w_kb, tb=True, name="kv_bk_dx"), _mm(dv_sum, w_vb, tb=True, name="kv_bv_dx")]), f32, "dckv_sum")
    g_kb = _unpad_heads(_mm(kv_saved["ckv"], dkraw, ta=True, name="kv_bk_dw"), NOPE, 1)
    g_vb = _unpad_heads(_mm(kv_saved["ckv"], dv_sum, ta=True, name="kv_bv_dw"), VDIM, 1)
    gW["w_kv_b"] = jnp.concatenate([g_kb.reshape(KV_RANK, N_HEADS, NOPE), g_vb.reshape(KV_RANK, N_HEADS, VDIM)], axis=2).reshape(KV_RANK, -1)
    dkva, dga, dgkr = _rowwise_bwd(_kv_a_post, [kv_saved["kva"], cf, ss], [ga, gkr], [dckv, dkr], [0], [bf16], name="kv_a_post_bwd")
    dhk = _mm(dkva, w_kv_a, tb=True, name="kv_a_dx")
    g_kva = _mm(kv_saved["hk"], dkva, ta=True, name="kv_a_dw")
    gW["w_kv_a"] = jnp.concatenate([g_kva[:, :KV_RANK], g_kva[:, KV_RANK + NOPE:KV_RANK + NOPE + ROPE]], axis=1)
    dx, dkvg, dkmod = _rowwise_bwd(lambda x, g, m: _normmod(x, g, m[0:1], m[1:2]), [kv_saved["x"]], [kvg, kmod], [dhk], [0], [f32],
                                   adds={0: dx}, name="kv_norm_bwd")
    gW["kv_norm_g"], gW["kv_a_norm_g"] = dkvg[0], dga[0]
    gW["k_nope_norm_g"], gW["k_rope_norm_g"] = dgkn[0, :NOPE], dgkr[0, NOPE:NOPE + ROPE]

    dxs = _to_segments(dx)
    g_s5 = {k: [None] * N_A for k in ("lam_re", "lam_im", "log_dt", "b_re", "b_im", "c_re", "c_im", "d", "w_glu", "b_glu")}
    for l in range(N_A - 1, -1, -1):
        tag = f"l{l}"
        sv = saved[l]
        mod = mods[l]
        p, sp = s5p[l]
        dsk = W["s5_d"][l][None]
        bgl = W["s5_b_glu"][l][None]
        dx1, dw_gu, dw_down, g_n2[l], dmod = _ffn_bwd(dxs, sv["ffn"], mod, W["norm2_g"][l][None], w_gu[l], W["ffn_w_down"][l], tag)
        g_gate[l], g_up[l], g_down[l], g_gu[l] = dw_gu[:, :F], dw_gu[:, F:], dw_down, dw_gu

        def res1(g, z, b, m):
            return m[2:3] * (g * jax.nn.sigmoid(z + b))
        dg1, dz, dbgl, dmod_g = _rowwise_bwd(res1, [sv["g"], sv["z"]], [bgl, mod], [dx1], [0, 1], [f32, bf16], name=tag + "_res1_bwd")
        dg2 = _mm(dz, W["s5_w_glu"][l], tb=True, name=tag + "_glu_dx")
        g_s5["w_glu"][l] = _mm(sv["g"], dz, ta=True, name=tag + "_glu_dw")
        g_s5["b_glu"][l] = dbgl[0]

        def gelu_bwd(ys, h, dga_, dgb_, d):
            _, vjp = jax.vjp(lambda ys, h, d: jax.nn.gelu(ys + d * h), ys, h, d)
            dys, dh, dd = vjp(dga_ + dgb_)
            return dys, dh, dd
        dys, dh_a, ddsk = _rowwise(gelu_bwd, [sv["y_ssm"], sv["h"], dg1, dg2], [dsk], [(D, f32), (D, f32)], [(1, D)], name=tag + "_gelu_bwd")
        g_s5["d"][l] = ddsk[0]
        du, scan_g = _s5_mix_bwd(sv["h"], dys, sp, sv["scan"], tag + "_scan_bwd")
        pg = _s5_params_bwd(p, sp, *scan_g)
        for k in ("lam_re", "lam_im", "log_dt", "b_re", "b_im", "c_re", "c_im"):
            g_s5[k][l] = pg[k]
        dxs, dn1g, dmod1 = _rowwise_bwd(lambda x, g, m: _normmod(x, g, m[0:1], m[1:2]), [sv["x0"]], [W["norm1_g"][l][None], mod],
                                        [_sum_lead(jnp.stack([dh_a, du]), f32, tag + "_dh_sum")], [0], [f32], adds={0: dx1}, name=tag + "_norm1_bwd")
        g_n1[l] = dn1g[0]
        dmods[l] = dmod + dmod1 + dmod_g
    dx = _from_segments(dxs)

    gW.update(
        norm1_g=jnp.stack(g_n1), norm2_g=jnp.stack([g[0] for g in g_n2]),
        ffn_w_gate=jnp.stack(g_gate), ffn_w_up=jnp.stack(g_up), ffn_w_down=jnp.stack(g_down), ffn_w_gu=g_gu,
        mla_w_dq=jnp.stack(g_dq), mla_q_norm_g=jnp.stack(g_qn), mla_w_uq=jnp.stack(g_uq),
        mla_q_nope_norm_g=jnp.stack(g_qnn), mla_q_rope_norm_g=jnp.stack(g_qrn), mla_w_o=jnp.stack(g_wo),
    )
    for k, v in g_s5.items():
        gW["s5_" + k] = jnp.stack(v)
    return loss, dx, gW, jnp.stack(dmods), dkmod


_WEIGHTS = ['ada_w', 'ada_b', 'norm1_g', 'norm2_g', 'ffn_w_gate', 'ffn_w_up', 'ffn_w_down', 's5_lam_re', 's5_lam_im', 's5_log_dt',
            's5_b_re', 's5_b_im', 's5_c_re', 's5_c_im', 's5_d', 's5_w_glu', 's5_b_glu', 'kv_ada_w', 'kv_ada_b', 'kv_norm_g', 'w_kv_a',
            'kv_a_norm_g', 'w_kv_b', 'k_nope_norm_g', 'k_rope_norm_g', 'mla_w_dq', 'mla_q_norm_g', 'mla_w_uq', 'mla_q_nope_norm_g',
            'mla_q_rope_norm_g', 'mla_w_o']
_BIG = {"ffn_w_gate": 2, "ffn_w_up": 2, "ffn_w_down": 1, "s5_w_glu": 1, "w_kv_a": 0, "w_kv_b": 1, "mla_w_dq": 1, "mla_w_uq": 2, "mla_w_o": 1}
_TENSOR_PARALLEL = ("ada_w", "kv_ada_w")
_SHARDED_VECS = ("s5_d", "s5_b_glu")
_CHIPS = 4
_FLAT_W = 1024
_FLAT_ALIGN = 512
_W_GROUPS = (("gu", ("ffn_w_gate", "ffn_w_up")), ("rows", ("ffn_w_down", "s5_w_glu", "mla_w_o")),
             ("misc", ("w_kv_a", "w_kv_b", "mla_w_dq", "mla_w_uq")))


def _shard_rows(arrs, flat):
    if flat:
        return _pack_rows(arrs, _FLAT_W, _FLAT_ALIGN)
    C = arrs[0].shape[-1]
    return jnp.concatenate([a.reshape(-1, C) for a in arrs], axis=0)


def _take_rows(halves, r0, r1):
    R2 = halves.shape[2]
    pieces = []
    if r0 < R2:
        pieces.append(halves[0][:, r0:min(r1, R2)])
    if r1 > R2:
        pieces.append(halves[1][:, max(r0, R2) - R2:r1 - R2])
    return pieces[0] if len(pieces) == 1 else jnp.concatenate(pieces, axis=1)


def _pack_rows(arrs, width, align):
    flat = jnp.concatenate([a.reshape(-1) for a in arrs])
    rows = -(-flat.shape[0] // (width * align)) * align
    flat = jnp.pad(flat, (0, rows * width - flat.shape[0]))
    return flat.reshape(rows, width)


def _unpack(flat, shapes):
    flat = flat.reshape(-1)
    out, off = [], 0
    for s in shapes:
        n = math.prod(s)
        out.append(flat[off:off + n].reshape(s))
        off += n
    return out


def _adamw(w, g, m, v, name):
    c1 = 1.0 - ADAM_B1 ** ADAM_STEP
    c2 = 1.0 - ADAM_B2 ** ADAM_STEP

    def fn(w, g, m, v):
        m = ADAM_B1 * m + (1.0 - ADAM_B1) * g
        v = ADAM_B2 * v + (1.0 - ADAM_B2) * jnp.square(g)
        delta = -ADAM_LR * ((m / c1) / (jnp.sqrt(v / c2) + ADAM_EPS) + ADAM_WD * w)
        return delta, m, v

    C = w.shape[1]
    tm = _pick(w.shape[0], (256, 128, 64, 32, 16, 8))
    return _rowwise(fn, [w, g, m, v], [], [(C, f32)] * 3, tm=tm, name=name)


def kernel(x, c, positions, ada_w, ada_b, norm1_g, norm2_g, ffn_w_gate, ffn_w_up, ffn_w_down, s5_lam_re, s5_lam_im, s5_log_dt, s5_b_re, s5_b_im, s5_c_re, s5_c_im, s5_d, s5_w_glu, s5_b_glu, kv_ada_w, kv_ada_b, kv_norm_g, w_kv_a, kv_a_norm_g, w_kv_b, k_nope_norm_g, k_rope_norm_g, mla_w_dq, mla_q_norm_g, mla_w_uq, mla_q_nope_norm_g, mla_q_rope_norm_g, mla_w_o, loss_target, m_ada_w, m_ada_b, m_norm1_g, m_norm2_g, m_ffn_w_gate, m_ffn_w_up, m_ffn_w_down, m_s5_lam_re, m_s5_lam_im, m_s5_log_dt, m_s5_b_re, m_s5_b_im, m_s5_c_re, m_s5_c_im, m_s5_d, m_s5_w_glu, m_s5_b_glu, m_kv_ada_w, m_kv_ada_b, m_kv_norm_g, m_w_kv_a, m_kv_a_norm_g, m_w_kv_b, m_k_nope_norm_g, m_k_rope_norm_g, m_mla_w_dq, m_mla_q_norm_g, m_mla_w_uq, m_mla_q_nope_norm_g, m_mla_q_rope_norm_g, m_mla_w_o, v_ada_w, v_ada_b, v_norm1_g, v_norm2_g, v_ffn_w_gate, v_ffn_w_up, v_ffn_w_down, v_s5_lam_re, v_s5_lam_im, v_s5_log_dt, v_s5_b_re, v_s5_b_im, v_s5_c_re, v_s5_c_im, v_s5_d, v_s5_w_glu, v_s5_b_glu, v_kv_ada_w, v_kv_ada_b, v_kv_norm_g, v_w_kv_a, v_kv_a_norm_g, v_w_kv_b, v_k_nope_norm_g, v_k_rope_norm_g, v_mla_w_dq, v_mla_q_norm_g, v_mla_w_uq, v_mla_q_nope_norm_g, v_mla_q_rope_norm_g, v_mla_w_o):
    given = dict(locals())
    w = {n: given[n] for n in _WEIGHTS}
    m_in = {n: given["m_" + n] for n in _WEIGHTS}
    v_in = {n: given["v_" + n] for n in _WEIGHTS}
    S, D = x.shape[1], x.shape[2]
    ax, ay, ac = lax.axis_index("x"), lax.axis_index("y"), lax.axis_index("c")
    chip = 2 * ax + ay
    me = 4 * ax + 2 * ay + ac
    n_ada = ada_w.shape[2]
    n_kada = kv_ada_w.shape[1]

    g1 = _xchg(_pack_rows([c, s5_d, s5_b_glu], LANES, SUBLANES), "all", True, "gather_cond")
    g1 = g1.reshape(8, -1)
    c_all = g1[:, :D]
    vec = g1[0::2, D:D + 2 * s5_d.size].reshape(_CHIPS, 2, N_A, s5_d.shape[1])
    full_vecs = {"s5_d": vec[:, 0].transpose(1, 0, 2).reshape(N_A, D), "s5_b_glu": vec[:, 1].transpose(1, 0, 2).reshape(N_A, D)}
    c_act = _rowwise(jax.nn.silu, [c_all], [], [(D, f32)], tm=8, name="cond_silu")[0]

    mod_cols = [_mm(c_act, ada_w[l], cast=False, name="ada_proj") for l in range(DEPTH)] + [_mm(c_act, kv_ada_w, cast=False, name="kv_ada_proj")]
    g2 = _xchg(_pack_rows([jnp.concatenate(mod_cols, axis=1)], LANES, SUBLANES), "all", True, "gather_mod")
    g2 = g2.reshape(8, 8, -1)[0::2]
    mine = lax.dynamic_index_in_dim(g2, me, axis=1, keepdims=False)
    mod_lin = mine[:, :DEPTH * n_ada].reshape(_CHIPS, DEPTH, n_ada).transpose(1, 0, 2).reshape(DEPTH, 6 * D)
    kmod_lin = mine[:, DEPTH * n_ada:].reshape(1, 2 * D)
    mods = _rowwise(lambda a, b: a + b, [mod_lin, ada_b], [], [(6 * D, f32)], tm=DEPTH, name="ada_bias")[0].reshape(DEPTH, 6, D)
    kmod = _rowwise(lambda a, b: a + b, [kmod_lin, kv_ada_b.reshape(1, 2 * D)], [], [(2 * D, f32)], tm=1, name="kv_ada_bias")[0].reshape(2, D)

    chip_shards = {}
    for g, names in _W_GROUPS:
        buf = _shard_rows([w[n] for n in names], g == "misc").astype(bf16)
        Rg, C = buf.shape
        R2 = Rg // 2
        my_half = lax.dynamic_slice_in_dim(buf, ac * R2, R2, axis=0)
        halves_in = _xchg(my_half, "xy", True, "gather_" + g)
        both = _pair(halves_in.reshape(1, _CHIPS * R2, C), "swap", bf16, "gather_pair_" + g).reshape(2, _CHIPS, R2, C)
        if g == "misc":
            chip_shards.update(zip(names, _unpack_chips(both.reshape(2, _CHIPS, R2 * C), [w[n].shape for n in names])))
        else:
            r = 0
            for n in names:
                rows = math.prod(w[n].shape[:-1])
                chip_shards[n] = _take_rows(both, r, r + rows).reshape((_CHIPS,) + w[n].shape)
                r += rows
    W = {n: jnp.concatenate([chip_shards[n][q] for q in range(_CHIPS)], axis=_BIG[n]) for n in _BIG if n not in _W_GROUPS[0][1]}
    W["ffn_w_gu"] = [jnp.concatenate([chip_shards[n][q, l] for n in _W_GROUPS[0][1] for q in range(_CHIPS)], axis=1) for l in range(DEPTH)]
    for n in _WEIGHTS:
        if n not in _BIG and n not in _TENSOR_PARALLEL and n not in _SHARDED_VECS and n not in ("ada_b", "kv_ada_b"):
            W[n] = w[n]
    W.update(full_vecs)

    pos = positions.reshape(S, 1).astype(f32)
    loss, dx, gW, dmods, dkmod = _device_step(x[0], loss_target[0], pos, mods, kmod, W)

    g4 = _xchg(_pack_rows([dmods, dkmod], LANES, SUBLANES), "all", True, "gather_dmod")
    ada_b_sum = _sum_lead(g4, f32, "dmod_sum").reshape(-1)
    g_ada_b = ada_b_sum[:DEPTH * 6 * D].reshape(DEPTH, 6 * D)
    g_kv_ada_b = ada_b_sum[DEPTH * 6 * D:DEPTH * 6 * D + 2 * D]
    dm_all = g4.reshape(8, -1)
    dm_ada = lax.dynamic_slice_in_dim(dm_all[:, :DEPTH * 6 * D].reshape(8, DEPTH, 6 * D), chip * n_ada, n_ada, axis=2)
    dm_kv = lax.dynamic_slice_in_dim(dm_all[:, DEPTH * 6 * D:DEPTH * 6 * D + 2 * D], chip * n_kada, n_kada, axis=1)

    def outer(at, b):
        acc = at[:, 0:1] * b[0:1, :]
        for i in range(1, 8):
            acc = acc + at[:, i:i + 1] * b[i:i + 1, :]
        return acc

    c_act_t = c_act.T
    g_ada_w = jnp.stack([_rowwise(outer, [c_act_t], [dm_ada[:, l]], [(n_ada, f32)], name="ada_dw")[0] for l in range(DEPTH)])
    g_kv_ada_w = _rowwise(outer, [c_act_t], [dm_kv], [(n_kada, f32)], name="kv_ada_dw")[0]

    g_big = {}
    for g, names in _W_GROUPS:
        if g == "gu":
            Fq = w[names[0]].shape[-1]
            halves = jnp.stack([jnp.stack([jnp.concatenate([gW["ffn_w_gu"][l][:, (h * _CHIPS + q) * Fq:(h * _CHIPS + q + 1) * Fq]
                                                            for l in range(DEPTH)], axis=0) for q in range(_CHIPS)]) for h in range(2)])
            halves = halves.astype(bf16)
        else:
            per_chip = [_shard_rows([jnp.split(gW[n], _CHIPS, axis=_BIG[n])[q] for n in names], g == "misc") for q in range(_CHIPS)]
            full = jnp.stack(per_chip).astype(bf16)
            halves = full.reshape(_CHIPS, 2, full.shape[1] // 2, full.shape[2]).transpose(1, 0, 2, 3)
        _, _, R2, C = halves.shape
        hsum = _pair(halves.reshape(2, _CHIPS * R2, C), "add", bf16, "grads_pair_" + g).reshape(_CHIPS, R2, C)
        rb = _xchg(hsum, "xy", False, "grads_chips_" + g)
        red = _pair(rb, "swap", f32, "grads_halves_" + g).reshape(2 * R2, C)
        if g == "misc":
            g_big.update(zip(names, _unpack(red, [w[n].shape for n in names])))
        else:
            r = 0
            for n in names:
                rows = math.prod(w[n].shape[:-1])
                g_big[n] = red[r:r + rows].reshape(w[n].shape)
                r += rows

    small = [n for n in _WEIGHTS if n not in _BIG and n not in _TENSOR_PARALLEL and n not in ("ada_b", "kv_ada_b")]
    svec = _pack_rows([loss[0, :1]] + [gW[n] for n in small], LANES, 8 * SUBLANES)
    rows8 = svec.shape[0] // 8
    rs = _xchg(svec.reshape(8, rows8, LANES), "all", False, "small_scatter")
    red = _sum_lead(rs, f32, "small_sum")
    full = _xchg(red, "all", True, "small_gather").reshape(-1)
    loss_tot = full[0]
    g_small = dict(zip(small, _unpack(full[1:], [gW[n].shape for n in small])))
    for n in _SHARDED_VECS:
        g_small[n] = lax.dynamic_slice_in_dim(g_small[n], chip * w[n].shape[1], w[n].shape[1], axis=1)
    g_small["ada_b"] = g_ada_b
    g_small["kv_ada_b"] = g_kv_ada_b

    grads = dict(g_big)
    grads.update(g_small)
    grads["ada_w"] = g_ada_w
    grads["kv_ada_w"] = g_kv_ada_w

    delta, new_m, new_v = {}, {}, {}
    natural = _TENSOR_PARALLEL + _W_GROUPS[0][1] + _W_GROUPS[1][1]
    for n in natural:
        C = w[n].shape[-1]
        d_, m_, v_ = _adamw(w[n].reshape(-1, C), grads[n].reshape(-1, C), m_in[n].reshape(-1, C), v_in[n].reshape(-1, C), "adamw_" + n)
        delta[n], new_m[n], new_v[n] = d_.reshape(w[n].shape), m_.reshape(w[n].shape), v_.reshape(w[n].shape)
    rest = [n for n in _WEIGHTS if n not in natural]
    d_, m_, v_ = _adamw(*[_pack_rows([src[n] for n in rest], _FLAT_W, SUBLANES) for src in (w, grads, m_in, v_in)], "adamw_small")
    shapes = [w[n].shape for n in rest]
    for res, src in ((delta, d_), (new_m, m_), (new_v, v_)):
        res.update(zip(rest, _unpack(src, shapes)))

    return (loss_tot, dx[None], *[grads[n] for n in _WEIGHTS], *[delta[n] for n in _WEIGHTS],
            *[new_m[n] for n in _WEIGHTS], *[new_v[n] for n in _WEIGHTS])


def _unpack_chips(halves, shapes):
    _, n_chip, L = halves.shape
    out, off = [], 0
    for s in shapes:
        n = math.prod(s)
        pieces = []
        if off < L:
            pieces.append(halves[0][:, off:min(off + n, L)])
        if off + n > L:
            pieces.append(halves[1][:, max(off, L) - L:off + n - L])
        flat = pieces[0] if len(pieces) == 1 else jnp.concatenate(pieces, axis=1)
        out.append(flat.reshape((n_chip,) + tuple(s)))
        off += n
    return out
```

```python
import functools
import math

import jax
import jax.numpy as jnp
from jax import lax
from jax.experimental import pallas as pl
from jax.experimental.pallas import tpu as pltpu

f32 = jnp.float32
bf16 = jnp.bfloat16
SDS = jax.ShapeDtypeStruct

EPS = 1e-6
CHUNK = 64
N_HEADS = 16
NOPE = 64
ROPE = 32
VDIM = 64
HEAD_PAD = 128
KV_RANK = 256
ROPE_THETA = 10000.0
ATTN_SCALE = 1.0 / math.sqrt(NOPE + ROPE)
SSM_GROUP = 16
SSM_STATE = 64
N_A = 2
DEPTH = 4
LANES = 128
SUBLANES = 8
SEGMENTS = 8
SLAB_CH = 128
SLAB_ST = 512
ADAM_LR, ADAM_B1, ADAM_B2, ADAM_EPS, ADAM_WD, ADAM_STEP = 0.001, 0.9, 0.999, 1e-08, 0.01, 10


def _pick(n, prefs=(512, 256, 128)):
    for p in prefs:
        if n % p == 0:
            return p
    return n


V7X_VMEM_BYTES = 64 << 20
MM_VMEM_LIMIT = V7X_VMEM_BYTES - (8 << 20)
MM_VMEM_BUDGET = 40 << 20
MM_MAX_TM, MM_MAX_TN = 1536, 1536
BF16_ROWS = 16


def _largest_divisor(n, cap, unit):
    if n <= cap:
        return n
    for d in range(cap - cap % unit, 0, -unit):
        if n % d == 0:
            return d
    return n


def _mm_tiles(M, N, K, sa, sb, so, m_on_lanes):
    tm = _largest_divisor(M, MM_MAX_TM, LANES if m_on_lanes else BF16_ROWS)
    tn = _largest_divisor(N, MM_MAX_TN, LANES)
    tk = K

    def need(tm, tn, tk):
        return 2 * (tm * tk * sa + tk * tn * sb + tm * tn * so) + (0 if tk == K else tm * tn * 4)

    m_unit = LANES if m_on_lanes else BF16_ROWS
    while need(tm, tn, tk) > MM_VMEM_BUDGET:
        if tk % 256 == 0 and tk >= 1024:
            tk //= 2
        elif tm % (2 * m_unit) == 0 and tm >= 512:
            tm //= 2
        elif tn % 256 == 0:
            tn //= 2
        else:
            break
    return tm, tn, tk


def _mm(a, b, *, ta=False, tb=False, out_dtype=f32, cast=True, name="mm"):
    if ta:
        K, M = a.shape
    else:
        M, K = a.shape
    if tb:
        N, K2 = b.shape
    else:
        K2, N = b.shape
    assert K == K2, (a.shape, b.shape, ta, tb)
    tm, tn, tk = _mm_tiles(M, N, K, a.dtype.itemsize, b.dtype.itemsize, jnp.dtype(out_dtype).itemsize, ta)
    nk = K // tk
    dims = (((0 if ta else 1,), (1 if tb else 0,)), ((), ()))

    def dot(a_ref, b_ref):
        av, bv = a_ref[...], b_ref[...]
        if cast:
            return lax.dot_general(av.astype(bf16), bv.astype(bf16), dims, preferred_element_type=f32)
        return lax.dot_general(av, bv, dims, preferred_element_type=f32, precision=lax.Precision.HIGHEST)

    def body_one(a_ref, b_ref, o_ref):
        o_ref[...] = dot(a_ref, b_ref).astype(o_ref.dtype)

    def body_acc(a_ref, b_ref, o_ref, acc):
        k = pl.program_id(2)

        @pl.when(k == 0)
        def _():
            acc[...] = jnp.zeros_like(acc)

        acc[...] += dot(a_ref, b_ref)

        @pl.when(k == nk - 1)
        def _():
            o_ref[...] = acc[...].astype(o_ref.dtype)

    a_spec = pl.BlockSpec((tk, tm), lambda i, j, k: (k, i)) if ta else pl.BlockSpec((tm, tk), lambda i, j, k: (i, k))
    b_spec = pl.BlockSpec((tn, tk), lambda i, j, k: (j, k)) if tb else pl.BlockSpec((tk, tn), lambda i, j, k: (k, j))
    return pl.pallas_call(
        body_one if nk == 1 else body_acc, name=name, grid=(M // tm, N // tn, nk),
        in_specs=[a_spec, b_spec], out_specs=pl.BlockSpec((tm, tn), lambda i, j, k: (i, j)),
        out_shape=SDS((M, N), out_dtype), scratch_shapes=[] if nk == 1 else [pltpu.VMEM((tm, tn), f32)],
        compiler_params=pltpu.CompilerParams(dimension_semantics=("parallel", "parallel", "arbitrary"),
                                             vmem_limit_bytes=MM_VMEM_LIMIT),
    )(a, b)


def _rowwise(fn, rows, consts, row_outs, acc_outs=(), *, tm=256, name="rowwise"):
    S = rows[0].shape[0]
    tm = min(tm, S)
    assert S % tm == 0, (S, tm)
    n_r, n_c, n_ro, n_ao = len(rows), len(consts), len(row_outs), len(acc_outs)

    def body(*refs):
        vals = [r[...] for r in refs[:n_r + n_c]]
        outs = fn(*vals)
        if not isinstance(outs, (tuple, list)):
            outs = (outs,)
        assert len(outs) == n_ro + n_ao, (name, len(outs), n_ro, n_ao)
        o_refs = refs[n_r + n_c:]
        for ref, val in zip(o_refs[:n_ro], outs[:n_ro]):
            ref[...] = val.astype(ref.dtype)
        if n_ao:
            @pl.when(pl.program_id(0) == 0)
            def _():
                for ref in o_refs[n_ro:]:
                    ref[...] = jnp.zeros_like(ref)
            for ref, val in zip(o_refs[n_ro:], outs[n_ro:]):
                ref[...] += jnp.broadcast_to(val, ref.shape).astype(f32)

    def full(shape):
        nd = len(shape)
        return pl.BlockSpec(tuple(shape), lambda i: (0,) * nd)

    in_specs = [pl.BlockSpec((tm, r.shape[1]), lambda i: (i, 0)) for r in rows] + [full(c.shape) for c in consts]
    out_specs = [pl.BlockSpec((tm, w), lambda i: (i, 0)) for (w, _) in row_outs] + [full(s) for s in acc_outs]
    out_shape = [SDS((S, w), dt) for (w, dt) in row_outs] + [SDS(tuple(s), f32) for s in acc_outs]
    res = pl.pallas_call(
        body, name=name, grid=(S // tm,), in_specs=in_specs, out_specs=out_specs, out_shape=out_shape,
        compiler_params=pltpu.CompilerParams(dimension_semantics=("arbitrary",)),
    )(*rows, *consts)
    return res


def _rowwise_bwd(f, rows, consts, cts, want, row_dtypes, *, adds=None, tm=256, name="rowwise_bwd"):
    n_r, n_c, n_ct = len(rows), len(consts), len(cts)
    adds = adds or {}
    add_keys = sorted(adds)
    add_rows = [adds[k] for k in add_keys]

    def fn(*args):
        r = [a.astype(f32) for a in args[:n_r]]
        ct = [a.astype(f32) for a in args[n_r:n_r + n_ct]]
        ad = args[n_r + n_ct:n_r + n_ct + len(add_rows)]
        c = list(args[n_r + n_ct + len(add_rows):])
        _, vjp = jax.vjp(f, *r, *c)
        g = vjp(ct[0] if n_ct == 1 else tuple(ct))
        d_rows = []
        for pos, i in enumerate(want):
            d = g[i]
            if pos in adds:
                d = d + ad[add_keys.index(pos)].astype(f32)
            d_rows.append(d)
        return (*d_rows, *g[n_r:])

    return _rowwise(fn, list(rows) + list(cts) + add_rows, list(consts),
                    [(rows[i].shape[1], dt) for i, dt in zip(want, row_dtypes)],
                    [c.shape for c in consts], tm=tm, name=name)


def _sum_lead(arr, out_dtype, name):
    n, R, C = arr.shape
    tm = _pick(R, (256, 128, 64, 32, 16, 8))

    def body(a_ref, o_ref):
        acc = a_ref[0].astype(f32)
        for q in range(1, n):
            acc = acc + a_ref[q].astype(f32)
        o_ref[...] = acc.astype(o_ref.dtype)

    return pl.pallas_call(
        body, name=name, grid=(R // tm,), in_specs=[pl.BlockSpec((n, tm, C), lambda i: (0, i, 0))],
        out_specs=pl.BlockSpec((tm, C), lambda i: (i, 0)), out_shape=SDS((R, C), out_dtype),
    )(arr)


_GROUPS = {
    "all": [(kx, ky, kc) for kx in (0, 1) for ky in (0, 1) for kc in (0, 1)][1:],
    "xy": [(0, 1, 0), (1, 0, 0), (1, 1, 0)],
    "c": [(0, 0, 1)],
}


XCHG_CHUNK_MIN_BYTES = 1 << 20
XCHG_CHUNKS = 16


def _group_pos(group, x, y, c):
    return {"all": 4 * x + 2 * y + c, "xy": 2 * x + y, "c": c}[group]


def _xchg(send, group, bcast, name):
    flips = _GROUPS[group]
    n = len(flips) + 1
    piece = tuple(send.shape if bcast else send.shape[1:])
    if not bcast:
        assert send.shape[0] == n
    lead = piece[0]
    nch = 1
    if math.prod(piece) * send.dtype.itemsize >= XCHG_CHUNK_MIN_BYTES:
        nch = lead if len(piece) > 2 else (XCHG_CHUNKS if lead % (XCHG_CHUNKS * 2 * BF16_ROWS) == 0 else 1)
    rows = lead // nch

    def body(s_ref, r_ref, send_sems, recv_sems, own_sems):
        x, y, c = lax.axis_index("x"), lax.axis_index("y"), lax.axis_index("c")
        me = _group_pos(group, x, y, c)

        def src(p, ch):
            ref = s_ref if bcast else s_ref.at[p]
            return ref.at[pl.ds(ch * rows, rows)]

        def dst(ch):
            return r_ref.at[me].at[pl.ds(ch * rows, rows)]

        owns = [pltpu.make_async_copy(src(me, ch), dst(ch), own_sems.at[ch]) for ch in range(nch)]
        for cp in owns:
            cp.start()
        copies = []
        for ch in range(nch):
            for k, (kx, ky, kc) in enumerate(flips):
                tx, ty, tc = x ^ kx, y ^ ky, c ^ kc
                cp = pltpu.make_async_remote_copy(
                    src_ref=src(_group_pos(group, tx, ty, tc), ch), dst_ref=dst(ch),
                    send_sem=send_sems.at[k, ch], recv_sem=recv_sems.at[k, ch],
                    device_id=(tx, ty, tc), device_id_type=pl.DeviceIdType.MESH)
                cp.start()
                copies.append(cp)
        for cp in copies + owns:
            cp.wait()

    return pl.pallas_call(
        body, name=name, out_shape=SDS((n,) + piece, send.dtype),
        in_specs=[pl.BlockSpec(memory_space=pl.ANY)], out_specs=pl.BlockSpec(memory_space=pl.ANY),
        scratch_shapes=[pltpu.SemaphoreType.DMA((n - 1, nch)), pltpu.SemaphoreType.DMA((n - 1, nch)), pltpu.SemaphoreType.DMA((nch,))],
        compiler_params=pltpu.CompilerParams(has_side_effects=True),
    )(send)


PAIR_ROWS = 256


def _pair(arr, mode, out_dtype, name):
    P, N, C = arr.shape
    tm = _largest_divisor(N, PAIR_ROWS, BF16_ROWS)
    nb = N // tm
    assert N % tm == 0 and tm <= PAIR_ROWS
    add = mode == "add"
    assert not add or P == 2

    def body(core_ref, *refs):
        if add:
            keep_ref, send_ref, o_ref, slots, send_sems, recv_sems, credit = refs
        else:
            a_ref, o_ref, stage, slots, send_sems, recv_sems, credit = refs
        i = pl.program_id(0)
        x, y, c = lax.axis_index("x"), lax.axis_index("y"), lax.axis_index("c")
        sib = (x, y, 1 - c)
        slot = lax.rem(i, 2)
        if add:
            src = send_ref.at[0]
        else:
            t = a_ref[0].astype(f32)
            for q in range(1, P):
                t = t + a_ref[q].astype(f32)
            stage[...] = t.astype(stage.dtype)
            o_ref[c] = stage[...]
            src = stage

        @pl.when(i >= 2)
        def _():
            pl.semaphore_wait(credit.at[slot], 1)

        cp = pltpu.make_async_remote_copy(src_ref=src, dst_ref=slots.at[slot], send_sem=send_sems.at[slot], recv_sem=recv_sems.at[slot],
                                          device_id=sib, device_id_type=pl.DeviceIdType.MESH)
        cp.start()
        cp.wait_recv()
        if add:
            o_ref[...] = (keep_ref[0].astype(f32) + slots[slot].astype(f32)).astype(o_ref.dtype)
        else:
            o_ref[1 - c] = slots[slot]
        cp.wait_send()

        @pl.when(i + 2 < nb)
        def _():
            pl.semaphore_signal(credit.at[slot], 1, device_id=sib, device_id_type=pl.DeviceIdType.MESH)

    core = lax.axis_index("c").astype(jnp.int32).reshape(1)
    scratch = [pltpu.VMEM((2, tm, C), arr.dtype if add else out_dtype), pltpu.SemaphoreType.DMA((2,)), pltpu.SemaphoreType.DMA((2,)),
               pltpu.SemaphoreType.REGULAR((2,))]
    if add:
        in_specs = [pl.BlockSpec((1, tm, C), lambda i, core: (core[0], i, 0)), pl.BlockSpec((1, tm, C), lambda i, core: (1 - core[0], i, 0))]
        out_specs = pl.BlockSpec((tm, C), lambda i, core: (i, 0))
        out_shape = SDS((N, C), out_dtype)
        args = (arr, arr)
    else:
        in_specs = [pl.BlockSpec((P, tm, C), lambda i, core: (0, i, 0))]
        out_specs = pl.BlockSpec((2, tm, C), lambda i, core: (0, i, 0))
        out_shape = SDS((2, N, C), out_dtype)
        scratch = [pltpu.VMEM((tm, C), out_dtype)] + scratch
        args = (arr,)
    return pl.pallas_call(
        body, name=name, out_shape=out_shape,
        grid_spec=pltpu.PrefetchScalarGridSpec(num_scalar_prefetch=1, grid=(nb,), in_specs=in_specs, out_specs=out_specs, scratch_shapes=scratch),
        compiler_params=pltpu.CompilerParams(dimension_semantics=("arbitrary",), has_side_effects=True),
    )(core, *args)


def _rms(x, g):
    return x * lax.rsqrt(jnp.mean(x * x, axis=-1, keepdims=True) + EPS) * g


def _normmod(x, g, shift, scale):
    return _rms(x, g) * (1.0 + scale) + shift


def _lane(shape):
    return lax.broadcasted_iota(jnp.int32, shape, 1)


def _partner(x):
    lane = _lane(x.shape)
    lo = (lane >= NOPE) & (lane < NOPE + ROPE // 2)
    hi = (lane >= NOPE + ROPE // 2) & (lane < NOPE + ROPE)
    return jnp.where(lo, pltpu.roll(x, HEAD_PAD - ROPE // 2, 1), jnp.where(hi, pltpu.roll(x, ROPE // 2, 1), 0.0))


@jax.custom_vjp
def _rope(x, cf, ss):
    return x * cf + _partner(x) * ss


def _rope_fwd(x, cf, ss):
    return _rope(x, cf, ss), (cf, ss)


def _rope_bwd(res, dy):
    cf, ss = res
    return dy * cf + _partner(dy * ss), jnp.zeros_like(cf), jnp.zeros_like(ss)


_rope.defvjp(_rope_fwd, _rope_bwd)


def _head_norm(xh, gn, gr):
    lane = _lane(xh.shape)
    x2 = xh * xh
    ms_n = jnp.sum(jnp.where(lane < NOPE, x2, 0.0), axis=-1, keepdims=True) * (1.0 / NOPE)
    ms_r = jnp.sum(jnp.where((lane >= NOPE) & (lane < NOPE + ROPE), x2, 0.0), axis=-1, keepdims=True) * (1.0 / ROPE)
    return xh * (lax.rsqrt(ms_n + EPS) * gn + lax.rsqrt(ms_r + EPS) * gr)


def _q_heads(qraw, cf, ss, gn, gr):
    outs = []
    for h in range(N_HEADS):
        xh = qraw[:, h * HEAD_PAD:(h + 1) * HEAD_PAD]
        outs.append(_rope(_head_norm(xh, gn, gr), cf, ss))
    return jnp.concatenate(outs, axis=1)


def _k_heads(kraw, kr, gn):
    outs = []
    zero = jnp.zeros_like(gn)
    for h in range(N_HEADS):
        xh = kraw[:, h * HEAD_PAD:(h + 1) * HEAD_PAD]
        outs.append(_head_norm(xh, gn, zero) + kr)
    return jnp.concatenate(outs, axis=1)


def _kv_a_post(kva, cf, ss, ga, gr):
    ckv = _rms(kva[:, :KV_RANK], ga)
    kr = _rope(_head_norm(kva[:, KV_RANK:], jnp.zeros_like(gr), gr), cf, ss)
    return ckv, kr


SCAN_UNROLL = 8


def _cmul(ar, ai, br, bi):
    return ar * br - ai * bi, ar * bi + ai * br


def _s5_scan_fwd(u, wbr, wbi, ar, ai, init_r, init_i, wcr, wci, *, want_y, tb, name):
    S, D = u.shape
    ns = D // SLAB_CH
    seg = S // SEGMENTS
    tb = min(tb, seg)
    nb = seg // tb
    R = tb * SEGMENTS
    nt = (((1,), (0,)), ((), ()))

    def body(u_ref, wbr_ref, wbi_ref, ar_ref, ai_ref, ir_ref, ii_ref, wcr_ref, wci_ref, *rest):
        if want_y:
            y_ref, er_ref, ei_ref, bsr_ref, bsi_ref, cr, ci, bur, bui, sr, si = rest
        else:
            er_ref, ei_ref, bsr_ref, bsi_ref, cr, ci, bur, bui, sr, si = rest
        t = pl.program_id(1)

        @pl.when(t == 0)
        def _():
            cr[...] = ir_ref[0]
            ci[...] = ii_ref[0]

        bsr_ref[0, 0] = cr[...]
        bsi_ref[0, 0] = ci[...]
        ub = u_ref[...].astype(bf16)
        bur[...] = lax.dot_general(ub, wbr_ref[0].astype(bf16), nt, preferred_element_type=f32)
        bui[...] = lax.dot_general(ub, wbi_ref[0].astype(bf16), nt, preferred_element_type=f32)
        a_r = jnp.broadcast_to(ar_ref[0], (SEGMENTS, SLAB_ST))
        a_i = jnp.broadcast_to(ai_ref[0], (SEGMENTS, SLAB_ST))

        def step(tau, carry):
            c_r, c_i = carry
            r = pl.multiple_of(tau * SEGMENTS, SEGMENTS)
            p_r, p_i = _cmul(a_r, a_i, c_r, c_i)
            n_r = p_r + bur[pl.ds(r, SEGMENTS), :]
            n_i = p_i + bui[pl.ds(r, SEGMENTS), :]
            sr[pl.ds(r, SEGMENTS), :] = n_r
            si[pl.ds(r, SEGMENTS), :] = n_i
            return n_r, n_i

        c_r, c_i = lax.fori_loop(0, tb, step, (cr[...], ci[...]), unroll=SCAN_UNROLL)
        cr[...] = c_r
        ci[...] = c_i
        if want_y:
            y_ref[...] = (lax.dot_general(sr[...].astype(bf16), wcr_ref[0].astype(bf16), nt, preferred_element_type=f32)
                          - lax.dot_general(si[...].astype(bf16), wci_ref[0].astype(bf16), nt, preferred_element_type=f32))

        @pl.when(t == nb - 1)
        def _():
            er_ref[0] = c_r
            ei_ref[0] = c_i

    slab3 = lambda s: pl.BlockSpec((1,) + s, lambda k, t: (k, 0, 0))
    in_specs = [pl.BlockSpec((R, SLAB_CH), lambda k, t: (t, k)),
                slab3((SLAB_CH, SLAB_ST)), slab3((SLAB_CH, SLAB_ST)), slab3((1, SLAB_ST)), slab3((1, SLAB_ST)),
                slab3((SEGMENTS, SLAB_ST)), slab3((SEGMENTS, SLAB_ST)), slab3((SLAB_ST, SLAB_CH)), slab3((SLAB_ST, SLAB_CH))]
    out_specs = [slab3((SEGMENTS, SLAB_ST)), slab3((SEGMENTS, SLAB_ST)),
                 pl.BlockSpec((1, 1, SEGMENTS, SLAB_ST), lambda k, t: (t, k, 0, 0)),
                 pl.BlockSpec((1, 1, SEGMENTS, SLAB_ST), lambda k, t: (t, k, 0, 0))]
    out_shape = [SDS((ns, SEGMENTS, SLAB_ST), f32)] * 2 + [SDS((nb, ns, SEGMENTS, SLAB_ST), f32)] * 2
    if want_y:
        out_specs = [pl.BlockSpec((R, SLAB_CH), lambda k, t: (t, k))] + out_specs
        out_shape = [SDS((S, D), f32)] + out_shape
    return pl.pallas_call(
        body, name=name, grid=(ns, nb), in_specs=in_specs, out_specs=out_specs, out_shape=out_shape,
        scratch_shapes=[pltpu.VMEM((SEGMENTS, SLAB_ST), f32)] * 2 + [pltpu.VMEM((R, SLAB_ST), f32)] * 4,
        compiler_params=pltpu.CompilerParams(dimension_semantics=("parallel", "arbitrary")),
    )(u, wbr, wbi, ar, ai, init_r, init_i, wcr, wci)


def _s5_scan_bwd(u, dy, wbr, wbi, ar, ai, bs_r, bs_i, ginit_r, ginit_i, wcr, wci, *, full, tb, name):
    S, D = dy.shape
    ns = D // SLAB_CH
    seg = S // SEGMENTS
    tb = min(tb, seg)
    nb = seg // tb
    R = tb * SEGMENTS
    nn = (((1,), (0,)), ((), ()))
    nt = (((1,), (1,)), ((), ()))
    tn = (((0,), (0,)), ((), ()))

    def body(*refs):
        if full:
            (u_ref, dy_ref, wbr_ref, wbi_ref, ar_ref, ai_ref, bsr_ref, bsi_ref, gir_ref, gii_ref, wcr_ref, wci_ref,
             du_ref, dwbr_ref, dwbi_ref, dwcr_ref, dwci_ref, dar_ref, dai_ref, gfr_ref, gfi_ref,
             gr_c, gi_c, bur, bui, sr, si, gsr, gsi) = refs
        else:
            (dy_ref, ar_ref, ai_ref, gir_ref, gii_ref, wcr_ref, wci_ref, gfr_ref, gfi_ref, gr_c, gi_c, gsr, gsi) = refs
        t = pl.program_id(1)

        @pl.when(t == 0)
        def _():
            gr_c[...] = gir_ref[0]
            gi_c[...] = gii_ref[0]
            if full:
                for ref in (dwbr_ref, dwbi_ref, dwcr_ref, dwci_ref, dar_ref, dai_ref):
                    ref[...] = jnp.zeros_like(ref)

        a_r = jnp.broadcast_to(ar_ref[0], (SEGMENTS, SLAB_ST))
        a_i = jnp.broadcast_to(ai_ref[0], (SEGMENTS, SLAB_ST))
        dyb = dy_ref[...].astype(bf16)
        gsr[...] = lax.dot_general(dyb, wcr_ref[0].astype(bf16), nt, preferred_element_type=f32)
        gsi[...] = -lax.dot_general(dyb, wci_ref[0].astype(bf16), nt, preferred_element_type=f32)

        if full:
            ub = u_ref[...].astype(bf16)
            bur[...] = lax.dot_general(ub, wbr_ref[0].astype(bf16), nn, preferred_element_type=f32)
            bui[...] = lax.dot_general(ub, wbi_ref[0].astype(bf16), nn, preferred_element_type=f32)

            def fstep(tau, carry):
                c_r, c_i = carry
                r = pl.multiple_of(tau * SEGMENTS, SEGMENTS)
                p_r, p_i = _cmul(a_r, a_i, c_r, c_i)
                n_r = p_r + bur[pl.ds(r, SEGMENTS), :]
                n_i = p_i + bui[pl.ds(r, SEGMENTS), :]
                sr[pl.ds(r, SEGMENTS), :] = n_r
                si[pl.ds(r, SEGMENTS), :] = n_i
                return n_r, n_i

            lax.fori_loop(0, tb, fstep, (bsr_ref[0, 0], bsi_ref[0, 0]), unroll=SCAN_UNROLL)

        def adj(g_r, g_i):
            return a_r * g_r + a_i * g_i, a_r * g_i - a_i * g_r

        def rstep(k, carry):
            tau = tb - 1 - k
            r = pl.multiple_of(tau * SEGMENTS, SEGMENTS)
            if full:
                g_r, g_i, acc_r, acc_i = carry
            else:
                g_r, g_i = carry
            b_r, b_i = adj(g_r, g_i)
            n_r = b_r + gsr[pl.ds(r, SEGMENTS), :]
            n_i = b_i + gsi[pl.ds(r, SEGMENTS), :]
            gsr[pl.ds(r, SEGMENTS), :] = n_r
            gsi[pl.ds(r, SEGMENTS), :] = n_i
            if not full:
                return n_r, n_i
            rp = pl.multiple_of(jnp.maximum(tau - 1, 0) * SEGMENTS, SEGMENTS)
            first = tau == 0
            p_r = jnp.where(first, bsr_ref[0, 0], sr[pl.ds(rp, SEGMENTS), :])
            p_i = jnp.where(first, bsi_ref[0, 0], si[pl.ds(rp, SEGMENTS), :])
            return n_r, n_i, acc_r + n_r * p_r + n_i * p_i, acc_i + n_i * p_r - n_r * p_i

        zero = jnp.zeros((SEGMENTS, SLAB_ST), f32)
        if full:
            g_r, g_i, acc_r, acc_i = lax.fori_loop(0, tb, rstep, (gr_c[...], gi_c[...], zero, zero), unroll=SCAN_UNROLL)
            dar_ref[0] += acc_r
            dai_ref[0] += acc_i
        else:
            g_r, g_i = lax.fori_loop(0, tb, rstep, (gr_c[...], gi_c[...]), unroll=SCAN_UNROLL)
        gr_c[...] = g_r
        gi_c[...] = g_i

        if full:
            gbr = gsr[...].astype(bf16)
            gbi = gsi[...].astype(bf16)
            du_ref[...] = (lax.dot_general(gbr, wbr_ref[0].astype(bf16), nt, preferred_element_type=f32)
                           + lax.dot_general(gbi, wbi_ref[0].astype(bf16), nt, preferred_element_type=f32))
            dwbr_ref[0] += lax.dot_general(ub, gbr, tn, preferred_element_type=f32)
            dwbi_ref[0] += lax.dot_general(ub, gbi, tn, preferred_element_type=f32)
            dwcr_ref[0] += lax.dot_general(sr[...].astype(bf16), dyb, tn, preferred_element_type=f32)
            dwci_ref[0] -= lax.dot_general(si[...].astype(bf16), dyb, tn, preferred_element_type=f32)

        @pl.when(t == nb - 1)
        def _():
            gfr_ref[0] = g_r
            gfi_ref[0] = g_i

    slab3 = lambda s: pl.BlockSpec((1,) + s, lambda k, t: (k, 0, 0))
    rev_rows = pl.BlockSpec((R, SLAB_CH), lambda k, t: (nb - 1 - t, k))
    rev_bs = pl.BlockSpec((1, 1, SEGMENTS, SLAB_ST), lambda k, t: (nb - 1 - t, k, 0, 0))
    st = slab3((SEGMENTS, SLAB_ST))
    state_sds = SDS((ns, SEGMENTS, SLAB_ST), f32)
    if full:
        in_specs = [rev_rows, rev_rows, slab3((SLAB_CH, SLAB_ST)), slab3((SLAB_CH, SLAB_ST)), slab3((1, SLAB_ST)), slab3((1, SLAB_ST)),
                    rev_bs, rev_bs, st, st, slab3((SLAB_ST, SLAB_CH)), slab3((SLAB_ST, SLAB_CH))]
        args = (u, dy, wbr, wbi, ar, ai, bs_r, bs_i, ginit_r, ginit_i, wcr, wci)
        out_specs = [rev_rows, slab3((SLAB_CH, SLAB_ST)), slab3((SLAB_CH, SLAB_ST)), slab3((SLAB_ST, SLAB_CH)), slab3((SLAB_ST, SLAB_CH)),
                     st, st, st, st]
        out_shape = [SDS((S, D), f32), SDS((ns, SLAB_CH, SLAB_ST), f32), SDS((ns, SLAB_CH, SLAB_ST), f32),
                     SDS((ns, SLAB_ST, SLAB_CH), f32), SDS((ns, SLAB_ST, SLAB_CH), f32)] + [state_sds] * 4
        scratch = [pltpu.VMEM((SEGMENTS, SLAB_ST), f32)] * 2 + [pltpu.VMEM((R, SLAB_ST), f32)] * 6
    else:
        in_specs = [rev_rows, slab3((1, SLAB_ST)), slab3((1, SLAB_ST)), st, st, slab3((SLAB_ST, SLAB_CH)), slab3((SLAB_ST, SLAB_CH))]
        args = (dy, ar, ai, ginit_r, ginit_i, wcr, wci)
        out_specs = [st, st]
        out_shape = [state_sds] * 2
        scratch = [pltpu.VMEM((SEGMENTS, SLAB_ST), f32)] * 2 + [pltpu.VMEM((R, SLAB_ST), f32)] * 2
    return pl.pallas_call(
        body, name=name, grid=(ns, nb), in_specs=in_specs, out_specs=out_specs, out_shape=out_shape, scratch_shapes=scratch,
        compiler_params=pltpu.CompilerParams(dimension_semantics=("parallel", "arbitrary")),
    )(*args)


def _s5_chain(e_r, e_i, ar, ai, *, seg, reverse, name):
    ns = e_r.shape[0]
    assert seg & (seg - 1) == 0

    def body(er_ref, ei_ref, ar_ref, ai_ref, or_ref, oi_ref):
        p_r, p_i = ar_ref[0], ai_ref[0]
        if reverse:
            p_i = -p_i
        for _ in range(seg.bit_length() - 1):
            p_r, p_i = _cmul(p_r, p_i, p_r, p_i)
        c_r = jnp.zeros((1, SLAB_ST), f32)
        c_i = jnp.zeros((1, SLAB_ST), f32)
        order = range(SEGMENTS - 1, -1, -1) if reverse else range(SEGMENTS)
        for j in order:
            or_ref[0, pl.ds(j, 1), :] = c_r
            oi_ref[0, pl.ds(j, 1), :] = c_i
            m_r, m_i = _cmul(p_r, p_i, c_r, c_i)
            c_r = er_ref[0, pl.ds(j, 1), :] + m_r
            c_i = ei_ref[0, pl.ds(j, 1), :] + m_i

    st = pl.BlockSpec((1, SEGMENTS, SLAB_ST), lambda k: (k, 0, 0))
    av = pl.BlockSpec((1, 1, SLAB_ST), lambda k: (k, 0, 0))
    return pl.pallas_call(
        body, name=name, grid=(ns,), in_specs=[st, st, av, av], out_specs=[st, st],
        out_shape=[SDS(e_r.shape, f32)] * 2,
    )(e_r, e_i, ar, ai)


def _s5_disc(lr, li, log_dt):
    dt = jnp.exp(log_dt)
    mag = jnp.exp(lr * dt)
    ab_re = mag * jnp.cos(li * dt)
    ab_im = mag * jnp.sin(li * dt)
    den = lr * lr + li * li
    nr = ab_re - 1.0
    ni = ab_im
    return ab_re, ab_im, (nr * lr + ni * li) / den, (ni * lr - nr * li) / den


def _s5_bbar(b_r, b_i, fr, fi):
    return fr * b_r - fi * b_i, fr * b_i + fi * b_r


def _blockdiag(w, rows, cols):
    g = w.shape[0]
    w = w.reshape(g // 8, 8, rows, cols)
    eye = jnp.eye(8, dtype=w.dtype)
    full = w[:, :, :, None, :] * eye[None, :, None, :, None]
    return full.reshape(g // 8, 8 * rows, 8 * cols)


def _blockdiag_take(w, rows, cols):
    ns = w.shape[0]
    w = w.reshape(ns, 8, rows, 8, cols)
    idx = jnp.arange(8)
    d = w[:, idx, :, idx, :]
    return jnp.moveaxis(d, 0, 1).reshape(ns * 8, rows, cols)


def _mask(i, j, tq, tk):
    row = i * tq + lax.broadcasted_iota(jnp.int32, (tq, tk), 0)
    col = j * tk + lax.broadcasted_iota(jnp.int32, (tq, tk), 1)
    return (row // CHUNK) >= (col // CHUNK)


_NT = (((1,), (1,)), ((), ()))
_NN = (((1,), (0,)), ((), ()))
_TN = (((0,), (0,)), ((), ()))
_NEG = -1e30
ATTN_HEADS_PER_STEP = 2


def _attn_fwd(q, k, v, *, t, name):
    S = q.shape[0]
    t = min(t, S)
    nq = S // t

    hp = ATTN_HEADS_PER_STEP

    def body(q_ref, k_ref, v_ref, o_ref, lse_ref, m_s, acc):
        i = pl.program_id(1)
        m_s[...] = jnp.full_like(m_s, _NEG)
        acc[...] = jnp.zeros_like(acc)

        lane_v = _lane((t, HEAD_PAD))
        ones_col = jnp.where(lane_v == VDIM, 1.0, 0.0).astype(bf16)

        def block(j, masked):
            r = pl.multiple_of(j * t, t)
            heads = [slice(hh * HEAD_PAD, (hh + 1) * HEAD_PAD) for hh in range(hp)]
            scores = []
            for lanes in heads:
                s = lax.dot_general(q_ref[:, lanes], k_ref[pl.ds(r, t), lanes], _NT, preferred_element_type=f32) * ATTN_SCALE
                if masked:
                    s = jnp.where(_mask(0, 0, t, t), s, _NEG)
                scores.append(s)
            probs = []
            for hh, s in enumerate(scores):
                folded = s[:, :LANES]
                for cblk in range(1, t // LANES):
                    folded = jnp.maximum(folded, s[:, cblk * LANES:(cblk + 1) * LANES])
                m_old = m_s[hh]
                m_new = jnp.maximum(m_old, jnp.max(folded, axis=-1, keepdims=True))
                p = jnp.concatenate([jnp.exp(s[:, cblk * LANES:(cblk + 1) * LANES] - m_new) for cblk in range(t // LANES)], axis=1)
                probs.append((m_new, jnp.exp(m_old - m_new), p.astype(bf16)))
            for hh, (m_new, alpha, p) in enumerate(probs):
                vb = v_ref[pl.ds(r, t), heads[hh]] + ones_col
                acc[hh] = alpha * acc[hh] + lax.dot_general(p, vb, _NN, preferred_element_type=f32)
                m_s[hh] = m_new

        def below(j, carry):
            block(j, False)
            return carry

        lax.fori_loop(0, i, below, 0)
        block(i, True)
        for hh in range(hp):
            a = acc[hh]
            l = a[:, VDIM:VDIM + 1]
            o_ref[:, hh * HEAD_PAD:(hh + 1) * HEAD_PAD] = jnp.where(lane_v < VDIM, a / l, 0.0)
            lse_ref[hh] = m_s[hh][:, :1] + jnp.log(l)

    qs = pl.BlockSpec((t, hp * HEAD_PAD), lambda h, i: (i, h))
    ks = pl.BlockSpec((S, hp * HEAD_PAD), lambda h, i: (0, h))
    return pl.pallas_call(
        body, name=name, grid=(N_HEADS // hp, nq), in_specs=[qs, ks, ks],
        out_specs=[qs, pl.BlockSpec((hp, t, 1), lambda h, i: (h, i, 0))],
        out_shape=[SDS((S, N_HEADS * HEAD_PAD), f32), SDS((N_HEADS, S, 1), f32)],
        scratch_shapes=[pltpu.VMEM((hp, t, LANES), f32), pltpu.VMEM((hp, t, HEAD_PAD), f32)],
        compiler_params=pltpu.CompilerParams(dimension_semantics=("parallel", "arbitrary")),
    )(q, k, v)


def _attn_bwd(q, k, v, do, o, lse, *, t, name):
    S = q.shape[0]
    t = min(t, S)
    nq = S // t

    def body(q_ref, k_ref, v_ref, do_ref, o_ref, lse_ref, dq_ref, dk_ref, dv_ref, d_s, dk_acc, dv_acc):
        j = pl.program_id(1)

        @pl.when(j == 0)
        def _():
            dq_ref[...] = jnp.zeros_like(dq_ref)
            d_s[...] = jnp.sum(do_ref[...].astype(f32) * o_ref[...], axis=-1, keepdims=True)

        dk_acc[...] = jnp.zeros_like(dk_acc)
        dv_acc[...] = jnp.zeros_like(dv_acc)
        kb = k_ref[...]
        vb = v_ref[...]

        def block(i, masked):
            r = pl.multiple_of(i * t, t)
            qb = q_ref[pl.ds(r, t), :]
            dob = do_ref[pl.ds(r, t), :]
            s = lax.dot_general(qb, kb, _NT, preferred_element_type=f32) * ATTN_SCALE
            if masked:
                s = jnp.where(_mask(0, 0, t, t), s, _NEG)
            p = jnp.exp(s - lse_ref[0, pl.ds(r, t), :])
            dp = lax.dot_general(dob, vb, _NT, preferred_element_type=f32)
            ds = (p * (dp - d_s[pl.ds(r, t), :]) * ATTN_SCALE).astype(bf16)
            dv_acc[...] += lax.dot_general(p.astype(bf16), dob, _TN, preferred_element_type=f32)
            dk_acc[...] += lax.dot_general(ds, qb, _TN, preferred_element_type=f32)
            dq_ref[pl.ds(r, t), :] += lax.dot_general(ds, kb, _NN, preferred_element_type=f32)

        def below(i, carry):
            block(i, False)
            return carry

        block(j, True)
        lax.fori_loop(j + 1, nq, below, 0)
        dk_ref[...] = dk_acc[...]
        dv_ref[...] = dv_acc[...]

    hs = pl.BlockSpec((S, HEAD_PAD), lambda h, j: (0, h))
    ks = pl.BlockSpec((t, HEAD_PAD), lambda h, j: (j, h))
    ls = pl.BlockSpec((1, S, 1), lambda h, j: (h, 0, 0))
    return pl.pallas_call(
        body, name=name, grid=(N_HEADS, nq), in_specs=[hs, ks, ks, hs, hs, ls], out_specs=[hs, ks, ks],
        out_shape=[SDS((S, N_HEADS * HEAD_PAD), f32)] * 3,
        scratch_shapes=[pltpu.VMEM((S, 1), f32), pltpu.VMEM((t, HEAD_PAD), f32), pltpu.VMEM((t, HEAD_PAD), f32)],
        compiler_params=pltpu.CompilerParams(dimension_semantics=("parallel", "arbitrary")),
    )(q, k, v, do, o, lse)


def _pad_heads(w, per_head, axis):
    w = jnp.moveaxis(w, axis, -1)
    lead = w.shape[:-1]
    w = w.reshape(lead + (N_HEADS, per_head))
    w = jnp.pad(w, [(0, 0)] * len(lead) + [(0, 0), (0, HEAD_PAD - per_head)])
    return jnp.moveaxis(w.reshape(lead + (N_HEADS * HEAD_PAD,)), -1, axis)


def _unpad_heads(w, per_head, axis):
    w = jnp.moveaxis(w, axis, -1)
    lead = w.shape[:-1]
    w = w.reshape(lead + (N_HEADS, HEAD_PAD))[..., :per_head]
    return jnp.moveaxis(w.reshape(lead + (N_HEADS * per_head,)), -1, axis)


def _lanes128(vec, offset):
    return jnp.zeros((1, HEAD_PAD), f32).at[0, offset:offset + vec.shape[0]].set(vec)


def _to_segments(a):
    S, D = a.shape
    return a.reshape(SEGMENTS, S // SEGMENTS, D).transpose(1, 0, 2).reshape(S, D)


def _from_segments(a):
    S, D = a.shape
    return a.reshape(S // SEGMENTS, SEGMENTS, D).transpose(1, 0, 2).reshape(S, D)


def _s5_params_fwd(p):
    G, N, P = p["b_re"].shape
    def disc_body(lr, li, ld, o1, o2, o3, o4):
        o1[...], o2[...], o3[...], o4[...] = _s5_disc(lr[...], li[...], ld[...])

    ab_r, ab_i, f_r, f_i = pl.pallas_call(disc_body, name="s5_disc", out_shape=[SDS((G, N), f32)] * 4)(
        p["lam_re"], p["lam_im"], p["log_dt"].reshape(G, 1))
    bb_r, bb_i = _rowwise(_s5_bbar, [p["b_re"].reshape(G * N, P), p["b_im"].reshape(G * N, P), f_r.reshape(G * N, 1), f_i.reshape(G * N, 1)],
                          [], [(P, f32), (P, f32)], tm=512, name="s5_bbar")
    ns = G // 8
    out = dict(
        f_r=f_r, f_i=f_i,
        a_r=ab_r.reshape(ns, 1, SLAB_ST), a_i=ab_i.reshape(ns, 1, SLAB_ST),
        wb_r=_blockdiag(bb_r.reshape(G, N, P).transpose(0, 2, 1), P, N), wb_i=_blockdiag(bb_i.reshape(G, N, P).transpose(0, 2, 1), P, N),
        wc_r=_blockdiag(p["c_re"].transpose(0, 2, 1), N, P), wc_i=_blockdiag(p["c_im"].transpose(0, 2, 1), N, P),
    )
    return out


def _s5_params_bwd(p, sp, d_ar, d_ai, d_wbr, d_wbi, d_wcr, d_wci):
    G, N, P = p["b_re"].shape
    dbb_r = _blockdiag_take(d_wbr, P, N).transpose(0, 2, 1).reshape(G * N, P)
    dbb_i = _blockdiag_take(d_wbi, P, N).transpose(0, 2, 1).reshape(G * N, P)
    d_cre = _blockdiag_take(d_wcr, N, P).transpose(0, 2, 1)
    d_cim = _blockdiag_take(d_wci, N, P).transpose(0, 2, 1)
    rows = [p["b_re"].reshape(G * N, P), p["b_im"].reshape(G * N, P), sp["f_r"].reshape(G * N, 1), sp["f_i"].reshape(G * N, 1)]
    d_br, d_bi, d_fr, d_fi = _rowwise_bwd(_s5_bbar, rows, [], [dbb_r, dbb_i], [0, 1, 2, 3], [f32] * 4, tm=512, name="s5_bbar_bwd")

    def seg_sum(d, name):
        return _sum_lead(d.transpose(1, 0, 2).reshape(SEGMENTS, G, N), f32, name)

    def body(lr, li, ld, c1, c2, c3, c4, o1, o2, o3):
        _, vjp = jax.vjp(_s5_disc, lr[...], li[...], ld[...])
        o1[...], o2[...], o3[...] = vjp((c1[...], c2[...], c3[...], c4[...]))

    d_lr, d_li, d_ld = pl.pallas_call(
        body, name="s5_disc_bwd", out_shape=[SDS((G, N), f32), SDS((G, N), f32), SDS((G, 1), f32)],
    )(p["lam_re"], p["lam_im"], p["log_dt"].reshape(G, 1), seg_sum(d_ar, "s5_da_re_sum"), seg_sum(d_ai, "s5_da_im_sum"),
      d_fr.reshape(G, N), d_fi.reshape(G, N))
    return dict(lam_re=d_lr, lam_im=d_li, log_dt=d_ld.reshape(G), b_re=d_br.reshape(G, N, P), b_im=d_bi.reshape(G, N, P),
                c_re=d_cre, c_im=d_cim)


_SCAN_TB = 64
_ATTN_T = 512


def _s5_mix_fwd(h, sp, name):
    S = h.shape[0]
    seg = S // SEGMENTS
    zeros = jnp.zeros((h.shape[1] // SLAB_CH, SEGMENTS, SLAB_ST), f32)
    common = (sp["wb_r"], sp["wb_i"], sp["a_r"], sp["a_i"])
    e_r, e_i, _, _ = _s5_scan_fwd(h, *common, zeros, zeros, sp["wc_r"], sp["wc_i"], want_y=False, tb=_SCAN_TB, name=name + "_local")
    i_r, i_i = _s5_chain(e_r, e_i, sp["a_r"], sp["a_i"], seg=seg, reverse=False, name=name + "_chain")
    y, _, _, bs_r, bs_i = _s5_scan_fwd(h, *common, i_r, i_i, sp["wc_r"], sp["wc_i"], want_y=True, tb=_SCAN_TB, name=name)
    return y, (bs_r, bs_i)


def _s5_mix_bwd(h, dy, sp, saved, name):
    S = h.shape[0]
    seg = S // SEGMENTS
    bs_r, bs_i = saved
    zeros = jnp.zeros((h.shape[1] // SLAB_CH, SEGMENTS, SLAB_ST), f32)
    gf_r, gf_i = _s5_scan_bwd(None, dy, None, None, sp["a_r"], sp["a_i"], None, None, zeros, zeros, sp["wc_r"], sp["wc_i"],
                              full=False, tb=_SCAN_TB, name=name + "_local")
    gi_r, gi_i = _s5_chain(gf_r, gf_i, sp["a_r"], sp["a_i"], seg=seg, reverse=True, name=name + "_chain")
    du, d_wbr, d_wbi, d_wcr, d_wci, d_ar, d_ai, _, _ = _s5_scan_bwd(
        h, dy, sp["wb_r"], sp["wb_i"], sp["a_r"], sp["a_i"], bs_r, bs_i, gi_r, gi_i, sp["wc_r"], sp["wc_i"],
        full=True, tb=_SCAN_TB, name=name)
    return du, (d_ar, d_ai, d_wbr, d_wbi, d_wcr, d_wci)


def _ffn_fwd(x1, mod, n2g, w_gu, w_down, tag):
    F = w_down.shape[0]
    h2 = _rowwise(lambda x, g, m: _normmod(x, g, m[3:4], m[4:5]), [x1], [n2g, mod], [(x1.shape[1], bf16)], name=tag + "_norm2")[0]
    ab = _mm(h2, w_gu, out_dtype=bf16, name=tag + "_gu")
    act = _rowwise(lambda ab: jax.nn.silu(ab[:, :F].astype(f32)) * ab[:, F:].astype(f32), [ab], [], [(F, bf16)], name=tag + "_act")[0]
    f = _mm(act, w_down, name=tag + "_down")
    x2 = _rowwise(lambda x, f, m: x + m[5:6] * f, [x1, f], [mod], [(x1.shape[1], f32)], name=tag + "_res2")[0]
    return x2, dict(x1=x1, h2=h2, ab=ab, act=act, f=f)


def _ffn_bwd(dx2, sv, mod, n2g, w_gu, w_down, tag):
    F = w_down.shape[0]
    D = dx2.shape[1]
    df, dgate2 = _rowwise(lambda dx, f, m: (m[5:6] * dx, jnp.sum(dx * f, axis=0, keepdims=True)), [dx2, sv["f"]], [mod],
                          [(D, bf16)], [(1, D)], name=tag + "_res2_bwd")
    dact = _mm(df, w_down, tb=True, out_dtype=bf16, name=tag + "_down_dx")
    dw_down = _mm(sv["act"], df, ta=True, name=tag + "_down_dw")
    dab = _rowwise_bwd(lambda ab: jax.nn.silu(ab[:, :F]) * ab[:, F:], [sv["ab"]], [], [dact], [0], [bf16], name=tag + "_act_bwd")[0]
    dh2 = _mm(dab, w_gu, tb=True, name=tag + "_gu_dx")
    dw_gu = _mm(sv["h2"], dab, ta=True, name=tag + "_gu_dw")
    dx1, dn2g, dmod = _rowwise_bwd(lambda x, g, m: _normmod(x, g, m[3:4], m[4:5]), [sv["x1"]], [n2g, mod], [dh2], [0], [f32],
                                   adds={0: dx2}, name=tag + "_norm2_bwd")
    dmod = dmod.at[5:6].add(dgate2)
    return dx1, dw_gu, dw_down, dn2g, dmod


def _device_step(x, target, pos, mods, kmod, W):
    S, D = x.shape
    F = W["ffn_w_down"].shape[1]
    inv = 1.0 / (ROPE_THETA ** (jnp.arange(0, ROPE, 2, dtype=f32) / ROPE))
    inv128 = _lanes128(jnp.concatenate([inv, inv]), NOPE)
    sign128 = _lanes128(jnp.concatenate([-jnp.ones(ROPE // 2, f32), jnp.ones(ROPE // 2, f32)]), NOPE)
    cf, ss = _rowwise(lambda p, iv, sg: (jnp.cos(p * iv), jnp.sin(p * iv) * sg), [pos], [inv128, sign128],
                      [(HEAD_PAD, f32), (HEAD_PAD, f32)], name="rope_table")

    w_gu = W.get("ffn_w_gu") or [jnp.concatenate([W["ffn_w_gate"][l], W["ffn_w_up"][l]], axis=1) for l in range(DEPTH)]
    g_gu = [None] * DEPTH
    saved = []
    xs = _to_segments(x)

    s5p = []
    for l in range(N_A):
        tag = f"l{l}"
        mod = mods[l]
        p = {k: W["s5_" + k][l] for k in ("lam_re", "lam_im", "log_dt", "b_re", "b_im", "c_re", "c_im")}
        sp = _s5_params_fwd(p)
        s5p.append((p, sp))
        n1g = W["norm1_g"][l][None]
        dsk = W["s5_d"][l][None]
        bgl = W["s5_b_glu"][l][None]
        h = _rowwise(lambda x, g, m: _normmod(x, g, m[0:1], m[1:2]), [xs], [n1g, mod], [(D, f32)], name=tag + "_norm1")[0]
        y_ssm, scan_saved = _s5_mix_fwd(h, sp, tag + "_scan")
        g = _rowwise(lambda ys, h, d: jax.nn.gelu(ys + d * h), [y_ssm, h], [dsk], [(D, f32)], name=tag + "_gelu")[0]
        z = _mm(g, W["s5_w_glu"][l], name=tag + "_glu")
        x1 = _rowwise(lambda x, g, z, b, m: x + m[2:3] * (g * jax.nn.sigmoid(z + b)), [xs, g, z], [bgl, mod], [(D, f32)], name=tag + "_res1")[0]
        x2, fsv = _ffn_fwd(x1, mod, W["norm2_g"][l][None], w_gu[l], W["ffn_w_down"][l], tag)
        saved.append(dict(x0=xs, h=h, y_ssm=y_ssm, g=g, z=z, scan=scan_saved, ffn=fsv))
        xs = x2
    x = _from_segments(xs)

    w_kv_a = jnp.concatenate([W["w_kv_a"][:, :KV_RANK], jnp.zeros((D, NOPE), f32).astype(W["w_kv_a"].dtype), W["w_kv_a"][:, KV_RANK:],
                              jnp.zeros((D, HEAD_PAD - NOPE - ROPE), f32).astype(W["w_kv_a"].dtype)], axis=1)
    wkb = W["w_kv_b"].reshape(KV_RANK, N_HEADS, NOPE + VDIM)
    w_kb = _pad_heads(wkb[:, :, :NOPE].reshape(KV_RANK, N_HEADS * NOPE), NOPE, 1)
    w_vb = _pad_heads(wkb[:, :, NOPE:].reshape(KV_RANK, N_HEADS * VDIM), VDIM, 1)
    kvg = W["kv_norm_g"][None]
    ga = W["kv_a_norm_g"][None]
    gkr = _lanes128(W["k_rope_norm_g"], NOPE)
    gkn = _lanes128(W["k_nope_norm_g"], 0)
    hk = _rowwise(lambda x, g, m: _normmod(x, g, m[0:1], m[1:2]), [x], [kvg, kmod], [(D, bf16)], name="kv_norm")[0]
    kva = _mm(hk, w_kv_a, name="kv_a")
    ckv, kr = _rowwise(_kv_a_post, [kva, cf, ss], [ga, gkr], [(KV_RANK, f32), (HEAD_PAD, f32)], name="kv_a_post")
    kraw = _mm(ckv, w_kb, name="kv_bk")
    vpad = _mm(ckv, w_vb, out_dtype=bf16, name="kv_bv")
    kpad = _rowwise(_k_heads, [kraw, kr], [gkn], [(N_HEADS * HEAD_PAD, bf16)], name="k_heads")[0]
    kv_saved = dict(x=x, hk=hk, kva=kva, ckv=ckv, kr=kr, kraw=kraw)

    for l in range(N_A, DEPTH):
        tag = f"l{l}"
        j = l - N_A
        mod = mods[l]
        n1g = W["norm1_g"][l][None]
        w_uq = _pad_heads(W["mla_w_uq"][j], NOPE + ROPE, 1)
        w_o = _pad_heads(W["mla_w_o"][j], VDIM, 0)
        qg = W["mla_q_norm_g"][j][None]
        gqn = _lanes128(W["mla_q_nope_norm_g"][j], 0)
        gqr = _lanes128(W["mla_q_rope_norm_g"][j], NOPE)
        h = _rowwise(lambda x, g, m: _normmod(x, g, m[0:1], m[1:2]), [x], [n1g, mod], [(D, bf16)], name=tag + "_norm1")[0]
        qa = _mm(h, W["mla_w_dq"][j], name=tag + "_dq")
        qc = _rowwise(_rms, [qa], [qg], [(qa.shape[1], f32)], name=tag + "_qnorm")[0]
        qraw = _mm(qc, w_uq, name=tag + "_uq")
        q = _rowwise(_q_heads, [qraw, cf, ss], [gqn, gqr], [(N_HEADS * HEAD_PAD, bf16)], name=tag + "_q_heads")[0]
        o, lse = _attn_fwd(q, kpad, vpad, t=_ATTN_T, name=tag + "_attn")
        mix = _mm(o, w_o, name=tag + "_wo")
        x1 = _rowwise(lambda x, mx, m: x + m[2:3] * mx, [x, mix], [mod], [(D, f32)], name=tag + "_res1")[0]
        x2, fsv = _ffn_fwd(x1, mod, W["norm2_g"][l][None], w_gu[l], W["ffn_w_down"][l], tag)
        saved.append(dict(x0=x, h=h, qa=qa, qc=qc, qraw=qraw, q=q, o=o, lse=lse, mix=mix, w_uq=w_uq, w_o=w_o, ffn=fsv))
        x = x2

    dx, loss = _rowwise(
        lambda y, t: ((y - t) * (1.0 / D), jnp.full((1, LANES), 0.5 * jnp.sum(jnp.mean(jnp.square(y - t), axis=-1)), f32)),
        [x, target], [], [(D, f32)], [(1, LANES)], name="loss")

    gW = {}
    dmods = [None] * DEPTH
    g_gate, g_up, g_down, g_n1, g_n2 = [None] * DEPTH, [None] * DEPTH, [None] * DEPTH, [None] * DEPTH, [None] * DEPTH
    dks, dvs = [], []
    g_dq, g_qn, g_uq, g_qnn, g_qrn, g_wo = [None] * 2, [None] * 2, [None] * 2, [None] * 2, [None] * 2, [None] * 2
    for l in range(DEPTH - 1, N_A - 1, -1):
        tag = f"l{l}"
        j = l - N_A
        sv = saved[l]
        mod = mods[l]
        dx1, dw_gu, dw_down, g_n2[l], dmod = _ffn_bwd(dx, sv["ffn"], mod, W["norm2_g"][l][None], w_gu[l], W["ffn_w_down"][l], tag)
        g_gate[l], g_up[l], g_down[l], g_gu[l] = dw_gu[:, :F], dw_gu[:, F:], dw_down, dw_gu
        dmix, dgate1 = _rowwise(lambda dx, mx, m: (m[2:3] * dx, jnp.sum(dx * mx, axis=0, keepdims=True)), [dx1, sv["mix"]], [mod],
                                [(D, bf16)], [(1, D)], name=tag + "_res1_bwd")
        do = _mm(dmix, sv["w_o"], tb=True, out_dtype=bf16, name=tag + "_wo_dx")
        g_wo[j] = _unpad_heads(_mm(sv["o"], dmix, ta=True, name=tag + "_wo_dw"), VDIM, 0)
        dq, dk, dv = _attn_bwd(sv["q"], kpad, vpad, do, sv["o"], sv["lse"], t=_ATTN_T, name=tag + "_attn_bwd")
        dks.append(dk)
        dvs.append(dv)
        gqn = _lanes128(W["mla_q_nope_norm_g"][j], 0)
        gqr = _lanes128(W["mla_q_rope_norm_g"][j], NOPE)
        dqraw, dgqn, dgqr = _rowwise_bwd(lambda qr, c, s, a, b: _q_heads(qr, c, s, a, b), [sv["qraw"], cf, ss], [gqn, gqr], [dq], [0], [bf16],
                                         name=tag + "_q_heads_bwd")
        g_qnn[j], g_qrn[j] = dgqn[0, :NOPE], dgqr[0, NOPE:NOPE + ROPE]
        dqc = _mm(dqraw, sv["w_uq"], tb=True, name=tag + "_uq_dx")
        g_uq[j] = _unpad_heads(_mm(sv["qc"], dqraw, ta=True, name=tag + "_uq_dw"), NOPE + ROPE, 1)
        qg = W["mla_q_norm_g"][j][None]
        dqa, dqg = _rowwise_bwd(_rms, [sv["qa"]], [qg], [dqc], [0], [bf16], name=tag + "_qnorm_bwd")
        g_qn[j] = dqg[0]
        dh = _mm(dqa, W["mla_w_dq"][j], tb=True, name=tag + "_dq_dx")
        g_dq[j] = _mm(sv["h"], dqa, ta=True, name=tag + "_dq_dw")
        dx, dn1g, dmod1 = _rowwise_bwd(lambda x, g, m: _normmod(x, g, m[0:1], m[1:2]), [sv["x0"]], [W["norm1_g"][l][None], mod], [dh], [0], [f32],
                                       adds={0: dx1}, name=tag + "_norm1_bwd")
        g_n1[l] = dn1g[0]
        dmods[l] = (dmod + dmod1).at[2:3].add(dgate1)

    gkn = _lanes128(W["k_nope_norm_g"], 0)
    dk_sum = _sum_lead(jnp.stack(dks), f32, "dk_sum")
    dv_sum = _sum_lead(jnp.stack(dvs), bf16, "dv_sum")
    dkraw, dkr, dgkn = _rowwise_bwd(lambda kr_, r, g: _k_heads(kr_, r, g), [kv_saved["kraw"], kv_saved["kr"]], [gkn], [dk_sum], [0, 1], [bf16, f32],
                                    name="k_heads_bwd")
    dckv = _sum_lead(jnp.stack([_mm(dkraw, w_kb, tb=True, name="kv_bk_dx"), _mm(dv_sum, w_vb, tb=True, name="kv_bv_dx")]), f32, "dckv_sum")
    g_kb = _unpad_heads(_mm(kv_saved["ckv"], dkraw, ta=True, name="kv_bk_dw"), NOPE, 1)
    g_vb = _unpad_heads(_mm(kv_saved["ckv"], dv_sum, ta=True, name="kv_bv_dw"), VDIM, 1)
    gW["w_kv_b"] = jnp.concatenate([g_kb.reshape(KV_RANK, N_HEADS, NOPE), g_vb.reshape(KV_RANK, N_HEADS, VDIM)], axis=2).reshape(KV_RANK, -1)
    dkva, dga, dgkr = _rowwise_bwd(_kv_a_post, [kv_saved["kva"], cf, ss], [ga, gkr], [dckv, dkr], [0], [bf16], name="kv_a_post_bwd")
    dhk = _mm(dkva, w_kv_a, tb=True, name="kv_a_dx")
    g_kva = _mm(kv_saved["hk"], dkva, ta=True, name="kv_a_dw")
    gW["w_kv_a"] = jnp.concatenate([g_kva[:, :KV_RANK], g_kva[:, KV_RANK + NOPE:KV_RANK + NOPE + ROPE]], axis=1)
    dx, dkvg, dkmod = _rowwise_bwd(lambda x, g, m: _normmod(x, g, m[0:1], m[1:2]), [kv_saved["x"]], [kvg, kmod], [dhk], [0], [f32],
                                   adds={0: dx}, name="kv_norm_bwd")
    gW["kv_norm_g"], gW["kv_a_norm_g"] = dkvg[0], dga[0]
    gW["k_nope_norm_g"], gW["k_rope_norm_g"] = dgkn[0, :NOPE], dgkr[0, NOPE:NOPE + ROPE]

    dxs = _to_segments(dx)
    g_s5 = {k: [None] * N_A for k in ("lam_re", "lam_im", "log_dt", "b_re", "b_im", "c_re", "c_im", "d", "w_glu", "b_glu")}
    for l in range(N_A - 1, -1, -1):
        tag = f"l{l}"
        sv = saved[l]
        mod = mods[l]
        p, sp = s5p[l]
        dsk = W["s5_d"][l][None]
        bgl = W["s5_b_glu"][l][None]
        dx1, dw_gu, dw_down, g_n2[l], dmod = _ffn_bwd(dxs, sv["ffn"], mod, W["norm2_g"][l][None], w_gu[l], W["ffn_w_down"][l], tag)
        g_gate[l], g_up[l], g_down[l], g_gu[l] = dw_gu[:, :F], dw_gu[:, F:], dw_down, dw_gu

        def res1(g, z, b, m):
            return m[2:3] * (g * jax.nn.sigmoid(z + b))
        dg1, dz, dbgl, dmod_g = _rowwise_bwd(res1, [sv["g"], sv["z"]], [bgl, mod], [dx1], [0, 1], [f32, bf16], name=tag + "_res1_bwd")
        dg2 = _mm(dz, W["s5_w_glu"][l], tb=True, name=tag + "_glu_dx")
        g_s5["w_glu"][l] = _mm(sv["g"], dz, ta=True, name=tag + "_glu_dw")
        g_s5["b_glu"][l] = dbgl[0]

        def gelu_bwd(ys, h, dga_, dgb_, d):
            _, vjp = jax.vjp(lambda ys, h, d: jax.nn.gelu(ys + d * h), ys, h, d)
            dys, dh, dd = vjp(dga_ + dgb_)
            return dys, dh, dd
        dys, dh_a, ddsk = _rowwise(gelu_bwd, [sv["y_ssm"], sv["h"], dg1, dg2], [dsk], [(D, f32), (D, f32)], [(1, D)], name=tag + "_gelu_bwd")
        g_s5["d"][l] = ddsk[0]
        du, scan_g = _s5_mix_bwd(sv["h"], dys, sp, sv["scan"], tag + "_scan_bwd")
        pg = _s5_params_bwd(p, sp, *scan_g)
        for k in ("lam_re", "lam_im", "log_dt", "b_re", "b_im", "c_re", "c_im"):
            g_s5[k][l] = pg[k]
        dxs, dn1g, dmod1 = _rowwise_bwd(lambda x, g, m: _normmod(x, g, m[0:1], m[1:2]), [sv["x0"]], [W["norm1_g"][l][None], mod],
                                        [_sum_lead(jnp.stack([dh_a, du]), f32, tag + "_dh_sum")], [0], [f32], adds={0: dx1}, name=tag + "_norm1_bwd")
        g_n1[l] = dn1g[0]
        dmods[l] = dmod + dmod1 + dmod_g
    dx = _from_segments(dxs)

    gW.update(
        norm1_g=jnp.stack(g_n1), norm2_g=jnp.stack([g[0] for g in g_n2]),
        ffn_w_gate=jnp.stack(g_gate), ffn_w_up=jnp.stack(g_up), ffn_w_down=jnp.stack(g_down), ffn_w_gu=g_gu,
        mla_w_dq=jnp.stack(g_dq), mla_q_norm_g=jnp.stack(g_qn), mla_w_uq=jnp.stack(g_uq),
        mla_q_nope_norm_g=jnp.stack(g_qnn), mla_q_rope_norm_g=jnp.stack(g_qrn), mla_w_o=jnp.stack(g_wo),
    )
    for k, v in g_s5.items():
        gW["s5_" + k] = jnp.stack(v)
    return loss, dx, gW, jnp.stack(dmods), dkmod


_WEIGHTS = ['ada_w', 'ada_b', 'norm1_g', 'norm2_g', 'ffn_w_gate', 'ffn_w_up', 'ffn_w_down', 's5_lam_re', 's5_lam_im', 's5_log_dt',
            's5_b_re', 's5_b_im', 's5_c_re', 's5_c_im', 's5_d', 's5_w_glu', 's5_b_glu', 'kv_ada_w', 'kv_ada_b', 'kv_norm_g', 'w_kv_a',
            'kv_a_norm_g', 'w_kv_b', 'k_nope_norm_g', 'k_rope_norm_g', 'mla_w_dq', 'mla_q_norm_g', 'mla_w_uq', 'mla_q_nope_norm_g',
            'mla_q_rope_norm_g', 'mla_w_o']
_BIG = {"ffn_w_gate": 2, "ffn_w_up": 2, "ffn_w_down": 1, "s5_w_glu": 1, "w_kv_a": 0, "w_kv_b": 1, "mla_w_dq": 1, "mla_w_uq": 2, "mla_w_o": 1}
_TENSOR_PARALLEL = ("ada_w", "kv_ada_w")
_SHARDED_VECS = ("s5_d", "s5_b_glu")
_CHIPS = 4
_FLAT_W = 1024
_FLAT_ALIGN = 512
_W_GROUPS = (("gu", ("ffn_w_gate", "ffn_w_up")), ("rows", ("ffn_w_down", "s5_w_glu", "mla_w_o")),
             ("misc", ("w_kv_a", "w_kv_b", "mla_w_dq", "mla_w_uq")))


def _shard_rows(arrs, flat):
    if flat:
        return _pack_rows(arrs, _FLAT_W, _FLAT_ALIGN)
    C = arrs[0].shape[-1]
    return jnp.concatenate([a.reshape(-1, C) for a in arrs], axis=0)


def _take_rows(halves, r0, r1):
    R2 = halves.shape[2]
    pieces = []
    if r0 < R2:
        pieces.append(halves[0][:, r0:min(r1, R2)])
    if r1 > R2:
        pieces.append(halves[1][:, max(r0, R2) - R2:r1 - R2])
    return pieces[0] if len(pieces) == 1 else jnp.concatenate(pieces, axis=1)


def _pack_rows(arrs, width, align):
    flat = jnp.concatenate([a.reshape(-1) for a in arrs])
    rows = -(-flat.shape[0] // (width * align)) * align
    flat = jnp.pad(flat, (0, rows * width - flat.shape[0]))
    return flat.reshape(rows, width)


def _unpack(flat, shapes):
    flat = flat.reshape(-1)
    out, off = [], 0
    for s in shapes:
        n = math.prod(s)
        out.append(flat[off:off + n].reshape(s))
        off += n
    return out


def _adamw(w, g, m, v, name):
    c1 = 1.0 - ADAM_B1 ** ADAM_STEP
    c2 = 1.0 - ADAM_B2 ** ADAM_STEP

    def fn(w, g, m, v):
        m = ADAM_B1 * m + (1.0 - ADAM_B1) * g
        v = ADAM_B2 * v + (1.0 - ADAM_B2) * jnp.square(g)
        delta = -ADAM_LR * ((m / c1) / (jnp.sqrt(v / c2) + ADAM_EPS) + ADAM_WD * w)
        return delta, m, v

    C = w.shape[1]
    tm = _pick(w.shape[0], (256, 128, 64, 32, 16, 8))
    return _rowwise(fn, [w, g, m, v], [], [(C, f32)] * 3, tm=tm, name=name)


def kernel(x, c, positions, ada_w, ada_b, norm1_g, norm2_g, ffn_w_gate, ffn_w_up, ffn_w_down, s5_lam_re, s5_lam_im, s5_log_dt, s5_b_re, s5_b_im, s5_c_re, s5_c_im, s5_d, s5_w_glu, s5_b_glu, kv_ada_w, kv_ada_b, kv_norm_g, w_kv_a, kv_a_norm_g, w_kv_b, k_nope_norm_g, k_rope_norm_g, mla_w_dq, mla_q_norm_g, mla_w_uq, mla_q_nope_norm_g, mla_q_rope_norm_g, mla_w_o, loss_target, m_ada_w, m_ada_b, m_norm1_g, m_norm2_g, m_ffn_w_gate, m_ffn_w_up, m_ffn_w_down, m_s5_lam_re, m_s5_lam_im, m_s5_log_dt, m_s5_b_re, m_s5_b_im, m_s5_c_re, m_s5_c_im, m_s5_d, m_s5_w_glu, m_s5_b_glu, m_kv_ada_w, m_kv_ada_b, m_kv_norm_g, m_w_kv_a, m_kv_a_norm_g, m_w_kv_b, m_k_nope_norm_g, m_k_rope_norm_g, m_mla_w_dq, m_mla_q_norm_g, m_mla_w_uq, m_mla_q_nope_norm_g, m_mla_q_rope_norm_g, m_mla_w_o, v_ada_w, v_ada_b, v_norm1_g, v_norm2_g, v_ffn_w_gate, v_ffn_w_up, v_ffn_w_down, v_s5_lam_re, v_s5_lam_im, v_s5_log_dt, v_s5_b_re, v_s5_b_im, v_s5_c_re, v_s5_c_im, v_s5_d, v_s5_w_glu, v_s5_b_glu, v_kv_ada_w, v_kv_ada_b, v_kv_norm_g, v_w_kv_a, v_kv_a_norm_g, v_w_kv_b, v_k_nope_norm_g, v_k_rope_norm_g, v_mla_w_dq, v_mla_q_norm_g, v_mla_w_uq, v_mla_q_nope_norm_g, v_mla_q_rope_norm_g, v_mla_w_o):
    given = dict(locals())
    w = {n: given[n] for n in _WEIGHTS}
    m_in = {n: given["m_" + n] for n in _WEIGHTS}
    v_in = {n: given["v_" + n] for n in _WEIGHTS}
    S, D = x.shape[1], x.shape[2]
    ax, ay, ac = lax.axis_index("x"), lax.axis_index("y"), lax.axis_index("c")
    chip = 2 * ax + ay
    me = 4 * ax + 2 * ay + ac
    n_ada = ada_w.shape[2]
    n_kada = kv_ada_w.shape[1]

    g1 = _xchg(_pack_rows([c, s5_d, s5_b_glu], LANES, SUBLANES), "all", True, "gather_cond")
    g1 = g1.reshape(8, -1)
    c_all = g1[:, :D]
    vec = g1[0::2, D:D + 2 * s5_d.size].reshape(_CHIPS, 2, N_A, s5_d.shape[1])
    full_vecs = {"s5_d": vec[:, 0].transpose(1, 0, 2).reshape(N_A, D), "s5_b_glu": vec[:, 1].transpose(1, 0, 2).reshape(N_A, D)}
    c_act = _rowwise(jax.nn.silu, [c_all], [], [(D, f32)], tm=8, name="cond_silu")[0]

    mod_cols = [_mm(c_act, ada_w[l], cast=False, name="ada_proj") for l in range(DEPTH)] + [_mm(c_act, kv_ada_w, cast=False, name="kv_ada_proj")]
    g2 = _xchg(_pack_rows([jnp.concatenate(mod_cols, axis=1)], LANES, SUBLANES), "all", True, "gather_mod")
    g2 = g2.reshape(8, 8, -1)[0::2]
    mine = lax.dynamic_index_in_dim(g2, me, axis=1, keepdims=False)
    mod_lin = mine[:, :DEPTH * n_ada].reshape(_CHIPS, DEPTH, n_ada).transpose(1, 0, 2).reshape(DEPTH, 6 * D)
    kmod_lin = mine[:, DEPTH * n_ada:].reshape(1, 2 * D)
    mods = _rowwise(lambda a, b: a + b, [mod_lin, ada_b], [], [(6 * D, f32)], tm=DEPTH, name="ada_bias")[0].reshape(DEPTH, 6, D)
    kmod = _rowwise(lambda a, b: a + b, [kmod_lin, kv_ada_b.reshape(1, 2 * D)], [], [(2 * D, f32)], tm=1, name="kv_ada_bias")[0].reshape(2, D)

    chip_shards = {}
    for g, names in _W_GROUPS:
        buf = _shard_rows([w[n] for n in names], g == "misc").astype(bf16)
        Rg, C = buf.shape
        R2 = Rg // 2
        my_half = lax.dynamic_slice_in_dim(buf, ac * R2, R2, axis=0)
        halves_in = _xchg(my_half, "xy", True, "gather_" + g)
        both = _pair(halves_in.reshape(1, _CHIPS * R2, C), "swap", bf16, "gather_pair_" + g).reshape(2, _CHIPS, R2, C)
        if g == "misc":
            chip_shards.update(zip(names, _unpack_chips(both.reshape(2, _CHIPS, R2 * C), [w[n].shape for n in names])))
        else:
            r = 0
            for n in names:
                rows = math.prod(w[n].shape[:-1])
                chip_shards[n] = _take_rows(both, r, r + rows).reshape((_CHIPS,) + w[n].shape)
                r += rows
    W = {n: jnp.concatenate([chip_shards[n][q] for q in range(_CHIPS)], axis=_BIG[n]) for n in _BIG if n not in _W_GROUPS[0][1]}
    W["ffn_w_gu"] = [jnp.concatenate([chip_shards[n][q, l] for n in _W_GROUPS[0][1] for q in range(_CHIPS)], axis=1) for l in range(DEPTH)]
    for n in _WEIGHTS:
        if n not in _BIG and n not in _TENSOR_PARALLEL and n not in _SHARDED_VECS and n not in ("ada_b", "kv_ada_b"):
            W[n] = w[n]
    W.update(full_vecs)

    pos = positions.reshape(S, 1).astype(f32)
    loss, dx, gW, dmods, dkmod = _device_step(x[0], loss_target[0], pos, mods, kmod, W)

    g4 = _xchg(_pack_rows([dmods, dkmod], LANES, SUBLANES), "all", True, "gather_dmod")
    ada_b_sum = _sum_lead(g4, f32, "dmod_sum").reshape(-1)
    g_ada_b = ada_b_sum[:DEPTH * 6 * D].reshape(DEPTH, 6 * D)
    g_kv_ada_b = ada_b_sum[DEPTH * 6 * D:DEPTH * 6 * D + 2 * D]
    dm_all = g4.reshape(8, -1)
    dm_ada = lax.dynamic_slice_in_dim(dm_all[:, :DEPTH * 6 * D].reshape(8, DEPTH, 6 * D), chip * n_ada, n_ada, axis=2)
    dm_kv = lax.dynamic_slice_in_dim(dm_all[:, DEPTH * 6 * D:DEPTH * 6 * D + 2 * D], chip * n_kada, n_kada, axis=1)

    def outer(at, b):
        acc = at[:, 0:1] * b[0:1, :]
        for i in range(1, 8):
            acc = acc + at[:, i:i + 1] * b[i:i + 1, :]
        return acc

    c_act_t = c_act.T
    g_ada_w = jnp.stack([_rowwise(outer, [c_act_t], [dm_ada[:, l]], [(n_ada, f32)], name="ada_dw")[0] for l in range(DEPTH)])
    g_kv_ada_w = _rowwise(outer, [c_act_t], [dm_kv], [(n_kada, f32)], name="kv_ada_dw")[0]

    g_big = {}
    for g, names in _W_GROUPS:
        if g == "gu":
            Fq = w[names[0]].shape[-1]
            halves = jnp.stack([jnp.stack([jnp.concatenate([gW["ffn_w_gu"][l][:, (h * _CHIPS + q) * Fq:(h * _CHIPS + q + 1) * Fq]
                                                            for l in range(DEPTH)], axis=0) for q in range(_CHIPS)]) for h in range(2)])
            halves = halves.astype(bf16)
        else:
            per_chip = [_shard_rows([jnp.split(gW[n], _CHIPS, axis=_BIG[n])[q] for n in names], g == "misc") for q in range(_CHIPS)]
            full = jnp.stack(per_chip).astype(bf16)
            halves = full.reshape(_CHIPS, 2, full.shape[1] // 2, full.shape[2]).transpose(1, 0, 2, 3)
        _, _, R2, C = halves.shape
        hsum = _pair(halves.reshape(2, _CHIPS * R2, C), "add", bf16, "grads_pair_" + g).reshape(_CHIPS, R2, C)
        rb = _xchg(hsum, "xy", False, "grads_chips_" + g)
        red = _pair(rb, "swap", f32, "grads_halves_" + g).reshape(2 * R2, C)
        if g == "misc":
            g_big.update(zip(names, _unpack(red, [w[n].shape for n in names])))
        else:
            r = 0
            for n in names:
                rows = math.prod(w[n].shape[:-1])
                g_big[n] = red[r:r + rows].reshape(w[n].shape)
                r += rows

    small = [n for n in _WEIGHTS if n not in _BIG and n not in _TENSOR_PARALLEL and n not in ("ada_b", "kv_ada_b")]
    svec = _pack_rows([loss[0, :1]] + [gW[n] for n in small], LANES, 8 * SUBLANES)
    rows8 = svec.shape[0] // 8
    rs = _xchg(svec.reshape(8, rows8, LANES), "all", False, "small_scatter")
    red = _sum_lead(rs, f32, "small_sum")
    full = _xchg(red, "all", True, "small_gather").reshape(-1)
    loss_tot = full[0]
    g_small = dict(zip(small, _unpack(full[1:], [gW[n].shape for n in small])))
    for n in _SHARDED_VECS:
        g_small[n] = lax.dynamic_slice_in_dim(g_small[n], chip * w[n].shape[1], w[n].shape[1], axis=1)
    g_small["ada_b"] = g_ada_b
    g_small["kv_ada_b"] = g_kv_ada_b

    grads = dict(g_big)
    grads.update(g_small)
    grads["ada_w"] = g_ada_w
    grads["kv_ada_w"] = g_kv_ada_w

    delta, new_m, new_v = {}, {}, {}
    natural = _TENSOR_PARALLEL + _W_GROUPS[0][1] + _W_GROUPS[1][1]
    for n in natural:
        C = w[n].shape[-1]
        d_, m_, v_ = _adamw(w[n].reshape(-1, C), grads[n].reshape(-1, C), m_in[n].reshape(-1, C), v_in[n].reshape(-1, C), "adamw_" + n)
        delta[n], new_m[n], new_v[n] = d_.reshape(w[n].shape), m_.reshape(w[n].shape), v_.reshape(w[n].shape)
    rest = [n for n in _WEIGHTS if n not in natural]
    d_, m_, v_ = _adamw(*[_pack_rows([src[n] for n in rest], _FLAT_W, SUBLANES) for src in (w, grads, m_in, v_in)], "adamw_small")
    shapes = [w[n].shape for n in rest]
    for res, src in ((delta, d_), (new_m, m_), (new_v, v_)):
        res.update(zip(rest, _unpack(src, shapes)))

    return (loss_tot, dx[None], *[grads[n] for n in _WEIGHTS], *[delta[n] for n in _WEIGHTS],
            *[new_m[n] for n in _WEIGHTS], *[new_v[n] for n in _WEIGHTS])


def _unpack_chips(halves, shapes):
    _, n_chip, L = halves.shape
    out, off = [], 0
    for s in shapes:
        n = math.prod(s)
        pieces = []
        if off < L:
            pieces.append(halves[0][:, off:min(off + n, L)])
        if off + n > L:
            pieces.append(halves[1][:, max(off, L) - L:off + n - L])
        flat = pieces[0] if len(pieces) == 1 else jnp.concatenate(pieces, axis=1)
        out.append(flat.reshape((n_chip,) + tuple(s)))
        off += n
    return out
```

```python
import functools
import math

import jax
import jax.numpy as jnp
from jax import lax
from jax.experimental import pallas as pl
from jax.experimental.pallas import tpu as pltpu

f32 = jnp.float32
bf16 = jnp.bfloat16
SDS = jax.ShapeDtypeStruct

EPS = 1e-6
CHUNK = 64
N_HEADS = 16
NOPE = 64
ROPE = 32
VDIM = 64
HEAD_PAD = 128
KV_RANK = 256
ROPE_THETA = 10000.0
ATTN_SCALE = 1.0 / math.sqrt(NOPE + ROPE)
SSM_GROUP = 16
SSM_STATE = 64
N_A = 2
DEPTH = 4
LANES = 128
SUBLANES = 8
SEGMENTS = 8
SLAB_CH = 128
SLAB_ST = 512
ADAM_LR, ADAM_B1, ADAM_B2, ADAM_EPS, ADAM_WD, ADAM_STEP = 0.001, 0.9, 0.999, 1e-08, 0.01, 10


def _pick(n, prefs=(512, 256, 128)):
    for p in prefs:
        if n % p == 0:
            return p
    return n


V7X_VMEM_BYTES = 64 << 20
MM_VMEM_LIMIT = V7X_VMEM_BYTES - (8 << 20)
MM_VMEM_BUDGET = 40 << 20
MM_MAX_TM, MM_MAX_TN = 1536, 1536
BF16_ROWS = 16


def _largest_divisor(n, cap, unit):
    if n <= cap:
        return n
    for d in range(cap - cap % unit, 0, -unit):
        if n % d == 0:
            return d
    return n


def _mm_tiles(M, N, K, sa, sb, so, m_on_lanes):
    tm = _largest_divisor(M, MM_MAX_TM, LANES if m_on_lanes else BF16_ROWS)
    tn = _largest_divisor(N, MM_MAX_TN, LANES)
    tk = K

    def need(tm, tn, tk):
        return 2 * (tm * tk * sa + tk * tn * sb + tm * tn * so) + (0 if tk == K else tm * tn * 4)

    m_unit = LANES if m_on_lanes else BF16_ROWS
    while need(tm, tn, tk) > MM_VMEM_BUDGET:
        if tk % 256 == 0 and tk >= 1024:
            tk //= 2
        elif tm % (2 * m_unit) == 0 and tm >= 512:
            tm //= 2
        elif tn % 256 == 0:
            tn //= 2
        else:
            break
    return tm, tn, tk


def _mm(a, b, *, ta=False, tb=False, out_dtype=f32, cast=True, name="mm"):
    if ta:
        K, M = a.shape
    else:
        M, K = a.shape
    if tb:
        N, K2 = b.shape
    else:
        K2, N = b.shape
    assert K == K2, (a.shape, b.shape, ta, tb)
    tm, tn, tk = _mm_tiles(M, N, K, a.dtype.itemsize, b.dtype.itemsize, jnp.dtype(out_dtype).itemsize, ta)
    nk = K // tk
    dims = (((0 if ta else 1,), (1 if tb else 0,)), ((), ()))

    def dot(a_ref, b_ref):
        av, bv = a_ref[...], b_ref[...]
        if cast:
            return lax.dot_general(av.astype(bf16), bv.astype(bf16), dims, preferred_element_type=f32)
        return lax.dot_general(av, bv, dims, preferred_element_type=f32, precision=lax.Precision.HIGHEST)

    def body_one(a_ref, b_ref, o_ref):
        o_ref[...] = dot(a_ref, b_ref).astype(o_ref.dtype)

    def body_acc(a_ref, b_ref, o_ref, acc):
        k = pl.program_id(2)

        @pl.when(k == 0)
        def _():
            acc[...] = jnp.zeros_like(acc)

        acc[...] += dot(a_ref, b_ref)

        @pl.when(k == nk - 1)
        def _():
            o_ref[...] = acc[...].astype(o_ref.dtype)

    a_spec = pl.BlockSpec((tk, tm), lambda i, j, k: (k, i)) if ta else pl.BlockSpec((tm, tk), lambda i, j, k: (i, k))
    b_spec = pl.BlockSpec((tn, tk), lambda i, j, k: (j, k)) if tb else pl.BlockSpec((tk, tn), lambda i, j, k: (k, j))
    return pl.pallas_call(
        body_one if nk == 1 else body_acc, name=name, grid=(M // tm, N // tn, nk),
        in_specs=[a_spec, b_spec], out_specs=pl.BlockSpec((tm, tn), lambda i, j, k: (i, j)),
        out_shape=SDS((M, N), out_dtype), scratch_shapes=[] if nk == 1 else [pltpu.VMEM((tm, tn), f32)],
        compiler_params=pltpu.CompilerParams(dimension_semantics=("parallel", "parallel", "arbitrary"),
                                             vmem_limit_bytes=MM_VMEM_LIMIT),
    )(a, b)


def _rowwise(fn, rows, consts, row_outs, acc_outs=(), *, tm=256, name="rowwise"):
    S = rows[0].shape[0]
    tm = min(tm, S)
    assert S % tm == 0, (S, tm)
    n_r, n_c, n_ro, n_ao = len(rows), len(consts), len(row_outs), len(acc_outs)

    def body(*refs):
        vals = [r[...] for r in refs[:n_r + n_c]]
        outs = fn(*vals)
        if not isinstance(outs, (tuple, list)):
            outs = (outs,)
        assert len(outs) == n_ro + n_ao, (name, len(outs), n_ro, n_ao)
        o_refs = refs[n_r + n_c:]
        for ref, val in zip(o_refs[:n_ro], outs[:n_ro]):
            ref[...] = val.astype(ref.dtype)
        if n_ao:
            @pl.when(pl.program_id(0) == 0)
            def _():
                for ref in o_refs[n_ro:]:
                    ref[...] = jnp.zeros_like(ref)
            for ref, val in zip(o_refs[n_ro:], outs[n_ro:]):
                ref[...] += jnp.broadcast_to(val, ref.shape).astype(f32)

    def full(shape):
        nd = len(shape)
        return pl.BlockSpec(tuple(shape), lambda i: (0,) * nd)

    in_specs = [pl.BlockSpec((tm, r.shape[1]), lambda i: (i, 0)) for r in rows] + [full(c.shape) for c in consts]
    out_specs = [pl.BlockSpec((tm, w), lambda i: (i, 0)) for (w, _) in row_outs] + [full(s) for s in acc_outs]
    out_shape = [SDS((S, w), dt) for (w, dt) in row_outs] + [SDS(tuple(s), f32) for s in acc_outs]
    res = pl.pallas_call(
        body, name=name, grid=(S // tm,), in_specs=in_specs, out_specs=out_specs, out_shape=out_shape,
        compiler_params=pltpu.CompilerParams(dimension_semantics=("arbitrary",)),
    )(*rows, *consts)
    return res


def _rowwise_bwd(f, rows, consts, cts, want, row_dtypes, *, adds=None, tm=256, name="rowwise_bwd"):
    n_r, n_c, n_ct = len(rows), len(consts), len(cts)
    adds = adds or {}
    add_keys = sorted(adds)
    add_rows = [adds[k] for k in add_keys]

    def fn(*args):
        r = [a.astype(f32) for a in args[:n_r]]
        ct = [a.astype(f32) for a in args[n_r:n_r + n_ct]]
        ad = args[n_r + n_ct:n_r + n_ct + len(add_rows)]
        c = list(args[n_r + n_ct + len(add_rows):])
        _, vjp = jax.vjp(f, *r, *c)
        g = vjp(ct[0] if n_ct == 1 else tuple(ct))
        d_rows = []
        for pos, i in enumerate(want):
            d = g[i]
            if pos in adds:
                d = d + ad[add_keys.index(pos)].astype(f32)
            d_rows.append(d)
        return (*d_rows, *g[n_r:])

    return _rowwise(fn, list(rows) + list(cts) + add_rows, list(consts),
                    [(rows[i].shape[1], dt) for i, dt in zip(want, row_dtypes)],
                    [c.shape for c in consts], tm=tm, name=name)


def _sum_lead(arr, out_dtype, name):
    n, R, C = arr.shape
    tm = _pick(R, (256, 128, 64, 32, 16, 8))

    def body(a_ref, o_ref):
        acc = a_ref[0].astype(f32)
        for q in range(1, n):
            acc = acc + a_ref[q].astype(f32)
        o_ref[...] = acc.astype(o_ref.dtype)

    return pl.pallas_call(
        body, name=name, grid=(R // tm,), in_specs=[pl.BlockSpec((n, tm, C), lambda i: (0, i, 0))],
        out_specs=pl.BlockSpec((tm, C), lambda i: (i, 0)), out_shape=SDS((R, C), out_dtype),
    )(arr)


_GROUPS = {
    "all": [(kx, ky, kc) for kx in (0, 1) for ky in (0, 1) for kc in (0, 1)][1:],
    "xy": [(0, 1, 0), (1, 0, 0), (1, 1, 0)],
    "c": [(0, 0, 1)],
}


XCHG_CHUNK_MIN_BYTES = 1 << 20
XCHG_CHUNKS = 16


def _group_pos(group, x, y, c):
    return {"all": 4 * x + 2 * y + c, "xy": 2 * x + y, "c": c}[group]


def _xchg(send, group, bcast, name):
    flips = _GROUPS[group]
    n = len(flips) + 1
    piece = tuple(send.shape if bcast else send.shape[1:])
    if not bcast:
        assert send.shape[0] == n
    lead = piece[0]
    nch = 1
    if math.prod(piece) * send.dtype.itemsize >= XCHG_CHUNK_MIN_BYTES:
        nch = lead if len(piece) > 2 else (XCHG_CHUNKS if lead % (XCHG_CHUNKS * 2 * BF16_ROWS) == 0 else 1)
    rows = lead // nch

    def body(s_ref, r_ref, send_sems, recv_sems, own_sems):
        x, y, c = lax.axis_index("x"), lax.axis_index("y"), lax.axis_index("c")
        me = _group_pos(group, x, y, c)

        def src(p, ch):
            ref = s_ref if bcast else s_ref.at[p]
            return ref.at[pl.ds(ch * rows, rows)]

        def dst(ch):
            return r_ref.at[me].at[pl.ds(ch * rows, rows)]

        owns = [pltpu.make_async_copy(src(me, ch), dst(ch), own_sems.at[ch]) for ch in range(nch)]
        for cp in owns:
            cp.start()
        copies = []
        for ch in range(nch):
            for k, (kx, ky, kc) in enumerate(flips):
                tx, ty, tc = x ^ kx, y ^ ky, c ^ kc
                cp = pltpu.make_async_remote_copy(
                    src_ref=src(_group_pos(group, tx, ty, tc), ch), dst_ref=dst(ch),
                    send_sem=send_sems.at[k, ch], recv_sem=recv_sems.at[k, ch],
                    device_id=(tx, ty, tc), device_id_type=pl.DeviceIdType.MESH)
                cp.start()
                copies.append(cp)
        for cp in copies + owns:
            cp.wait()

    return pl.pallas_call(
        body, name=name, out_shape=SDS((n,) + piece, send.dtype),
        in_specs=[pl.BlockSpec(memory_space=pl.ANY)], out_specs=pl.BlockSpec(memory_space=pl.ANY),
        scratch_shapes=[pltpu.SemaphoreType.DMA((n - 1, nch)), pltpu.SemaphoreType.DMA((n - 1, nch)), pltpu.SemaphoreType.DMA((nch,))],
        compiler_params=pltpu.CompilerParams(has_side_effects=True),
    )(send)


PAIR_ROWS = 256


def _pair(arr, mode, out_dtype, name):
    P, N, C = arr.shape
    tm = _largest_divisor(N, PAIR_ROWS, BF16_ROWS)
    nb = N // tm
    assert N % tm == 0 and tm <= PAIR_ROWS
    add = mode == "add"
    assert not add or P == 2

    def body(core_ref, *refs):
        if add:
            keep_ref, send_ref, o_ref, slots, send_sems, recv_sems, credit = refs
        else:
            a_ref, o_ref, stage, slots, send_sems, recv_sems, credit = refs
        i = pl.program_id(0)
        x, y, c = lax.axis_index("x"), lax.axis_index("y"), lax.axis_index("c")
        sib = (x, y, 1 - c)
        slot = lax.rem(i, 2)
        if add:
            src = send_ref.at[0]
        else:
            t = a_ref[0].astype(f32)
            for q in range(1, P):
                t = t + a_ref[q].astype(f32)
            stage[...] = t.astype(stage.dtype)
            o_ref[c] = stage[...]
            src = stage

        @pl.when(i >= 2)
        def _():
            pl.semaphore_wait(credit.at[slot], 1)

        cp = pltpu.make_async_remote_copy(src_ref=src, dst_ref=slots.at[slot], send_sem=send_sems.at[slot], recv_sem=recv_sems.at[slot],
                                          device_id=sib, device_id_type=pl.DeviceIdType.MESH)
        cp.start()
        cp.wait_recv()
        if add:
            o_ref[...] = (keep_ref[0].astype(f32) + slots[slot].astype(f32)).astype(o_ref.dtype)
        else:
            o_ref[1 - c] = slots[slot]
        cp.wait_send()

        @pl.when(i + 2 < nb)
        def _():
            pl.semaphore_signal(credit.at[slot], 1, device_id=sib, device_id_type=pl.DeviceIdType.MESH)

    core = lax.axis_index("c").astype(jnp.int32).reshape(1)
    scratch = [pltpu.VMEM((2, tm, C), arr.dtype if add else out_dtype), pltpu.SemaphoreType.DMA((2,)), pltpu.SemaphoreType.DMA((2,)),
               pltpu.SemaphoreType.REGULAR((2,))]
    if add:
        in_specs = [pl.BlockSpec((1, tm, C), lambda i, core: (core[0], i, 0)), pl.BlockSpec((1, tm, C), lambda i, core: (1 - core[0], i, 0))]
        out_specs = pl.BlockSpec((tm, C), lambda i, core: (i, 0))
        out_shape = SDS((N, C), out_dtype)
        args = (arr, arr)
    else:
        in_specs = [pl.BlockSpec((P, tm, C), lambda i, core: (0, i, 0))]
        out_specs = pl.BlockSpec((2, tm, C), lambda i, core: (0, i, 0))
        out_shape = SDS((2, N, C), out_dtype)
        scratch = [pltpu.VMEM((tm, C), out_dtype)] + scratch
        args = (arr,)
    return pl.pallas_call(
        body, name=name, out_shape=out_shape,
        grid_spec=pltpu.PrefetchScalarGridSpec(num_scalar_prefetch=1, grid=(nb,), in_specs=in_specs, out_specs=out_specs, scratch_shapes=scratch),
        compiler_params=pltpu.CompilerParams(dimension_semantics=("arbitrary",), has_side_effects=True),
    )(core, *args)


def _rms(x, g):
    return x * lax.rsqrt(jnp.mean(x * x, axis=-1, keepdims=True) + EPS) * g


def _normmod(x, g, shift, scale):
    return _rms(x, g) * (1.0 + scale) + shift


def _lane(shape):
    return lax.broadcasted_iota(jnp.int32, shape, 1)


def _partner(x):
    lane = _lane(x.shape)
    lo = (lane >= NOPE) & (lane < NOPE + ROPE // 2)
    hi = (lane >= NOPE + ROPE // 2) & (lane < NOPE + ROPE)
    return jnp.where(lo, pltpu.roll(x, HEAD_PAD - ROPE // 2, 1), jnp.where(hi, pltpu.roll(x, ROPE // 2, 1), 0.0))


@jax.custom_vjp
def _rope(x, cf, ss):
    return x * cf + _partner(x) * ss


def _rope_fwd(x, cf, ss):
    return _rope(x, cf, ss), (cf, ss)


def _rope_bwd(res, dy):
    cf, ss = res
    return dy * cf + _partner(dy * ss), jnp.zeros_like(cf), jnp.zeros_like(ss)


_rope.defvjp(_rope_fwd, _rope_bwd)


def _head_norm(xh, gn, gr):
    lane = _lane(xh.shape)
    x2 = xh * xh
    ms_n = jnp.sum(jnp.where(lane < NOPE, x2, 0.0), axis=-1, keepdims=True) * (1.0 / NOPE)
    ms_r = jnp.sum(jnp.where((lane >= NOPE) & (lane < NOPE + ROPE), x2, 0.0), axis=-1, keepdims=True) * (1.0 / ROPE)
    return xh * (lax.rsqrt(ms_n + EPS) * gn + lax.rsqrt(ms_r + EPS) * gr)


def _q_heads(qraw, cf, ss, gn, gr):
    outs = []
    for h in range(N_HEADS):
        xh = qraw[:, h * HEAD_PAD:(h + 1) * HEAD_PAD]
        outs.append(_rope(_head_norm(xh, gn, gr), cf, ss))
    return jnp.concatenate(outs, axis=1)


def _k_heads(kraw, kr, gn):
    outs = []
    zero = jnp.zeros_like(gn)
    for h in range(N_HEADS):
        xh = kraw[:, h * HEAD_PAD:(h + 1) * HEAD_PAD]
        outs.append(_head_norm(xh, gn, zero) + kr)
    return jnp.concatenate(outs, axis=1)


def _kv_a_post(kva, cf, ss, ga, gr):
    ckv = _rms(kva[:, :KV_RANK], ga)
    kr = _rope(_head_norm(kva[:, KV_RANK:], jnp.zeros_like(gr), gr), cf, ss)
    return ckv, kr


SCAN_UNROLL = 8


def _cmul(ar, ai, br, bi):
    return ar * br - ai * bi, ar * bi + ai * br


def _s5_scan_fwd(u, wbr, wbi, ar, ai, init_r, init_i, wcr, wci, *, want_y, tb, name):
    S, D = u.shape
    ns = D // SLAB_CH
    seg = S // SEGMENTS
    tb = min(tb, seg)
    nb = seg // tb
    R = tb * SEGMENTS
    nt = (((1,), (0,)), ((), ()))

    def body(u_ref, wbr_ref, wbi_ref, ar_ref, ai_ref, ir_ref, ii_ref, wcr_ref, wci_ref, *rest):
        if want_y:
            y_ref, er_ref, ei_ref, bsr_ref, bsi_ref, cr, ci, bur, bui, sr, si = rest
        else:
            er_ref, ei_ref, bsr_ref, bsi_ref, cr, ci, bur, bui, sr, si = rest
        t = pl.program_id(1)

        @pl.when(t == 0)
        def _():
            cr[...] = ir_ref[0]
            ci[...] = ii_ref[0]

        bsr_ref[0, 0] = cr[...]
        bsi_ref[0, 0] = ci[...]
        ub = u_ref[...].astype(bf16)
        bur[...] = lax.dot_general(ub, wbr_ref[0].astype(bf16), nt, preferred_element_type=f32)
        bui[...] = lax.dot_general(ub, wbi_ref[0].astype(bf16), nt, preferred_element_type=f32)
        a_r = jnp.broadcast_to(ar_ref[0], (SEGMENTS, SLAB_ST))
        a_i = jnp.broadcast_to(ai_ref[0], (SEGMENTS, SLAB_ST))

        def step(tau, carry):
            c_r, c_i = carry
            r = pl.multiple_of(tau * SEGMENTS, SEGMENTS)
            p_r, p_i = _cmul(a_r, a_i, c_r, c_i)
            n_r = p_r + bur[pl.ds(r, SEGMENTS), :]
            n_i = p_i + bui[pl.ds(r, SEGMENTS), :]
            sr[pl.ds(r, SEGMENTS), :] = n_r
            si[pl.ds(r, SEGMENTS), :] = n_i
            return n_r, n_i

        c_r, c_i = lax.fori_loop(0, tb, step, (cr[...], ci[...]), unroll=SCAN_UNROLL)
        cr[...] = c_r
        ci[...] = c_i
        if want_y:
            y_ref[...] = (lax.dot_general(sr[...].astype(bf16), wcr_ref[0].astype(bf16), nt, preferred_element_type=f32)
                          - lax.dot_general(si[...].astype(bf16), wci_ref[0].astype(bf16), nt, preferred_element_type=f32))

        @pl.when(t == nb - 1)
        def _():
            er_ref[0] = c_r
            ei_ref[0] = c_i

    slab3 = lambda s: pl.BlockSpec((1,) + s, lambda k, t: (k, 0, 0))
    in_specs = [pl.BlockSpec((R, SLAB_CH), lambda k, t: (t, k)),
                slab3((SLAB_CH, SLAB_ST)), slab3((SLAB_CH, SLAB_ST)), slab3((1, SLAB_ST)), slab3((1, SLAB_ST)),
                slab3((SEGMENTS, SLAB_ST)), slab3((SEGMENTS, SLAB_ST)), slab3((SLAB_ST, SLAB_CH)), slab3((SLAB_ST, SLAB_CH))]
    out_specs = [slab3((SEGMENTS, SLAB_ST)), slab3((SEGMENTS, SLAB_ST)),
                 pl.BlockSpec((1, 1, SEGMENTS, SLAB_ST), lambda k, t: (t, k, 0, 0)),
                 pl.BlockSpec((1, 1, SEGMENTS, SLAB_ST), lambda k, t: (t, k, 0, 0))]
    out_shape = [SDS((ns, SEGMENTS, SLAB_ST), f32)] * 2 + [SDS((nb, ns, SEGMENTS, SLAB_ST), f32)] * 2
    if want_y:
        out_specs = [pl.BlockSpec((R, SLAB_CH), lambda k, t: (t, k))] + out_specs
        out_shape = [SDS((S, D), f32)] + out_shape
    return pl.pallas_call(
        body, name=name, grid=(ns, nb), in_specs=in_specs, out_specs=out_specs, out_shape=out_shape,
        scratch_shapes=[pltpu.VMEM((SEGMENTS, SLAB_ST), f32)] * 2 + [pltpu.VMEM((R, SLAB_ST), f32)] * 4,
        compiler_params=pltpu.CompilerParams(dimension_semantics=("parallel", "arbitrary")),
    )(u, wbr, wbi, ar, ai, init_r, init_i, wcr, wci)


def _s5_scan_bwd(u, dy, wbr, wbi, ar, ai, bs_r, bs_i, ginit_r, ginit_i, wcr, wci, *, full, tb, name):
    S, D = dy.shape
    ns = D // SLAB_CH
    seg = S // SEGMENTS
    tb = min(tb, seg)
    nb = seg // tb
    R = tb * SEGMENTS
    nn = (((1,), (0,)), ((), ()))
    nt = (((1,), (1,)), ((), ()))
    tn = (((0,), (0,)), ((), ()))

    def body(*refs):
        if full:
            (u_ref, dy_ref, wbr_ref, wbi_ref, ar_ref, ai_ref, bsr_ref, bsi_ref, gir_ref, gii_ref, wcr_ref, wci_ref,
             du_ref, dwbr_ref, dwbi_ref, dwcr_ref, dwci_ref, dar_ref, dai_ref, gfr_ref, gfi_ref,
             gr_c, gi_c, bur, bui, sr, si, gsr, gsi) = refs
        else:
            (dy_ref, ar_ref, ai_ref, gir_ref, gii_ref, wcr_ref, wci_ref, gfr_ref, gfi_ref, gr_c, gi_c, gsr, gsi) = refs
        t = pl.program_id(1)

        @pl.when(t == 0)
        def _():
            gr_c[...] = gir_ref[0]
            gi_c[...] = gii_ref[0]
            if full:
                for ref in (dwbr_ref, dwbi_ref, dwcr_ref, dwci_ref, dar_ref, dai_ref):
                    ref[...] = jnp.zeros_like(ref)

        a_r = jnp.broadcast_to(ar_ref[0], (SEGMENTS, SLAB_ST))
        a_i = jnp.broadcast_to(ai_ref[0], (SEGMENTS, SLAB_ST))
        dyb = dy_ref[...].astype(bf16)
        gsr[...] = lax.dot_general(dyb, wcr_ref[0].astype(bf16), nt, preferred_element_type=f32)
        gsi[...] = -lax.dot_general(dyb, wci_ref[0].astype(bf16), nt, preferred_element_type=f32)

        if full:
            ub = u_ref[...].astype(bf16)
            bur[...] = lax.dot_general(ub, wbr_ref[0].astype(bf16), nn, preferred_element_type=f32)
            bui[...] = lax.dot_general(ub, wbi_ref[0].astype(bf16), nn, preferred_element_type=f32)

            def fstep(tau, carry):
                c_r, c_i = carry
                r = pl.multiple_of(tau * SEGMENTS, SEGMENTS)
                p_r, p_i = _cmul(a_r, a_i, c_r, c_i)
                n_r = p_r + bur[pl.ds(r, SEGMENTS), :]
                n_i = p_i + bui[pl.ds(r, SEGMENTS), :]
                sr[pl.ds(r, SEGMENTS), :] = n_r
                si[pl.ds(r, SEGMENTS), :] = n_i
                return n_r, n_i

            lax.fori_loop(0, tb, fstep, (bsr_ref[0, 0], bsi_ref[0, 0]), unroll=SCAN_UNROLL)

        def adj(g_r, g_i):
            return a_r * g_r + a_i * g_i, a_r * g_i - a_i * g_r

        def rstep(k, carry):
            tau = tb - 1 - k
            r = pl.multiple_of(tau * SEGMENTS, SEGMENTS)
            if full:
                g_r, g_i, acc_r, acc_i = carry
            else:
                g_r, g_i = carry
            b_r, b_i = adj(g_r, g_i)
            n_r = b_r + gsr[pl.ds(r, SEGMENTS), :]
            n_i = b_i + gsi[pl.ds(r, SEGMENTS), :]
            gsr[pl.ds(r, SEGMENTS), :] = n_r
            gsi[pl.ds(r, SEGMENTS), :] = n_i
            if not full:
                return n_r, n_i
            rp = pl.multiple_of(jnp.maximum(tau - 1, 0) * SEGMENTS, SEGMENTS)
            first = tau == 0
            p_r = jnp.where(first, bsr_ref[0, 0], sr[pl.ds(rp, SEGMENTS), :])
            p_i = jnp.where(first, bsi_ref[0, 0], si[pl.ds(rp, SEGMENTS), :])
            return n_r, n_i, acc_r + n_r * p_r + n_i * p_i, acc_i + n_i * p_r - n_r * p_i

        zero = jnp.zeros((SEGMENTS, SLAB_ST), f32)
        if full:
            g_r, g_i, acc_r, acc_i = lax.fori_loop(0, tb, rstep, (gr_c[...], gi_c[...], zero, zero), unroll=SCAN_UNROLL)
            dar_ref[0] += acc_r
            dai_ref[0] += acc_i
        else:
            g_r, g_i = lax.fori_loop(0, tb, rstep, (gr_c[...], gi_c[...]), unroll=SCAN_UNROLL)
        gr_c[...] = g_r
        gi_c[...] = g_i

        if full:
            gbr = gsr[...].astype(bf16)
            gbi = gsi[...].astype(bf16)
            du_ref[...] = (lax.dot_general(gbr, wbr_ref[0].astype(bf16), nt, preferred_element_type=f32)
                           + lax.dot_general(gbi, wbi_ref[0].astype(bf16), nt, preferred_element_type=f32))
            dwbr_ref[0] += lax.dot_general(ub, gbr, tn, preferred_element_type=f32)
            dwbi_ref[0] += lax.dot_general(ub, gbi, tn, preferred_element_type=f32)
            dwcr_ref[0] += lax.dot_general(sr[...].astype(bf16), dyb, tn, preferred_element_type=f32)
            dwci_ref[0] -= lax.dot_general(si[...].astype(bf16), dyb, tn, preferred_element_type=f32)

        @pl.when(t == nb - 1)
        def _():
            gfr_ref[0] = g_r
            gfi_ref[0] = g_i

    slab3 = lambda s: pl.BlockSpec((1,) + s, lambda k, t: (k, 0, 0))
    rev_rows = pl.BlockSpec((R, SLAB_CH), lambda k, t: (nb - 1 - t, k))
    rev_bs = pl.BlockSpec((1, 1, SEGMENTS, SLAB_ST), lambda k, t: (nb - 1 - t, k, 0, 0))
    st = slab3((SEGMENTS, SLAB_ST))
    state_sds = SDS((ns, SEGMENTS, SLAB_ST), f32)
    if full:
        in_specs = [rev_rows, rev_rows, slab3((SLAB_CH, SLAB_ST)), slab3((SLAB_CH, SLAB_ST)), slab3((1, SLAB_ST)), slab3((1, SLAB_ST)),
                    rev_bs, rev_bs, st, st, slab3((SLAB_ST, SLAB_CH)), slab3((SLAB_ST, SLAB_CH))]
        args = (u, dy, wbr, wbi, ar, ai, bs_r, bs_i, ginit_r, ginit_i, wcr, wci)
        out_specs = [rev_rows, slab3((SLAB_CH, SLAB_ST)), slab3((SLAB_CH, SLAB_ST)), slab3((SLAB_ST, SLAB_CH)), slab3((SLAB_ST, SLAB_CH)),
                     st, st, st, st]
        out_shape = [SDS((S, D), f32), SDS((ns, SLAB_CH, SLAB_ST), f32), SDS((ns, SLAB_CH, SLAB_ST), f32),
                     SDS((ns, SLAB_ST, SLAB_CH), f32), SDS((ns, SLAB_ST, SLAB_CH), f32)] + [state_sds] * 4
        scratch = [pltpu.VMEM((SEGMENTS, SLAB_ST), f32)] * 2 + [pltpu.VMEM((R, SLAB_ST), f32)] * 6
    else:
        in_specs = [rev_rows, slab3((1, SLAB_ST)), slab3((1, SLAB_ST)), st, st, slab3((SLAB_ST, SLAB_CH)), slab3((SLAB_ST, SLAB_CH))]
        args = (dy, ar, ai, ginit_r, ginit_i, wcr, wci)
        out_specs = [st, st]
        out_shape = [state_sds] * 2
        scratch = [pltpu.VMEM((SEGMENTS, SLAB_ST), f32)] * 2 + [pltpu.VMEM((R, SLAB_ST), f32)] * 2
    return pl.pallas_call(
        body, name=name, grid=(ns, nb), in_specs=in_specs, out_specs=out_specs, out_shape=out_shape, scratch_shapes=scratch,
        compiler_params=pltpu.CompilerParams(dimension_semantics=("parallel", "arbitrary")),
    )(*args)


def _s5_chain(e_r, e_i, ar, ai, *, seg, reverse, name):
    ns = e_r.shape[0]
    assert seg & (seg - 1) == 0

    def body(er_ref, ei_ref, ar_ref, ai_ref, or_ref, oi_ref):
        p_r, p_i = ar_ref[0], ai_ref[0]
        if reverse:
            p_i = -p_i
        for _ in range(seg.bit_length() - 1):
            p_r, p_i = _cmul(p_r, p_i, p_r, p_i)
        c_r = jnp.zeros((1, SLAB_ST), f32)
        c_i = jnp.zeros((1, SLAB_ST), f32)
        order = range(SEGMENTS - 1, -1, -1) if reverse else range(SEGMENTS)
        for j in order:
            or_ref[0, pl.ds(j, 1), :] = c_r
            oi_ref[0, pl.ds(j, 1), :] = c_i
            m_r, m_i = _cmul(p_r, p_i, c_r, c_i)
            c_r = er_ref[0, pl.ds(j, 1), :] + m_r
            c_i = ei_ref[0, pl.ds(j, 1), :] + m_i

    st = pl.BlockSpec((1, SEGMENTS, SLAB_ST), lambda k: (k, 0, 0))
    av = pl.BlockSpec((1, 1, SLAB_ST), lambda k: (k, 0, 0))
    return pl.pallas_call(
        body, name=name, grid=(ns,), in_specs=[st, st, av, av], out_specs=[st, st],
        out_shape=[SDS(e_r.shape, f32)] * 2,
    )(e_r, e_i, ar, ai)


def _s5_disc(lr, li, log_dt):
    dt = jnp.exp(log_dt)
    mag = jnp.exp(lr * dt)
    ab_re = mag * jnp.cos(li * dt)
    ab_im = mag * jnp.sin(li * dt)
    den = lr * lr + li * li
    nr = ab_re - 1.0
    ni = ab_im
    return ab_re, ab_im, (nr * lr + ni * li) / den, (ni * lr - nr * li) / den


def _s5_bbar(b_r, b_i, fr, fi):
    return fr * b_r - fi * b_i, fr * b_i + fi * b_r


def _blockdiag(w, rows, cols):
    g = w.shape[0]
    w = w.reshape(g // 8, 8, rows, cols)
    eye = jnp.eye(8, dtype=w.dtype)
    full = w[:, :, :, None, :] * eye[None, :, None, :, None]
    return full.reshape(g // 8, 8 * rows, 8 * cols)


def _blockdiag_take(w, rows, cols):
    ns = w.shape[0]
    w = w.reshape(ns, 8, rows, 8, cols)
    idx = jnp.arange(8)
    d = w[:, idx, :, idx, :]
    return jnp.moveaxis(d, 0, 1).reshape(ns * 8, rows, cols)


def _mask(i, j, tq, tk):
    row = i * tq + lax.broadcasted_iota(jnp.int32, (tq, tk), 0)
    col = j * tk + lax.broadcasted_iota(jnp.int32, (tq, tk), 1)
    return (row // CHUNK) >= (col // CHUNK)


_NT = (((1,), (1,)), ((), ()))
_NN = (((1,), (0,)), ((), ()))
_TN = (((0,), (0,)), ((), ()))
_NEG = -1e30
ATTN_HEADS_PER_STEP = 2


def _attn_fwd(q, k, v, *, t, name):
    S = q.shape[0]
    t = min(t, S)
    nq = S // t

    hp = ATTN_HEADS_PER_STEP

    def body(q_ref, k_ref, v_ref, o_ref, lse_ref, m_s, acc):
        i = pl.program_id(1)
        m_s[...] = jnp.full_like(m_s, _NEG)
        acc[...] = jnp.zeros_like(acc)

        lane_v = _lane((t, HEAD_PAD))
        ones_col = jnp.where(lane_v == VDIM, 1.0, 0.0).astype(bf16)

        def block(j, masked):
            r = pl.multiple_of(j * t, t)
            heads = [slice(hh * HEAD_PAD, (hh + 1) * HEAD_PAD) for hh in range(hp)]
            scores = []
            for lanes in heads:
                s = lax.dot_general(q_ref[:, lanes], k_ref[pl.ds(r, t), lanes], _NT, preferred_element_type=f32) * ATTN_SCALE
                if masked:
                    s = jnp.where(_mask(0, 0, t, t), s, _NEG)
                scores.append(s)
            probs = []
            for hh, s in enumerate(scores):
                folded = s[:, :LANES]
                for cblk in range(1, t // LANES):
                    folded = jnp.maximum(folded, s[:, cblk * LANES:(cblk + 1) * LANES])
                m_old = m_s[hh]
                m_new = jnp.maximum(m_old, jnp.max(folded, axis=-1, keepdims=True))
                p = jnp.concatenate([jnp.exp(s[:, cblk * LANES:(cblk + 1) * LANES] - m_new) for cblk in range(t // LANES)], axis=1)
                probs.append((m_new, jnp.exp(m_old - m_new), p.astype(bf16)))
            for hh, (m_new, alpha, p) in enumerate(probs):
                vb = v_ref[pl.ds(r, t), heads[hh]] + ones_col
                acc[hh] = alpha * acc[hh] + lax.dot_general(p, vb, _NN, preferred_element_type=f32)
                m_s[hh] = m_new

        def below(j, carry):
            block(j, False)
            return carry

        lax.fori_loop(0, i, below, 0)
        block(i, True)
        for hh in range(hp):
            a = acc[hh]
            l = a[:, VDIM:VDIM + 1]
            o_ref[:, hh * HEAD_PAD:(hh + 1) * HEAD_PAD] = jnp.where(lane_v < VDIM, a / l, 0.0)
            lse_ref[hh] = m_s[hh][:, :1] + jnp.log(l)

    qs = pl.BlockSpec((t, hp * HEAD_PAD), lambda h, i: (i, h))
    ks = pl.BlockSpec((S, hp * HEAD_PAD), lambda h, i: (0, h))
    return pl.pallas_call(
        body, name=name, grid=(N_HEADS // hp, nq), in_specs=[qs, ks, ks],
        out_specs=[qs, pl.BlockSpec((hp, t, 1), lambda h, i: (h, i, 0))],
        out_shape=[SDS((S, N_HEADS * HEAD_PAD), f32), SDS((N_HEADS, S, 1), f32)],
        scratch_shapes=[pltpu.VMEM((hp, t, LANES), f32), pltpu.VMEM((hp, t, HEAD_PAD), f32)],
        compiler_params=pltpu.CompilerParams(dimension_semantics=("parallel", "arbitrary")),
    )(q, k, v)


def _attn_bwd(q, k, v, do, o, lse, *, t, name):
    S = q.shape[0]
    t = min(t, S)
    nq = S // t

    def body(q_ref, k_ref, v_ref, do_ref, o_ref, lse_ref, dq_ref, dk_ref, dv_ref, d_s, lse_s, dk_acc, dv_acc):
        j = pl.program_id(1)

        @pl.when(j == 0)
        def _():
            dq_ref[...] = jnp.zeros_like(dq_ref)
            d_s[...] = jnp.broadcast_to(jnp.sum(do_ref[...].astype(f32) * o_ref[...], axis=-1, keepdims=True), d_s.shape)
            lse_s[...] = jnp.broadcast_to(lse_ref[0], lse_s.shape)

        dk_acc[...] = jnp.zeros_like(dk_acc)
        dv_acc[...] = jnp.zeros_like(dv_acc)
        kb = k_ref[...]
        vb = v_ref[...]

        def block(i, masked):
            r = pl.multiple_of(i * t, t)
            qb = q_ref[pl.ds(r, t), :]
            dob = do_ref[pl.ds(r, t), :]
            s = lax.dot_general(qb, kb, _NT, preferred_element_type=f32) * ATTN_SCALE
            if masked:
                s = jnp.where(_mask(0, 0, t, t), s, _NEG)
            lse_b = lse_s[pl.ds(r, t), :]
            d_b = d_s[pl.ds(r, t), :]
            chunks = [slice(cblk * LANES, (cblk + 1) * LANES) for cblk in range(t // LANES)]
            p = jnp.concatenate([jnp.exp(s[:, ck] - lse_b) for ck in chunks], axis=1)
            dp = lax.dot_general(dob, vb, _NT, preferred_element_type=f32)
            ds = jnp.concatenate([p[:, ck] * (dp[:, ck] - d_b) * ATTN_SCALE for ck in chunks], axis=1).astype(bf16)
            dv_acc[...] += lax.dot_general(p.astype(bf16), dob, _TN, preferred_element_type=f32)
            dk_acc[...] += lax.dot_general(ds, qb, _TN, preferred_element_type=f32)
            dq_ref[pl.ds(r, t), :] += lax.dot_general(ds, kb, _NN, preferred_element_type=f32)

        def below(i, carry):
            block(i, False)
            return carry

        block(j, True)
        lax.fori_loop(j + 1, nq, below, 0)
        dk_ref[...] = dk_acc[...]
        dv_ref[...] = dv_acc[...]

    hs = pl.BlockSpec((S, HEAD_PAD), lambda h, j: (0, h))
    ks = pl.BlockSpec((t, HEAD_PAD), lambda h, j: (j, h))
    ls = pl.BlockSpec((1, S, 1), lambda h, j: (h, 0, 0))
    return pl.pallas_call(
        body, name=name, grid=(N_HEADS, nq), in_specs=[hs, ks, ks, hs, hs, ls], out_specs=[hs, ks, ks],
        out_shape=[SDS((S, N_HEADS * HEAD_PAD), f32)] * 3,
        scratch_shapes=[pltpu.VMEM((S, LANES), f32), pltpu.VMEM((S, LANES), f32), pltpu.VMEM((t, HEAD_PAD), f32), pltpu.VMEM((t, HEAD_PAD), f32)],
        compiler_params=pltpu.CompilerParams(dimension_semantics=("parallel", "arbitrary")),
    )(q, k, v, do, o, lse)


def _pad_heads(w, per_head, axis):
    w = jnp.moveaxis(w, axis, -1)
    lead = w.shape[:-1]
    w = w.reshape(lead + (N_HEADS, per_head))
    w = jnp.pad(w, [(0, 0)] * len(lead) + [(0, 0), (0, HEAD_PAD - per_head)])
    return jnp.moveaxis(w.reshape(lead + (N_HEADS * HEAD_PAD,)), -1, axis)


def _unpad_heads(w, per_head, axis):
    w = jnp.moveaxis(w, axis, -1)
    lead = w.shape[:-1]
    w = w.reshape(lead + (N_HEADS, HEAD_PAD))[..., :per_head]
    return jnp.moveaxis(w.reshape(lead + (N_HEADS * per_head,)), -1, axis)


def _lanes128(vec, offset):
    return jnp.zeros((1, HEAD_PAD), f32).at[0, offset:offset + vec.shape[0]].set(vec)


def _to_segments(a):
    S, D = a.shape
    return a.reshape(SEGMENTS, S // SEGMENTS, D).transpose(1, 0, 2).reshape(S, D)


def _from_segments(a):
    S, D = a.shape
    return a.reshape(S // SEGMENTS, SEGMENTS, D).transpose(1, 0, 2).reshape(S, D)


def _s5_params_fwd(p):
    G, N, P = p["b_re"].shape
    def disc_body(lr, li, ld, o1, o2, o3, o4):
        o1[...], o2[...], o3[...], o4[...] = _s5_disc(lr[...], li[...], ld[...])

    ab_r, ab_i, f_r, f_i = pl.pallas_call(disc_body, name="s5_disc", out_shape=[SDS((G, N), f32)] * 4)(
        p["lam_re"], p["lam_im"], p["log_dt"].reshape(G, 1))
    bb_r, bb_i = _rowwise(_s5_bbar, [p["b_re"].reshape(G * N, P), p["b_im"].reshape(G * N, P), f_r.reshape(G * N, 1), f_i.reshape(G * N, 1)],
                          [], [(P, f32), (P, f32)], tm=512, name="s5_bbar")
    ns = G // 8
    out = dict(
        f_r=f_r, f_i=f_i,
        a_r=ab_r.reshape(ns, 1, SLAB_ST), a_i=ab_i.reshape(ns, 1, SLAB_ST),
        wb_r=_blockdiag(bb_r.reshape(G, N, P).transpose(0, 2, 1), P, N), wb_i=_blockdiag(bb_i.reshape(G, N, P).transpose(0, 2, 1), P, N),
        wc_r=_blockdiag(p["c_re"].transpose(0, 2, 1), N, P), wc_i=_blockdiag(p["c_im"].transpose(0, 2, 1), N, P),
    )
    return out


def _s5_params_bwd(p, sp, d_ar, d_ai, d_wbr, d_wbi, d_wcr, d_wci):
    G, N, P = p["b_re"].shape
    dbb_r = _blockdiag_take(d_wbr, P, N).transpose(0, 2, 1).reshape(G * N, P)
    dbb_i = _blockdiag_take(d_wbi, P, N).transpose(0, 2, 1).reshape(G * N, P)
    d_cre = _blockdiag_take(d_wcr, N, P).transpose(0, 2, 1)
    d_cim = _blockdiag_take(d_wci, N, P).transpose(0, 2, 1)
    rows = [p["b_re"].reshape(G * N, P), p["b_im"].reshape(G * N, P), sp["f_r"].reshape(G * N, 1), sp["f_i"].reshape(G * N, 1)]
    d_br, d_bi, d_fr, d_fi = _rowwise_bwd(_s5_bbar, rows, [], [dbb_r, dbb_i], [0, 1, 2, 3], [f32] * 4, tm=512, name="s5_bbar_bwd")

    def seg_sum(d, name):
        return _sum_lead(d.transpose(1, 0, 2).reshape(SEGMENTS, G, N), f32, name)

    def body(lr, li, ld, c1, c2, c3, c4, o1, o2, o3):
        _, vjp = jax.vjp(_s5_disc, lr[...], li[...], ld[...])
        o1[...], o2[...], o3[...] = vjp((c1[...], c2[...], c3[...], c4[...]))

    d_lr, d_li, d_ld = pl.pallas_call(
        body, name="s5_disc_bwd", out_shape=[SDS((G, N), f32), SDS((G, N), f32), SDS((G, 1), f32)],
    )(p["lam_re"], p["lam_im"], p["log_dt"].reshape(G, 1), seg_sum(d_ar, "s5_da_re_sum"), seg_sum(d_ai, "s5_da_im_sum"),
      d_fr.reshape(G, N), d_fi.reshape(G, N))
    return dict(lam_re=d_lr, lam_im=d_li, log_dt=d_ld.reshape(G), b_re=d_br.reshape(G, N, P), b_im=d_bi.reshape(G, N, P),
                c_re=d_cre, c_im=d_cim)


_SCAN_TB = 64
_ATTN_T = 512


def _s5_mix_fwd(h, sp, name):
    S = h.shape[0]
    seg = S // SEGMENTS
    zeros = jnp.zeros((h.shape[1] // SLAB_CH, SEGMENTS, SLAB_ST), f32)
    common = (sp["wb_r"], sp["wb_i"], sp["a_r"], sp["a_i"])
    e_r, e_i, _, _ = _s5_scan_fwd(h, *common, zeros, zeros, sp["wc_r"], sp["wc_i"], want_y=False, tb=_SCAN_TB, name=name + "_local")
    i_r, i_i = _s5_chain(e_r, e_i, sp["a_r"], sp["a_i"], seg=seg, reverse=False, name=name + "_chain")
    y, _, _, bs_r, bs_i = _s5_scan_fwd(h, *common, i_r, i_i, sp["wc_r"], sp["wc_i"], want_y=True, tb=_SCAN_TB, name=name)
    return y, (bs_r, bs_i)


def _s5_mix_bwd(h, dy, sp, saved, name):
    S = h.shape[0]
    seg = S // SEGMENTS
    bs_r, bs_i = saved
    zeros = jnp.zeros((h.shape[1] // SLAB_CH, SEGMENTS, SLAB_ST), f32)
    gf_r, gf_i = _s5_scan_bwd(None, dy, None, None, sp["a_r"], sp["a_i"], None, None, zeros, zeros, sp["wc_r"], sp["wc_i"],
                              full=False, tb=_SCAN_TB, name=name + "_local")
    gi_r, gi_i = _s5_chain(gf_r, gf_i, sp["a_r"], sp["a_i"], seg=seg, reverse=True, name=name + "_chain")
    du, d_wbr, d_wbi, d_wcr, d_wci, d_ar, d_ai, _, _ = _s5_scan_bwd(
        h, dy, sp["wb_r"], sp["wb_i"], sp["a_r"], sp["a_i"], bs_r, bs_i, gi_r, gi_i, sp["wc_r"], sp["wc_i"],
        full=True, tb=_SCAN_TB, name=name)
    return du, (d_ar, d_ai, d_wbr, d_wbi, d_wcr, d_wci)


def _ffn_fwd(x1, mod, n2g, w_gu, w_down, tag):
    F = w_down.shape[0]
    h2 = _rowwise(lambda x, g, m: _normmod(x, g, m[3:4], m[4:5]), [x1], [n2g, mod], [(x1.shape[1], bf16)], name=tag + "_norm2")[0]
    ab = _mm(h2, w_gu, out_dtype=bf16, name=tag + "_gu")
    act = _rowwise(lambda ab: jax.nn.silu(ab[:, :F].astype(f32)) * ab[:, F:].astype(f32), [ab], [], [(F, bf16)], name=tag + "_act")[0]
    f = _mm(act, w_down, name=tag + "_down")
    x2 = _rowwise(lambda x, f, m: x + m[5:6] * f, [x1, f], [mod], [(x1.shape[1], f32)], name=tag + "_res2")[0]
    return x2, dict(x1=x1, h2=h2, ab=ab, act=act, f=f)


def _ffn_bwd(dx2, sv, mod, n2g, w_gu, w_down, tag):
    F = w_down.shape[0]
    D = dx2.shape[1]
    df, dgate2 = _rowwise(lambda dx, f, m: (m[5:6] * dx, jnp.sum(dx * f, axis=0, keepdims=True)), [dx2, sv["f"]], [mod],
                          [(D, bf16)], [(1, D)], name=tag + "_res2_bwd")
    dact = _mm(df, w_down, tb=True, out_dtype=bf16, name=tag + "_down_dx")
    dw_down = _mm(sv["act"], df, ta=True, name=tag + "_down_dw")
    dab = _rowwise_bwd(lambda ab: jax.nn.silu(ab[:, :F]) * ab[:, F:], [sv["ab"]], [], [dact], [0], [bf16], name=tag + "_act_bwd")[0]
    dh2 = _mm(dab, w_gu, tb=True, name=tag + "_gu_dx")
    dw_gu = _mm(sv["h2"], dab, ta=True, name=tag + "_gu_dw")
    dx1, dn2g, dmod = _rowwise_bwd(lambda x, g, m: _normmod(x, g, m[3:4], m[4:5]), [sv["x1"]], [n2g, mod], [dh2], [0], [f32],
                                   adds={0: dx2}, name=tag + "_norm2_bwd")
    dmod = dmod.at[5:6].add(dgate2)
    return dx1, dw_gu, dw_down, dn2g, dmod


def _device_step(x, target, pos, mods, kmod, W):
    S, D = x.shape
    F = W["ffn_w_down"].shape[1]
    inv = 1.0 / (ROPE_THETA ** (jnp.arange(0, ROPE, 2, dtype=f32) / ROPE))
    inv128 = _lanes128(jnp.concatenate([inv, inv]), NOPE)
    sign128 = _lanes128(jnp.concatenate([-jnp.ones(ROPE // 2, f32), jnp.ones(ROPE // 2, f32)]), NOPE)
    cf, ss = _rowwise(lambda p, iv, sg: (jnp.cos(p * iv), jnp.sin(p * iv) * sg), [pos], [inv128, sign128],
                      [(HEAD_PAD, f32), (HEAD_PAD, f32)], name="rope_table")

    w_gu = W.get("ffn_w_gu") or [jnp.concatenate([W["ffn_w_gate"][l], W["ffn_w_up"][l]], axis=1) for l in range(DEPTH)]
    g_gu = [None] * DEPTH
    saved = []
    xs = _to_segments(x)

    s5p = []
    for l in range(N_A):
        tag = f"l{l}"
        mod = mods[l]
        p = {k: W["s5_" + k][l] for k in ("lam_re", "lam_im", "log_dt", "b_re", "b_im", "c_re", "c_im")}
        sp = _s5_params_fwd(p)
        s5p.append((p, sp))
        n1g = W["norm1_g"][l][None]
        dsk = W["s5_d"][l][None]
        bgl = W["s5_b_glu"][l][None]
        h = _rowwise(lambda x, g, m: _normmod(x, g, m[0:1], m[1:2]), [xs], [n1g, mod], [(D, f32)], name=tag + "_norm1")[0]
        y_ssm, scan_saved = _s5_mix_fwd(h, sp, tag + "_scan")
        g = _rowwise(lambda ys, h, d: jax.nn.gelu(ys + d * h), [y_ssm, h], [dsk], [(D, f32)], name=tag + "_gelu")[0]
        z = _mm(g, W["s5_w_glu"][l], name=tag + "_glu")
        x1 = _rowwise(lambda x, g, z, b, m: x + m[2:3] * (g * jax.nn.sigmoid(z + b)), [xs, g, z], [bgl, mod], [(D, f32)], name=tag + "_res1")[0]
        x2, fsv = _ffn_fwd(x1, mod, W["norm2_g"][l][None], w_gu[l], W["ffn_w_down"][l], tag)
        saved.append(dict(x0=xs, h=h, y_ssm=y_ssm, g=g, z=z, scan=scan_saved, ffn=fsv))
        xs = x2
    x = _from_segments(xs)

    w_kv_a = jnp.concatenate([W["w_kv_a"][:, :KV_RANK], jnp.zeros((D, NOPE), f32).astype(W["w_kv_a"].dtype), W["w_kv_a"][:, KV_RANK:],
                              jnp.zeros((D, HEAD_PAD - NOPE - ROPE), f32).astype(W["w_kv_a"].dtype)], axis=1)
    wkb = W["w_kv_b"].reshape(KV_RANK, N_HEADS, NOPE + VDIM)
    w_kb = _pad_heads(wkb[:, :, :NOPE].reshape(KV_RANK, N_HEADS * NOPE), NOPE, 1)
    w_vb = _pad_heads(wkb[:, :, NOPE:].reshape(KV_RANK, N_HEADS * VDIM), VDIM, 1)
    kvg = W["kv_norm_g"][None]
    ga = W["kv_a_norm_g"][None]
    gkr = _lanes128(W["k_rope_norm_g"], NOPE)
    gkn = _lanes128(W["k_nope_norm_g"], 0)
    hk = _rowwise(lambda x, g, m: _normmod(x, g, m[0:1], m[1:2]), [x], [kvg, kmod], [(D, bf16)], name="kv_norm")[0]
    kva = _mm(hk, w_kv_a, name="kv_a")
    ckv, kr = _rowwise(_kv_a_post, [kva, cf, ss], [ga, gkr], [(KV_RANK, f32), (HEAD_PAD, f32)], name="kv_a_post")
    kraw = _mm(ckv, w_kb, name="kv_bk")
    vpad = _mm(ckv, w_vb, out_dtype=bf16, name="kv_bv")
    kpad = _rowwise(_k_heads, [kraw, kr], [gkn], [(N_HEADS * HEAD_PAD, bf16)], name="k_heads")[0]
    kv_saved = dict(x=x, hk=hk, kva=kva, ckv=ckv, kr=kr, kraw=kraw)

    for l in range(N_A, DEPTH):
        tag = f"l{l}"
        j = l - N_A
        mod = mods[l]
        n1g = W["norm1_g"][l][None]
        w_uq = _pad_heads(W["mla_w_uq"][j], NOPE + ROPE, 1)
        w_o = _pad_heads(W["mla_w_o"][j], VDIM, 0)
        qg = W["mla_q_norm_g"][j][None]
        gqn = _lanes128(W["mla_q_nope_norm_g"][j], 0)
        gqr = _lanes128(W["mla_q_rope_norm_g"][j], NOPE)
        h = _rowwise(lambda x, g, m: _normmod(x, g, m[0:1], m[1:2]), [x], [n1g, mod], [(D, bf16)], name=tag + "_norm1")[0]
        qa = _mm(h, W["mla_w_dq"][j], name=tag + "_dq")
        qc = _rowwise(_rms, [qa], [qg], [(qa.shape[1], f32)], name=tag + "_qnorm")[0]
        qraw = _mm(qc, w_uq, name=tag + "_uq")
        q = _rowwise(_q_heads, [qraw, cf, ss], [gqn, gqr], [(N_HEADS * HEAD_PAD, bf16)], name=tag + "_q_heads")[0]
        o, lse = _attn_fwd(q, kpad, vpad, t=_ATTN_T, name=tag + "_attn")
        mix = _mm(o, w_o, name=tag + "_wo")
        x1 = _rowwise(lambda x, mx, m: x + m[2:3] * mx, [x, mix], [mod], [(D, f32)], name=tag + "_res1")[0]
        x2, fsv = _ffn_fwd(x1, mod, W["norm2_g"][l][None], w_gu[l], W["ffn_w_down"][l], tag)
        saved.append(dict(x0=x, h=h, qa=qa, qc=qc, qraw=qraw, q=q, o=o, lse=lse, mix=mix, w_uq=w_uq, w_o=w_o, ffn=fsv))
        x = x2

    dx, loss = _rowwise(
        lambda y, t: ((y - t) * (1.0 / D), jnp.full((1, LANES), 0.5 * jnp.sum(jnp.mean(jnp.square(y - t), axis=-1)), f32)),
        [x, target], [], [(D, f32)], [(1, LANES)], name="loss")

    gW = {}
    dmods = [None] * DEPTH
    g_gate, g_up, g_down, g_n1, g_n2 = [None] * DEPTH, [None] * DEPTH, [None] * DEPTH, [None] * DEPTH, [None] * DEPTH
    dks, dvs = [], []
    g_dq, g_qn, g_uq, g_qnn, g_qrn, g_wo = [None] * 2, [None] * 2, [None] * 2, [None] * 2, [None] * 2, [None] * 2
    for l in range(DEPTH - 1, N_A - 1, -1):
        tag = f"l{l}"
        j = l - N_A
        sv = saved[l]
        mod = mods[l]
        dx1, dw_gu, dw_down, g_n2[l], dmod = _ffn_bwd(dx, sv["ffn"], mod, W["norm2_g"][l][None], w_gu[l], W["ffn_w_down"][l], tag)
        g_gate[l], g_up[l], g_down[l], g_gu[l] = dw_gu[:, :F], dw_gu[:, F:], dw_down, dw_gu
        dmix, dgate1 = _rowwise(lambda dx, mx, m: (m[2:3] * dx, jnp.sum(dx * mx, axis=0, keepdims=True)), [dx1, sv["mix"]], [mod],
                                [(D, bf16)], [(1, D)], name=tag + "_res1_bwd")
        do = _mm(dmix, sv["w_o"], tb=True, out_dtype=bf16, name=tag + "_wo_dx")
        g_wo[j] = _unpad_heads(_mm(sv["o"], dmix, ta=True, name=tag + "_wo_dw"), VDIM, 0)
        dq, dk, dv = _attn_bwd(sv["q"], kpad, vpad, do, sv["o"], sv["lse"], t=_ATTN_T, name=tag + "_attn_bwd")
        dks.append(dk)
        dvs.append(dv)
        gqn = _lanes128(W["mla_q_nope_norm_g"][j], 0)
        gqr = _lanes128(W["mla_q_rope_norm_g"][j], NOPE)
        dqraw, dgqn, dgqr = _rowwise_bwd(lambda qr, c, s, a, b: _q_heads(qr, c, s, a, b), [sv["qraw"], cf, ss], [gqn, gqr], [dq], [0], [bf16],
                                         name=tag + "_q_heads_bwd")
        g_qnn[j], g_qrn[j] = dgqn[0, :NOPE], dgqr[0, NOPE:NOPE + ROPE]
        dqc = _mm(dqraw, sv["w_uq"], tb=True, name=tag + "_uq_dx")
        g_uq[j] = _unpad_heads(_mm(sv["qc"], dqraw, ta=True, name=tag + "_uq_dw"), NOPE + ROPE, 1)
        qg = W["mla_q_norm_g"][j][None]
        dqa, dqg = _rowwise_bwd(_rms, [sv["qa"]], [qg], [dqc], [0], [bf16], name=tag + "_qnorm_bwd")
        g_qn[j] = dqg[0]
        dh = _mm(dqa, W["mla_w_dq"][j], tb=True, name=tag + "_dq_dx")
        g_dq[j] = _mm(sv["h"], dqa, ta=True, name=tag + "_dq_dw")
        dx, dn1g, dmod1 = _rowwise_bwd(lambda x, g, m: _normmod(x, g, m[0:1], m[1:2]), [sv["x0"]], [W["norm1_g"][l][None], mod], [dh], [0], [f32],
                                       adds={0: dx1}, name=tag + "_norm1_bwd")
        g_n1[l] = dn1g[0]
        dmods[l] = (dmod + dmod1).at[2:3].add(dgate1)

    gkn = _lanes128(W["k_nope_norm_g"], 0)
    dk_sum = _sum_lead(jnp.stack(dks), f32, "dk_sum")
    dv_sum = _sum_lead(jnp.stack(dvs), bf16, "dv_sum")
    dkraw, dkr, dgkn = _rowwise_bwd(lambda kr_, r, g: _k_heads(kr_, r, g), [kv_saved["kraw"], kv_saved["kr"]], [gkn], [dk_sum], [0, 1], [bf16, f32],
                                    name="k_heads_bwd")
    dckv = _sum_lead(jnp.stack([_mm(dkraw, w_kb, tb=True, name="kv_bk_dx"), _mm(dv_sum, w_vb, tb=True, name="kv_bv_dx")]), f32, "dckv_sum")
    g_kb = _unpad_heads(_mm(kv_saved["ckv"], dkraw, ta=True, name="kv_bk_dw"), NOPE, 1)
    g_vb = _unpad_heads(_mm(kv_saved["ckv"], dv_sum, ta=True, name="kv_bv_dw"), VDIM, 1)
    gW["w_kv_b"] = jnp.concatenate([g_kb.reshape(KV_RANK, N_HEADS, NOPE), g_vb.reshape(KV_RANK, N_HEADS, VDIM)], axis=2).reshape(KV_RANK, -1)
    dkva, dga, dgkr = _rowwise_bwd(_kv_a_post, [kv_saved["kva"], cf, ss], [ga, gkr], [dckv, dkr], [0], [bf16], name="kv_a_post_bwd")
    dhk = _mm(dkva, w_kv_a, tb=True, name="kv_a_dx")
    g_kva = _mm(kv_saved["hk"], dkva, ta=True, name="kv_a_dw")
    gW["w_kv_a"] = jnp.concatenate([g_kva[:, :KV_RANK], g_kva[:, KV_RANK + NOPE:KV_RANK + NOPE + ROPE]], axis=1)
    dx, dkvg, dkmod = _rowwise_bwd(lambda x, g, m: _normmod(x, g, m[0:1], m[1:2]), [kv_saved["x"]], [kvg, kmod], [dhk], [0], [f32],
                                   adds={0: dx}, name="kv_norm_bwd")
    gW["kv_norm_g"], gW["kv_a_norm_g"] = dkvg[0], dga[0]
    gW["k_nope_norm_g"], gW["k_rope_norm_g"] = dgkn[0, :NOPE], dgkr[0, NOPE:NOPE + ROPE]

    dxs = _to_segments(dx)
    g_s5 = {k: [None] * N_A for k in ("lam_re", "lam_im", "log_dt", "b_re", "b_im", "c_re", "c_im", "d", "w_glu", "b_glu")}
    for l in range(N_A - 1, -1, -1):
        tag = f"l{l}"
        sv = saved[l]
        mod = mods[l]
        p, sp = s5p[l]
        dsk = W["s5_d"][l][None]
        bgl = W["s5_b_glu"][l][None]
        dx1, dw_gu, dw_down, g_n2[l], dmod = _ffn_bwd(dxs, sv["ffn"], mod, W["norm2_g"][l][None], w_gu[l], W["ffn_w_down"][l], tag)
        g_gate[l], g_up[l], g_down[l], g_gu[l] = dw_gu[:, :F], dw_gu[:, F:], dw_down, dw_gu

        def res1(g, z, b, m):
            return m[2:3] * (g * jax.nn.sigmoid(z + b))
        dg1, dz, dbgl, dmod_g = _rowwise_bwd(res1, [sv["g"], sv["z"]], [bgl, mod], [dx1], [0, 1], [f32, bf16], name=tag + "_res1_bwd")
        dg2 = _mm(dz, W["s5_w_glu"][l], tb=True, name=tag + "_glu_dx")
        g_s5["w_glu"][l] = _mm(sv["g"], dz, ta=True, name=tag + "_glu_dw")
        g_s5["b_glu"][l] = dbgl[0]

        def gelu_bwd(ys, h, dga_, dgb_, d):
            _, vjp = jax.vjp(lambda ys, h, d: jax.nn.gelu(ys + d * h), ys, h, d)
            dys, dh, dd = vjp(dga_ + dgb_)
            return dys, dh, dd
        dys, dh_a, ddsk = _rowwise(gelu_bwd, [sv["y_ssm"], sv["h"], dg1, dg2], [dsk], [(D, f32), (D, f32)], [(1, D)], name=tag + "_gelu_bwd")
        g_s5["d"][l] = ddsk[0]
        du, scan_g = _s5_mix_bwd(sv["h"], dys, sp, sv["scan"], tag + "_scan_bwd")
        pg = _s5_params_bwd(p, sp, *scan_g)
        for k in ("lam_re", "lam_im", "log_dt", "b_re", "b_im", "c_re", "c_im"):
            g_s5[k][l] = pg[k]
        dxs, dn1g, dmod1 = _rowwise_bwd(lambda x, g, m: _normmod(x, g, m[0:1], m[1:2]), [sv["x0"]], [W["norm1_g"][l][None], mod],
                                        [_sum_lead(jnp.stack([dh_a, du]), f32, tag + "_dh_sum")], [0], [f32], adds={0: dx1}, name=tag + "_norm1_bwd")
        g_n1[l] = dn1g[0]
        dmods[l] = dmod + dmod1 + dmod_g
    dx = _from_segments(dxs)

    gW.update(
        norm1_g=jnp.stack(g_n1), norm2_g=jnp.stack([g[0] for g in g_n2]),
        ffn_w_gate=jnp.stack(g_gate), ffn_w_up=jnp.stack(g_up), ffn_w_down=jnp.stack(g_down), ffn_w_gu=g_gu,
        mla_w_dq=jnp.stack(g_dq), mla_q_norm_g=jnp.stack(g_qn), mla_w_uq=jnp.stack(g_uq),
        mla_q_nope_norm_g=jnp.stack(g_qnn), mla_q_rope_norm_g=jnp.stack(g_qrn), mla_w_o=jnp.stack(g_wo),
    )
    for k, v in g_s5.items():
        gW["s5_" + k] = jnp.stack(v)
    return loss, dx, gW, jnp.stack(dmods), dkmod


_WEIGHTS = ['ada_w', 'ada_b', 'norm1_g', 'norm2_g', 'ffn_w_gate', 'ffn_w_up', 'ffn_w_down', 's5_lam_re', 's5_lam_im', 's5_log_dt',
            's5_b_re', 's5_b_im', 's5_c_re', 's5_c_im', 's5_d', 's5_w_glu', 's5_b_glu', 'kv_ada_w', 'kv_ada_b', 'kv_norm_g', 'w_kv_a',
            'kv_a_norm_g', 'w_kv_b', 'k_nope_norm_g', 'k_rope_norm_g', 'mla_w_dq', 'mla_q_norm_g', 'mla_w_uq', 'mla_q_nope_norm_g',
            'mla_q_rope_norm_g', 'mla_w_o']
_BIG = {"ffn_w_gate": 2, "ffn_w_up": 2, "ffn_w_down": 1, "s5_w_glu": 1, "w_kv_a": 0, "w_kv_b": 1, "mla_w_dq": 1, "mla_w_uq": 2, "mla_w_o": 1}
_TENSOR_PARALLEL = ("ada_w", "kv_ada_w")
_SHARDED_VECS = ("s5_d", "s5_b_glu")
_CHIPS = 4
_FLAT_W = 1024
_FLAT_ALIGN = 512
_W_GROUPS = (("gu", ("ffn_w_gate", "ffn_w_up")), ("rows", ("ffn_w_down", "s5_w_glu", "mla_w_o")),
             ("misc", ("w_kv_a", "w_kv_b", "mla_w_dq", "mla_w_uq")))


def _shard_rows(arrs, flat):
    if flat:
        return _pack_rows(arrs, _FLAT_W, _FLAT_ALIGN)
    C = arrs[0].shape[-1]
    return jnp.concatenate([a.reshape(-1, C) for a in arrs], axis=0)


def _take_rows(halves, r0, r1):
    R2 = halves.shape[2]
    pieces = []
    if r0 < R2:
        pieces.append(halves[0][:, r0:min(r1, R2)])
    if r1 > R2:
        pieces.append(halves[1][:, max(r0, R2) - R2:r1 - R2])
    return pieces[0] if len(pieces) == 1 else jnp.concatenate(pieces, axis=1)


def _pack_rows(arrs, width, align):
    flat = jnp.concatenate([a.reshape(-1) for a in arrs])
    rows = -(-flat.shape[0] // (width * align)) * align
    flat = jnp.pad(flat, (0, rows * width - flat.shape[0]))
    return flat.reshape(rows, width)


def _unpack(flat, shapes):
    flat = flat.reshape(-1)
    out, off = [], 0
    for s in shapes:
        n = math.prod(s)
        out.append(flat[off:off + n].reshape(s))
        off += n
    return out


def _adamw(w, g, m, v, name):
    c1 = 1.0 - ADAM_B1 ** ADAM_STEP
    c2 = 1.0 - ADAM_B2 ** ADAM_STEP

    def fn(w, g, m, v):
        m = ADAM_B1 * m + (1.0 - ADAM_B1) * g
        v = ADAM_B2 * v + (1.0 - ADAM_B2) * jnp.square(g)
        delta = -ADAM_LR * ((m / c1) / (jnp.sqrt(v / c2) + ADAM_EPS) + ADAM_WD * w)
        return delta, m, v

    C = w.shape[1]
    tm = _pick(w.shape[0], (256, 128, 64, 32, 16, 8))
    return _rowwise(fn, [w, g, m, v], [], [(C, f32)] * 3, tm=tm, name=name)


def kernel(x, c, positions, ada_w, ada_b, norm1_g, norm2_g, ffn_w_gate, ffn_w_up, ffn_w_down, s5_lam_re, s5_lam_im, s5_log_dt, s5_b_re, s5_b_im, s5_c_re, s5_c_im, s5_d, s5_w_glu, s5_b_glu, kv_ada_w, kv_ada_b, kv_norm_g, w_kv_a, kv_a_norm_g, w_kv_b, k_nope_norm_g, k_rope_norm_g, mla_w_dq, mla_q_norm_g, mla_w_uq, mla_q_nope_norm_g, mla_q_rope_norm_g, mla_w_o, loss_target, m_ada_w, m_ada_b, m_norm1_g, m_norm2_g, m_ffn_w_gate, m_ffn_w_up, m_ffn_w_down, m_s5_lam_re, m_s5_lam_im, m_s5_log_dt, m_s5_b_re, m_s5_b_im, m_s5_c_re, m_s5_c_im, m_s5_d, m_s5_w_glu, m_s5_b_glu, m_kv_ada_w, m_kv_ada_b, m_kv_norm_g, m_w_kv_a, m_kv_a_norm_g, m_w_kv_b, m_k_nope_norm_g, m_k_rope_norm_g, m_mla_w_dq, m_mla_q_norm_g, m_mla_w_uq, m_mla_q_nope_norm_g, m_mla_q_rope_norm_g, m_mla_w_o, v_ada_w, v_ada_b, v_norm1_g, v_norm2_g, v_ffn_w_gate, v_ffn_w_up, v_ffn_w_down, v_s5_lam_re, v_s5_lam_im, v_s5_log_dt, v_s5_b_re, v_s5_b_im, v_s5_c_re, v_s5_c_im, v_s5_d, v_s5_w_glu, v_s5_b_glu, v_kv_ada_w, v_kv_ada_b, v_kv_norm_g, v_w_kv_a, v_kv_a_norm_g, v_w_kv_b, v_k_nope_norm_g, v_k_rope_norm_g, v_mla_w_dq, v_mla_q_norm_g, v_mla_w_uq, v_mla_q_nope_norm_g, v_mla_q_rope_norm_g, v_mla_w_o):
    given = dict(locals())
    w = {n: given[n] for n in _WEIGHTS}
    m_in = {n: given["m_" + n] for n in _WEIGHTS}
    v_in = {n: given["v_" + n] for n in _WEIGHTS}
    S, D = x.shape[1], x.shape[2]
    ax, ay, ac = lax.axis_index("x"), lax.axis_index("y"), lax.axis_index("c")
    chip = 2 * ax + ay
    me = 4 * ax + 2 * ay + ac
    n_ada = ada_w.shape[2]
    n_kada = kv_ada_w.shape[1]

    g1 = _xchg(_pack_rows([c, s5_d, s5_b_glu], LANES, SUBLANES), "all", True, "gather_cond")
    g1 = g1.reshape(8, -1)
    c_all = g1[:, :D]
    vec = g1[0::2, D:D + 2 * s5_d.size].reshape(_CHIPS, 2, N_A, s5_d.shape[1])
    full_vecs = {"s5_d": vec[:, 0].transpose(1, 0, 2).reshape(N_A, D), "s5_b_glu": vec[:, 1].transpose(1, 0, 2).reshape(N_A, D)}
    c_act = _rowwise(jax.nn.silu, [c_all], [], [(D, f32)], tm=8, name="cond_silu")[0]

    mod_cols = [_mm(c_act, ada_w[l], cast=False, name="ada_proj") for l in range(DEPTH)] + [_mm(c_act, kv_ada_w, cast=False, name="kv_ada_proj")]
    g2 = _xchg(_pack_rows([jnp.concatenate(mod_cols, axis=1)], LANES, SUBLANES), "all", True, "gather_mod")
    g2 = g2.reshape(8, 8, -1)[0::2]
    mine = lax.dynamic_index_in_dim(g2, me, axis=1, keepdims=False)
    mod_lin = mine[:, :DEPTH * n_ada].reshape(_CHIPS, DEPTH, n_ada).transpose(1, 0, 2).reshape(DEPTH, 6 * D)
    kmod_lin = mine[:, DEPTH * n_ada:].reshape(1, 2 * D)
    mods = _rowwise(lambda a, b: a + b, [mod_lin, ada_b], [], [(6 * D, f32)], tm=DEPTH, name="ada_bias")[0].reshape(DEPTH, 6, D)
    kmod = _rowwise(lambda a, b: a + b, [kmod_lin, kv_ada_b.reshape(1, 2 * D)], [], [(2 * D, f32)], tm=1, name="kv_ada_bias")[0].reshape(2, D)

    chip_shards = {}
    for g, names in _W_GROUPS:
        buf = _shard_rows([w[n] for n in names], g == "misc").astype(bf16)
        Rg, C = buf.shape
        R2 = Rg // 2
        my_half = lax.dynamic_slice_in_dim(buf, ac * R2, R2, axis=0)
        halves_in = _xchg(my_half, "xy", True, "gather_" + g)
        both = _pair(halves_in.reshape(1, _CHIPS * R2, C), "swap", bf16, "gather_pair_" + g).reshape(2, _CHIPS, R2, C)
        if g == "misc":
            chip_shards.update(zip(names, _unpack_chips(both.reshape(2, _CHIPS, R2 * C), [w[n].shape for n in names])))
        else:
            r = 0
            for n in names:
                rows = math.prod(w[n].shape[:-1])
                chip_shards[n] = _take_rows(both, r, r + rows).reshape((_CHIPS,) + w[n].shape)
                r += rows
    W = {n: jnp.concatenate([chip_shards[n][q] for q in range(_CHIPS)], axis=_BIG[n]) for n in _BIG if n not in _W_GROUPS[0][1]}
    W["ffn_w_gu"] = [jnp.concatenate([chip_shards[n][q, l] for n in _W_GROUPS[0][1] for q in range(_CHIPS)], axis=1) for l in range(DEPTH)]
    for n in _WEIGHTS:
        if n not in _BIG and n not in _TENSOR_PARALLEL and n not in _SHARDED_VECS and n not in ("ada_b", "kv_ada_b"):
            W[n] = w[n]
    W.update(full_vecs)

    pos = positions.reshape(S, 1).astype(f32)
    loss, dx, gW, dmods, dkmod = _device_step(x[0], loss_target[0], pos, mods, kmod, W)

    g4 = _xchg(_pack_rows([dmods, dkmod], LANES, SUBLANES), "all", True, "gather_dmod")
    ada_b_sum = _sum_lead(g4, f32, "dmod_sum").reshape(-1)
    g_ada_b = ada_b_sum[:DEPTH * 6 * D].reshape(DEPTH, 6 * D)
    g_kv_ada_b = ada_b_sum[DEPTH * 6 * D:DEPTH * 6 * D + 2 * D]
    dm_all = g4.reshape(8, -1)
    dm_ada = lax.dynamic_slice_in_dim(dm_all[:, :DEPTH * 6 * D].reshape(8, DEPTH, 6 * D), chip * n_ada, n_ada, axis=2)
    dm_kv = lax.dynamic_slice_in_dim(dm_all[:, DEPTH * 6 * D:DEPTH * 6 * D + 2 * D], chip * n_kada, n_kada, axis=1)

    def outer(at, b):
        acc = at[:, 0:1] * b[0:1, :]
        for i in range(1, 8):
            acc = acc + at[:, i:i + 1] * b[i:i + 1, :]
        return acc

    c_act_t = c_act.T
    g_ada_w = jnp.stack([_rowwise(outer, [c_act_t], [dm_ada[:, l]], [(n_ada, f32)], name="ada_dw")[0] for l in range(DEPTH)])
    g_kv_ada_w = _rowwise(outer, [c_act_t], [dm_kv], [(n_kada, f32)], name="kv_ada_dw")[0]

    g_big = {}
    for g, names in _W_GROUPS:
        if g == "gu":
            Fq = w[names[0]].shape[-1]
            halves = jnp.stack([jnp.stack([jnp.concatenate([gW["ffn_w_gu"][l][:, (h * _CHIPS + q) * Fq:(h * _CHIPS + q + 1) * Fq]
                                                            for l in range(DEPTH)], axis=0) for q in range(_CHIPS)]) for h in range(2)])
            halves = halves.astype(bf16)
        else:
            per_chip = [_shard_rows([jnp.split(gW[n], _CHIPS, axis=_BIG[n])[q] for n in names], g == "misc") for q in range(_CHIPS)]
            full = jnp.stack(per_chip).astype(bf16)
            halves = full.reshape(_CHIPS, 2, full.shape[1] // 2, full.shape[2]).transpose(1, 0, 2, 3)
        _, _, R2, C = halves.shape
        hsum = _pair(halves.reshape(2, _CHIPS * R2, C), "add", bf16, "grads_pair_" + g).reshape(_CHIPS, R2, C)
        rb = _xchg(hsum, "xy", False, "grads_chips_" + g)
        red = _pair(rb, "swap", f32, "grads_halves_" + g).reshape(2 * R2, C)
        if g == "misc":
            g_big.update(zip(names, _unpack(red, [w[n].shape for n in names])))
        else:
            r = 0
            for n in names:
                rows = math.prod(w[n].shape[:-1])
                g_big[n] = red[r:r + rows].reshape(w[n].shape)
                r += rows

    small = [n for n in _WEIGHTS if n not in _BIG and n not in _TENSOR_PARALLEL and n not in ("ada_b", "kv_ada_b")]
    svec = _pack_rows([loss[0, :1]] + [gW[n] for n in small], LANES, 8 * SUBLANES)
    rows8 = svec.shape[0] // 8
    rs = _xchg(svec.reshape(8, rows8, LANES), "all", False, "small_scatter")
    red = _sum_lead(rs, f32, "small_sum")
    full = _xchg(red, "all", True, "small_gather").reshape(-1)
    loss_tot = full[0]
    g_small = dict(zip(small, _unpack(full[1:], [gW[n].shape for n in small])))
    for n in _SHARDED_VECS:
        g_small[n] = lax.dynamic_slice_in_dim(g_small[n], chip * w[n].shape[1], w[n].shape[1], axis=1)
    g_small["ada_b"] = g_ada_b
    g_small["kv_ada_b"] = g_kv_ada_b

    grads = dict(g_big)
    grads.update(g_small)
    grads["ada_w"] = g_ada_w
    grads["kv_ada_w"] = g_kv_ada_w

    delta, new_m, new_v = {}, {}, {}
    natural = _TENSOR_PARALLEL + _W_GROUPS[0][1] + _W_GROUPS[1][1]
    for n in natural:
        C = w[n].shape[-1]
        d_, m_, v_ = _adamw(w[n].reshape(-1, C), grads[n].reshape(-1, C), m_in[n].reshape(-1, C), v_in[n].reshape(-1, C), "adamw_" + n)
        delta[n], new_m[n], new_v[n] = d_.reshape(w[n].shape), m_.reshape(w[n].shape), v_.reshape(w[n].shape)
    rest = [n for n in _WEIGHTS if n not in natural]
    d_, m_, v_ = _adamw(*[_pack_rows([src[n] for n in rest], _FLAT_W, SUBLANES) for src in (w, grads, m_in, v_in)], "adamw_small")
    shapes = [w[n].shape for n in rest]
    for res, src in ((delta, d_), (new_m, m_), (new_v, v_)):
        res.update(zip(rest, _unpack(src, shapes)))

    return (loss_tot, dx[None], *[grads[n] for n in _WEIGHTS], *[delta[n] for n in _WEIGHTS],
            *[new_m[n] for n in _WEIGHTS], *[new_v[n] for n in _WEIGHTS])


def _unpack_chips(halves, shapes):
    _, n_chip, L = halves.shape
    out, off = [], 0
    for s in shapes:
        n = math.prod(s)
        pieces = []
        if off < L:
            pieces.append(halves[0][:, off:min(off + n, L)])
        if off + n > L:
            pieces.append(halves[1][:, max(off, L) - L:off + n - L])
        flat = pieces[0] if len(pieces) == 1 else jnp.concatenate(pieces, axis=1)
        out.append(flat.reshape((n_chip,) + tuple(s)))
        off += n
    return out
```
